```python
import jax, jax.numpy as jnp
from jax import lax
import numpy as np

D_MODEL = 1024
BATCH = 4
SEQ = 4096
DEPTH = 1
DEC_BATCH = 128
DEC_SEQ = 4
PAST_LEN = 8192
PAGE_SIZE = 128

MIX_WIDTH = D_MODEL
ATT_WIDTH = MIX_WIDTH // 2
HG_WIDTH = MIX_WIDTH - ATT_WIDTH
ATT_HEAD_DIM = 64
ATT_HEADS = ATT_WIDTH // ATT_HEAD_DIM
DILATIONS = ((128, 1), (512, 4), (2048, 16))
ATT_WINDOW_MAX = 2048
ATT_BLOCK = 128
ROPE_THETA = 10000.0
HG_EXPAND = 128
HG_HEADS = HG_WIDTH // HG_EXPAND
HG_DK = HG_EXPAND
HG_DV = HG_WIDTH // HG_HEADS
HG_CHUNK = 32
D_FF = 2816
IN_COLS = 3 * ATT_WIDTH + 4 * HG_WIDTH
IN_SPLITS = [ATT_WIDTH, 2 * ATT_WIDTH, 3 * ATT_WIDTH,
             3 * ATT_WIDTH + HG_WIDTH, 3 * ATT_WIDTH + 2 * HG_WIDTH, 3 * ATT_WIDTH + 3 * HG_WIDTH]
DN_ALPHA = (2.0 * DEPTH) ** 0.25
DN_BETA = (8.0 * DEPTH) ** -0.25
LN_EPS = 1e-5
NEG_INF = -1e30

kernel_name = "hymba_longnet_hgrn2_macaron_deepnorm_step"


def layer_norm(x, g, b):
    xf = x.astype(jnp.float32)
    mu = jnp.mean(xf, axis=-1, keepdims=True)
    var = jnp.mean(jnp.square(xf - mu), axis=-1, keepdims=True)
    return ((xf - mu) * lax.rsqrt(var + LN_EPS)).astype(x.dtype) * g + b


def head_rms(o):
    of = o.astype(jnp.float32)
    return (of * lax.rsqrt(jnp.mean(of * of, axis=-1, keepdims=True) + LN_EPS)).astype(o.dtype)


def swiglu(x, w_gate, w_up, w_down):
    return (jax.nn.silu(x @ w_gate) * (x @ w_up)) @ w_down


def rope(x, pos):
    half = x.shape[-1] // 2
    inv = ROPE_THETA ** (-jnp.arange(half, dtype=jnp.float32) / half)
    ang = pos.astype(jnp.float32)[:, None] * inv[None, :]
    cos = jnp.cos(ang)[:, None, :].astype(x.dtype)
    sin = jnp.sin(ang)[:, None, :].astype(x.dtype)
    x1, x2 = x[..., :half], x[..., half:]
    return jnp.concatenate([x1 * cos - x2 * sin, x2 * cos + x1 * sin], axis=-1)


def project(x, w_in, lb, pos):
    b, t, _ = x.shape
    z = x @ w_in
    qa, ka, va, qh, fh, ih, gh = jnp.split(z, IN_SPLITS, axis=-1)
    qa = rope(qa.reshape(b, t, ATT_HEADS, ATT_HEAD_DIM), pos)
    ka = rope(ka.reshape(b, t, ATT_HEADS, ATT_HEAD_DIM), pos)
    va = va.reshape(b, t, ATT_HEADS, ATT_HEAD_DIM)
    zf = fh.astype(jnp.float32).reshape(b, t, HG_HEADS, HG_DK)
    lbh = lb.reshape(HG_HEADS, HG_DK)
    logf = jnp.log(lbh + (1.0 - lbh) * jax.nn.sigmoid(zf))
    kh = (1.0 - lbh) * jax.nn.sigmoid(-zf)
    qh = jax.nn.silu(qh).reshape(b, t, HG_HEADS, HG_DK)
    ih = ih.reshape(b, t, HG_HEADS, HG_DV)
    return (qa, ka, va), (qh, kh, ih, logf), gh


def dilated_branch_prompt(q, k, v, window, dil):
    b, t, h, e = q.shape
    w_sub = window // dil
    span = dil * ATT_BLOCK
    t_pad = -(-t // span) * span
    n_blk = t_pad // span

    def to_sub(a):
        a = jnp.pad(a, ((0, 0), (0, t_pad - t), (0, 0), (0, 0)))
        a = a.reshape(b, t_pad // dil, dil, h, e).transpose(0, 2, 1, 3, 4)
        return a.reshape(b, dil, n_blk, ATT_BLOCK, h, e)

    qs, ks, vs = to_sub(q), to_sub(k), to_sub(v)
    prev = ((0, 0), (0, 0), (1, 0), (0, 0), (0, 0), (0, 0))
    kb = jnp.concatenate([jnp.pad(ks, prev)[:, :, :-1], ks], axis=3)
    vb = jnp.concatenate([jnp.pad(vs, prev)[:, :, :-1], vs], axis=3)
    qi = jnp.arange(ATT_BLOCK)[:, None]
    kc = jnp.arange(2 * ATT_BLOCK)[None, :]
    dist = qi + ATT_BLOCK - kc
    rel = (dist >= 0) & (dist <= w_sub)
    has_prev = (jnp.arange(n_blk) > 0)[:, None, None] | (kc >= ATT_BLOCK)[None]
    mask = rel[None] & has_prev
    s = jnp.einsum('bdnqhe,bdnkhe->bdnhqk', qs, kb).astype(jnp.float32) * (e ** -0.5)
    s = jnp.where(mask[None, None, :, None], s, NEG_INF)
    lse = jax.nn.logsumexp(s, axis=-1)
    p = jnp.exp(s - lse[..., None]).astype(v.dtype)
    o = jnp.einsum('bdnhqk,bdnkhe->bdnqhe', p, vb)
    o = o.reshape(b, dil, t_pad // dil, h, e).transpose(0, 2, 1, 3, 4).reshape(b, t_pad, h, e)[:, :t]
    lse = lse.transpose(0, 1, 2, 4, 3).reshape(b, dil, t_pad // dil, h)
    lse = lse.transpose(0, 2, 1, 3).reshape(b, t_pad, h)[:, :t]
    return o, lse


def dilated_branch_sample(q, k_all, v_all, past_rows, window, dil):
    s_len, e = q.shape[1], q.shape[-1]
    n_keys = window // dil + 1
    idx = past_rows + jnp.arange(s_len)[:, None] - dil * jnp.arange(n_keys)[None, :]
    valid = idx >= 0
    idx = jnp.maximum(idx, 0)
    kg = k_all[:, idx]
    vg = v_all[:, idx]
    s = jnp.einsum('bshe,bsjhe->bshj', q, kg).astype(jnp.float32) * (e ** -0.5)
    s = jnp.where(valid[None, :, None, :], s, NEG_INF)
    lse = jax.nn.logsumexp(s, axis=-1)
    p = jnp.exp(s - lse[..., None]).astype(v_all.dtype)
    o = jnp.einsum('bshj,bsjhe->bshe', p, vg)
    return o, lse


def combine_branches(branches):
    lse = jnp.stack([l for _, l in branches], axis=0)
    wts = jax.nn.softmax(lse, axis=0)
    o = sum(wts[i][..., None] * branches[i][0].astype(jnp.float32) for i in range(len(branches)))
    return o.astype(branches[0][0].dtype)


def hgrn2_chunked(q, k, v, logf, s0):
    b, t, h, dk = q.shape
    dv = v.shape[-1]
    t_pad = -(-t // HG_CHUNK) * HG_CHUNK
    n_ch = t_pad // HG_CHUNK

    def chunks(a):
        a = jnp.pad(a.astype(jnp.float32), ((0, 0), (0, t_pad - t), (0, 0), (0, 0)))
        return a.reshape(b, n_ch, HG_CHUNK, h, a.shape[-1])

    qc, kc, vc, gc = chunks(q), chunks(k), chunks(v), chunks(logf)
    cum = jnp.cumsum(gc, axis=2)
    cum_end = cum[:, :, -1:]
    q_dec = qc * jnp.exp(cum)
    k_dec = kc * jnp.exp(-cum)
    k_end = kc * jnp.exp(cum_end - cum)
    causal = jnp.tril(jnp.ones((HG_CHUNK, HG_CHUNK), dtype=bool))
    a = jnp.einsum('bnihk,bnjhk->bnhij', q_dec, k_dec)
    a = jnp.where(causal, a, 0.0)
    o_intra = jnp.einsum('bnhij,bnjhv->bnihv', a, vc)
    u = jnp.einsum('bnjhk,bnjhv->bnhkv', k_end, vc)
    decay = jnp.exp(cum_end[:, :, 0])

    def step(state, inp):
        dec, upd = inp
        return dec[..., None] * state + upd, state

    s_final, s_start = lax.scan(step, s0.astype(jnp.float32),
                                (decay.transpose(1, 0, 2, 3), u.transpose(1, 0, 2, 3, 4)))
    s_start = s_start.transpose(1, 0, 2, 3, 4)
    o_inter = jnp.einsum('bnihk,bnhkv->bnihv', q_dec, s_start)
    o = (o_intra + o_inter).reshape(b, t_pad, h, dv)[:, :t]
    return o.astype(v.dtype), s_final.astype(s0.dtype)


def mix_out(o_att, o_hg, gh, hg_norm_g, w_out):
    b, t = o_att.shape[:2]
    o_hg = head_rms(o_hg).reshape(b, t, HG_WIDTH) * hg_norm_g * jax.nn.silu(gh)
    cat = jnp.concatenate([o_att.reshape(b, t, ATT_WIDTH), o_hg.astype(o_att.dtype)], axis=-1)
    return cat @ w_out


def setup_inputs(seed: int = 0) -> dict:
    key = jax.random.key(seed)
    ks = jax.random.split(key, 24)
    win_buf = min(ATT_WINDOW_MAX, PAST_LEN)

    def nrm(k, shape, scale):
        return jax.random.normal(k, shape, jnp.float32) * scale

    col_scale = jnp.ones((IN_COLS,), jnp.float32)
    col_scale = col_scale.at[2 * ATT_WIDTH:3 * ATT_WIDTH].set(DN_BETA)
    col_scale = col_scale.at[3 * ATT_WIDTH + 2 * HG_WIDTH:3 * ATT_WIDTH + 3 * HG_WIDTH].set(DN_BETA)
    return {
        "x_prompt": nrm(ks[0], (BATCH, SEQ, D_MODEL), 1.0),
        "x_sample": nrm(ks[1], (DEC_BATCH, DEC_SEQ, D_MODEL), 1.0),
        "cache_k": nrm(ks[2], (DEPTH, DEC_BATCH, win_buf, ATT_HEADS, ATT_HEAD_DIM), 1.0),
        "cache_v": nrm(ks[3], (DEPTH, DEC_BATCH, win_buf, ATT_HEADS, ATT_HEAD_DIM), DN_BETA),
        "state_hgrn": nrm(ks[4], (DEPTH, DEC_BATCH, HG_HEADS, HG_DK, HG_DV), 0.5),
        "ffn1_w_gate": nrm(ks[5], (DEPTH, D_MODEL, D_FF), D_MODEL ** -0.5),
        "ffn1_w_up": nrm(ks[6], (DEPTH, D_MODEL, D_FF), D_MODEL ** -0.5),
        "ffn1_w_down": nrm(ks[7], (DEPTH, D_FF, D_MODEL), DN_BETA * D_FF ** -0.5),
        "ln1_g": 1.0 + nrm(ks[8], (DEPTH, D_MODEL), 0.1),
        "ln1_b": nrm(ks[9], (DEPTH, D_MODEL), 0.02),
        "w_in": nrm(ks[10], (DEPTH, D_MODEL, IN_COLS), D_MODEL ** -0.5) * col_scale,
        "hg_lower_bound": nrm(ks[11], (DEPTH + 1, HG_WIDTH), 0.1),
        "hg_norm_g": 1.0 + nrm(ks[12], (DEPTH, HG_WIDTH), 0.1),
        "w_out": nrm(ks[13], (DEPTH, MIX_WIDTH, D_MODEL), DN_BETA * MIX_WIDTH ** -0.5),
        "ln2_g": 1.0 + nrm(ks[14], (DEPTH, D_MODEL), 0.1),
        "ln2_b": nrm(ks[15], (DEPTH, D_MODEL), 0.02),
        "ffn2_w_gate": nrm(ks[16], (DEPTH, D_MODEL, D_FF), D_MODEL ** -0.5),
        "ffn2_w_up": nrm(ks[17], (DEPTH, D_MODEL, D_FF), D_MODEL ** -0.5),
        "ffn2_w_down": nrm(ks[18], (DEPTH, D_FF, D_MODEL), DN_BETA * D_FF ** -0.5),
        "ln3_g": 1.0 + nrm(ks[19], (DEPTH, D_MODEL), 0.1),
        "ln3_b": nrm(ks[20], (DEPTH, D_MODEL), 0.02),
    }


def reference(x_prompt, x_sample, cache_k, cache_v, state_hgrn,
              ffn1_w_gate, ffn1_w_up, ffn1_w_down, ln1_g, ln1_b,
              w_in, hg_lower_bound, hg_norm_g, w_out, ln2_g, ln2_b,
              ffn2_w_gate, ffn2_w_up, ffn2_w_down, ln3_g, ln3_b):
    lower_bounds = jnp.cumsum(jax.nn.softmax(hg_lower_bound.astype(jnp.float32), axis=0), axis=0)
    t_p = x_prompt.shape[1]
    t_s = x_sample.shape[1]
    past_rows = cache_k.shape[2]
    win_p = min(ATT_WINDOW_MAX, t_p)
    pos_p = jnp.arange(t_p, dtype=jnp.int32)
    pos_s = PAST_LEN + jnp.arange(t_s, dtype=jnp.int32)
    xp, xs = x_prompt, x_sample
    kp_l, vp_l, sp_l, ks_l, vs_l, ss_l = [], [], [], [], [], []
    for l in range(DEPTH):
        xp = layer_norm(DN_ALPHA * xp + 0.5 * swiglu(xp, ffn1_w_gate[l], ffn1_w_up[l], ffn1_w_down[l]),
                        ln1_g[l], ln1_b[l])
        xs = layer_norm(DN_ALPHA * xs + 0.5 * swiglu(xs, ffn1_w_gate[l], ffn1_w_up[l], ffn1_w_down[l]),
                        ln1_g[l], ln1_b[l])

        (qa, ka, va), hg, gh = project(xp, w_in[l], lower_bounds[l], pos_p)
        o_att = combine_branches([dilated_branch_prompt(qa, ka, va, w, d) for (w, d) in DILATIONS])
        s0 = jnp.zeros((xp.shape[0], HG_HEADS, HG_DK, HG_DV), state_hgrn.dtype)
        o_hg, s_p = hgrn2_chunked(hg[0], hg[1], hg[2], hg[3], s0)
        mix_p = mix_out(o_att, o_hg, gh, hg_norm_g[l], w_out[l])
        kp_l.append(ka[:, t_p - win_p:])
        vp_l.append(va[:, t_p - win_p:])
        sp_l.append(s_p)

        (qa_s, ka_s, va_s), hg_s, gh_s = project(xs, w_in[l], lower_bounds[l], pos_s)
        k_all = jnp.concatenate([cache_k[l].astype(ka_s.dtype), ka_s], axis=1)
        v_all = jnp.concatenate([cache_v[l].astype(va_s.dtype), va_s], axis=1)
        o_att_s = combine_branches([dilated_branch_sample(qa_s, k_all, v_all, past_rows, w, d)
                                    for (w, d) in DILATIONS])
        o_hg_s, s_s = hgrn2_chunked(hg_s[0], hg_s[1], hg_s[2], hg_s[3], state_hgrn[l])
        mix_s = mix_out(o_att_s, o_hg_s, gh_s, hg_norm_g[l], w_out[l])
        ks_l.append(k_all[:, -past_rows:])
        vs_l.append(v_all[:, -past_rows:])
        ss_l.append(s_s)

        xp = layer_norm(DN_ALPHA * xp + mix_p, ln2_g[l], ln2_b[l])
        xs = layer_norm(DN_ALPHA * xs + mix_s, ln2_g[l], ln2_b[l])

        xp = layer_norm(DN_ALPHA * xp + 0.5 * swiglu(xp, ffn2_w_gate[l], ffn2_w_up[l], ffn2_w_down[l]),
                        ln3_g[l], ln3_b[l])
        xs = layer_norm(DN_ALPHA * xs + 0.5 * swiglu(xs, ffn2_w_gate[l], ffn2_w_up[l], ffn2_w_down[l]),
                        ln3_g[l], ln3_b[l])
    return (xp, xs, jnp.stack(kp_l), jnp.stack(vp_l), jnp.stack(sp_l),
            jnp.stack(ks_l), jnp.stack(vs_l), jnp.stack(ss_l))
```

```python
import functools

import jax
import jax.numpy as jnp
from jax import lax
from jax.experimental import pallas as pl
from jax.experimental.pallas import tpu as pltpu

F32 = jnp.float32
BF16 = jnp.bfloat16

D_MODEL = 1024
ATT_WIDTH = 512
HG_WIDTH = 512
ATT_HEAD_DIM = 64
ATT_HEADS = 8
DILATIONS = ((128, 1), (512, 4), (2048, 16))
ATT_WINDOW_MAX = 2048
ATT_BLOCK = 128
ROPE_THETA = 10000.0
HG_HEADS = 4
HG_DK = 128
HG_DV = 128
HG_CHUNK = 32
D_FF = 2816
IN_COLS = 3 * ATT_WIDTH + 4 * HG_WIDTH
PAST_LEN = 8192
DN_ALPHA = 2.0 ** 0.25
LN_EPS = 1e-5
NEG_INF = -1e30

LANES = 128
ROW_TILE = 512
VMEM_LIMIT = 56 * 1024 * 1024


def _const_spec(shape):
    return pl.BlockSpec(shape, lambda *_: (0,) * len(shape), pipeline_mode=pl.Buffered(1))


def _layer_norm(r, g, b):
    mu = jnp.mean(r, axis=-1, keepdims=True)
    c = r - mu
    var = jnp.mean(c * c, axis=-1, keepdims=True)
    return c * lax.rsqrt(var + LN_EPS) * g + b


def _sigmoid(x):
    return 1.0 / (1.0 + jnp.exp(-x))


def _split3(x):
    hi = x.astype(BF16)
    r1 = x - hi.astype(F32)
    mid = r1.astype(BF16)
    lo = (r1 - mid.astype(F32)).astype(BF16)
    return hi, mid, lo


def _ffn_ln_body(x_ref, wg_ref, wu_ref, wd_ref, g_ref, b_ref, o_ref):
    x = x_ref[...]
    xb = x.astype(BF16)
    gate = jnp.dot(xb, wg_ref[...], preferred_element_type=F32)
    up = jnp.dot(xb, wu_ref[...], preferred_element_type=F32)
    h = (gate * _sigmoid(gate) * up).astype(BF16)
    y = jnp.dot(h, wd_ref[...], preferred_element_type=F32)
    o_ref[...] = _layer_norm(DN_ALPHA * x + 0.5 * y, g_ref[...], b_ref[...])


def ffn_ln(x, wg, wu, wd, g, b):
    m = x.shape[0]
    row = pl.BlockSpec((ROW_TILE, D_MODEL), lambda i: (i, 0))
    return pl.pallas_call(
        _ffn_ln_body,
        grid=(m // ROW_TILE,),
        in_specs=[row, _const_spec((D_MODEL, D_FF)), _const_spec((D_MODEL, D_FF)),
                  _const_spec((D_FF, D_MODEL)), _const_spec((1, D_MODEL)), _const_spec((1, D_MODEL))],
        out_specs=row,
        out_shape=jax.ShapeDtypeStruct((m, D_MODEL), F32),
        compiler_params=pltpu.CompilerParams(dimension_semantics=("arbitrary",),
                                             vmem_limit_bytes=VMEM_LIMIT),
        name="ffn_ln",
    )(x, wg, wu, wd, g, b)


def _rope(z, cos_t, sin_t):
    lane = lax.broadcasted_iota(jnp.int32, (1, LANES), 1)
    first_half = (lane % ATT_HEAD_DIM) < (ATT_HEAD_DIM // 2)
    outs = []
    for c in range(z.shape[1] // LANES):
        zc = z[:, c * LANES:(c + 1) * LANES]
        partner = jnp.where(first_half, pltpu.roll(zc, LANES - 32, 1), pltpu.roll(zc, 32, 1))
        outs.append(zc * cos_t + partner * sin_t)
    return outs


def _proj_body(x_ref, w_ref, cos_ref, sin_ref, lbraw_ref,
               q_ref, k_ref, v_ref, qh_ref, kh_ref, ih_ref, lf_ref, gh_ref):
    xb = x_ref[...].astype(BF16)
    z = jnp.dot(xb, w_ref[...], preferred_element_type=F32)
    cos_t = cos_ref[...]
    sin_t = sin_ref[...]
    aw = ATT_WIDTH
    for c, val in enumerate(_rope(z[:, 0:aw], cos_t, sin_t)):
        q_ref[:, c * LANES:(c + 1) * LANES] = val * (ATT_HEAD_DIM ** -0.5)
    for c, val in enumerate(_rope(z[:, aw:2 * aw], cos_t, sin_t)):
        k_ref[:, c * LANES:(c + 1) * LANES] = val
    v_ref[...] = z[:, 2 * aw:3 * aw]
    base = 3 * aw
    zq = z[:, base:base + HG_WIDTH]
    zf = z[:, base + HG_WIDTH:base + 2 * HG_WIDTH]
    lbr = lbraw_ref[...]
    mx = jnp.max(lbr, axis=0, keepdims=True)
    ex = jnp.exp(lbr - mx)
    lb = ex[0:1, :] / jnp.sum(ex, axis=0, keepdims=True)
    sg = _sigmoid(zf)
    qh_ref[...] = zq * _sigmoid(zq)
    lf_ref[...] = jnp.log(lb + (1.0 - lb) * sg)
    kh_ref[...] = (1.0 - lb) * _sigmoid(-zf)
    ih_ref[...] = z[:, base + 2 * HG_WIDTH:base + 3 * HG_WIDTH]
    gh_ref[...] = z[:, base + 3 * HG_WIDTH:base + 4 * HG_WIDTH]


def proj(x, w_in, cos_t, sin_t, lb_raw, table_tiles):
    m = x.shape[0]
    row = pl.BlockSpec((ROW_TILE, D_MODEL), lambda i: (i, 0))
    tab = pl.BlockSpec((ROW_TILE, LANES), lambda i: (i % table_tiles, 0))
    out = pl.BlockSpec((ROW_TILE, ATT_WIDTH), lambda i: (i, 0))
    sds = jax.ShapeDtypeStruct((m, ATT_WIDTH), F32)
    return pl.pallas_call(
        _proj_body,
        grid=(m // ROW_TILE,),
        in_specs=[row, _const_spec((D_MODEL, IN_COLS)), tab, tab, _const_spec((2, HG_WIDTH))],
        out_specs=[out] * 8,
        out_shape=[sds] * 8,
        compiler_params=pltpu.CompilerParams(dimension_semantics=("arbitrary",),
                                             vmem_limit_bytes=VMEM_LIMIT),
        name="proj",
    )(x, w_in, cos_t, sin_t, lb_raw)


def _mix_ln_body(x_ref, oa_ref, oh_ref, wa_ref, wh_ref, g_ref, b_ref, o_ref):
    mix = jnp.dot(oa_ref[...].astype(BF16), wa_ref[...], preferred_element_type=F32)
    mix = mix + jnp.dot(oh_ref[...].astype(BF16), wh_ref[...], preferred_element_type=F32)
    o_ref[...] = _layer_norm(DN_ALPHA * x_ref[...] + mix, g_ref[...], b_ref[...])


def mix_ln(x, o_att, o_hg, w_att, w_hg, g, b):
    m = x.shape[0]
    row = pl.BlockSpec((ROW_TILE, D_MODEL), lambda i: (i, 0))
    half = pl.BlockSpec((ROW_TILE, ATT_WIDTH), lambda i: (i, 0))
    return pl.pallas_call(
        _mix_ln_body,
        grid=(m // ROW_TILE,),
        in_specs=[row, half, half, _const_spec((ATT_WIDTH, D_MODEL)), _const_spec((HG_WIDTH, D_MODEL)),
                  _const_spec((1, D_MODEL)), _const_spec((1, D_MODEL))],
        out_specs=row,
        out_shape=jax.ShapeDtypeStruct((m, D_MODEL), F32),
        compiler_params=pltpu.CompilerParams(dimension_semantics=("arbitrary",),
                                             vmem_limit_bytes=VMEM_LIMIT),
        name="mix_ln",
    )(x, o_att, o_hg, w_att, w_hg, g, b)


def _attn_prompt_body(q_ref, k_ref, v_ref, o_ref, kt_ref, vt_ref, ob_ref, lse_ref, *, seq):
    blk = ATT_BLOCK
    lane = lax.broadcasted_iota(jnp.int32, (1, LANES), 1)
    head0 = lane < ATT_HEAD_DIM
    qi = lax.broadcasted_iota(jnp.int32, (blk, 2 * blk), 0)
    kc = lax.broadcasted_iota(jnp.int32, (blk, 2 * blk), 1)
    dist = qi + blk - kc
    nt = (((1,), (1,)), ((), ()))

    for br, (window, dil) in enumerate(DILATIONS):
        w_sub = window // dil
        rel = (dist >= 0) & (dist <= w_sub)
        n_units = seq // blk

        def unit(u, carry, br=br, dil=dil, rel=rel):
            r = u % dil
            n = u // dil
            start = r + n * (blk * dil)
            prev = jnp.maximum(start - blk * dil, 0)
            mask = rel & (kc >= jnp.where(n > 0, 0, blk))
            q = q_ref[pl.ds(start, blk, stride=dil), :]
            kb = jnp.concatenate([k_ref[pl.ds(prev, blk, stride=dil), :],
                                  k_ref[pl.ds(start, blk, stride=dil), :]], axis=0).astype(BF16)
            vb = jnp.concatenate([v_ref[pl.ds(prev, blk, stride=dil), :],
                                  v_ref[pl.ds(start, blk, stride=dil), :]], axis=0).astype(BF16)
            outs, lses = [], []
            for hsel in (head0, ~head0):
                qh = jnp.where(hsel, q, 0.0).astype(BF16)
                s = lax.dot_general(qh, kb, nt, preferred_element_type=F32)
                s = jnp.where(mask, s, NEG_INF)
                m = jnp.max(s, axis=-1, keepdims=True)
                p = jnp.exp(s - m)
                l = jnp.sum(p, axis=-1, keepdims=True)
                o = jnp.dot(p.astype(BF16), vb, preferred_element_type=F32)
                outs.append(o / l)
                lses.append(m + jnp.log(l))
            o2 = jnp.where(head0, outs[0], outs[1])
            l2 = jnp.where(head0, lses[0], lses[1])
            ob_ref[br, pl.ds(start, blk, stride=dil), :] = o2
            lse_ref[br, pl.ds(start, blk, stride=dil), :] = jnp.broadcast_to(l2, (blk, LANES))
            return carry

        lax.fori_loop(0, n_units, unit, 0)

    def combine(c, carry):
        rows = pl.ds(pl.multiple_of(c * blk, blk), blk)
        l0, l1, l2 = lse_ref[0, rows, :], lse_ref[1, rows, :], lse_ref[2, rows, :]
        mx = jnp.maximum(jnp.maximum(l0, l1), l2)
        w0, w1, w2 = jnp.exp(l0 - mx), jnp.exp(l1 - mx), jnp.exp(l2 - mx)
        num = w0 * ob_ref[0, rows, :] + w1 * ob_ref[1, rows, :] + w2 * ob_ref[2, rows, :]
        o_ref[rows, :] = num / (w0 + w1 + w2)
        return carry

    lax.fori_loop(0, seq // blk, combine, 0)

    win = kt_ref.shape[-1]
    for j in range(win // LANES):
        rows = pl.ds(seq - win + j * LANES, LANES)
        kt_ref[0, :, :, j * LANES:(j + 1) * LANES] = k_ref[rows, :].T.reshape(2, ATT_HEAD_DIM, LANES)
        vt_ref[0, :, :, j * LANES:(j + 1) * LANES] = v_ref[rows, :].T.reshape(2, ATT_HEAD_DIM, LANES)


def attn_prompt(q, k, v, batch, seq):
    win = min(ATT_WINDOW_MAX, seq)
    n_hp = ATT_WIDTH // LANES
    blk = pl.BlockSpec((seq, LANES), lambda b, h: (b, h))
    tblk = pl.BlockSpec((1, 2, ATT_HEAD_DIM, win), lambda b, h: (b, h, 0, 0))
    tsds = jax.ShapeDtypeStruct((batch, ATT_HEADS, ATT_HEAD_DIM, win), F32)
    return pl.pallas_call(
        functools.partial(_attn_prompt_body, seq=seq),
        grid=(batch, n_hp),
        in_specs=[blk, blk, blk],
        out_specs=[blk, tblk, tblk],
        out_shape=[jax.ShapeDtypeStruct((batch * seq, ATT_WIDTH), F32), tsds, tsds],
        scratch_shapes=[pltpu.VMEM((3, seq, LANES), F32), pltpu.VMEM((3, seq, LANES), F32)],
        compiler_params=pltpu.CompilerParams(dimension_semantics=("arbitrary", "arbitrary"),
                                             vmem_limit_bytes=VMEM_LIMIT),
        name="attn_prompt",
    )(q, k, v)


def _hgrn_prompt_body(q_ref, k_ref, v_ref, lf_ref, gh_ref, ng_ref, o_ref, s_ref, *, seq):
    grp = LANES
    n_ch = grp // HG_CHUNK
    ri = lax.broadcasted_iota(jnp.int32, (grp, grp), 0)
    ci = lax.broadcasted_iota(jnp.int32, (grp, grp), 1)
    same_chunk = (ri // HG_CHUNK) == (ci // HG_CHUNK)
    causal = same_chunk & (ci <= ri)
    summat = jnp.concatenate([jnp.where(causal, 1.0, 0.0), jnp.where(same_chunk, 1.0, 0.0)],
                             axis=0).astype(BF16)
    row1 = lax.broadcasted_iota(jnp.int32, (grp, 1), 0)
    nt = (((1,), (1,)), ((), ()))
    ng = ng_ref[...]

    def group(gi, st):
        rows = pl.ds(pl.multiple_of(gi * grp, grp), grp)
        q, k, v, g = q_ref[rows, :], k_ref[rows, :], v_ref[rows, :], lf_ref[rows, :]
        g_hi, g_mid, g_lo = _split3(g)
        sums = (jnp.dot(summat, g_hi, preferred_element_type=F32)
                + jnp.dot(summat, g_mid, preferred_element_type=F32)
                + jnp.dot(summat, g_lo, preferred_element_type=F32))
        cum = sums[0:grp]
        cum_end = sums[grp:2 * grp]
        q_dec = (q * jnp.exp(cum)).astype(BF16)
        k_dec = (k * jnp.exp(-cum)).astype(BF16)
        k_end = k * jnp.exp(cum_end - cum)
        vb = v.astype(BF16)
        a = lax.dot_general(q_dec, k_dec, nt, preferred_element_type=F32)
        a = jnp.where(causal, a, 0.0).astype(BF16)
        o = jnp.dot(a, vb, preferred_element_type=F32)
        vt = v.T.astype(BF16)
        inter = []
        for c in range(n_ch):
            lo = c * HG_CHUNK
            inter.append(lax.dot_general(q_dec[lo:lo + HG_CHUNK], st.astype(BF16), nt,
                                         preferred_element_type=F32))
            in_chunk = (row1 >= lo) & (row1 < lo + HG_CHUNK)
            ut = jnp.dot(vt, jnp.where(in_chunk, k_end, 0.0).astype(BF16),
                         preferred_element_type=F32)
            st = st * jnp.exp(cum_end[lo:lo + 1, :]) + ut
        o = o + jnp.concatenate(inter, axis=0)
        ms = jnp.mean(o * o, axis=-1, keepdims=True)
        gate = gh_ref[rows, :]
        o_ref[rows, :] = o * lax.rsqrt(ms + LN_EPS) * ng * (gate * _sigmoid(gate))
        return st

    st = lax.fori_loop(0, seq // grp, group, jnp.zeros((HG_DV, HG_DK), F32))
    s_ref[0, 0] = st.T


def hgrn_prompt(qh, kh, ih, lf, gh, norm_g, batch, seq):
    blk = pl.BlockSpec((seq, LANES), lambda b, h: (b, h))
    return pl.pallas_call(
        functools.partial(_hgrn_prompt_body, seq=seq),
        grid=(batch, HG_HEADS),
        in_specs=[blk, blk, blk, blk, blk, pl.BlockSpec((1, LANES), lambda b, h: (0, h))],
        out_specs=[blk, pl.BlockSpec((1, 1, HG_DK, HG_DV), lambda b, h: (b, h, 0, 0))],
        out_shape=[jax.ShapeDtypeStruct((batch * seq, HG_WIDTH), F32),
                   jax.ShapeDtypeStruct((batch, HG_HEADS, HG_DK, HG_DV), F32)],
        compiler_params=pltpu.CompilerParams(dimension_semantics=("arbitrary", "arbitrary"),
                                             vmem_limit_bytes=VMEM_LIMIT),
        name="hgrn_prompt",
    )(qh, kh, ih, lf, gh, norm_g)


def _attn_sample_body(q_ref, kn_ref, knt_ref, vnt_ref, kc_ref, vc_ref, o_ref, nk_ref, nv_ref,
                      *, n_new, past):
    srow = lax.broadcasted_iota(jnp.int32, (n_new, past), 0)
    tcol = lax.broadcasted_iota(jnp.int32, (n_new, past), 1)
    dist = past + srow - tcol
    cnt = jnp.zeros((n_new, past), F32)
    for window, dil in DILATIONS:
        cnt = cnt + jnp.where((dist % dil == 0) & (dist <= window), 1.0, 0.0)
    si = lax.broadcasted_iota(jnp.int32, (n_new, n_new), 0)
    sj = lax.broadcasted_iota(jnp.int32, (n_new, n_new), 1)
    dn = si - sj
    cnt_new = jnp.zeros((n_new, n_new), F32)
    for window, dil in DILATIONS:
        cnt_new = cnt_new + jnp.where((dn >= 0) & (dn % dil == 0) & (dn <= window), 1.0, 0.0)
    lane = lax.broadcasted_iota(jnp.int32, (1, LANES), 1)
    ones = jnp.ones((8, past), BF16)
    ones_new = jnp.ones((8, n_new), BF16)
    nt = (((1,), (1,)), ((), ()))
    shift = past - n_new

    def head(h, carry):
        kt = kc_ref[0, h]
        vt = vc_ref[0, h]
        q = q_ref[0, h].astype(BF16)
        s_c = jnp.dot(q, kt.astype(BF16), preferred_element_type=F32)
        s_n = lax.dot_general(q, kn_ref[0, h].astype(BF16), nt, preferred_element_type=F32)
        s_c = jnp.where(cnt > 0, s_c, NEG_INF)
        s_n = jnp.where(cnt_new > 0, s_n, NEG_INF)
        m = jnp.maximum(jnp.max(s_c, axis=-1, keepdims=True), jnp.max(s_n, axis=-1, keepdims=True))
        p_c = (cnt * jnp.exp(s_c - m)).astype(BF16)
        p_n = (cnt_new * jnp.exp(s_n - m)).astype(BF16)
        vnt = vnt_ref[0, h]
        num = (lax.dot_general(vt.astype(BF16), p_c, nt, preferred_element_type=F32)
               + lax.dot_general(vnt.astype(BF16), p_n, nt, preferred_element_type=F32))
        den = (lax.dot_general(ones, p_c, nt, preferred_element_type=F32)
               + lax.dot_general(ones_new, p_n, nt, preferred_element_type=F32))
        o_ref[0, h] = num / den[0:1, :]
        knt = knt_ref[0, h]
        for src, new, dst in ((kt, knt, nk_ref), (vt, vnt, nv_ref)):
            rolled = pltpu.roll(src, shift, 1)
            dst[0, h, :, 0:past - LANES] = rolled[:, 0:past - LANES]
            last = rolled[:, past - LANES:past]
            for s in range(n_new):
                col = jnp.broadcast_to(new[:, s:s + 1], last.shape)
                last = jnp.where(lane == LANES - n_new + s, col, last)
            dst[0, h, :, past - LANES:past] = last
        return carry

    lax.fori_loop(0, ATT_HEADS, head, 0)


def attn_sample(q, kn, knt, vnt, kc, vc):
    bsz, _, n_new, _ = q.shape
    past = kc.shape[-1]
    small = pl.BlockSpec((1, ATT_HEADS, n_new, ATT_HEAD_DIM), lambda b: (b, 0, 0, 0))
    small_t = pl.BlockSpec((1, ATT_HEADS, ATT_HEAD_DIM, n_new), lambda b: (b, 0, 0, 0))
    big = pl.BlockSpec((1, ATT_HEADS, ATT_HEAD_DIM, past), lambda b: (b, 0, 0, 0))
    big_sds = jax.ShapeDtypeStruct(kc.shape, F32)
    return pl.pallas_call(
        functools.partial(_attn_sample_body, n_new=n_new, past=past),
        grid=(bsz,),
        in_specs=[small, small, small_t, small_t, big, big],
        out_specs=[small_t, big, big],
        out_shape=[jax.ShapeDtypeStruct((bsz, ATT_HEADS, ATT_HEAD_DIM, n_new), F32), big_sds, big_sds],
        compiler_params=pltpu.CompilerParams(dimension_semantics=("arbitrary",),
                                             vmem_limit_bytes=VMEM_LIMIT),
        name="attn_sample",
    )(q, kn, knt, vnt, kc, vc)


HS_BATCH = 8


def _hgrn_sample_body(q_ref, k_ref, v_ref, lf_ref, gh_ref, ng_ref, s0_ref, o_ref, s_ref, *, n_new):
    row = lax.broadcasted_iota(jnp.int32, (n_new, 1), 0)
    tn = (((0,), (0,)), ((), ()))
    ones12 = jnp.ones((3 * n_new, HG_DV), BF16)

    def one_batch(b, carry):
        for h in range(HG_HEADS):
            cols = slice(h * LANES, (h + 1) * LANES)
            q, k, v, g = q_ref[b, :, cols], k_ref[b, :, cols], v_ref[b, :, cols], lf_ref[b, :, cols]
            cum = jnp.zeros((n_new, LANES), F32)
            for j in range(n_new):
                cum = cum + jnp.where(row >= j, g[j:j + 1, :], 0.0)
            cum_end = cum[n_new - 1:n_new, :]
            q_dec = q * jnp.exp(cum)
            k_dec = k * jnp.exp(-cum)
            k_end = k * jnp.exp(cum_end - cum)
            s0 = s0_ref[b, h]
            o = jnp.dot(q_dec.astype(BF16), s0.astype(BF16), preferred_element_type=F32)
            for j in range(n_new):
                a_j = jnp.sum(q_dec * k_dec[j:j + 1, :], axis=-1, keepdims=True)
                o = o + jnp.where(row >= j, a_j, 0.0) * v[j:j + 1, :]
            ms = jnp.mean(o * o, axis=-1, keepdims=True)
            gate = gh_ref[b, :, cols]
            o_ref[b, :, cols] = o * lax.rsqrt(ms + LN_EPS) * ng_ref[:, cols] * (gate * _sigmoid(gate))
            dec = lax.dot_general(jnp.concatenate(_split3(g), axis=0), ones12, tn,
                                  preferred_element_type=F32)
            u = lax.dot_general(k_end.astype(BF16), v.astype(BF16), tn, preferred_element_type=F32)
            s_ref[b, h] = jnp.exp(dec) * s0 + u
        return carry

    lax.fori_loop(0, HS_BATCH, one_batch, 0)


def hgrn_sample(qh, kh, ih, lf, gh, norm_g, s0):
    bsz, n_new, _ = qh.shape
    blk = pl.BlockSpec((HS_BATCH, n_new, HG_WIDTH), lambda i: (i, 0, 0))
    sblk = pl.BlockSpec((HS_BATCH, HG_HEADS, HG_DK, HG_DV), lambda i: (i, 0, 0, 0))
    return pl.pallas_call(
        functools.partial(_hgrn_sample_body, n_new=n_new),
        grid=(bsz // HS_BATCH,),
        in_specs=[blk, blk, blk, blk, blk, _const_spec((1, HG_WIDTH)), sblk],
        out_specs=[blk, sblk],
        out_shape=[jax.ShapeDtypeStruct((bsz, n_new, HG_WIDTH), F32),
                   jax.ShapeDtypeStruct(s0.shape, F32)],
        compiler_params=pltpu.CompilerParams(dimension_semantics=("arbitrary",),
                                             vmem_limit_bytes=VMEM_LIMIT),
        name="hgrn_sample",
    )(qh, kh, ih, lf, gh, norm_g, s0)


def _rope_tables(pos):
    half = ATT_HEAD_DIM // 2
    inv = ROPE_THETA ** (-jnp.arange(half, dtype=F32) / half)
    ang = pos.astype(F32)[:, None] * inv[None, :]
    cos, sin = jnp.cos(ang), jnp.sin(ang)
    return jnp.tile(cos, (1, 4)), jnp.tile(jnp.concatenate([-sin, sin], axis=1), (1, 2))


def kernel(x_prompt, x_sample, cache_k, cache_v, state_hgrn, ffn1_w_gate, ffn1_w_up, ffn1_w_down, ln1_g, ln1_b, w_in, hg_lower_bound, hg_norm_g, w_out, ln2_g, ln2_b, ffn2_w_gate, ffn2_w_up, ffn2_w_down, ln3_g, ln3_b):
    batch, seq, _ = x_prompt.shape
    dec_batch, n_new, _ = x_sample.shape
    assert ffn1_w_gate.shape[0] == 1, "single layer"
    win_p = min(ATT_WINDOW_MAX, seq)

    w1 = (ffn1_w_gate[0].astype(BF16), ffn1_w_up[0].astype(BF16), ffn1_w_down[0].astype(BF16))
    w2 = (ffn2_w_gate[0].astype(BF16), ffn2_w_up[0].astype(BF16), ffn2_w_down[0].astype(BF16))
    w_in_b = w_in[0].astype(BF16)
    w_att = w_out[0, :ATT_WIDTH].astype(BF16)
    w_hg = w_out[0, ATT_WIDTH:].astype(BF16)

    cos_p, sin_p = _rope_tables(jnp.arange(seq, dtype=jnp.int32))
    pos_s = PAST_LEN + (jnp.arange(dec_batch * n_new, dtype=jnp.int32) % n_new)
    cos_s, sin_s = _rope_tables(pos_s)

    xp = x_prompt.reshape(batch * seq, D_MODEL)
    xs = x_sample.reshape(dec_batch * n_new, D_MODEL)

    xp1 = ffn_ln(xp, *w1, ln1_g, ln1_b)
    q, k, v, qh, kh, ih, lf, gh = proj(xp1, w_in_b, cos_p, sin_p, hg_lower_bound, seq // ROW_TILE)
    o_att, kt_p, vt_p = attn_prompt(q, k, v, batch, seq)
    o_hg, s_p = hgrn_prompt(qh, kh, ih, lf, gh, hg_norm_g, batch, seq)
    xp2 = mix_ln(xp1, o_att, o_hg, w_att, w_hg, ln2_g, ln2_b)
    yp = ffn_ln(xp2, *w2, ln3_g, ln3_b)

    xs1 = ffn_ln(xs, *w1, ln1_g, ln1_b)
    qs, ks, vs, qhs, khs, ihs, lfs, ghs = proj(xs1, w_in_b, cos_s, sin_s, hg_lower_bound, 1)

    def heads(a):
        return a.reshape(dec_batch, n_new, ATT_HEADS, ATT_HEAD_DIM).transpose(0, 2, 1, 3)

    q4, k4, v4 = heads(qs), heads(ks), heads(vs)
    kc = cache_k[0].transpose(0, 2, 3, 1)
    vc = cache_v[0].transpose(0, 2, 3, 1)
    o_t, nk, nv = attn_sample(q4, k4, k4.transpose(0, 1, 3, 2), v4.transpose(0, 1, 3, 2), kc, vc)
    o_att_s = o_t.transpose(0, 3, 1, 2).reshape(dec_batch * n_new, ATT_WIDTH)

    def b3(a):
        return a.reshape(dec_batch, n_new, HG_WIDTH)

    o_hg_s, s_s = hgrn_sample(b3(qhs), b3(khs), b3(ihs), b3(lfs), b3(ghs), hg_norm_g, state_hgrn[0])
    xs2 = mix_ln(xs1, o_att_s, o_hg_s.reshape(dec_batch * n_new, HG_WIDTH), w_att, w_hg, ln2_g, ln2_b)
    ys = ffn_ln(xs2, *w2, ln3_g, ln3_b)

    return (yp.reshape(batch, seq, D_MODEL),
            ys.reshape(dec_batch, n_new, D_MODEL),
            kt_p.transpose(0, 3, 1, 2)[None],
            vt_p.transpose(0, 3, 1, 2)[None],
            s_p[None],
            nk.transpose(0, 3, 1, 2)[None],
            nv.transpose(0, 3, 1, 2)[None],
            s_s[None])
```

```python
import functools

import jax
import jax.numpy as jnp
from jax import lax
from jax.experimental import pallas as pl
from jax.experimental.pallas import tpu as pltpu

F32 = jnp.float32
BF16 = jnp.bfloat16

D_MODEL = 1024
ATT_WIDTH = 512
HG_WIDTH = 512
ATT_HEAD_DIM = 64
ATT_HEADS = 8
DILATIONS = ((128, 1), (512, 4), (2048, 16))
ATT_WINDOW_MAX = 2048
ATT_BLOCK = 128
ROPE_THETA = 10000.0
HG_HEADS = 4
HG_DK = 128
HG_DV = 128
HG_CHUNK = 32
D_FF = 2816
IN_COLS = 3 * ATT_WIDTH + 4 * HG_WIDTH
PAST_LEN = 8192
DN_ALPHA = 2.0 ** 0.25
LN_EPS = 1e-5
NEG_INF = -1e30

LANES = 128
ROW_TILE = 512
VMEM_LIMIT = 56 * 1024 * 1024


def _const_spec(shape):
    return pl.BlockSpec(shape, lambda *_: (0,) * len(shape), pipeline_mode=pl.Buffered(1))


def _layer_norm(r, g, b):
    mu = jnp.mean(r, axis=-1, keepdims=True)
    c = r - mu
    var = jnp.mean(c * c, axis=-1, keepdims=True)
    return c * lax.rsqrt(var + LN_EPS) * g + b


def _sigmoid(x):
    return 1.0 / (1.0 + jnp.exp(-x))


def _split3(x):
    hi = x.astype(BF16)
    r1 = x - hi.astype(F32)
    mid = r1.astype(BF16)
    lo = (r1 - mid.astype(F32)).astype(BF16)
    return hi, mid, lo


def _ffn_ln_body(x_ref, wg_ref, wu_ref, wd_ref, g_ref, b_ref, o_ref):
    x = x_ref[...]
    xb = x.astype(BF16)
    gate = jnp.dot(xb, wg_ref[...], preferred_element_type=F32)
    up = jnp.dot(xb, wu_ref[...], preferred_element_type=F32)
    h = (gate * _sigmoid(gate) * up).astype(BF16)
    y = jnp.dot(h, wd_ref[...], preferred_element_type=F32)
    o_ref[...] = _layer_norm(DN_ALPHA * x + 0.5 * y, g_ref[...], b_ref[...])


def ffn_ln(x, wg, wu, wd, g, b):
    m = x.shape[0]
    row = pl.BlockSpec((ROW_TILE, D_MODEL), lambda i: (i, 0))
    return pl.pallas_call(
        _ffn_ln_body,
        grid=(m // ROW_TILE,),
        in_specs=[row, _const_spec((D_MODEL, D_FF)), _const_spec((D_MODEL, D_FF)),
                  _const_spec((D_FF, D_MODEL)), _const_spec((1, D_MODEL)), _const_spec((1, D_MODEL))],
        out_specs=row,
        out_shape=jax.ShapeDtypeStruct((m, D_MODEL), F32),
        compiler_params=pltpu.CompilerParams(dimension_semantics=("arbitrary",),
                                             vmem_limit_bytes=VMEM_LIMIT),
        name="ffn_ln",
    )(x, wg, wu, wd, g, b)


def _rope(z, cos_t, sin_t):
    lane = lax.broadcasted_iota(jnp.int32, (1, LANES), 1)
    first_half = (lane % ATT_HEAD_DIM) < (ATT_HEAD_DIM // 2)
    outs = []
    for c in range(z.shape[1] // LANES):
        zc = z[:, c * LANES:(c + 1) * LANES]
        partner = jnp.where(first_half, pltpu.roll(zc, LANES - 32, 1), pltpu.roll(zc, 32, 1))
        outs.append(zc * cos_t + partner * sin_t)
    return outs


def _proj_body(x_ref, w_ref, cos_ref, sin_ref, lbraw_ref,
               q_ref, k_ref, v_ref, qh_ref, kh_ref, ih_ref, lf_ref, gh_ref):
    xb = x_ref[...].astype(BF16)
    z = jnp.dot(xb, w_ref[...], preferred_element_type=F32)
    cos_t = cos_ref[...]
    sin_t = sin_ref[...]
    aw = ATT_WIDTH
    for c, val in enumerate(_rope(z[:, 0:aw], cos_t, sin_t)):
        q_ref[:, c * LANES:(c + 1) * LANES] = val * (ATT_HEAD_DIM ** -0.5)
    for c, val in enumerate(_rope(z[:, aw:2 * aw], cos_t, sin_t)):
        k_ref[:, c * LANES:(c + 1) * LANES] = val
    v_ref[...] = z[:, 2 * aw:3 * aw]
    base = 3 * aw
    zq = z[:, base:base + HG_WIDTH]
    zf = z[:, base + HG_WIDTH:base + 2 * HG_WIDTH]
    lbr = lbraw_ref[...]
    mx = jnp.max(lbr, axis=0, keepdims=True)
    ex = jnp.exp(lbr - mx)
    lb = ex[0:1, :] / jnp.sum(ex, axis=0, keepdims=True)
    sg = _sigmoid(zf)
    qh_ref[...] = zq * _sigmoid(zq)
    lf_ref[...] = jnp.log(lb + (1.0 - lb) * sg)
    kh_ref[...] = (1.0 - lb) * _sigmoid(-zf)
    ih_ref[...] = z[:, base + 2 * HG_WIDTH:base + 3 * HG_WIDTH]
    gh_ref[...] = z[:, base + 3 * HG_WIDTH:base + 4 * HG_WIDTH]


def proj(x, w_in, cos_t, sin_t, lb_raw, table_tiles):
    m = x.shape[0]
    row = pl.BlockSpec((ROW_TILE, D_MODEL), lambda i: (i, 0))
    tab = pl.BlockSpec((ROW_TILE, LANES), lambda i: (i % table_tiles, 0))
    out = pl.BlockSpec((ROW_TILE, ATT_WIDTH), lambda i: (i, 0))
    sds = jax.ShapeDtypeStruct((m, ATT_WIDTH), F32)
    return pl.pallas_call(
        _proj_body,
        grid=(m // ROW_TILE,),
        in_specs=[row, _const_spec((D_MODEL, IN_COLS)), tab, tab, _const_spec((2, HG_WIDTH))],
        out_specs=[out] * 8,
        out_shape=[sds] * 8,
        compiler_params=pltpu.CompilerParams(dimension_semantics=("arbitrary",),
                                             vmem_limit_bytes=VMEM_LIMIT),
        name="proj",
    )(x, w_in, cos_t, sin_t, lb_raw)


def _mix_ln_body(x_ref, oa_ref, oh_ref, wa_ref, wh_ref, g_ref, b_ref, o_ref):
    mix = jnp.dot(oa_ref[...].astype(BF16), wa_ref[...], preferred_element_type=F32)
    mix = mix + jnp.dot(oh_ref[...].astype(BF16), wh_ref[...], preferred_element_type=F32)
    o_ref[...] = _layer_norm(DN_ALPHA * x_ref[...] + mix, g_ref[...], b_ref[...])


def mix_ln(x, o_att, o_hg, w_att, w_hg, g, b):
    m = x.shape[0]
    row = pl.BlockSpec((ROW_TILE, D_MODEL), lambda i: (i, 0))
    half = pl.BlockSpec((ROW_TILE, ATT_WIDTH), lambda i: (i, 0))
    return pl.pallas_call(
        _mix_ln_body,
        grid=(m // ROW_TILE,),
        in_specs=[row, half, half, _const_spec((ATT_WIDTH, D_MODEL)), _const_spec((HG_WIDTH, D_MODEL)),
                  _const_spec((1, D_MODEL)), _const_spec((1, D_MODEL))],
        out_specs=row,
        out_shape=jax.ShapeDtypeStruct((m, D_MODEL), F32),
        compiler_params=pltpu.CompilerParams(dimension_semantics=("arbitrary",),
                                             vmem_limit_bytes=VMEM_LIMIT),
        name="mix_ln",
    )(x, o_att, o_hg, w_att, w_hg, g, b)


ATT_UNROLL = 4


def _attn_prompt_body(q_ref, k_ref, v_ref, o_ref, kt_ref, vt_ref, ob_ref, lse_ref, *, seq):
    blk = ATT_BLOCK
    lane = lax.broadcasted_iota(jnp.int32, (1, LANES), 1)
    head0 = lane < ATT_HEAD_DIM
    qi = lax.broadcasted_iota(jnp.int32, (blk, 2 * blk), 0)
    kc = lax.broadcasted_iota(jnp.int32, (blk, 2 * blk), 1)
    dist = qi + blk - kc
    nt = (((1,), (1,)), ((), ()))

    for br, (window, dil) in enumerate(DILATIONS):
        w_sub = window // dil
        rel = (dist >= 0) & (dist <= w_sub)
        n_units = seq // blk

        def units(it, carry, br=br, dil=dil, rel=rel):
            starts, masks, vbs, scores = [], [], [], []
            for uu in range(ATT_UNROLL):
                u = it * ATT_UNROLL + uu
                r = u % dil
                n = u // dil
                start = r + n * (blk * dil)
                prev = jnp.maximum(start - blk * dil, 0)
                starts.append(start)
                masks.append(rel & (kc >= jnp.where(n > 0, 0, blk)))
                q = q_ref[pl.ds(start, blk, stride=dil), :]
                kb = jnp.concatenate([k_ref[pl.ds(prev, blk, stride=dil), :],
                                      k_ref[pl.ds(start, blk, stride=dil), :]], axis=0).astype(BF16)
                vbs.append(jnp.concatenate([v_ref[pl.ds(prev, blk, stride=dil), :],
                                            v_ref[pl.ds(start, blk, stride=dil), :]], axis=0).astype(BF16))
                for hsel in (head0, ~head0):
                    qh = jnp.where(hsel, q, 0.0).astype(BF16)
                    scores.append(lax.dot_general(qh, kb, nt, preferred_element_type=F32))
            probs, ls, ms = [], [], []
            for i, s in enumerate(scores):
                s = jnp.where(masks[i // 2], s, NEG_INF)
                m = jnp.max(s, axis=-1, keepdims=True)
                p = jnp.exp(s - m)
                ls.append(jnp.sum(p, axis=-1, keepdims=True))
                ms.append(m)
                probs.append(p.astype(BF16))
            outs = [jnp.dot(p, vbs[i // 2], preferred_element_type=F32) for i, p in enumerate(probs)]
            for uu in range(ATT_UNROLL):
                i0, i1 = 2 * uu, 2 * uu + 1
                o2 = jnp.where(head0, outs[i0] / ls[i0], outs[i1] / ls[i1])
                l2 = jnp.where(head0, ms[i0] + jnp.log(ls[i0]), ms[i1] + jnp.log(ls[i1]))
                ob_ref[br, pl.ds(starts[uu], blk, stride=dil), :] = o2
                lse_ref[br, pl.ds(starts[uu], blk, stride=dil), :] = jnp.broadcast_to(l2, (blk, LANES))
            return carry

        lax.fori_loop(0, n_units // ATT_UNROLL, units, 0)

    def combine(c, carry):
        rows = pl.ds(pl.multiple_of(c * blk, blk), blk)
        l0, l1, l2 = lse_ref[0, rows, :], lse_ref[1, rows, :], lse_ref[2, rows, :]
        mx = jnp.maximum(jnp.maximum(l0, l1), l2)
        w0, w1, w2 = jnp.exp(l0 - mx), jnp.exp(l1 - mx), jnp.exp(l2 - mx)
        num = w0 * ob_ref[0, rows, :] + w1 * ob_ref[1, rows, :] + w2 * ob_ref[2, rows, :]
        o_ref[rows, :] = num / (w0 + w1 + w2)
        return carry

    lax.fori_loop(0, seq // blk, combine, 0)

    win = kt_ref.shape[-1]
    for j in range(win // LANES):
        rows = pl.ds(seq - win + j * LANES, LANES)
        kt_ref[0, :, :, j * LANES:(j + 1) * LANES] = k_ref[rows, :].T.reshape(2, ATT_HEAD_DIM, LANES)
        vt_ref[0, :, :, j * LANES:(j + 1) * LANES] = v_ref[rows, :].T.reshape(2, ATT_HEAD_DIM, LANES)


def attn_prompt(q, k, v, batch, seq):
    win = min(ATT_WINDOW_MAX, seq)
    n_hp = ATT_WIDTH // LANES
    blk = pl.BlockSpec((seq, LANES), lambda b, h: (b, h))
    tblk = pl.BlockSpec((1, 2, ATT_HEAD_DIM, win), lambda b, h: (b, h, 0, 0))
    tsds = jax.ShapeDtypeStruct((batch, ATT_HEADS, ATT_HEAD_DIM, win), F32)
    return pl.pallas_call(
        functools.partial(_attn_prompt_body, seq=seq),
        grid=(batch, n_hp),
        in_specs=[blk, blk, blk],
        out_specs=[blk, tblk, tblk],
        out_shape=[jax.ShapeDtypeStruct((batch * seq, ATT_WIDTH), F32), tsds, tsds],
        scratch_shapes=[pltpu.VMEM((3, seq, LANES), F32), pltpu.VMEM((3, seq, LANES), F32)],
        compiler_params=pltpu.CompilerParams(dimension_semantics=("arbitrary", "arbitrary"),
                                             vmem_limit_bytes=VMEM_LIMIT),
        name="attn_prompt",
    )(q, k, v)


def _hgrn_prompt_body(q_ref, k_ref, v_ref, lf_ref, gh_ref, ng_ref, o_ref, s_ref, st_ref, *, rows_per_step):
    grp = LANES
    n_ch = grp // HG_CHUNK
    ri = lax.broadcasted_iota(jnp.int32, (grp, grp), 0)
    ci = lax.broadcasted_iota(jnp.int32, (grp, grp), 1)
    same_chunk = (ri // HG_CHUNK) == (ci // HG_CHUNK)
    causal = same_chunk & (ci <= ri)
    summat = jnp.concatenate([jnp.where(causal, 1.0, 0.0), jnp.where(same_chunk, 1.0, 0.0)],
                             axis=0).astype(BF16)
    nt = (((1,), (1,)), ((), ()))
    tb = pl.program_id(1)

    @pl.when(tb == 0)
    def _():
        st_ref[...] = jnp.zeros(st_ref.shape, F32)

    heads = [slice(h * LANES, (h + 1) * LANES) for h in range(HG_HEADS)]
    in_chunk = [(ri // HG_CHUNK) == c for c in range(n_ch)]

    def by_chunk(x):
        return jnp.concatenate([jnp.where(m, x, 0.0).astype(BF16) for m in in_chunk], axis=1)

    def group(gi, carry):
        rows = pl.ds(pl.multiple_of(gi * grp, grp), grp)
        parts = _split3(lf_ref[rows, :])
        sums = (jnp.dot(summat, parts[0], preferred_element_type=F32)
                + jnp.dot(summat, parts[1], preferred_element_type=F32)
                + jnp.dot(summat, parts[2], preferred_element_type=F32))
        cum = sums[0:grp]
        cum_end = sums[grp:2 * grp]
        q, k, v = q_ref[rows, :], k_ref[rows, :], v_ref[rows, :]
        q_dec = q * jnp.exp(cum)
        k_dec = (k * jnp.exp(-cum)).astype(BF16)
        k_end = k * jnp.exp(cum_end - cum)
        decay = jnp.exp(cum_end)
        q_decb = q_dec.astype(BF16)
        vb = v.astype(BF16)
        a = [lax.dot_general(q_decb[:, hs], k_dec[:, hs], nt, preferred_element_type=F32) for hs in heads]
        o = [jnp.dot(jnp.where(causal, a[h], 0.0).astype(BF16), vb[:, hs], preferred_element_type=F32)
             for h, hs in enumerate(heads)]
        ut = [jnp.dot(v[:, hs].T.astype(BF16), by_chunk(k_end[:, hs]), preferred_element_type=F32)
              for hs in heads]
        inter = []
        for h, hs in enumerate(heads):
            st = st_ref[h]
            starts = []
            for c in range(n_ch):
                starts.append(st.astype(BF16))
                st = st * decay[c * HG_CHUNK:c * HG_CHUNK + 1, hs] + ut[h][:, c * LANES:(c + 1) * LANES]
            st_ref[h] = st
            inter.append(lax.dot_general(by_chunk(q_dec[:, hs]), jnp.concatenate(starts, axis=1), nt,
                                         preferred_element_type=F32))
        for h, hs in enumerate(heads):
            oh = o[h] + inter[h]
            ms = jnp.mean(oh * oh, axis=-1, keepdims=True)
            gate = gh_ref[rows, hs]
            o_ref[rows, hs] = oh * lax.rsqrt(ms + LN_EPS) * ng_ref[:, hs] * (gate * _sigmoid(gate))
        return carry

    lax.fori_loop(0, rows_per_step // grp, group, 0)

    @pl.when(tb == pl.num_programs(1) - 1)
    def _():
        for h in range(HG_HEADS):
            s_ref[0, h] = st_ref[h].T


HG_ROWS = 1024


def hgrn_prompt(qh, kh, ih, lf, gh, norm_g, batch, seq):
    n_tb = seq // HG_ROWS
    blk = pl.BlockSpec((HG_ROWS, HG_WIDTH), lambda b, t: (b * n_tb + t, 0))
    return pl.pallas_call(
        functools.partial(_hgrn_prompt_body, rows_per_step=HG_ROWS),
        grid=(batch, n_tb),
        in_specs=[blk, blk, blk, blk, blk, _const_spec((1, HG_WIDTH))],
        out_specs=[blk, pl.BlockSpec((1, HG_HEADS, HG_DK, HG_DV), lambda b, t: (b, 0, 0, 0))],
        out_shape=[jax.ShapeDtypeStruct((batch * seq, HG_WIDTH), F32),
                   jax.ShapeDtypeStruct((batch, HG_HEADS, HG_DK, HG_DV), F32)],
        scratch_shapes=[pltpu.VMEM((HG_HEADS, HG_DV, HG_DK), F32)],
        compiler_params=pltpu.CompilerParams(dimension_semantics=("arbitrary", "arbitrary"),
                                             vmem_limit_bytes=VMEM_LIMIT),
        name="hgrn_prompt",
    )(qh, kh, ih, lf, gh, norm_g)


def _attn_sample_body(q_ref, kn_ref, knt_ref, vnt_ref, kc_ref, vc_ref, o_ref, nk_ref, nv_ref,
                      *, n_new, past):
    srow = lax.broadcasted_iota(jnp.int32, (n_new, past), 0)
    tcol = lax.broadcasted_iota(jnp.int32, (n_new, past), 1)
    dist = past + srow - tcol
    cnt = jnp.zeros((n_new, past), F32)
    for window, dil in DILATIONS:
        cnt = cnt + jnp.where((dist % dil == 0) & (dist <= window), 1.0, 0.0)
    si = lax.broadcasted_iota(jnp.int32, (n_new, n_new), 0)
    sj = lax.broadcasted_iota(jnp.int32, (n_new, n_new), 1)
    dn = si - sj
    cnt_new = jnp.zeros((n_new, n_new), F32)
    for window, dil in DILATIONS:
        cnt_new = cnt_new + jnp.where((dn >= 0) & (dn % dil == 0) & (dn <= window), 1.0, 0.0)
    lane = lax.broadcasted_iota(jnp.int32, (1, LANES), 1)
    ones = jnp.ones((8, past), BF16)
    ones_new = jnp.ones((8, n_new), BF16)
    nt = (((1,), (1,)), ((), ()))
    shift = past - n_new

    def head(h, carry):
        kt = kc_ref[0, h]
        vt = vc_ref[0, h]
        q = q_ref[0, h].astype(BF16)
        s_c = jnp.dot(q, kt.astype(BF16), preferred_element_type=F32)
        s_n = lax.dot_general(q, kn_ref[0, h].astype(BF16), nt, preferred_element_type=F32)
        s_c = jnp.where(cnt > 0, s_c, NEG_INF)
        s_n = jnp.where(cnt_new > 0, s_n, NEG_INF)
        m = jnp.maximum(jnp.max(s_c, axis=-1, keepdims=True), jnp.max(s_n, axis=-1, keepdims=True))
        p_c = (cnt * jnp.exp(s_c - m)).astype(BF16)
        p_n = (cnt_new * jnp.exp(s_n - m)).astype(BF16)
        vnt = vnt_ref[0, h]
        num = (lax.dot_general(vt.astype(BF16), p_c, nt, preferred_element_type=F32)
               + lax.dot_general(vnt.astype(BF16), p_n, nt, preferred_element_type=F32))
        den = (lax.dot_general(ones, p_c, nt, preferred_element_type=F32)
               + lax.dot_general(ones_new, p_n, nt, preferred_element_type=F32))
        o_ref[0, h] = num / den[0:1, :]
        knt = knt_ref[0, h]
        for src, new, dst in ((kt, knt, nk_ref), (vt, vnt, nv_ref)):
            rolled = pltpu.roll(src, shift, 1)
            dst[0, h, :, 0:past - LANES] = rolled[:, 0:past - LANES]
            last = rolled[:, past - LANES:past]
            for s in range(n_new):
                col = jnp.broadcast_to(new[:, s:s + 1], last.shape)
                last = jnp.where(lane == LANES - n_new + s, col, last)
            dst[0, h, :, past - LANES:past] = last
        return carry

    lax.fori_loop(0, ATT_HEADS, head, 0)


def attn_sample(q, kn, knt, vnt, kc, vc):
    bsz, _, n_new, _ = q.shape
    past = kc.shape[-1]
    small = pl.BlockSpec((1, ATT_HEADS, n_new, ATT_HEAD_DIM), lambda b: (b, 0, 0, 0))
    small_t = pl.BlockSpec((1, ATT_HEADS, ATT_HEAD_DIM, n_new), lambda b: (b, 0, 0, 0))
    big = pl.BlockSpec((1, ATT_HEADS, ATT_HEAD_DIM, past), lambda b: (b, 0, 0, 0))
    big_sds = jax.ShapeDtypeStruct(kc.shape, F32)
    return pl.pallas_call(
        functools.partial(_attn_sample_body, n_new=n_new, past=past),
        grid=(bsz,),
        in_specs=[small, small, small_t, small_t, big, big],
        out_specs=[small_t, big, big],
        out_shape=[jax.ShapeDtypeStruct((bsz, ATT_HEADS, ATT_HEAD_DIM, n_new), F32), big_sds, big_sds],
        compiler_params=pltpu.CompilerParams(dimension_semantics=("arbitrary",),
                                             vmem_limit_bytes=VMEM_LIMIT),
        name="attn_sample",
    )(q, kn, knt, vnt, kc, vc)


HS_BATCH = 8


def _hgrn_sample_body(q_ref, k_ref, v_ref, lf_ref, gh_ref, ng_ref, s0_ref, o_ref, s_ref, *, n_new):
    row = lax.broadcasted_iota(jnp.int32, (n_new, 1), 0)
    tn = (((0,), (0,)), ((), ()))
    ones12 = jnp.ones((3 * n_new, HG_DV), BF16)

    def one_batch(b, carry):
        for h in range(HG_HEADS):
            cols = slice(h * LANES, (h + 1) * LANES)
            q, k, v, g = q_ref[b, :, cols], k_ref[b, :, cols], v_ref[b, :, cols], lf_ref[b, :, cols]
            cum = jnp.zeros((n_new, LANES), F32)
            for j in range(n_new):
                cum = cum + jnp.where(row >= j, g[j:j + 1, :], 0.0)
            cum_end = cum[n_new - 1:n_new, :]
            q_dec = q * jnp.exp(cum)
            k_dec = k * jnp.exp(-cum)
            k_end = k * jnp.exp(cum_end - cum)
            s0 = s0_ref[b, h]
            o = jnp.dot(q_dec.astype(BF16), s0.astype(BF16), preferred_element_type=F32)
            for j in range(n_new):
                a_j = jnp.sum(q_dec * k_dec[j:j + 1, :], axis=-1, keepdims=True)
                o = o + jnp.where(row >= j, a_j, 0.0) * v[j:j + 1, :]
            ms = jnp.mean(o * o, axis=-1, keepdims=True)
            gate = gh_ref[b, :, cols]
            o_ref[b, :, cols] = o * lax.rsqrt(ms + LN_EPS) * ng_ref[:, cols] * (gate * _sigmoid(gate))
            dec = lax.dot_general(jnp.concatenate(_split3(g), axis=0), ones12, tn,
                                  preferred_element_type=F32)
            u = lax.dot_general(k_end.astype(BF16), v.astype(BF16), tn, preferred_element_type=F32)
            s_ref[b, h] = jnp.exp(dec) * s0 + u
        return carry

    lax.fori_loop(0, HS_BATCH, one_batch, 0)


def hgrn_sample(qh, kh, ih, lf, gh, norm_g, s0):
    bsz, n_new, _ = qh.shape
    blk = pl.BlockSpec((HS_BATCH, n_new, HG_WIDTH), lambda i: (i, 0, 0))
    sblk = pl.BlockSpec((HS_BATCH, HG_HEADS, HG_DK, HG_DV), lambda i: (i, 0, 0, 0))
    return pl.pallas_call(
        functools.partial(_hgrn_sample_body, n_new=n_new),
        grid=(bsz // HS_BATCH,),
        in_specs=[blk, blk, blk, blk, blk, _const_spec((1, HG_WIDTH)), sblk],
        out_specs=[blk, sblk],
        out_shape=[jax.ShapeDtypeStruct((bsz, n_new, HG_WIDTH), F32),
                   jax.ShapeDtypeStruct(s0.shape, F32)],
        compiler_params=pltpu.CompilerParams(dimension_semantics=("arbitrary",),
                                             vmem_limit_bytes=VMEM_LIMIT),
        name="hgrn_sample",
    )(qh, kh, ih, lf, gh, norm_g, s0)


def _rope_tables(pos):
    half = ATT_HEAD_DIM // 2
    inv = ROPE_THETA ** (-jnp.arange(half, dtype=F32) / half)
    ang = pos.astype(F32)[:, None] * inv[None, :]
    cos, sin = jnp.cos(ang), jnp.sin(ang)
    return jnp.tile(cos, (1, 4)), jnp.tile(jnp.concatenate([-sin, sin], axis=1), (1, 2))


def kernel(x_prompt, x_sample, cache_k, cache_v, state_hgrn, ffn1_w_gate, ffn1_w_up, ffn1_w_down, ln1_g, ln1_b, w_in, hg_lower_bound, hg_norm_g, w_out, ln2_g, ln2_b, ffn2_w_gate, ffn2_w_up, ffn2_w_down, ln3_g, ln3_b):
    batch, seq, _ = x_prompt.shape
    dec_batch, n_new, _ = x_sample.shape
    assert ffn1_w_gate.shape[0] == 1, "single layer"
    win_p = min(ATT_WINDOW_MAX, seq)

    w1 = (ffn1_w_gate[0].astype(BF16), ffn1_w_up[0].astype(BF16), ffn1_w_down[0].astype(BF16))
    w2 = (ffn2_w_gate[0].astype(BF16), ffn2_w_up[0].astype(BF16), ffn2_w_down[0].astype(BF16))
    w_in_b = w_in[0].astype(BF16)
    w_att = w_out[0, :ATT_WIDTH].astype(BF16)
    w_hg = w_out[0, ATT_WIDTH:].astype(BF16)

    cos_p, sin_p = _rope_tables(jnp.arange(seq, dtype=jnp.int32))
    pos_s = PAST_LEN + (jnp.arange(dec_batch * n_new, dtype=jnp.int32) % n_new)
    cos_s, sin_s = _rope_tables(pos_s)

    xp = x_prompt.reshape(batch * seq, D_MODEL)
    xs = x_sample.reshape(dec_batch * n_new, D_MODEL)

    xp1 = ffn_ln(xp, *w1, ln1_g, ln1_b)
    q, k, v, qh, kh, ih, lf, gh = proj(xp1, w_in_b, cos_p, sin_p, hg_lower_bound, seq // ROW_TILE)
    o_att, kt_p, vt_p = attn_prompt(q, k, v, batch, seq)
    o_hg, s_p = hgrn_prompt(qh, kh, ih, lf, gh, hg_norm_g, batch, seq)
    xp2 = mix_ln(xp1, o_att, o_hg, w_att, w_hg, ln2_g, ln2_b)
    yp = ffn_ln(xp2, *w2, ln3_g, ln3_b)

    xs1 = ffn_ln(xs, *w1, ln1_g, ln1_b)
    qs, ks, vs, qhs, khs, ihs, lfs, ghs = proj(xs1, w_in_b, cos_s, sin_s, hg_lower_bound, 1)

    def heads(a):
        return a.reshape(dec_batch, n_new, ATT_HEADS, ATT_HEAD_DIM).transpose(0, 2, 1, 3)

    q4, k4, v4 = heads(qs), heads(ks), heads(vs)
    kc = cache_k[0].transpose(0, 2, 3, 1)
    vc = cache_v[0].transpose(0, 2, 3, 1)
    o_t, nk, nv = attn_sample(q4, k4, k4.transpose(0, 1, 3, 2), v4.transpose(0, 1, 3, 2), kc, vc)
    o_att_s = o_t.transpose(0, 3, 1, 2).reshape(dec_batch * n_new, ATT_WIDTH)

    def b3(a):
        return a.reshape(dec_batch, n_new, HG_WIDTH)

    o_hg_s, s_s = hgrn_sample(b3(qhs), b3(khs), b3(ihs), b3(lfs), b3(ghs), hg_norm_g, state_hgrn[0])
    xs2 = mix_ln(xs1, o_att_s, o_hg_s.reshape(dec_batch * n_new, HG_WIDTH), w_att, w_hg, ln2_g, ln2_b)
    ys = ffn_ln(xs2, *w2, ln3_g, ln3_b)

    return (yp.reshape(batch, seq, D_MODEL),
            ys.reshape(dec_batch, n_new, D_MODEL),
            kt_p.transpose(0, 3, 1, 2)[None],
            vt_p.transpose(0, 3, 1, 2)[None],
            s_p[None],
            nk.transpose(0, 3, 1, 2)[None],
            nv.transpose(0, 3, 1, 2)[None],
            s_s[None])
```

```python
import functools

import jax
import jax.numpy as jnp
from jax import lax
from jax.experimental import pallas as pl
from jax.experimental.pallas import tpu as pltpu

F32 = jnp.float32
BF16 = jnp.bfloat16

D_MODEL = 1024
ATT_WIDTH = 512
HG_WIDTH = 512
ATT_HEAD_DIM = 64
ATT_HEADS = 8
DILATIONS = ((128, 1), (512, 4), (2048, 16))
ATT_WINDOW_MAX = 2048
ATT_BLOCK = 128
ROPE_THETA = 10000.0
HG_HEADS = 4
HG_DK = 128
HG_DV = 128
HG_CHUNK = 32
D_FF = 2816
IN_COLS = 3 * ATT_WIDTH + 4 * HG_WIDTH
PAST_LEN = 8192
DN_ALPHA = 2.0 ** 0.25
LN_EPS = 1e-5
NEG_INF = -1e30

LANES = 128
ROW_TILE = 512
VMEM_LIMIT = 56 * 1024 * 1024


def _const_spec(shape):
    return pl.BlockSpec(shape, lambda *_: (0,) * len(shape), pipeline_mode=pl.Buffered(1))


def _layer_norm(r, g, b):
    mu = jnp.mean(r, axis=-1, keepdims=True)
    c = r - mu
    var = jnp.mean(c * c, axis=-1, keepdims=True)
    return c * lax.rsqrt(var + LN_EPS) * g + b


def _sigmoid(x):
    return 1.0 / (1.0 + jnp.exp(-x))


def _split3(x):
    hi = x.astype(BF16)
    r1 = x - hi.astype(F32)
    mid = r1.astype(BF16)
    lo = (r1 - mid.astype(F32)).astype(BF16)
    return hi, mid, lo


def _ffn_ln_body(x_ref, wg_ref, wu_ref, wd_ref, g_ref, b_ref, o_ref):
    x = x_ref[...]
    xb = x.astype(BF16)
    gate = jnp.dot(xb, wg_ref[...], preferred_element_type=F32)
    up = jnp.dot(xb, wu_ref[...], preferred_element_type=F32)
    h = (gate * _sigmoid(gate) * up).astype(BF16)
    y = jnp.dot(h, wd_ref[...], preferred_element_type=F32)
    o_ref[...] = _layer_norm(DN_ALPHA * x + 0.5 * y, g_ref[...], b_ref[...])


def ffn_ln(x, wg, wu, wd, g, b):
    m = x.shape[0]
    row = pl.BlockSpec((ROW_TILE, D_MODEL), lambda i: (i, 0))
    return pl.pallas_call(
        _ffn_ln_body,
        grid=(m // ROW_TILE,),
        in_specs=[row, _const_spec((D_MODEL, D_FF)), _const_spec((D_MODEL, D_FF)),
                  _const_spec((D_FF, D_MODEL)), _const_spec((1, D_MODEL)), _const_spec((1, D_MODEL))],
        out_specs=row,
        out_shape=jax.ShapeDtypeStruct((m, D_MODEL), F32),
        compiler_params=pltpu.CompilerParams(dimension_semantics=("arbitrary",),
                                             vmem_limit_bytes=VMEM_LIMIT),
        name="ffn_ln",
    )(x, wg, wu, wd, g, b)


def _rope(z, cos_t, sin_t):
    lane = lax.broadcasted_iota(jnp.int32, (1, LANES), 1)
    first_half = (lane % ATT_HEAD_DIM) < (ATT_HEAD_DIM // 2)
    outs = []
    for c in range(z.shape[1] // LANES):
        zc = z[:, c * LANES:(c + 1) * LANES]
        partner = jnp.where(first_half, pltpu.roll(zc, LANES - 32, 1), pltpu.roll(zc, 32, 1))
        outs.append(zc * cos_t + partner * sin_t)
    return outs


def _proj_body(x_ref, w_ref, cos_ref, sin_ref, lbraw_ref,
               q_ref, k_ref, v_ref, qh_ref, kh_ref, ih_ref, lf_ref, gh_ref):
    xb = x_ref[...].astype(BF16)
    z = jnp.dot(xb, w_ref[...], preferred_element_type=F32)
    cos_t = cos_ref[...]
    sin_t = sin_ref[...]
    aw = ATT_WIDTH
    for c, val in enumerate(_rope(z[:, 0:aw], cos_t, sin_t)):
        q_ref[:, c * LANES:(c + 1) * LANES] = val * (ATT_HEAD_DIM ** -0.5)
    for c, val in enumerate(_rope(z[:, aw:2 * aw], cos_t, sin_t)):
        k_ref[:, c * LANES:(c + 1) * LANES] = val
    v_ref[...] = z[:, 2 * aw:3 * aw]
    base = 3 * aw
    zq = z[:, base:base + HG_WIDTH]
    zf = z[:, base + HG_WIDTH:base + 2 * HG_WIDTH]
    lbr = lbraw_ref[...]
    mx = jnp.max(lbr, axis=0, keepdims=True)
    ex = jnp.exp(lbr - mx)
    lb = ex[0:1, :] / jnp.sum(ex, axis=0, keepdims=True)
    sg = _sigmoid(zf)
    qh_ref[...] = zq * _sigmoid(zq)
    lf_ref[...] = jnp.log(lb + (1.0 - lb) * sg)
    kh_ref[...] = (1.0 - lb) * _sigmoid(-zf)
    ih_ref[...] = z[:, base + 2 * HG_WIDTH:base + 3 * HG_WIDTH]
    gh_ref[...] = z[:, base + 3 * HG_WIDTH:base + 4 * HG_WIDTH]


def proj(x, w_in, cos_t, sin_t, lb_raw, table_tiles):
    m = x.shape[0]
    row = pl.BlockSpec((ROW_TILE, D_MODEL), lambda i: (i, 0))
    tab = pl.BlockSpec((ROW_TILE, LANES), lambda i: (i % table_tiles, 0))
    out = pl.BlockSpec((ROW_TILE, ATT_WIDTH), lambda i: (i, 0))
    sds = jax.ShapeDtypeStruct((m, ATT_WIDTH), F32)
    return pl.pallas_call(
        _proj_body,
        grid=(m // ROW_TILE,),
        in_specs=[row, _const_spec((D_MODEL, IN_COLS)), tab, tab, _const_spec((2, HG_WIDTH))],
        out_specs=[out] * 8,
        out_shape=[sds] * 8,
        compiler_params=pltpu.CompilerParams(dimension_semantics=("arbitrary",),
                                             vmem_limit_bytes=VMEM_LIMIT),
        name="proj",
    )(x, w_in, cos_t, sin_t, lb_raw)


def _mix_ln_body(x_ref, oa_ref, oh_ref, wa_ref, wh_ref, g_ref, b_ref, o_ref):
    mix = jnp.dot(oa_ref[...].astype(BF16), wa_ref[...], preferred_element_type=F32)
    mix = mix + jnp.dot(oh_ref[...].astype(BF16), wh_ref[...], preferred_element_type=F32)
    o_ref[...] = _layer_norm(DN_ALPHA * x_ref[...] + mix, g_ref[...], b_ref[...])


def mix_ln(x, o_att, o_hg, w_att, w_hg, g, b):
    m = x.shape[0]
    row = pl.BlockSpec((ROW_TILE, D_MODEL), lambda i: (i, 0))
    half = pl.BlockSpec((ROW_TILE, ATT_WIDTH), lambda i: (i, 0))
    return pl.pallas_call(
        _mix_ln_body,
        grid=(m // ROW_TILE,),
        in_specs=[row, half, half, _const_spec((ATT_WIDTH, D_MODEL)), _const_spec((HG_WIDTH, D_MODEL)),
                  _const_spec((1, D_MODEL)), _const_spec((1, D_MODEL))],
        out_specs=row,
        out_shape=jax.ShapeDtypeStruct((m, D_MODEL), F32),
        compiler_params=pltpu.CompilerParams(dimension_semantics=("arbitrary",),
                                             vmem_limit_bytes=VMEM_LIMIT),
        name="mix_ln",
    )(x, o_att, o_hg, w_att, w_hg, g, b)


ATT_UNROLL = 4


def _attn_prompt_body(q_ref, k_ref, v_ref, o_ref, kt_ref, vt_ref, ob_ref, lse_ref, *, seq):
    blk = ATT_BLOCK
    lane = lax.broadcasted_iota(jnp.int32, (1, LANES), 1)
    head0 = lane < ATT_HEAD_DIM
    qi = lax.broadcasted_iota(jnp.int32, (blk, 2 * blk), 0)
    kc = lax.broadcasted_iota(jnp.int32, (blk, 2 * blk), 1)
    dist = qi + blk - kc
    nt = (((1,), (1,)), ((), ()))

    for br, (window, dil) in enumerate(DILATIONS):
        w_sub = window // dil
        rel = (dist >= 0) & (dist <= w_sub)
        n_units = seq // blk

        def units(it, carry, br=br, dil=dil, rel=rel):
            starts, masks, vbs, scores = [], [], [], []
            for uu in range(ATT_UNROLL):
                u = it * ATT_UNROLL + uu
                r = u % dil
                n = u // dil
                start = r + n * (blk * dil)
                prev = jnp.maximum(start - blk * dil, 0)
                starts.append(start)
                masks.append(rel & (kc >= jnp.where(n > 0, 0, blk)))
                q = q_ref[pl.ds(start, blk, stride=dil), :]
                kb = jnp.concatenate([k_ref[pl.ds(prev, blk, stride=dil), :],
                                      k_ref[pl.ds(start, blk, stride=dil), :]], axis=0).astype(BF16)
                vbs.append(jnp.concatenate([v_ref[pl.ds(prev, blk, stride=dil), :],
                                            v_ref[pl.ds(start, blk, stride=dil), :]], axis=0).astype(BF16))
                for hsel in (head0, ~head0):
                    qh = jnp.where(hsel, q, 0.0).astype(BF16)
                    scores.append(lax.dot_general(qh, kb, nt, preferred_element_type=F32))
            probs, ls, ms = [], [], []
            for i, s in enumerate(scores):
                s = jnp.where(masks[i // 2], s, NEG_INF)
                m = jnp.max(s, axis=-1, keepdims=True)
                p = jnp.exp(s - m)
                ls.append(jnp.sum(p, axis=-1, keepdims=True))
                ms.append(m)
                probs.append(p.astype(BF16))
            outs = [jnp.dot(p, vbs[i // 2], preferred_element_type=F32) for i, p in enumerate(probs)]
            for uu in range(ATT_UNROLL):
                i0, i1 = 2 * uu, 2 * uu + 1
                o2 = jnp.where(head0, outs[i0] / ls[i0], outs[i1] / ls[i1])
                l2 = jnp.where(head0, ms[i0] + jnp.log(ls[i0]), ms[i1] + jnp.log(ls[i1]))
                ob_ref[br, pl.ds(starts[uu], blk, stride=dil), :] = o2
                lse_ref[br, pl.ds(starts[uu], blk, stride=dil), :] = jnp.broadcast_to(l2, (blk, LANES))
            return carry

        lax.fori_loop(0, n_units // ATT_UNROLL, units, 0)

    def combine(c, carry):
        rows = pl.ds(pl.multiple_of(c * blk, blk), blk)
        l0, l1, l2 = lse_ref[0, rows, :], lse_ref[1, rows, :], lse_ref[2, rows, :]
        mx = jnp.maximum(jnp.maximum(l0, l1), l2)
        w0, w1, w2 = jnp.exp(l0 - mx), jnp.exp(l1 - mx), jnp.exp(l2 - mx)
        num = w0 * ob_ref[0, rows, :] + w1 * ob_ref[1, rows, :] + w2 * ob_ref[2, rows, :]
        o_ref[rows, :] = num / (w0 + w1 + w2)
        return carry

    lax.fori_loop(0, seq // blk, combine, 0)

    win = kt_ref.shape[-1]
    for j in range(win // LANES):
        rows = pl.ds(seq - win + j * LANES, LANES)
        kt_ref[0, :, :, j * LANES:(j + 1) * LANES] = k_ref[rows, :].T.reshape(2, ATT_HEAD_DIM, LANES)
        vt_ref[0, :, :, j * LANES:(j + 1) * LANES] = v_ref[rows, :].T.reshape(2, ATT_HEAD_DIM, LANES)


def attn_prompt(q, k, v, batch, seq):
    win = min(ATT_WINDOW_MAX, seq)
    n_hp = ATT_WIDTH // LANES
    blk = pl.BlockSpec((seq, LANES), lambda b, h: (b, h))
    tblk = pl.BlockSpec((1, 2, ATT_HEAD_DIM, win), lambda b, h: (b, h, 0, 0))
    tsds = jax.ShapeDtypeStruct((batch, ATT_HEADS, ATT_HEAD_DIM, win), F32)
    return pl.pallas_call(
        functools.partial(_attn_prompt_body, seq=seq),
        grid=(batch, n_hp),
        in_specs=[blk, blk, blk],
        out_specs=[blk, tblk, tblk],
        out_shape=[jax.ShapeDtypeStruct((batch * seq, ATT_WIDTH), F32), tsds, tsds],
        scratch_shapes=[pltpu.VMEM((3, seq, LANES), F32), pltpu.VMEM((3, seq, LANES), F32)],
        compiler_params=pltpu.CompilerParams(dimension_semantics=("arbitrary", "arbitrary"),
                                             vmem_limit_bytes=VMEM_LIMIT),
        name="attn_prompt",
    )(q, k, v)


def _hgrn_prompt_body(q_ref, k_ref, v_ref, lf_ref, gh_ref, ng_ref, o_ref, s_ref, st_ref, *, rows_per_step):
    grp = LANES
    n_ch = grp // HG_CHUNK
    ri = lax.broadcasted_iota(jnp.int32, (grp, grp), 0)
    ci = lax.broadcasted_iota(jnp.int32, (grp, grp), 1)
    same_chunk = (ri // HG_CHUNK) == (ci // HG_CHUNK)
    causal = same_chunk & (ci <= ri)
    summat = jnp.concatenate([jnp.where(causal, 1.0, 0.0), jnp.where(same_chunk, 1.0, 0.0)],
                             axis=0).astype(BF16)
    nt = (((1,), (1,)), ((), ()))
    tb = pl.program_id(1)

    @pl.when(tb == 0)
    def _():
        st_ref[...] = jnp.zeros(st_ref.shape, F32)

    heads = [slice(h * LANES, (h + 1) * LANES) for h in range(HG_HEADS)]
    in_chunk = [(ri // HG_CHUNK) == c for c in range(n_ch)]

    def by_chunk(x):
        return jnp.concatenate([jnp.where(m, x, 0.0).astype(BF16) for m in in_chunk], axis=1)

    def group(gi, carry):
        rows = pl.ds(pl.multiple_of(gi * grp, grp), grp)
        parts = _split3(lf_ref[rows, :])
        sums = (jnp.dot(summat, parts[0], preferred_element_type=F32)
                + jnp.dot(summat, parts[1], preferred_element_type=F32)
                + jnp.dot(summat, parts[2], preferred_element_type=F32))
        cum = sums[0:grp]
        cum_end = sums[grp:2 * grp]
        q, k, v = q_ref[rows, :], k_ref[rows, :], v_ref[rows, :]
        q_dec = q * jnp.exp(cum)
        k_dec = (k * jnp.exp(-cum)).astype(BF16)
        k_end = k * jnp.exp(cum_end - cum)
        decay = jnp.exp(cum_end)
        q_decb = q_dec.astype(BF16)
        vb = v.astype(BF16)
        a = [lax.dot_general(q_decb[:, hs], k_dec[:, hs], nt, preferred_element_type=F32) for hs in heads]
        o = [jnp.dot(jnp.where(causal, a[h], 0.0).astype(BF16), vb[:, hs], preferred_element_type=F32)
             for h, hs in enumerate(heads)]
        ut = [jnp.dot(v[:, hs].T.astype(BF16), by_chunk(k_end[:, hs]), preferred_element_type=F32)
              for hs in heads]
        inter = []
        for h, hs in enumerate(heads):
            st = st_ref[h]
            starts = []
            for c in range(n_ch):
                starts.append(st.astype(BF16))
                st = st * decay[c * HG_CHUNK:c * HG_CHUNK + 1, hs] + ut[h][:, c * LANES:(c + 1) * LANES]
            st_ref[h] = st
            inter.append(lax.dot_general(by_chunk(q_dec[:, hs]), jnp.concatenate(starts, axis=1), nt,
                                         preferred_element_type=F32))
        for h, hs in enumerate(heads):
            oh = o[h] + inter[h]
            ms = jnp.mean(oh * oh, axis=-1, keepdims=True)
            gate = gh_ref[rows, hs]
            o_ref[rows, hs] = oh * lax.rsqrt(ms + LN_EPS) * ng_ref[:, hs] * (gate * _sigmoid(gate))
        return carry

    lax.fori_loop(0, rows_per_step // grp, group, 0)

    @pl.when(tb == pl.num_programs(1) - 1)
    def _():
        for h in range(HG_HEADS):
            s_ref[0, h] = st_ref[h].T


HG_ROWS = 1024


def hgrn_prompt(qh, kh, ih, lf, gh, norm_g, batch, seq):
    n_tb = seq // HG_ROWS
    blk = pl.BlockSpec((HG_ROWS, HG_WIDTH), lambda b, t: (b * n_tb + t, 0))
    return pl.pallas_call(
        functools.partial(_hgrn_prompt_body, rows_per_step=HG_ROWS),
        grid=(batch, n_tb),
        in_specs=[blk, blk, blk, blk, blk, _const_spec((1, HG_WIDTH))],
        out_specs=[blk, pl.BlockSpec((1, HG_HEADS, HG_DK, HG_DV), lambda b, t: (b, 0, 0, 0))],
        out_shape=[jax.ShapeDtypeStruct((batch * seq, HG_WIDTH), F32),
                   jax.ShapeDtypeStruct((batch, HG_HEADS, HG_DK, HG_DV), F32)],
        scratch_shapes=[pltpu.VMEM((HG_HEADS, HG_DV, HG_DK), F32)],
        compiler_params=pltpu.CompilerParams(dimension_semantics=("arbitrary", "arbitrary"),
                                             vmem_limit_bytes=VMEM_LIMIT),
        name="hgrn_prompt",
    )(qh, kh, ih, lf, gh, norm_g)


FA_ROWS = 128
FA_HEADS = 4


def _ffn_attn_body(x_ref, wg_ref, wu_ref, wd_ref, g_ref, b_ref,
                   q_ref, kn_ref, knt_ref, vnt_ref, kc_ref, vc_ref, *rest, n_new, past):
    y_ref, o_ref, nk_ref, nv_ref = rest[-4:]
    srow = lax.broadcasted_iota(jnp.int32, (n_new, past), 0)
    tcol = lax.broadcasted_iota(jnp.int32, (n_new, past), 1)
    dist = past + srow - tcol
    cnt = jnp.zeros((n_new, past), F32)
    for window, dil in DILATIONS:
        cnt = cnt + jnp.where((dist % dil == 0) & (dist <= window), 1.0, 0.0)
    si = lax.broadcasted_iota(jnp.int32, (n_new, n_new), 0)
    sj = lax.broadcasted_iota(jnp.int32, (n_new, n_new), 1)
    dn = si - sj
    cnt_new = jnp.zeros((n_new, n_new), F32)
    for window, dil in DILATIONS:
        cnt_new = cnt_new + jnp.where((dn >= 0) & (dn % dil == 0) & (dn <= window), 1.0, 0.0)
    lane = lax.broadcasted_iota(jnp.int32, (1, LANES), 1)
    ones = jnp.ones((8, past), BF16)
    ones_new = jnp.ones((8, n_new), BF16)
    nt = (((1,), (1,)), ((), ()))
    shift = past - n_new
    hs = range(FA_HEADS)

    qs = [q_ref[0, h].astype(BF16) for h in hs]
    s_c = [jnp.dot(qs[h], kc_ref[0, h].astype(BF16), preferred_element_type=F32) for h in hs]
    s_n = [lax.dot_general(qs[h], kn_ref[0, h].astype(BF16), nt, preferred_element_type=F32) for h in hs]

    x = x_ref[...]
    xb = x.astype(BF16)
    gate = jnp.dot(xb, wg_ref[...], preferred_element_type=F32)
    up = jnp.dot(xb, wu_ref[...], preferred_element_type=F32)

    p_c, p_n = [], []
    for h in hs:
        sc = jnp.where(cnt > 0, s_c[h], NEG_INF)
        sn = jnp.where(cnt_new > 0, s_n[h], NEG_INF)
        m = jnp.maximum(jnp.max(sc, axis=-1, keepdims=True), jnp.max(sn, axis=-1, keepdims=True))
        p_c.append((cnt * jnp.exp(sc - m)).astype(BF16))
        p_n.append((cnt_new * jnp.exp(sn - m)).astype(BF16))
    for h in hs:
        num = (lax.dot_general(vc_ref[0, h].astype(BF16), p_c[h], nt, preferred_element_type=F32)
               + lax.dot_general(vnt_ref[0, h].astype(BF16), p_n[h], nt, preferred_element_type=F32))
        den = (lax.dot_general(ones, p_c[h], nt, preferred_element_type=F32)
               + lax.dot_general(ones_new, p_n[h], nt, preferred_element_type=F32))
        o_ref[0, h] = num / den[0:1, :]

    hmid = (gate * _sigmoid(gate) * up).astype(BF16)
    y = jnp.dot(hmid, wd_ref[...], preferred_element_type=F32)
    y_ref[...] = _layer_norm(DN_ALPHA * x + 0.5 * y, g_ref[...], b_ref[...])

    for h in hs:
        for src_ref, new_ref, dst in ((kc_ref, knt_ref, nk_ref), (vc_ref, vnt_ref, nv_ref)):
            rolled = pltpu.roll(src_ref[0, h], shift, 1)
            dst[0, h, :, 0:past - LANES] = rolled[:, 0:past - LANES]
            last = rolled[:, past - LANES:past]
            new = new_ref[0, h]
            for s in range(n_new):
                col = jnp.broadcast_to(new[:, s:s + 1], last.shape)
                last = jnp.where(lane == LANES - n_new + s, col, last)
            dst[0, h, :, past - LANES:past] = last


def ffn_attn(x, wg, wu, wd, g, b, q, kn, knt, vnt, kc, vc, b_off, prev=None):
    m = x.shape[0]
    bsz, n_heads, n_new, _ = q.shape
    past = kc.shape[-1]
    halves = n_heads // FA_HEADS
    steps = m // FA_ROWS

    def samp(i):
        return (b_off + i // halves, i % halves, 0, 0)

    row = pl.BlockSpec((FA_ROWS, D_MODEL), lambda i: (i, 0))
    small = pl.BlockSpec((1, FA_HEADS, n_new, ATT_HEAD_DIM), samp)
    small_t = pl.BlockSpec((1, FA_HEADS, ATT_HEAD_DIM, n_new), samp)
    big = pl.BlockSpec((1, FA_HEADS, ATT_HEAD_DIM, past), samp)
    in_specs = [row, _const_spec((D_MODEL, D_FF)), _const_spec((D_MODEL, D_FF)), _const_spec((D_FF, D_MODEL)),
                _const_spec((1, D_MODEL)), _const_spec((1, D_MODEL)), small, small, small_t, small_t, big, big]
    args = [x, wg, wu, wd, g, b, q, kn, knt, vnt, kc, vc]
    aliases = {}
    if prev is not None:
        for j, a in enumerate(prev):
            aliases[len(args)] = 1 + j
            in_specs.append(pl.BlockSpec(memory_space=pl.ANY))
            args.append(a)
    big_sds = jax.ShapeDtypeStruct(kc.shape, F32)
    return pl.pallas_call(
        functools.partial(_ffn_attn_body, n_new=n_new, past=past),
        grid=(steps,),
        in_specs=in_specs,
        out_specs=[row, small_t, big, big],
        out_shape=[jax.ShapeDtypeStruct((m, D_MODEL), F32),
                   jax.ShapeDtypeStruct((bsz, n_heads, ATT_HEAD_DIM, n_new), F32), big_sds, big_sds],
        input_output_aliases=aliases,
        compiler_params=pltpu.CompilerParams(dimension_semantics=("arbitrary",),
                                             vmem_limit_bytes=VMEM_LIMIT),
        name="ffn_attn",
    )(*args)


HS_BATCH = 8


def _hgrn_sample_body(q_ref, k_ref, v_ref, lf_ref, gh_ref, ng_ref, s0_ref, o_ref, s_ref, *, n_new):
    row = lax.broadcasted_iota(jnp.int32, (n_new, 1), 0)
    tn = (((0,), (0,)), ((), ()))
    ones12 = jnp.ones((3 * n_new, HG_DV), BF16)

    def one_batch(b, carry):
        for h in range(HG_HEADS):
            cols = slice(h * LANES, (h + 1) * LANES)
            q, k, v, g = q_ref[b, :, cols], k_ref[b, :, cols], v_ref[b, :, cols], lf_ref[b, :, cols]
            cum = jnp.zeros((n_new, LANES), F32)
            for j in range(n_new):
                cum = cum + jnp.where(row >= j, g[j:j + 1, :], 0.0)
            cum_end = cum[n_new - 1:n_new, :]
            q_dec = q * jnp.exp(cum)
            k_dec = k * jnp.exp(-cum)
            k_end = k * jnp.exp(cum_end - cum)
            s0 = s0_ref[b, h]
            o = jnp.dot(q_dec.astype(BF16), s0.astype(BF16), preferred_element_type=F32)
            for j in range(n_new):
                a_j = jnp.sum(q_dec * k_dec[j:j + 1, :], axis=-1, keepdims=True)
                o = o + jnp.where(row >= j, a_j, 0.0) * v[j:j + 1, :]
            ms = jnp.mean(o * o, axis=-1, keepdims=True)
            gate = gh_ref[b, :, cols]
            o_ref[b, :, cols] = o * lax.rsqrt(ms + LN_EPS) * ng_ref[:, cols] * (gate * _sigmoid(gate))
            dec = lax.dot_general(jnp.concatenate(_split3(g), axis=0), ones12, tn,
                                  preferred_element_type=F32)
            u = lax.dot_general(k_end.astype(BF16), v.astype(BF16), tn, preferred_element_type=F32)
            s_ref[b, h] = jnp.exp(dec) * s0 + u
        return carry

    lax.fori_loop(0, HS_BATCH, one_batch, 0)


def hgrn_sample(qh, kh, ih, lf, gh, norm_g, s0):
    bsz, n_new, _ = qh.shape
    blk = pl.BlockSpec((HS_BATCH, n_new, HG_WIDTH), lambda i: (i, 0, 0))
    sblk = pl.BlockSpec((HS_BATCH, HG_HEADS, HG_DK, HG_DV), lambda i: (i, 0, 0, 0))
    return pl.pallas_call(
        functools.partial(_hgrn_sample_body, n_new=n_new),
        grid=(bsz // HS_BATCH,),
        in_specs=[blk, blk, blk, blk, blk, _const_spec((1, HG_WIDTH)), sblk],
        out_specs=[blk, sblk],
        out_shape=[jax.ShapeDtypeStruct((bsz, n_new, HG_WIDTH), F32),
                   jax.ShapeDtypeStruct(s0.shape, F32)],
        compiler_params=pltpu.CompilerParams(dimension_semantics=("arbitrary",),
                                             vmem_limit_bytes=VMEM_LIMIT),
        name="hgrn_sample",
    )(qh, kh, ih, lf, gh, norm_g, s0)


def _rope_tables(pos):
    half = ATT_HEAD_DIM // 2
    inv = ROPE_THETA ** (-jnp.arange(half, dtype=F32) / half)
    ang = pos.astype(F32)[:, None] * inv[None, :]
    cos, sin = jnp.cos(ang), jnp.sin(ang)
    return jnp.tile(cos, (1, 4)), jnp.tile(jnp.concatenate([-sin, sin], axis=1), (1, 2))


def kernel(x_prompt, x_sample, cache_k, cache_v, state_hgrn, ffn1_w_gate, ffn1_w_up, ffn1_w_down, ln1_g, ln1_b, w_in, hg_lower_bound, hg_norm_g, w_out, ln2_g, ln2_b, ffn2_w_gate, ffn2_w_up, ffn2_w_down, ln3_g, ln3_b):
    batch, seq, _ = x_prompt.shape
    dec_batch, n_new, _ = x_sample.shape
    assert ffn1_w_gate.shape[0] == 1, "single layer"
    win_p = min(ATT_WINDOW_MAX, seq)

    w1 = (ffn1_w_gate[0].astype(BF16), ffn1_w_up[0].astype(BF16), ffn1_w_down[0].astype(BF16))
    w2 = (ffn2_w_gate[0].astype(BF16), ffn2_w_up[0].astype(BF16), ffn2_w_down[0].astype(BF16))
    w_in_b = w_in[0].astype(BF16)
    w_att = w_out[0, :ATT_WIDTH].astype(BF16)
    w_hg = w_out[0, ATT_WIDTH:].astype(BF16)

    cos_p, sin_p = _rope_tables(jnp.arange(seq, dtype=jnp.int32))
    pos_s = PAST_LEN + (jnp.arange(dec_batch * n_new, dtype=jnp.int32) % n_new)
    cos_s, sin_s = _rope_tables(pos_s)

    xp = x_prompt.reshape(batch * seq, D_MODEL)
    xs = x_sample.reshape(dec_batch * n_new, D_MODEL)

    xs1 = ffn_ln(xs, *w1, ln1_g, ln1_b)
    qs, ks, vs, qhs, khs, ihs, lfs, ghs = proj(xs1, w_in_b, cos_s, sin_s, hg_lower_bound, 1)

    def heads(a):
        return a.reshape(dec_batch, n_new, ATT_HEADS, ATT_HEAD_DIM).transpose(0, 2, 1, 3)

    q4, k4, v4 = heads(qs), heads(ks), heads(vs)
    samp = (q4, k4, k4.transpose(0, 1, 3, 2), v4.transpose(0, 1, 3, 2),
            cache_k[0].transpose(0, 2, 3, 1),
            cache_v[0].transpose(0, 2, 3, 1))
    steps = (batch * seq) // FA_ROWS
    per_call = steps // (ATT_HEADS // FA_HEADS)
    assert 2 * per_call == dec_batch, "the two prompt FFN calls must cover the sample batch"

    xp1, o_t, nk, nv = ffn_attn(xp, *w1, ln1_g, ln1_b, *samp, 0)
    q, k, v, qh, kh, ih, lf, gh = proj(xp1, w_in_b, cos_p, sin_p, hg_lower_bound, seq // ROW_TILE)
    o_att, kt_p, vt_p = attn_prompt(q, k, v, batch, seq)
    o_hg, s_p = hgrn_prompt(qh, kh, ih, lf, gh, hg_norm_g, batch, seq)
    xp2 = mix_ln(xp1, o_att, o_hg, w_att, w_hg, ln2_g, ln2_b)
    yp, o_t, nk, nv = ffn_attn(xp2, *w2, ln3_g, ln3_b, *samp, per_call, prev=(o_t, nk, nv))

    o_att_s = o_t.transpose(0, 3, 1, 2).reshape(dec_batch * n_new, ATT_WIDTH)

    def b3(a):
        return a.reshape(dec_batch, n_new, HG_WIDTH)

    o_hg_s, s_s = hgrn_sample(b3(qhs), b3(khs), b3(ihs), b3(lfs), b3(ghs), hg_norm_g, state_hgrn[0])
    xs2 = mix_ln(xs1, o_att_s, o_hg_s.reshape(dec_batch * n_new, HG_WIDTH), w_att, w_hg, ln2_g, ln2_b)
    ys = ffn_ln(xs2, *w2, ln3_g, ln3_b)

    return (yp.reshape(batch, seq, D_MODEL),
            ys.reshape(dec_batch, n_new, D_MODEL),
            kt_p.transpose(0, 3, 1, 2)[None],
            vt_p.transpose(0, 3, 1, 2)[None],
            s_p[None],
            nk.transpose(0, 3, 1, 2)[None],
            nv.transpose(0, 3, 1, 2)[None],
            s_s[None])
```

```python
import functools

import jax
import jax.numpy as jnp
from jax import lax
from jax.experimental import pallas as pl
from jax.experimental.pallas import tpu as pltpu

F32 = jnp.float32
BF16 = jnp.bfloat16

D_MODEL = 1024
ATT_WIDTH = 512
HG_WIDTH = 512
ATT_HEAD_DIM = 64
ATT_HEADS = 8
DILATIONS = ((128, 1), (512, 4), (2048, 16))
ATT_WINDOW_MAX = 2048
ATT_BLOCK = 128
ROPE_THETA = 10000.0
HG_HEADS = 4
HG_DK = 128
HG_DV = 128
HG_CHUNK = 32
D_FF = 2816
IN_COLS = 3 * ATT_WIDTH + 4 * HG_WIDTH
PAST_LEN = 8192
DN_ALPHA = 2.0 ** 0.25
LN_EPS = 1e-5
NEG_INF = -1e30

LANES = 128
ROW_TILE = 512
VMEM_LIMIT = 56 * 1024 * 1024


def _const_spec(shape):
    return pl.BlockSpec(shape, lambda *_: (0,) * len(shape), pipeline_mode=pl.Buffered(1))


def _layer_norm(r, g, b):
    mu = jnp.mean(r, axis=-1, keepdims=True)
    c = r - mu
    var = jnp.mean(c * c, axis=-1, keepdims=True)
    return c * lax.rsqrt(var + LN_EPS) * g + b


def _sigmoid(x):
    return 1.0 / (1.0 + jnp.exp(-x))


def _split3(x):
    hi = x.astype(BF16)
    r1 = x - hi.astype(F32)
    mid = r1.astype(BF16)
    lo = (r1 - mid.astype(F32)).astype(BF16)
    return hi, mid, lo


def _ffn_ln_body(x_ref, wg_ref, wu_ref, wd_ref, g_ref, b_ref, o_ref):
    x = x_ref[...]
    xb = x.astype(BF16)
    gate = jnp.dot(xb, wg_ref[...], preferred_element_type=F32)
    up = jnp.dot(xb, wu_ref[...], preferred_element_type=F32)
    h = (gate * _sigmoid(gate) * up).astype(BF16)
    y = jnp.dot(h, wd_ref[...], preferred_element_type=F32)
    o_ref[...] = _layer_norm(DN_ALPHA * x + 0.5 * y, g_ref[...], b_ref[...])


def ffn_ln(x, wg, wu, wd, g, b):
    m = x.shape[0]
    row = pl.BlockSpec((ROW_TILE, D_MODEL), lambda i: (i, 0))
    return pl.pallas_call(
        _ffn_ln_body,
        grid=(m // ROW_TILE,),
        in_specs=[row, _const_spec((D_MODEL, D_FF)), _const_spec((D_MODEL, D_FF)),
                  _const_spec((D_FF, D_MODEL)), _const_spec((1, D_MODEL)), _const_spec((1, D_MODEL))],
        out_specs=row,
        out_shape=jax.ShapeDtypeStruct((m, D_MODEL), F32),
        compiler_params=pltpu.CompilerParams(dimension_semantics=("arbitrary",),
                                             vmem_limit_bytes=VMEM_LIMIT),
        name="ffn_ln",
    )(x, wg, wu, wd, g, b)


def _rope(z, cos_t, sin_t):
    lane = lax.broadcasted_iota(jnp.int32, (1, LANES), 1)
    first_half = (lane % ATT_HEAD_DIM) < (ATT_HEAD_DIM // 2)
    outs = []
    for c in range(z.shape[1] // LANES):
        zc = z[:, c * LANES:(c + 1) * LANES]
        partner = jnp.where(first_half, pltpu.roll(zc, LANES - 32, 1), pltpu.roll(zc, 32, 1))
        outs.append(zc * cos_t + partner * sin_t)
    return outs


def _proj_body(x_ref, w_ref, cos_ref, sin_ref, lbraw_ref,
               q_ref, k_ref, v_ref, qh_ref, kh_ref, ih_ref, lf_ref, gh_ref):
    xb = x_ref[...].astype(BF16)
    z = jnp.dot(xb, w_ref[...], preferred_element_type=F32)
    cos_t = cos_ref[...]
    sin_t = sin_ref[...]
    aw = ATT_WIDTH
    for c, val in enumerate(_rope(z[:, 0:aw], cos_t, sin_t)):
        q_ref[:, c * LANES:(c + 1) * LANES] = val * (ATT_HEAD_DIM ** -0.5)
    for c, val in enumerate(_rope(z[:, aw:2 * aw], cos_t, sin_t)):
        k_ref[:, c * LANES:(c + 1) * LANES] = val
    v_ref[...] = z[:, 2 * aw:3 * aw]
    base = 3 * aw
    zq = z[:, base:base + HG_WIDTH]
    zf = z[:, base + HG_WIDTH:base + 2 * HG_WIDTH]
    lbr = lbraw_ref[...]
    mx = jnp.max(lbr, axis=0, keepdims=True)
    ex = jnp.exp(lbr - mx)
    lb = ex[0:1, :] / jnp.sum(ex, axis=0, keepdims=True)
    sg = _sigmoid(zf)
    qh_ref[...] = zq * _sigmoid(zq)
    lf_ref[...] = jnp.log(lb + (1.0 - lb) * sg)
    kh_ref[...] = (1.0 - lb) * _sigmoid(-zf)
    ih_ref[...] = z[:, base + 2 * HG_WIDTH:base + 3 * HG_WIDTH]
    gh_ref[...] = z[:, base + 3 * HG_WIDTH:base + 4 * HG_WIDTH]


def proj(x, w_in, cos_t, sin_t, lb_raw, table_tiles):
    m = x.shape[0]
    row = pl.BlockSpec((ROW_TILE, D_MODEL), lambda i: (i, 0))
    tab = pl.BlockSpec((ROW_TILE, LANES), lambda i: (i % table_tiles, 0))
    out = pl.BlockSpec((ROW_TILE, ATT_WIDTH), lambda i: (i, 0))
    sds = jax.ShapeDtypeStruct((m, ATT_WIDTH), F32)
    return pl.pallas_call(
        _proj_body,
        grid=(m // ROW_TILE,),
        in_specs=[row, _const_spec((D_MODEL, IN_COLS)), tab, tab, _const_spec((2, HG_WIDTH))],
        out_specs=[out] * 8,
        out_shape=[sds] * 8,
        compiler_params=pltpu.CompilerParams(dimension_semantics=("arbitrary",),
                                             vmem_limit_bytes=VMEM_LIMIT),
        name="proj",
    )(x, w_in, cos_t, sin_t, lb_raw)


def _mix_ln_body(x_ref, oa_ref, oh_ref, wa_ref, wh_ref, g_ref, b_ref, o_ref):
    mix = jnp.dot(oa_ref[...].astype(BF16), wa_ref[...], preferred_element_type=F32)
    mix = mix + jnp.dot(oh_ref[...].astype(BF16), wh_ref[...], preferred_element_type=F32)
    o_ref[...] = _layer_norm(DN_ALPHA * x_ref[...] + mix, g_ref[...], b_ref[...])


def mix_ln(x, o_att, o_hg, w_att, w_hg, g, b):
    m = x.shape[0]
    row = pl.BlockSpec((ROW_TILE, D_MODEL), lambda i: (i, 0))
    half = pl.BlockSpec((ROW_TILE, ATT_WIDTH), lambda i: (i, 0))
    return pl.pallas_call(
        _mix_ln_body,
        grid=(m // ROW_TILE,),
        in_specs=[row, half, half, _const_spec((ATT_WIDTH, D_MODEL)), _const_spec((HG_WIDTH, D_MODEL)),
                  _const_spec((1, D_MODEL)), _const_spec((1, D_MODEL))],
        out_specs=row,
        out_shape=jax.ShapeDtypeStruct((m, D_MODEL), F32),
        compiler_params=pltpu.CompilerParams(dimension_semantics=("arbitrary",),
                                             vmem_limit_bytes=VMEM_LIMIT),
        name="mix_ln",
    )(x, o_att, o_hg, w_att, w_hg, g, b)


ATT_UNROLL = 4


def _attn_prompt_body(q_ref, k_ref, v_ref, o_ref, kt_ref, vt_ref, ob_ref, lse_ref, *, seq):
    blk = ATT_BLOCK
    lane = lax.broadcasted_iota(jnp.int32, (1, LANES), 1)
    head0 = lane < ATT_HEAD_DIM
    qi = lax.broadcasted_iota(jnp.int32, (blk, 2 * blk), 0)
    kc = lax.broadcasted_iota(jnp.int32, (blk, 2 * blk), 1)
    dist = qi + blk - kc
    first_mask = (lax.broadcasted_iota(jnp.int32, (blk, blk), 1)
                  <= lax.broadcasted_iota(jnp.int32, (blk, blk), 0))
    nt = (((1,), (1,)), ((), ()))
    log2e = 1.4426950408889634
    ln2 = 0.6931471805599453

    for br, (window, dil) in enumerate(DILATIONS):
        w_sub = window // dil
        rel = (dist >= 0) & (dist <= w_sub)
        span = blk * dil
        n_blk = seq // span
        run = min(ATT_UNROLL, n_blk)
        n_res = ATT_UNROLL // run
        runs_per_res = n_blk // run

        def units(it, carry, br=br, dil=dil, rel=rel, span=span, run=run, n_res=n_res,
                  runs_per_res=runs_per_res):
            starts, masks, vbs, scores = [], [], [], []
            for rr in range(n_res):
                if runs_per_res == 1:
                    r, n0 = it * n_res + rr, 0
                else:
                    r, n0 = it // runs_per_res, (it % runs_per_res) * run
                base = r + n0 * span

                def kv_block(j):
                    rows = pl.ds(jnp.maximum(base + j * span, r), blk, stride=dil)
                    return k_ref[rows, :].astype(BF16), v_ref[rows, :].astype(BF16)

                prev = None if runs_per_res == 1 else kv_block(-1)
                for j in range(run):
                    cur = kv_block(j)
                    start = base + j * span
                    q = q_ref[pl.ds(start, blk, stride=dil), :] * log2e
                    if prev is None:
                        kb, vb, mask = cur[0], cur[1], first_mask
                    else:
                        kb = jnp.concatenate([prev[0], cur[0]], axis=0)
                        vb = jnp.concatenate([prev[1], cur[1]], axis=0)
                        mask = rel & (kc >= jnp.where(n0 > 0, 0, blk)) if (j == 0 and runs_per_res > 1) else rel
                    starts.append(start)
                    masks.append(mask)
                    vbs.append(vb)
                    for hsel in (head0, ~head0):
                        qh = jnp.where(hsel, q, 0.0).astype(BF16)
                        scores.append(lax.dot_general(qh, kb, nt, preferred_element_type=F32))
                    prev = cur
            probs, ls, ms = [], [], []
            for i, s in enumerate(scores):
                s = jnp.where(masks[i // 2], s, NEG_INF)
                m = jnp.max(s, axis=-1, keepdims=True)
                p = jnp.exp2(s - m)
                ls.append(jnp.sum(p, axis=-1, keepdims=True))
                ms.append(m)
                probs.append(p.astype(BF16))
            outs = [jnp.dot(p, vbs[i // 2], preferred_element_type=F32) for i, p in enumerate(probs)]
            for uu in range(ATT_UNROLL):
                i0, i1 = 2 * uu, 2 * uu + 1
                o2 = jnp.where(head0, outs[i0] / ls[i0], outs[i1] / ls[i1])
                l2 = jnp.where(head0, ms[i0] + jnp.log2(ls[i0]), ms[i1] + jnp.log2(ls[i1])) * ln2
                ob_ref[br, pl.ds(starts[uu], blk, stride=dil), :] = o2
                lse_ref[br, pl.ds(starts[uu], blk, stride=dil), :] = jnp.broadcast_to(l2, (blk, LANES))
            return carry

        lax.fori_loop(0, (dil * n_blk) // ATT_UNROLL, units, 0)

    def combine(c, carry):
        rows = pl.ds(pl.multiple_of(c * blk, blk), blk)
        l0, l1, l2 = lse_ref[0, rows, :], lse_ref[1, rows, :], lse_ref[2, rows, :]
        mx = jnp.maximum(jnp.maximum(l0, l1), l2)
        w0, w1, w2 = jnp.exp(l0 - mx), jnp.exp(l1 - mx), jnp.exp(l2 - mx)
        num = w0 * ob_ref[0, rows, :] + w1 * ob_ref[1, rows, :] + w2 * ob_ref[2, rows, :]
        o_ref[rows, :] = num / (w0 + w1 + w2)
        return carry

    lax.fori_loop(0, seq // blk, combine, 0)

    win = kt_ref.shape[-1]
    for j in range(win // LANES):
        rows = pl.ds(seq - win + j * LANES, LANES)
        kt_ref[0, :, :, j * LANES:(j + 1) * LANES] = k_ref[rows, :].T.reshape(2, ATT_HEAD_DIM, LANES)
        vt_ref[0, :, :, j * LANES:(j + 1) * LANES] = v_ref[rows, :].T.reshape(2, ATT_HEAD_DIM, LANES)


def attn_prompt(q, k, v, batch, seq):
    win = min(ATT_WINDOW_MAX, seq)
    n_hp = ATT_WIDTH // LANES
    blk = pl.BlockSpec((seq, LANES), lambda b, h: (b, h))
    tblk = pl.BlockSpec((1, 2, ATT_HEAD_DIM, win), lambda b, h: (b, h, 0, 0))
    tsds = jax.ShapeDtypeStruct((batch, ATT_HEADS, ATT_HEAD_DIM, win), F32)
    return pl.pallas_call(
        functools.partial(_attn_prompt_body, seq=seq),
        grid=(batch, n_hp),
        in_specs=[blk, blk, blk],
        out_specs=[blk, tblk, tblk],
        out_shape=[jax.ShapeDtypeStruct((batch * seq, ATT_WIDTH), F32), tsds, tsds],
        scratch_shapes=[pltpu.VMEM((3, seq, LANES), F32), pltpu.VMEM((3, seq, LANES), F32)],
        compiler_params=pltpu.CompilerParams(dimension_semantics=("arbitrary", "arbitrary"),
                                             vmem_limit_bytes=VMEM_LIMIT),
        name="attn_prompt",
    )(q, k, v)


def _hgrn_prompt_body(q_ref, k_ref, v_ref, lf_ref, gh_ref, ng_ref, o_ref, s_ref, st_ref, *, rows_per_step):
    grp = LANES
    n_ch = grp // HG_CHUNK
    ri = lax.broadcasted_iota(jnp.int32, (grp, grp), 0)
    ci = lax.broadcasted_iota(jnp.int32, (grp, grp), 1)
    same_chunk = (ri // HG_CHUNK) == (ci // HG_CHUNK)
    causal = same_chunk & (ci <= ri)
    summat = jnp.concatenate([jnp.where(causal, 1.0, 0.0), jnp.where(same_chunk, 1.0, 0.0)],
                             axis=0).astype(BF16)
    nt = (((1,), (1,)), ((), ()))
    tb = pl.program_id(1)

    @pl.when(tb == 0)
    def _():
        st_ref[...] = jnp.zeros(st_ref.shape, F32)

    heads = [slice(h * LANES, (h + 1) * LANES) for h in range(HG_HEADS)]
    in_chunk = [(ri // HG_CHUNK) == c for c in range(n_ch)]

    def by_chunk(x):
        return jnp.concatenate([jnp.where(m, x, 0.0).astype(BF16) for m in in_chunk], axis=1)

    def group(gi, carry):
        rows = pl.ds(pl.multiple_of(gi * grp, grp), grp)
        parts = _split3(lf_ref[rows, :])
        sums = (jnp.dot(summat, parts[0], preferred_element_type=F32)
                + jnp.dot(summat, parts[1], preferred_element_type=F32)
                + jnp.dot(summat, parts[2], preferred_element_type=F32))
        cum = sums[0:grp]
        cum_end = sums[grp:2 * grp]
        q, k, v = q_ref[rows, :], k_ref[rows, :], v_ref[rows, :]
        q_dec = q * jnp.exp(cum)
        k_dec = (k * jnp.exp(-cum)).astype(BF16)
        k_end = k * jnp.exp(cum_end - cum)
        decay = jnp.exp(cum_end)
        q_decb = q_dec.astype(BF16)
        vb = v.astype(BF16)
        a = [lax.dot_general(q_decb[:, hs], k_dec[:, hs], nt, preferred_element_type=F32) for hs in heads]
        o = [jnp.dot(jnp.where(causal, a[h], 0.0).astype(BF16), vb[:, hs], preferred_element_type=F32)
             for h, hs in enumerate(heads)]
        ut = [jnp.dot(v[:, hs].T.astype(BF16), by_chunk(k_end[:, hs]), preferred_element_type=F32)
              for hs in heads]
        inter = []
        for h, hs in enumerate(heads):
            st = st_ref[h]
            starts = []
            for c in range(n_ch):
                starts.append(st.astype(BF16))
                st = st * decay[c * HG_CHUNK:c * HG_CHUNK + 1, hs] + ut[h][:, c * LANES:(c + 1) * LANES]
            st_ref[h] = st
            inter.append(lax.dot_general(by_chunk(q_dec[:, hs]), jnp.concatenate(starts, axis=1), nt,
                                         preferred_element_type=F32))
        for h, hs in enumerate(heads):
            oh = o[h] + inter[h]
            ms = jnp.mean(oh * oh, axis=-1, keepdims=True)
            gate = gh_ref[rows, hs]
            o_ref[rows, hs] = oh * lax.rsqrt(ms + LN_EPS) * ng_ref[:, hs] * (gate * _sigmoid(gate))
        return carry

    lax.fori_loop(0, rows_per_step // grp, group, 0)

    @pl.when(tb == pl.num_programs(1) - 1)
    def _():
        for h in range(HG_HEADS):
            s_ref[0, h] = st_ref[h].T


HG_ROWS = 1024


def hgrn_prompt(qh, kh, ih, lf, gh, norm_g, batch, seq):
    n_tb = seq // HG_ROWS
    blk = pl.BlockSpec((HG_ROWS, HG_WIDTH), lambda b, t: (b * n_tb + t, 0))
    return pl.pallas_call(
        functools.partial(_hgrn_prompt_body, rows_per_step=HG_ROWS),
        grid=(batch, n_tb),
        in_specs=[blk, blk, blk, blk, blk, _const_spec((1, HG_WIDTH))],
        out_specs=[blk, pl.BlockSpec((1, HG_HEADS, HG_DK, HG_DV), lambda b, t: (b, 0, 0, 0))],
        out_shape=[jax.ShapeDtypeStruct((batch * seq, HG_WIDTH), F32),
                   jax.ShapeDtypeStruct((batch, HG_HEADS, HG_DK, HG_DV), F32)],
        scratch_shapes=[pltpu.VMEM((HG_HEADS, HG_DV, HG_DK), F32)],
        compiler_params=pltpu.CompilerParams(dimension_semantics=("arbitrary", "arbitrary"),
                                             vmem_limit_bytes=VMEM_LIMIT),
        name="hgrn_prompt",
    )(qh, kh, ih, lf, gh, norm_g)


FA_ROWS = 128
FA_HEADS = 4


def _ffn_attn_body(x_ref, wg_ref, wu_ref, wd_ref, g_ref, b_ref,
                   q_ref, kn_ref, knt_ref, vnt_ref, kc_ref, vc_ref, *rest, n_new, past):
    y_ref, o_ref, nk_ref, nv_ref = rest[-4:]
    srow = lax.broadcasted_iota(jnp.int32, (n_new, past), 0)
    tcol = lax.broadcasted_iota(jnp.int32, (n_new, past), 1)
    dist = past + srow - tcol
    cnt = jnp.zeros((n_new, past), F32)
    for window, dil in DILATIONS:
        cnt = cnt + jnp.where((dist % dil == 0) & (dist <= window), 1.0, 0.0)
    si = lax.broadcasted_iota(jnp.int32, (n_new, n_new), 0)
    sj = lax.broadcasted_iota(jnp.int32, (n_new, n_new), 1)
    dn = si - sj
    cnt_new = jnp.zeros((n_new, n_new), F32)
    for window, dil in DILATIONS:
        cnt_new = cnt_new + jnp.where((dn >= 0) & (dn % dil == 0) & (dn <= window), 1.0, 0.0)
    lane = lax.broadcasted_iota(jnp.int32, (1, LANES), 1)
    ones = jnp.ones((8, past), BF16)
    ones_new = jnp.ones((8, n_new), BF16)
    nt = (((1,), (1,)), ((), ()))
    shift = past - n_new
    hs = range(FA_HEADS)

    x = x_ref[...]
    xb = x.astype(BF16)
    gate = jnp.dot(xb, wg_ref[...], preferred_element_type=F32)
    up = jnp.dot(xb, wu_ref[...], preferred_element_type=F32)

    qs = [q_ref[0, h].astype(BF16) for h in hs]
    s_c = [jnp.dot(qs[h], kc_ref[0, h].astype(BF16), preferred_element_type=F32) for h in hs]
    s_n = [lax.dot_general(qs[h], kn_ref[0, h].astype(BF16), nt, preferred_element_type=F32) for h in hs]

    hmid = (gate * _sigmoid(gate) * up).astype(BF16)
    y = jnp.dot(hmid, wd_ref[...], preferred_element_type=F32)
    y_ref[...] = _layer_norm(DN_ALPHA * x + 0.5 * y, g_ref[...], b_ref[...])

    p_c, p_n = [], []
    for h in hs:
        sc = jnp.where(cnt > 0, s_c[h], NEG_INF)
        sn = jnp.where(cnt_new > 0, s_n[h], NEG_INF)
        m = jnp.maximum(jnp.max(sc, axis=-1, keepdims=True), jnp.max(sn, axis=-1, keepdims=True))
        p_c.append((cnt * jnp.exp(sc - m)).astype(BF16))
        p_n.append((cnt_new * jnp.exp(sn - m)).astype(BF16))
    for h in hs:
        num = (lax.dot_general(vc_ref[0, h].astype(BF16), p_c[h], nt, preferred_element_type=F32)
               + lax.dot_general(vnt_ref[0, h].astype(BF16), p_n[h], nt, preferred_element_type=F32))
        den = (lax.dot_general(ones, p_c[h], nt, preferred_element_type=F32)
               + lax.dot_general(ones_new, p_n[h], nt, preferred_element_type=F32))
        o_ref[0, h] = num / den[0:1, :]

    for h in hs:
        for src_ref, new_ref, dst in ((kc_ref, knt_ref, nk_ref), (vc_ref, vnt_ref, nv_ref)):
            rolled = pltpu.roll(src_ref[0, h], shift, 1)
            dst[0, h, :, 0:past - LANES] = rolled[:, 0:past - LANES]
            last = rolled[:, past - LANES:past]
            new = new_ref[0, h]
            for s in range(n_new):
                col = jnp.broadcast_to(new[:, s:s + 1], last.shape)
                last = jnp.where(lane == LANES - n_new + s, col, last)
            dst[0, h, :, past - LANES:past] = last


def ffn_attn(x, wg, wu, wd, g, b, q, kn, knt, vnt, kc, vc, b_off, prev=None):
    m = x.shape[0]
    bsz, n_heads, n_new, _ = q.shape
    past = kc.shape[-1]
    halves = n_heads // FA_HEADS
    steps = m // FA_ROWS

    def samp(i):
        return (b_off + i // halves, i % halves, 0, 0)

    row = pl.BlockSpec((FA_ROWS, D_MODEL), lambda i: (i, 0))
    small = pl.BlockSpec((1, FA_HEADS, n_new, ATT_HEAD_DIM), samp)
    small_t = pl.BlockSpec((1, FA_HEADS, ATT_HEAD_DIM, n_new), samp)
    big = pl.BlockSpec((1, FA_HEADS, ATT_HEAD_DIM, past), samp)
    in_specs = [row, _const_spec((D_MODEL, D_FF)), _const_spec((D_MODEL, D_FF)), _const_spec((D_FF, D_MODEL)),
                _const_spec((1, D_MODEL)), _const_spec((1, D_MODEL)), small, small, small_t, small_t, big, big]
    args = [x, wg, wu, wd, g, b, q, kn, knt, vnt, kc, vc]
    aliases = {}
    if prev is not None:
        for j, a in enumerate(prev):
            aliases[len(args)] = 1 + j
            in_specs.append(pl.BlockSpec(memory_space=pl.ANY))
            args.append(a)
    big_sds = jax.ShapeDtypeStruct(kc.shape, F32)
    return pl.pallas_call(
        functools.partial(_ffn_attn_body, n_new=n_new, past=past),
        grid=(steps,),
        in_specs=in_specs,
        out_specs=[row, small_t, big, big],
        out_shape=[jax.ShapeDtypeStruct((m, D_MODEL), F32),
                   jax.ShapeDtypeStruct((bsz, n_heads, ATT_HEAD_DIM, n_new), F32), big_sds, big_sds],
        input_output_aliases=aliases,
        compiler_params=pltpu.CompilerParams(dimension_semantics=("arbitrary",),
                                             vmem_limit_bytes=VMEM_LIMIT),
        name="ffn_attn",
    )(*args)


HS_BATCH = 8


def _hgrn_sample_body(q_ref, k_ref, v_ref, lf_ref, gh_ref, ng_ref, s0_ref, o_ref, s_ref, *, n_new):
    row = lax.broadcasted_iota(jnp.int32, (n_new, 1), 0)
    tn = (((0,), (0,)), ((), ()))
    ones12 = jnp.ones((3 * n_new, HG_DV), BF16)

    def one_batch(b, carry):
        for h in range(HG_HEADS):
            cols = slice(h * LANES, (h + 1) * LANES)
            q, k, v, g = q_ref[b, :, cols], k_ref[b, :, cols], v_ref[b, :, cols], lf_ref[b, :, cols]
            cum = jnp.zeros((n_new, LANES), F32)
            for j in range(n_new):
                cum = cum + jnp.where(row >= j, g[j:j + 1, :], 0.0)
            cum_end = cum[n_new - 1:n_new, :]
            q_dec = q * jnp.exp(cum)
            k_dec = k * jnp.exp(-cum)
            k_end = k * jnp.exp(cum_end - cum)
            s0 = s0_ref[b, h]
            o = jnp.dot(q_dec.astype(BF16), s0.astype(BF16), preferred_element_type=F32)
            for j in range(n_new):
                a_j = jnp.sum(q_dec * k_dec[j:j + 1, :], axis=-1, keepdims=True)
                o = o + jnp.where(row >= j, a_j, 0.0) * v[j:j + 1, :]
            ms = jnp.mean(o * o, axis=-1, keepdims=True)
            gate = gh_ref[b, :, cols]
            o_ref[b, :, cols] = o * lax.rsqrt(ms + LN_EPS) * ng_ref[:, cols] * (gate * _sigmoid(gate))
            dec = lax.dot_general(jnp.concatenate(_split3(g), axis=0), ones12, tn,
                                  preferred_element_type=F32)
            u = lax.dot_general(k_end.astype(BF16), v.astype(BF16), tn, preferred_element_type=F32)
            s_ref[b, h] = jnp.exp(dec) * s0 + u
        return carry

    lax.fori_loop(0, HS_BATCH, one_batch, 0)


def hgrn_sample(qh, kh, ih, lf, gh, norm_g, s0):
    bsz, n_new, _ = qh.shape
    blk = pl.BlockSpec((HS_BATCH, n_new, HG_WIDTH), lambda i: (i, 0, 0))
    sblk = pl.BlockSpec((HS_BATCH, HG_HEADS, HG_DK, HG_DV), lambda i: (i, 0, 0, 0))
    return pl.pallas_call(
        functools.partial(_hgrn_sample_body, n_new=n_new),
        grid=(bsz // HS_BATCH,),
        in_specs=[blk, blk, blk, blk, blk, _const_spec((1, HG_WIDTH)), sblk],
        out_specs=[blk, sblk],
        out_shape=[jax.ShapeDtypeStruct((bsz, n_new, HG_WIDTH), F32),
                   jax.ShapeDtypeStruct(s0.shape, F32)],
        compiler_params=pltpu.CompilerParams(dimension_semantics=("arbitrary",),
                                             vmem_limit_bytes=VMEM_LIMIT),
        name="hgrn_sample",
    )(qh, kh, ih, lf, gh, norm_g, s0)


def _rope_tables(pos):
    half = ATT_HEAD_DIM // 2
    inv = ROPE_THETA ** (-jnp.arange(half, dtype=F32) / half)
    ang = pos.astype(F32)[:, None] * inv[None, :]
    cos, sin = jnp.cos(ang), jnp.sin(ang)
    return jnp.tile(cos, (1, 4)), jnp.tile(jnp.concatenate([-sin, sin], axis=1), (1, 2))


def kernel(x_prompt, x_sample, cache_k, cache_v, state_hgrn, ffn1_w_gate, ffn1_w_up, ffn1_w_down, ln1_g, ln1_b, w_in, hg_lower_bound, hg_norm_g, w_out, ln2_g, ln2_b, ffn2_w_gate, ffn2_w_up, ffn2_w_down, ln3_g, ln3_b):
    batch, seq, _ = x_prompt.shape
    dec_batch, n_new, _ = x_sample.shape
    assert ffn1_w_gate.shape[0] == 1, "single layer"
    win_p = min(ATT_WINDOW_MAX, seq)

    w1 = (ffn1_w_gate[0].astype(BF16), ffn1_w_up[0].astype(BF16), ffn1_w_down[0].astype(BF16))
    w2 = (ffn2_w_gate[0].astype(BF16), ffn2_w_up[0].astype(BF16), ffn2_w_down[0].astype(BF16))
    w_in_b = w_in[0].astype(BF16)
    w_att = w_out[0, :ATT_WIDTH].astype(BF16)
    w_hg = w_out[0, ATT_WIDTH:].astype(BF16)

    cos_p, sin_p = _rope_tables(jnp.arange(seq, dtype=jnp.int32))
    pos_s = PAST_LEN + (jnp.arange(dec_batch * n_new, dtype=jnp.int32) % n_new)
    cos_s, sin_s = _rope_tables(pos_s)

    xp = x_prompt.reshape(batch * seq, D_MODEL)
    xs = x_sample.reshape(dec_batch * n_new, D_MODEL)

    xs1 = ffn_ln(xs, *w1, ln1_g, ln1_b)
    qs, ks, vs, qhs, khs, ihs, lfs, ghs = proj(xs1, w_in_b, cos_s, sin_s, hg_lower_bound, 1)

    def heads(a):
        return a.reshape(dec_batch, n_new, ATT_HEADS, ATT_HEAD_DIM).transpose(0, 2, 1, 3)

    q4, k4, v4 = heads(qs), heads(ks), heads(vs)
    samp = (q4, k4, k4.transpose(0, 1, 3, 2), v4.transpose(0, 1, 3, 2),
            cache_k[0].transpose(0, 2, 3, 1),
            cache_v[0].transpose(0, 2, 3, 1))
    steps = (batch * seq) // FA_ROWS
    per_call = steps // (ATT_HEADS // FA_HEADS)
    assert 2 * per_call == dec_batch, "the two prompt FFN calls must cover the sample batch"

    xp1, o_t, nk, nv = ffn_attn(xp, *w1, ln1_g, ln1_b, *samp, 0)
    q, k, v, qh, kh, ih, lf, gh = proj(xp1, w_in_b, cos_p, sin_p, hg_lower_bound, seq // ROW_TILE)
    o_att, kt_p, vt_p = attn_prompt(q, k, v, batch, seq)
    o_hg, s_p = hgrn_prompt(qh, kh, ih, lf, gh, hg_norm_g, batch, seq)
    xp2 = mix_ln(xp1, o_att, o_hg, w_att, w_hg, ln2_g, ln2_b)
    yp, o_t, nk, nv = ffn_attn(xp2, *w2, ln3_g, ln3_b, *samp, per_call, prev=(o_t, nk, nv))

    o_att_s = o_t.transpose(0, 3, 1, 2).reshape(dec_batch * n_new, ATT_WIDTH)

    def b3(a):
        return a.reshape(dec_batch, n_new, HG_WIDTH)

    o_hg_s, s_s = hgrn_sample(b3(qhs), b3(khs), b3(ihs), b3(lfs), b3(ghs), hg_norm_g, state_hgrn[0])
    xs2 = mix_ln(xs1, o_att_s, o_hg_s.reshape(dec_batch * n_new, HG_WIDTH), w_att, w_hg, ln2_g, ln2_b)
    ys = ffn_ln(xs2, *w2, ln3_g, ln3_b)

    return (yp.reshape(batch, seq, D_MODEL),
            ys.reshape(dec_batch, n_new, D_MODEL),
            kt_p.transpose(0, 3, 1, 2)[None],
            vt_p.transpose(0, 3, 1, 2)[None],
            s_p[None],
            nk.transpose(0, 3, 1, 2)[None],
            nv.transpose(0, 3, 1, 2)[None],
            s_s[None])
```

```python
import functools

import jax
import jax.numpy as jnp
from jax import lax
from jax.experimental import pallas as pl
from jax.experimental.pallas import tpu as pltpu

F32 = jnp.float32
BF16 = jnp.bfloat16

D_MODEL = 1024
ATT_WIDTH = 512
HG_WIDTH = 512
ATT_HEAD_DIM = 64
ATT_HEADS = 8
DILATIONS = ((128, 1), (512, 4), (2048, 16))
ATT_WINDOW_MAX = 2048
ATT_BLOCK = 128
ROPE_THETA = 10000.0
HG_HEADS = 4
HG_DK = 128
HG_DV = 128
HG_CHUNK = 32
D_FF = 2816
IN_COLS = 3 * ATT_WIDTH + 4 * HG_WIDTH
PAST_LEN = 8192
DN_ALPHA = 2.0 ** 0.25
LN_EPS = 1e-5
NEG_INF = -1e30

LANES = 128
ROW_TILE = 512
VMEM_LIMIT = 56 * 1024 * 1024


def _const_spec(shape):
    return pl.BlockSpec(shape, lambda *_: (0,) * len(shape), pipeline_mode=pl.Buffered(1))


def _layer_norm(r, g, b):
    mu = jnp.mean(r, axis=-1, keepdims=True)
    c = r - mu
    var = jnp.mean(c * c, axis=-1, keepdims=True)
    return c * lax.rsqrt(var + LN_EPS) * g + b


def _sigmoid(x):
    return 1.0 / (1.0 + jnp.exp(-x))


def _split3(x):
    hi = x.astype(BF16)
    r1 = x - hi.astype(F32)
    mid = r1.astype(BF16)
    lo = (r1 - mid.astype(F32)).astype(BF16)
    return hi, mid, lo


def _ffn_ln_body(x_ref, wg_ref, wu_ref, wd_ref, g_ref, b_ref, o_ref):
    x = x_ref[...]
    xb = x.astype(BF16)
    gate = jnp.dot(xb, wg_ref[...], preferred_element_type=F32)
    up = jnp.dot(xb, wu_ref[...], preferred_element_type=F32)
    h = (gate * _sigmoid(gate) * up).astype(BF16)
    y = jnp.dot(h, wd_ref[...], preferred_element_type=F32)
    o_ref[...] = _layer_norm(DN_ALPHA * x + 0.5 * y, g_ref[...], b_ref[...])


def ffn_ln(x, wg, wu, wd, g, b):
    m = x.shape[0]
    row = pl.BlockSpec((ROW_TILE, D_MODEL), lambda i: (i, 0))
    return pl.pallas_call(
        _ffn_ln_body,
        grid=(m // ROW_TILE,),
        in_specs=[row, _const_spec((D_MODEL, D_FF)), _const_spec((D_MODEL, D_FF)),
                  _const_spec((D_FF, D_MODEL)), _const_spec((1, D_MODEL)), _const_spec((1, D_MODEL))],
        out_specs=row,
        out_shape=jax.ShapeDtypeStruct((m, D_MODEL), F32),
        compiler_params=pltpu.CompilerParams(dimension_semantics=("arbitrary",),
                                             vmem_limit_bytes=VMEM_LIMIT),
        name="ffn_ln",
    )(x, wg, wu, wd, g, b)


def _rope(z, cos_t, sin_t):
    lane = lax.broadcasted_iota(jnp.int32, (1, LANES), 1)
    first_half = (lane % ATT_HEAD_DIM) < (ATT_HEAD_DIM // 2)
    outs = []
    for c in range(z.shape[1] // LANES):
        zc = z[:, c * LANES:(c + 1) * LANES]
        partner = jnp.where(first_half, pltpu.roll(zc, LANES - 32, 1), pltpu.roll(zc, 32, 1))
        outs.append(zc * cos_t + partner * sin_t)
    return outs


def _proj_body(x_ref, w_ref, cos_ref, sin_ref, lbraw_ref,
               q_ref, k_ref, v_ref, qh_ref, kh_ref, ih_ref, lf_ref, gh_ref):
    xb = x_ref[...].astype(BF16)
    z = jnp.dot(xb, w_ref[...], preferred_element_type=F32)
    cos_t = cos_ref[...]
    sin_t = sin_ref[...]
    aw = ATT_WIDTH
    for c, val in enumerate(_rope(z[:, 0:aw], cos_t, sin_t)):
        q_ref[:, c * LANES:(c + 1) * LANES] = val * (ATT_HEAD_DIM ** -0.5)
    for c, val in enumerate(_rope(z[:, aw:2 * aw], cos_t, sin_t)):
        k_ref[:, c * LANES:(c + 1) * LANES] = val
    v_ref[...] = z[:, 2 * aw:3 * aw]
    base = 3 * aw
    zq = z[:, base:base + HG_WIDTH]
    zf = z[:, base + HG_WIDTH:base + 2 * HG_WIDTH]
    lbr = lbraw_ref[...]
    mx = jnp.max(lbr, axis=0, keepdims=True)
    ex = jnp.exp(lbr - mx)
    lb = ex[0:1, :] / jnp.sum(ex, axis=0, keepdims=True)
    sg = _sigmoid(zf)
    qh_ref[...] = zq * _sigmoid(zq)
    lf_ref[...] = jnp.log(lb + (1.0 - lb) * sg)
    kh_ref[...] = (1.0 - lb) * _sigmoid(-zf)
    ih_ref[...] = z[:, base + 2 * HG_WIDTH:base + 3 * HG_WIDTH]
    gh_ref[...] = z[:, base + 3 * HG_WIDTH:base + 4 * HG_WIDTH]


def proj(x, w_in, cos_t, sin_t, lb_raw, table_tiles):
    m = x.shape[0]
    row = pl.BlockSpec((ROW_TILE, D_MODEL), lambda i: (i, 0))
    tab = pl.BlockSpec((ROW_TILE, LANES), lambda i: (i % table_tiles, 0))
    out = pl.BlockSpec((ROW_TILE, ATT_WIDTH), lambda i: (i, 0))
    sds = jax.ShapeDtypeStruct((m, ATT_WIDTH), F32)
    return pl.pallas_call(
        _proj_body,
        grid=(m // ROW_TILE,),
        in_specs=[row, _const_spec((D_MODEL, IN_COLS)), tab, tab, _const_spec((2, HG_WIDTH))],
        out_specs=[out] * 8,
        out_shape=[sds] * 8,
        compiler_params=pltpu.CompilerParams(dimension_semantics=("arbitrary",),
                                             vmem_limit_bytes=VMEM_LIMIT),
        name="proj",
    )(x, w_in, cos_t, sin_t, lb_raw)


def _mix_ln_body(x_ref, oa_ref, oh_ref, wa_ref, wh_ref, g_ref, b_ref, o_ref):
    mix = jnp.dot(oa_ref[...].astype(BF16), wa_ref[...], preferred_element_type=F32)
    mix = mix + jnp.dot(oh_ref[...].astype(BF16), wh_ref[...], preferred_element_type=F32)
    o_ref[...] = _layer_norm(DN_ALPHA * x_ref[...] + mix, g_ref[...], b_ref[...])


def mix_ln(x, o_att, o_hg, w_att, w_hg, g, b):
    m = x.shape[0]
    row = pl.BlockSpec((ROW_TILE, D_MODEL), lambda i: (i, 0))
    half = pl.BlockSpec((ROW_TILE, ATT_WIDTH), lambda i: (i, 0))
    return pl.pallas_call(
        _mix_ln_body,
        grid=(m // ROW_TILE,),
        in_specs=[row, half, half, _const_spec((ATT_WIDTH, D_MODEL)), _const_spec((HG_WIDTH, D_MODEL)),
                  _const_spec((1, D_MODEL)), _const_spec((1, D_MODEL))],
        out_specs=row,
        out_shape=jax.ShapeDtypeStruct((m, D_MODEL), F32),
        compiler_params=pltpu.CompilerParams(dimension_semantics=("arbitrary",),
                                             vmem_limit_bytes=VMEM_LIMIT),
        name="mix_ln",
    )(x, o_att, o_hg, w_att, w_hg, g, b)


ATT_UNROLL = 4


def _attn_prompt_body(q_ref, k_ref, v_ref, o_ref, kt_ref, vt_ref, ob_ref, lse_ref, *, seq):
    blk = ATT_BLOCK
    lane = lax.broadcasted_iota(jnp.int32, (1, LANES), 1)
    head0 = lane < ATT_HEAD_DIM
    qi = lax.broadcasted_iota(jnp.int32, (blk, 2 * blk), 0)
    kc = lax.broadcasted_iota(jnp.int32, (blk, 2 * blk), 1)
    dist = qi + blk - kc
    first_mask = (lax.broadcasted_iota(jnp.int32, (blk, blk), 1)
                  <= lax.broadcasted_iota(jnp.int32, (blk, blk), 0))
    nt = (((1,), (1,)), ((), ()))
    log2e = 1.4426950408889634
    ln2 = 0.6931471805599453

    for br, (window, dil) in enumerate(DILATIONS):
        w_sub = window // dil
        rel = (dist >= 0) & (dist <= w_sub)
        span = blk * dil
        n_blk = seq // span
        run = min(ATT_UNROLL, n_blk)
        n_res = ATT_UNROLL // run
        runs_per_res = n_blk // run

        def units(it, carry, br=br, dil=dil, rel=rel, span=span, run=run, n_res=n_res,
                  runs_per_res=runs_per_res):
            starts, masks, vbs, scores = [], [], [], []
            for rr in range(n_res):
                if runs_per_res == 1:
                    r, n0 = it * n_res + rr, 0
                else:
                    r, n0 = it // runs_per_res, (it % runs_per_res) * run
                base = r + n0 * span

                def kv_block(j):
                    rows = pl.ds(jnp.maximum(base + j * span, r), blk, stride=dil)
                    return k_ref[rows, :].astype(BF16), v_ref[rows, :].astype(BF16)

                prev = None if runs_per_res == 1 else kv_block(-1)
                for j in range(run):
                    cur = kv_block(j)
                    start = base + j * span
                    q = q_ref[pl.ds(start, blk, stride=dil), :] * log2e
                    if prev is None:
                        kb, vb, mask = cur[0], cur[1], first_mask
                    else:
                        kb = jnp.concatenate([prev[0], cur[0]], axis=0)
                        vb = jnp.concatenate([prev[1], cur[1]], axis=0)
                        mask = rel & (kc >= jnp.where(n0 > 0, 0, blk)) if (j == 0 and runs_per_res > 1) else rel
                    starts.append(start)
                    masks.append(mask)
                    vbs.append(vb)
                    for hsel in (head0, ~head0):
                        qh = jnp.where(hsel, q, 0.0).astype(BF16)
                        scores.append(lax.dot_general(qh, kb, nt, preferred_element_type=F32))
                    prev = cur
            probs, ls, ms = [], [], []
            for i, s in enumerate(scores):
                s = jnp.where(masks[i // 2], s, NEG_INF)
                m = jnp.max(s, axis=-1, keepdims=True)
                p = jnp.exp2(s - m)
                ls.append(jnp.sum(p, axis=-1, keepdims=True))
                ms.append(m)
                probs.append(p.astype(BF16))
            outs = [jnp.dot(p, vbs[i // 2], preferred_element_type=F32) for i, p in enumerate(probs)]
            for uu in range(ATT_UNROLL):
                i0, i1 = 2 * uu, 2 * uu + 1
                o2 = jnp.where(head0, outs[i0] / ls[i0], outs[i1] / ls[i1])
                l2 = jnp.where(head0, ms[i0] + jnp.log2(ls[i0]), ms[i1] + jnp.log2(ls[i1])) * ln2
                ob_ref[br, pl.ds(starts[uu], blk, stride=dil), :] = o2
                lse_ref[br, pl.ds(starts[uu], blk, stride=dil), :] = jnp.broadcast_to(l2, (blk, LANES))
            return carry

        lax.fori_loop(0, (dil * n_blk) // ATT_UNROLL, units, 0)

    def combine(c, carry):
        rows = pl.ds(pl.multiple_of(c * blk, blk), blk)
        l0, l1, l2 = lse_ref[0, rows, :], lse_ref[1, rows, :], lse_ref[2, rows, :]
        mx = jnp.maximum(jnp.maximum(l0, l1), l2)
        w0, w1, w2 = jnp.exp(l0 - mx), jnp.exp(l1 - mx), jnp.exp(l2 - mx)
        num = w0 * ob_ref[0, rows, :] + w1 * ob_ref[1, rows, :] + w2 * ob_ref[2, rows, :]
        o_ref[rows, :] = num / (w0 + w1 + w2)
        return carry

    lax.fori_loop(0, seq // blk, combine, 0)

    win = kt_ref.shape[-1]
    for j in range(win // LANES):
        rows = pl.ds(seq - win + j * LANES, LANES)
        kt_ref[0, :, :, j * LANES:(j + 1) * LANES] = k_ref[rows, :].T.reshape(2, ATT_HEAD_DIM, LANES)
        vt_ref[0, :, :, j * LANES:(j + 1) * LANES] = v_ref[rows, :].T.reshape(2, ATT_HEAD_DIM, LANES)


def attn_prompt(q, k, v, batch, seq):
    win = min(ATT_WINDOW_MAX, seq)
    n_hp = ATT_WIDTH // LANES
    blk = pl.BlockSpec((seq, LANES), lambda b, h: (b, h))
    tblk = pl.BlockSpec((1, 2, ATT_HEAD_DIM, win), lambda b, h: (b, h, 0, 0))
    tsds = jax.ShapeDtypeStruct((batch, ATT_HEADS, ATT_HEAD_DIM, win), F32)
    return pl.pallas_call(
        functools.partial(_attn_prompt_body, seq=seq),
        grid=(batch, n_hp),
        in_specs=[blk, blk, blk],
        out_specs=[blk, tblk, tblk],
        out_shape=[jax.ShapeDtypeStruct((batch * seq, ATT_WIDTH), F32), tsds, tsds],
        scratch_shapes=[pltpu.VMEM((3, seq, LANES), F32), pltpu.VMEM((3, seq, LANES), F32)],
        compiler_params=pltpu.CompilerParams(dimension_semantics=("arbitrary", "arbitrary"),
                                             vmem_limit_bytes=VMEM_LIMIT),
        name="attn_prompt",
    )(q, k, v)


def _hgrn_prompt_body(q_ref, k_ref, v_ref, lf_ref, gh_ref, ng_ref, o_ref, s_ref, st_ref, *, rows_per_step):
    grp = LANES
    n_ch = grp // HG_CHUNK
    ri = lax.broadcasted_iota(jnp.int32, (grp, grp), 0)
    ci = lax.broadcasted_iota(jnp.int32, (grp, grp), 1)
    same_chunk = (ri // HG_CHUNK) == (ci // HG_CHUNK)
    causal = same_chunk & (ci <= ri)
    summat = jnp.concatenate([jnp.where(causal, 1.0, 0.0), jnp.where(same_chunk, 1.0, 0.0)],
                             axis=0).astype(BF16)
    nt = (((1,), (1,)), ((), ()))
    tb = pl.program_id(1)

    @pl.when(tb == 0)
    def _():
        st_ref[...] = jnp.zeros(st_ref.shape, F32)

    heads = [slice(h * LANES, (h + 1) * LANES) for h in range(HG_HEADS)]
    in_chunk = [(ri // HG_CHUNK) == c for c in range(n_ch)]

    def by_chunk(x):
        return jnp.concatenate([jnp.where(m, x, 0.0).astype(BF16) for m in in_chunk], axis=1)

    def group(gi, carry):
        rows = pl.ds(pl.multiple_of(gi * grp, grp), grp)
        parts = _split3(lf_ref[rows, :])
        sums = (jnp.dot(summat, parts[0], preferred_element_type=F32)
                + jnp.dot(summat, parts[1], preferred_element_type=F32)
                + jnp.dot(summat, parts[2], preferred_element_type=F32))
        cum = sums[0:grp]
        cum_end = sums[grp:2 * grp]
        q, k, v = q_ref[rows, :], k_ref[rows, :], v_ref[rows, :]
        q_dec = q * jnp.exp(cum)
        k_dec = (k * jnp.exp(-cum)).astype(BF16)
        k_end = k * jnp.exp(cum_end - cum)
        decay = jnp.exp(cum_end)
        q_decb = q_dec.astype(BF16)
        vb = v.astype(BF16)
        a = [lax.dot_general(q_decb[:, hs], k_dec[:, hs], nt, preferred_element_type=F32) for hs in heads]
        o = [jnp.dot(jnp.where(causal, a[h], 0.0).astype(BF16), vb[:, hs], preferred_element_type=F32)
             for h, hs in enumerate(heads)]
        ut = [jnp.dot(v[:, hs].T.astype(BF16), by_chunk(k_end[:, hs]), preferred_element_type=F32)
              for hs in heads]
        inter = []
        for h, hs in enumerate(heads):
            st = st_ref[h]
            starts = []
            for c in range(n_ch):
                starts.append(st.astype(BF16))
                st = st * decay[c * HG_CHUNK:c * HG_CHUNK + 1, hs] + ut[h][:, c * LANES:(c + 1) * LANES]
            st_ref[h] = st
            inter.append(lax.dot_general(by_chunk(q_dec[:, hs]), jnp.concatenate(starts, axis=1), nt,
                                         preferred_element_type=F32))
        for h, hs in enumerate(heads):
            oh = o[h] + inter[h]
            ms = jnp.mean(oh * oh, axis=-1, keepdims=True)
            gate = gh_ref[rows, hs]
            o_ref[rows, hs] = oh * lax.rsqrt(ms + LN_EPS) * ng_ref[:, hs] * (gate * _sigmoid(gate))
        return carry

    lax.fori_loop(0, rows_per_step // grp, group, 0)

    @pl.when(tb == pl.num_programs(1) - 1)
    def _():
        for h in range(HG_HEADS):
            s_ref[0, h] = st_ref[h].T


HG_ROWS = 1024


def hgrn_prompt(qh, kh, ih, lf, gh, norm_g, batch, seq):
    n_tb = seq // HG_ROWS
    blk = pl.BlockSpec((HG_ROWS, HG_WIDTH), lambda b, t: (b * n_tb + t, 0))
    return pl.pallas_call(
        functools.partial(_hgrn_prompt_body, rows_per_step=HG_ROWS),
        grid=(batch, n_tb),
        in_specs=[blk, blk, blk, blk, blk, _const_spec((1, HG_WIDTH))],
        out_specs=[blk, pl.BlockSpec((1, HG_HEADS, HG_DK, HG_DV), lambda b, t: (b, 0, 0, 0))],
        out_shape=[jax.ShapeDtypeStruct((batch * seq, HG_WIDTH), F32),
                   jax.ShapeDtypeStruct((batch, HG_HEADS, HG_DK, HG_DV), F32)],
        scratch_shapes=[pltpu.VMEM((HG_HEADS, HG_DV, HG_DK), F32)],
        compiler_params=pltpu.CompilerParams(dimension_semantics=("arbitrary", "arbitrary"),
                                             vmem_limit_bytes=VMEM_LIMIT),
        name="hgrn_prompt",
    )(qh, kh, ih, lf, gh, norm_g)


FA_ROWS = 128
FA_HEADS = 4


def _ffn_attn_body(x_ref, wg_ref, wu_ref, wd_ref, g_ref, b_ref,
                   q_ref, kn_ref, vn_ref, kc_ref, vc_ref, *rest, n_new, past, b_off):
    y_ref, o_ref, nk_ref, nv_ref = rest[-4:]
    halves = ATT_HEADS // FA_HEADS
    odd = ((b_off + pl.program_id(0) // halves) % 2) == 1

    def new_rows(ref):
        both = ref[...]
        return jnp.where(odd, both[n_new:2 * n_new], both[0:n_new])

    q_new, k_new, v_new = new_rows(q_ref), new_rows(kn_ref), new_rows(vn_ref)
    head_cols = [slice(h * ATT_HEAD_DIM, (h + 1) * ATT_HEAD_DIM) for h in range(FA_HEADS)]
    srow = lax.broadcasted_iota(jnp.int32, (n_new, past), 0)
    tcol = lax.broadcasted_iota(jnp.int32, (n_new, past), 1)
    dist = past + srow - tcol
    cnt = jnp.zeros((n_new, past), F32)
    for window, dil in DILATIONS:
        cnt = cnt + jnp.where((dist % dil == 0) & (dist <= window), 1.0, 0.0)
    si = lax.broadcasted_iota(jnp.int32, (n_new, n_new), 0)
    sj = lax.broadcasted_iota(jnp.int32, (n_new, n_new), 1)
    dn = si - sj
    cnt_new = jnp.zeros((n_new, n_new), F32)
    for window, dil in DILATIONS:
        cnt_new = cnt_new + jnp.where((dn >= 0) & (dn % dil == 0) & (dn <= window), 1.0, 0.0)
    lane = lax.broadcasted_iota(jnp.int32, (1, LANES), 1)
    ones = jnp.ones((8, past), BF16)
    ones_new = jnp.ones((8, n_new), BF16)
    nt = (((1,), (1,)), ((), ()))
    shift = past - n_new
    hs = range(FA_HEADS)

    x = x_ref[...]
    xb = x.astype(BF16)
    gate = jnp.dot(xb, wg_ref[...], preferred_element_type=F32)
    up = jnp.dot(xb, wu_ref[...], preferred_element_type=F32)

    qs = [q_new[:, c].astype(BF16) for c in head_cols]
    s_c = [jnp.dot(qs[h], kc_ref[0, h].astype(BF16), preferred_element_type=F32) for h in hs]
    s_n = [lax.dot_general(qs[h], k_new[:, head_cols[h]].astype(BF16), nt, preferred_element_type=F32)
           for h in hs]
    knt = [k_new[:, c].T for c in head_cols]
    vnt = [v_new[:, c].T for c in head_cols]

    hmid = (gate * _sigmoid(gate) * up).astype(BF16)
    y = jnp.dot(hmid, wd_ref[...], preferred_element_type=F32)
    y_ref[...] = _layer_norm(DN_ALPHA * x + 0.5 * y, g_ref[...], b_ref[...])

    p_c, p_n = [], []
    for h in hs:
        sc = jnp.where(cnt > 0, s_c[h], NEG_INF)
        sn = jnp.where(cnt_new > 0, s_n[h], NEG_INF)
        m = jnp.maximum(jnp.max(sc, axis=-1, keepdims=True), jnp.max(sn, axis=-1, keepdims=True))
        p_c.append((cnt * jnp.exp(sc - m)).astype(BF16))
        p_n.append((cnt_new * jnp.exp(sn - m)).astype(BF16))
    outs = []
    for h in hs:
        num = (lax.dot_general(vc_ref[0, h].astype(BF16), p_c[h], nt, preferred_element_type=F32)
               + lax.dot_general(vnt[h].astype(BF16), p_n[h], nt, preferred_element_type=F32))
        den = (lax.dot_general(ones, p_c[h], nt, preferred_element_type=F32)
               + lax.dot_general(ones_new, p_n[h], nt, preferred_element_type=F32))
        outs.append((num / den[0:1, :]).T)
    o_ref[0] = jnp.concatenate(outs, axis=1)

    for h in hs:
        for src_ref, new, dst in ((kc_ref, knt[h], nk_ref), (vc_ref, vnt[h], nv_ref)):
            rolled = pltpu.roll(src_ref[0, h], shift, 1)
            dst[0, h, :, 0:past - LANES] = rolled[:, 0:past - LANES]
            last = rolled[:, past - LANES:past]
            for s in range(n_new):
                col = jnp.broadcast_to(new[:, s:s + 1], last.shape)
                last = jnp.where(lane == LANES - n_new + s, col, last)
            dst[0, h, :, past - LANES:past] = last


def ffn_attn(x, wg, wu, wd, g, b, q, kn, vn, kc, vc, n_new, b_off, prev=None):
    m = x.shape[0]
    bsz, n_heads, _, past = kc.shape
    halves = n_heads // FA_HEADS
    steps = m // FA_ROWS
    assert b_off % 2 == 0 and bsz % 2 == 0, "new-token rows are fetched per pair of batch rows"

    def samp(i):
        return (b_off + i // halves, i % halves, 0, 0)

    row = pl.BlockSpec((FA_ROWS, D_MODEL), lambda i: (i, 0))
    pair = pl.BlockSpec((2 * n_new, FA_HEADS * ATT_HEAD_DIM), lambda i: ((b_off + i // halves) // 2, i % halves))
    o_blk = pl.BlockSpec((1, n_new, FA_HEADS * ATT_HEAD_DIM), lambda i: (b_off + i // halves, 0, i % halves))
    big = pl.BlockSpec((1, FA_HEADS, ATT_HEAD_DIM, past), samp)
    in_specs = [row, _const_spec((D_MODEL, D_FF)), _const_spec((D_MODEL, D_FF)), _const_spec((D_FF, D_MODEL)),
                _const_spec((1, D_MODEL)), _const_spec((1, D_MODEL)), pair, pair, pair, big, big]
    args = [x, wg, wu, wd, g, b, q, kn, vn, kc, vc]
    aliases = {}
    if prev is not None:
        for j, a in enumerate(prev):
            aliases[len(args)] = 1 + j
            in_specs.append(pl.BlockSpec(memory_space=pl.ANY))
            args.append(a)
    big_sds = jax.ShapeDtypeStruct(kc.shape, F32)
    return pl.pallas_call(
        functools.partial(_ffn_attn_body, n_new=n_new, past=past, b_off=b_off),
        grid=(steps,),
        in_specs=in_specs,
        out_specs=[row, o_blk, big, big],
        out_shape=[jax.ShapeDtypeStruct((m, D_MODEL), F32),
                   jax.ShapeDtypeStruct((bsz, n_new, n_heads * ATT_HEAD_DIM), F32), big_sds, big_sds],
        input_output_aliases=aliases,
        compiler_params=pltpu.CompilerParams(dimension_semantics=("arbitrary",),
                                             vmem_limit_bytes=VMEM_LIMIT),
        name="ffn_attn",
    )(*args)


HS_BATCH = 8


def _hgrn_sample_body(q_ref, k_ref, v_ref, lf_ref, gh_ref, ng_ref, s0_ref, o_ref, s_ref, *, n_new):
    row = lax.broadcasted_iota(jnp.int32, (n_new, 1), 0)
    tn = (((0,), (0,)), ((), ()))
    ones12 = jnp.ones((3 * n_new, HG_DV), BF16)

    sub = 8
    per_tile = sub // n_new

    def tile(t, carry):
        rows = pl.ds(pl.multiple_of(t * sub, sub), sub)
        units = []
        for h in range(HG_HEADS):
            cols = slice(h * LANES, (h + 1) * LANES)
            q8, k8, v8, g8 = q_ref[rows, cols], k_ref[rows, cols], v_ref[rows, cols], lf_ref[rows, cols]
            for i in range(per_tile):
                part = slice(i * n_new, (i + 1) * n_new)
                q, k, v, g = q8[part], k8[part], v8[part], g8[part]
                cum = jnp.zeros((n_new, LANES), F32)
                for j in range(n_new):
                    cum = cum + jnp.where(row >= j, g[j:j + 1, :], 0.0)
                cum_end = cum[n_new - 1:n_new, :]
                units.append(dict(h=h, b=t * per_tile + i, v=v, g=g, q_dec=q * jnp.exp(cum),
                                  k_dec=k * jnp.exp(-cum), k_end=k * jnp.exp(cum_end - cum)))
        for u in units:
            u["s0"] = s0_ref[u["b"], u["h"]]
            u["o"] = jnp.dot(u["q_dec"].astype(BF16), u["s0"].astype(BF16), preferred_element_type=F32)
        for u in units:
            dec = lax.dot_general(jnp.concatenate(_split3(u["g"]), axis=0), ones12, tn,
                                  preferred_element_type=F32)
            upd = lax.dot_general(u["k_end"].astype(BF16), u["v"].astype(BF16), tn,
                                  preferred_element_type=F32)
            s_ref[u["b"], u["h"]] = jnp.exp(dec) * u["s0"] + upd
        for h in range(HG_HEADS):
            cols = slice(h * LANES, (h + 1) * LANES)
            outs = []
            for u in units[h * per_tile:(h + 1) * per_tile]:
                o = u["o"]
                for j in range(n_new):
                    a_j = jnp.sum(u["q_dec"] * u["k_dec"][j:j + 1, :], axis=-1, keepdims=True)
                    o = o + jnp.where(row >= j, a_j, 0.0) * u["v"][j:j + 1, :]
                outs.append(o * lax.rsqrt(jnp.mean(o * o, axis=-1, keepdims=True) + LN_EPS))
            gate = gh_ref[rows, cols]
            o_ref[rows, cols] = jnp.concatenate(outs, axis=0) * ng_ref[:, cols] * (gate * _sigmoid(gate))
        return carry

    lax.fori_loop(0, HS_BATCH // per_tile, tile, 0)


def hgrn_sample(qh, kh, ih, lf, gh, norm_g, s0, n_new):
    bsz = s0.shape[0]
    blk = pl.BlockSpec((HS_BATCH * n_new, HG_WIDTH), lambda i: (i, 0))
    sblk = pl.BlockSpec((HS_BATCH, HG_HEADS, HG_DK, HG_DV), lambda i: (i, 0, 0, 0))
    return pl.pallas_call(
        functools.partial(_hgrn_sample_body, n_new=n_new),
        grid=(bsz // HS_BATCH,),
        in_specs=[blk, blk, blk, blk, blk, _const_spec((1, HG_WIDTH)), sblk],
        out_specs=[blk, sblk],
        out_shape=[jax.ShapeDtypeStruct((bsz * n_new, HG_WIDTH), F32),
                   jax.ShapeDtypeStruct(s0.shape, F32)],
        compiler_params=pltpu.CompilerParams(dimension_semantics=("arbitrary",),
                                             vmem_limit_bytes=VMEM_LIMIT),
        name="hgrn_sample",
    )(qh, kh, ih, lf, gh, norm_g, s0)


def _rope_tables(pos):
    half = ATT_HEAD_DIM // 2
    inv = ROPE_THETA ** (-jnp.arange(half, dtype=F32) / half)
    ang = pos.astype(F32)[:, None] * inv[None, :]
    cos, sin = jnp.cos(ang), jnp.sin(ang)
    return jnp.tile(cos, (1, 4)), jnp.tile(jnp.concatenate([-sin, sin], axis=1), (1, 2))


def kernel(x_prompt, x_sample, cache_k, cache_v, state_hgrn, ffn1_w_gate, ffn1_w_up, ffn1_w_down, ln1_g, ln1_b, w_in, hg_lower_bound, hg_norm_g, w_out, ln2_g, ln2_b, ffn2_w_gate, ffn2_w_up, ffn2_w_down, ln3_g, ln3_b):
    batch, seq, _ = x_prompt.shape
    dec_batch, n_new, _ = x_sample.shape
    assert ffn1_w_gate.shape[0] == 1, "single layer"
    win_p = min(ATT_WINDOW_MAX, seq)

    w1 = (ffn1_w_gate[0].astype(BF16), ffn1_w_up[0].astype(BF16), ffn1_w_down[0].astype(BF16))
    w2 = (ffn2_w_gate[0].astype(BF16), ffn2_w_up[0].astype(BF16), ffn2_w_down[0].astype(BF16))
    w_in_b = w_in[0].astype(BF16)
    w_att = w_out[0, :ATT_WIDTH].astype(BF16)
    w_hg = w_out[0, ATT_WIDTH:].astype(BF16)

    cos_p, sin_p = _rope_tables(jnp.arange(seq, dtype=jnp.int32))
    pos_s = PAST_LEN + (jnp.arange(dec_batch * n_new, dtype=jnp.int32) % n_new)
    cos_s, sin_s = _rope_tables(pos_s)

    xp = x_prompt.reshape(batch * seq, D_MODEL)
    xs = x_sample.reshape(dec_batch * n_new, D_MODEL)

    xs1 = ffn_ln(xs, *w1, ln1_g, ln1_b)
    qs, ks, vs, qhs, khs, ihs, lfs, ghs = proj(xs1, w_in_b, cos_s, sin_s, hg_lower_bound, 1)

    samp = (qs, ks, vs,
            cache_k[0].transpose(0, 2, 3, 1),
            cache_v[0].transpose(0, 2, 3, 1), n_new)
    steps = (batch * seq) // FA_ROWS
    per_call = steps // (ATT_HEADS // FA_HEADS)
    assert 2 * per_call == dec_batch, "the two prompt FFN calls must cover the sample batch"

    xp1, o_t, nk, nv = ffn_attn(xp, *w1, ln1_g, ln1_b, *samp, 0)
    q, k, v, qh, kh, ih, lf, gh = proj(xp1, w_in_b, cos_p, sin_p, hg_lower_bound, seq // ROW_TILE)
    o_att, kt_p, vt_p = attn_prompt(q, k, v, batch, seq)
    o_hg, s_p = hgrn_prompt(qh, kh, ih, lf, gh, hg_norm_g, batch, seq)
    xp2 = mix_ln(xp1, o_att, o_hg, w_att, w_hg, ln2_g, ln2_b)
    yp, o_t, nk, nv = ffn_attn(xp2, *w2, ln3_g, ln3_b, *samp, per_call, prev=(o_t, nk, nv))

    o_att_s = o_t.reshape(dec_batch * n_new, ATT_WIDTH)

    o_hg_s, s_s = hgrn_sample(qhs, khs, ihs, lfs, ghs, hg_norm_g, state_hgrn[0], n_new)
    xs2 = mix_ln(xs1, o_att_s, o_hg_s, w_att, w_hg, ln2_g, ln2_b)
    ys = ffn_ln(xs2, *w2, ln3_g, ln3_b)

    return (yp.reshape(batch, seq, D_MODEL),
            ys.reshape(dec_batch, n_new, D_MODEL),
            kt_p.transpose(0, 3, 1, 2)[None],
            vt_p.transpose(0, 3, 1, 2)[None],
            s_p[None],
            nk.transpose(0, 3, 1, 2)[None],
            nv.transpose(0, 3, 1, 2)[None],
            s_s[None])
```

```python
import functools

import jax
import jax.numpy as jnp
from jax import lax
from jax.experimental import pallas as pl
from jax.experimental.pallas import tpu as pltpu

F32 = jnp.float32
BF16 = jnp.bfloat16

D_MODEL = 1024
ATT_WIDTH = 512
HG_WIDTH = 512
ATT_HEAD_DIM = 64
ATT_HEADS = 8
DILATIONS = ((128, 1), (512, 4), (2048, 16))
ATT_WINDOW_MAX = 2048
ATT_BLOCK = 128
ROPE_THETA = 10000.0
HG_HEADS = 4
HG_DK = 128
HG_DV = 128
HG_CHUNK = 32
D_FF = 2816
IN_COLS = 3 * ATT_WIDTH + 4 * HG_WIDTH
PAST_LEN = 8192
DN_ALPHA = 2.0 ** 0.25
LN_EPS = 1e-5
NEG_INF = -1e30

LANES = 128
ROW_TILE = 512
VMEM_LIMIT = 56 * 1024 * 1024


def _const_spec(shape):
    return pl.BlockSpec(shape, lambda *_: (0,) * len(shape), pipeline_mode=pl.Buffered(1))


def _layer_norm(r, g, b):
    mu = jnp.mean(r, axis=-1, keepdims=True)
    c = r - mu
    var = jnp.mean(c * c, axis=-1, keepdims=True)
    return c * lax.rsqrt(var + LN_EPS) * g + b


def _sigmoid(x):
    return 1.0 / (1.0 + jnp.exp(-x))


def _split3(x):
    hi = x.astype(BF16)
    r1 = x - hi.astype(F32)
    mid = r1.astype(BF16)
    lo = (r1 - mid.astype(F32)).astype(BF16)
    return hi, mid, lo


def _ffn_ln_body(x_ref, wg_ref, wu_ref, wd_ref, g_ref, b_ref, o_ref):
    x = x_ref[...]
    xb = x.astype(BF16)
    gate = jnp.dot(xb, wg_ref[...], preferred_element_type=F32)
    up = jnp.dot(xb, wu_ref[...], preferred_element_type=F32)
    h = (gate * _sigmoid(gate) * up).astype(BF16)
    y = jnp.dot(h, wd_ref[...], preferred_element_type=F32)
    o_ref[...] = _layer_norm(DN_ALPHA * x + 0.5 * y, g_ref[...], b_ref[...])


def ffn_ln(x, wg, wu, wd, g, b):
    m = x.shape[0]
    row = pl.BlockSpec((ROW_TILE, D_MODEL), lambda i: (i, 0))
    return pl.pallas_call(
        _ffn_ln_body,
        grid=(m // ROW_TILE,),
        in_specs=[row, _const_spec((D_MODEL, D_FF)), _const_spec((D_MODEL, D_FF)),
                  _const_spec((D_FF, D_MODEL)), _const_spec((1, D_MODEL)), _const_spec((1, D_MODEL))],
        out_specs=row,
        out_shape=jax.ShapeDtypeStruct((m, D_MODEL), F32),
        compiler_params=pltpu.CompilerParams(dimension_semantics=("arbitrary",),
                                             vmem_limit_bytes=VMEM_LIMIT),
        name="ffn_ln",
    )(x, wg, wu, wd, g, b)


def _rope(z, cos_t, sin_t):
    lane = lax.broadcasted_iota(jnp.int32, (1, LANES), 1)
    first_half = (lane % ATT_HEAD_DIM) < (ATT_HEAD_DIM // 2)
    outs = []
    for c in range(z.shape[1] // LANES):
        zc = z[:, c * LANES:(c + 1) * LANES]
        partner = jnp.where(first_half, pltpu.roll(zc, LANES - 32, 1), pltpu.roll(zc, 32, 1))
        outs.append(zc * cos_t + partner * sin_t)
    return outs


def _proj_body(x_ref, w_ref, cos_ref, sin_ref, lbraw_ref,
               q_ref, k_ref, v_ref, qh_ref, kh_ref, ih_ref, lf_ref, gh_ref):
    xb = x_ref[...].astype(BF16)
    z = jnp.dot(xb, w_ref[...], preferred_element_type=F32)
    cos_t = cos_ref[...]
    sin_t = sin_ref[...]
    aw = ATT_WIDTH
    for c, val in enumerate(_rope(z[:, 0:aw], cos_t, sin_t)):
        q_ref[:, c * LANES:(c + 1) * LANES] = val * (ATT_HEAD_DIM ** -0.5)
    for c, val in enumerate(_rope(z[:, aw:2 * aw], cos_t, sin_t)):
        k_ref[:, c * LANES:(c + 1) * LANES] = val
    v_ref[...] = z[:, 2 * aw:3 * aw]
    base = 3 * aw
    zq = z[:, base:base + HG_WIDTH]
    zf = z[:, base + HG_WIDTH:base + 2 * HG_WIDTH]
    lbr = lbraw_ref[...]
    mx = jnp.max(lbr, axis=0, keepdims=True)
    ex = jnp.exp(lbr - mx)
    lb = ex[0:1, :] / jnp.sum(ex, axis=0, keepdims=True)
    sg = _sigmoid(zf)
    qh_ref[...] = zq * _sigmoid(zq)
    lf_ref[...] = jnp.log(lb + (1.0 - lb) * sg)
    kh_ref[...] = (1.0 - lb) * _sigmoid(-zf)
    ih_ref[...] = z[:, base + 2 * HG_WIDTH:base + 3 * HG_WIDTH]
    gh_ref[...] = z[:, base + 3 * HG_WIDTH:base + 4 * HG_WIDTH]


def proj(x, w_in, cos_t, sin_t, lb_raw, table_tiles):
    m = x.shape[0]
    row = pl.BlockSpec((ROW_TILE, D_MODEL), lambda i: (i, 0))
    tab = pl.BlockSpec((ROW_TILE, LANES), lambda i: (i % table_tiles, 0))
    out = pl.BlockSpec((ROW_TILE, ATT_WIDTH), lambda i: (i, 0))
    sds = jax.ShapeDtypeStruct((m, ATT_WIDTH), F32)
    return pl.pallas_call(
        _proj_body,
        grid=(m // ROW_TILE,),
        in_specs=[row, _const_spec((D_MODEL, IN_COLS)), tab, tab, _const_spec((2, HG_WIDTH))],
        out_specs=[out] * 8,
        out_shape=[sds] * 8,
        compiler_params=pltpu.CompilerParams(dimension_semantics=("arbitrary",),
                                             vmem_limit_bytes=VMEM_LIMIT),
        name="proj",
    )(x, w_in, cos_t, sin_t, lb_raw)


def _mix_ln_body(x_ref, oa_ref, oh_ref, wa_ref, wh_ref, g_ref, b_ref, o_ref):
    mix = jnp.dot(oa_ref[...].astype(BF16), wa_ref[...], preferred_element_type=F32)
    mix = mix + jnp.dot(oh_ref[...].astype(BF16), wh_ref[...], preferred_element_type=F32)
    o_ref[...] = _layer_norm(DN_ALPHA * x_ref[...] + mix, g_ref[...], b_ref[...])


def mix_ln(x, o_att, o_hg, w_att, w_hg, g, b):
    m = x.shape[0]
    row = pl.BlockSpec((ROW_TILE, D_MODEL), lambda i: (i, 0))
    half = pl.BlockSpec((ROW_TILE, ATT_WIDTH), lambda i: (i, 0))
    return pl.pallas_call(
        _mix_ln_body,
        grid=(m // ROW_TILE,),
        in_specs=[row, half, half, _const_spec((ATT_WIDTH, D_MODEL)), _const_spec((HG_WIDTH, D_MODEL)),
                  _const_spec((1, D_MODEL)), _const_spec((1, D_MODEL))],
        out_specs=row,
        out_shape=jax.ShapeDtypeStruct((m, D_MODEL), F32),
        compiler_params=pltpu.CompilerParams(dimension_semantics=("arbitrary",),
                                             vmem_limit_bytes=VMEM_LIMIT),
        name="mix_ln",
    )(x, o_att, o_hg, w_att, w_hg, g, b)


ATT_UNROLL = 4


def _attn_prompt_body(q_ref, k_ref, v_ref, o_ref, kt_ref, vt_ref, acc_ref, m_ref, l_ref, *, seq):
    blk = ATT_BLOCK
    lane = lax.broadcasted_iota(jnp.int32, (1, LANES), 1)
    head0 = lane < ATT_HEAD_DIM
    qi = lax.broadcasted_iota(jnp.int32, (blk, 2 * blk), 0)
    kc = lax.broadcasted_iota(jnp.int32, (blk, 2 * blk), 1)
    dist = qi + blk - kc
    first_mask = (lax.broadcasted_iota(jnp.int32, (blk, blk), 1)
                  <= lax.broadcasted_iota(jnp.int32, (blk, blk), 0))
    nt = (((1,), (1,)), ((), ()))
    log2e = 1.4426950408889634

    for br, (window, dil) in enumerate(DILATIONS):
        w_sub = window // dil
        rel = (dist >= 0) & (dist <= w_sub)
        span = blk * dil
        n_blk = seq // span
        run = min(ATT_UNROLL, n_blk)
        n_res = ATT_UNROLL // run
        runs_per_res = n_blk // run

        def units(it, carry, br=br, dil=dil, rel=rel, span=span, run=run, n_res=n_res,
                  runs_per_res=runs_per_res):
            starts, masks, vbs, scores = [], [], [], []
            for rr in range(n_res):
                if runs_per_res == 1:
                    r, n0 = it * n_res + rr, 0
                else:
                    r, n0 = it // runs_per_res, (it % runs_per_res) * run
                base = r + n0 * span

                def kv_block(j):
                    rows = pl.ds(jnp.maximum(base + j * span, r), blk, stride=dil)
                    return k_ref[rows, :].astype(BF16), v_ref[rows, :].astype(BF16)

                prev = None if runs_per_res == 1 else kv_block(-1)
                for j in range(run):
                    cur = kv_block(j)
                    start = base + j * span
                    q = q_ref[pl.ds(start, blk, stride=dil), :] * log2e
                    if prev is None:
                        kb, vb, mask = cur[0], cur[1], first_mask
                    else:
                        kb = jnp.concatenate([prev[0], cur[0]], axis=0)
                        vb = jnp.concatenate([prev[1], cur[1]], axis=0)
                        mask = rel & (kc >= jnp.where(n0 > 0, 0, blk)) if (j == 0 and runs_per_res > 1) else rel
                    starts.append(start)
                    masks.append(mask)
                    vbs.append(vb)
                    for hsel in (head0, ~head0):
                        qh = jnp.where(hsel, q, 0.0).astype(BF16)
                        scores.append(lax.dot_general(qh, kb, nt, preferred_element_type=F32))
                    prev = cur
            probs, ls, ms = [], [], []
            for i, s in enumerate(scores):
                s = jnp.where(masks[i // 2], s, NEG_INF)
                m = jnp.max(s, axis=-1, keepdims=True)
                p = jnp.exp2(s - m)
                ls.append(jnp.sum(p, axis=-1, keepdims=True))
                ms.append(m)
                probs.append(p.astype(BF16))
            outs = [jnp.dot(p, vbs[i // 2], preferred_element_type=F32) for i, p in enumerate(probs)]
            for uu in range(ATT_UNROLL):
                i0, i1 = 2 * uu, 2 * uu + 1
                rows = pl.ds(starts[uu], blk, stride=dil)
                acc_ref[br, rows, :] = jnp.where(head0, outs[i0], outs[i1])
                m_ref[br, rows, :] = jnp.broadcast_to(jnp.where(head0, ms[i0], ms[i1]), (blk, LANES))
                l_ref[br, rows, :] = jnp.broadcast_to(jnp.where(head0, ls[i0], ls[i1]), (blk, LANES))
            return carry

        lax.fori_loop(0, (dil * n_blk) // ATT_UNROLL, units, 0)

    def combine(c, carry):
        rows = pl.ds(pl.multiple_of(c * blk, blk), blk)
        m0, m1, m2 = m_ref[0, rows, :], m_ref[1, rows, :], m_ref[2, rows, :]
        mx = jnp.maximum(jnp.maximum(m0, m1), m2)
        w0, w1, w2 = jnp.exp2(m0 - mx), jnp.exp2(m1 - mx), jnp.exp2(m2 - mx)
        num = w0 * acc_ref[0, rows, :] + w1 * acc_ref[1, rows, :] + w2 * acc_ref[2, rows, :]
        den = w0 * l_ref[0, rows, :] + w1 * l_ref[1, rows, :] + w2 * l_ref[2, rows, :]
        o_ref[rows, :] = num / den
        return carry

    lax.fori_loop(0, seq // blk, combine, 0)

    win = kt_ref.shape[-1]
    for j in range(win // LANES):
        rows = pl.ds(seq - win + j * LANES, LANES)
        kt_ref[0, :, :, j * LANES:(j + 1) * LANES] = k_ref[rows, :].T.reshape(2, ATT_HEAD_DIM, LANES)
        vt_ref[0, :, :, j * LANES:(j + 1) * LANES] = v_ref[rows, :].T.reshape(2, ATT_HEAD_DIM, LANES)


def attn_prompt(q, k, v, batch, seq):
    win = min(ATT_WINDOW_MAX, seq)
    n_hp = ATT_WIDTH // LANES
    blk = pl.BlockSpec((seq, LANES), lambda b, h: (b, h))
    tblk = pl.BlockSpec((1, 2, ATT_HEAD_DIM, win), lambda b, h: (b, h, 0, 0))
    tsds = jax.ShapeDtypeStruct((batch, ATT_HEADS, ATT_HEAD_DIM, win), F32)
    return pl.pallas_call(
        functools.partial(_attn_prompt_body, seq=seq),
        grid=(batch, n_hp),
        in_specs=[blk, blk, blk],
        out_specs=[blk, tblk, tblk],
        out_shape=[jax.ShapeDtypeStruct((batch * seq, ATT_WIDTH), F32), tsds, tsds],
        scratch_shapes=[pltpu.VMEM((3, seq, LANES), F32)] * 3,
        compiler_params=pltpu.CompilerParams(dimension_semantics=("arbitrary", "arbitrary"),
                                             vmem_limit_bytes=VMEM_LIMIT),
        name="attn_prompt",
    )(q, k, v)


def _hgrn_prompt_body(q_ref, k_ref, v_ref, lf_ref, gh_ref, ng_ref, o_ref, s_ref, st_ref, *, rows_per_step):
    grp = LANES
    n_ch = grp // HG_CHUNK
    ri = lax.broadcasted_iota(jnp.int32, (grp, grp), 0)
    ci = lax.broadcasted_iota(jnp.int32, (grp, grp), 1)
    same_chunk = (ri // HG_CHUNK) == (ci // HG_CHUNK)
    causal = same_chunk & (ci <= ri)
    summat = jnp.concatenate([jnp.where(causal, 1.0, 0.0), jnp.where(same_chunk, 1.0, 0.0)],
                             axis=0).astype(BF16)
    nt = (((1,), (1,)), ((), ()))
    tb = pl.program_id(1)

    @pl.when(tb == 0)
    def _():
        st_ref[...] = jnp.zeros(st_ref.shape, F32)

    heads = [slice(h * LANES, (h + 1) * LANES) for h in range(HG_HEADS)]
    in_chunk = [(ri // HG_CHUNK) == c for c in range(n_ch)]

    def by_chunk(x):
        return jnp.concatenate([jnp.where(m, x, 0.0).astype(BF16) for m in in_chunk], axis=1)

    def group(gi, carry):
        rows = pl.ds(pl.multiple_of(gi * grp, grp), grp)
        parts = _split3(lf_ref[rows, :])
        sums = (jnp.dot(summat, parts[0], preferred_element_type=F32)
                + jnp.dot(summat, parts[1], preferred_element_type=F32)
                + jnp.dot(summat, parts[2], preferred_element_type=F32))
        cum = sums[0:grp]
        cum_end = sums[grp:2 * grp]
        q, k, v = q_ref[rows, :], k_ref[rows, :], v_ref[rows, :]
        q_dec = q * jnp.exp(cum)
        k_dec = (k * jnp.exp(-cum)).astype(BF16)
        k_end = k * jnp.exp(cum_end - cum)
        decay = jnp.exp(cum_end)
        q_decb = q_dec.astype(BF16)
        vb = v.astype(BF16)
        a = [lax.dot_general(q_decb[:, hs], k_dec[:, hs], nt, preferred_element_type=F32) for hs in heads]
        o = [jnp.dot(jnp.where(causal, a[h], 0.0).astype(BF16), vb[:, hs], preferred_element_type=F32)
             for h, hs in enumerate(heads)]
        ut = [jnp.dot(v[:, hs].T.astype(BF16), by_chunk(k_end[:, hs]), preferred_element_type=F32)
              for hs in heads]
        inter = []
        for h, hs in enumerate(heads):
            st = st_ref[h]
            starts = []
            for c in range(n_ch):
                starts.append(st.astype(BF16))
                st = st * decay[c * HG_CHUNK:c * HG_CHUNK + 1, hs] + ut[h][:, c * LANES:(c + 1) * LANES]
            st_ref[h] = st
            inter.append(lax.dot_general(by_chunk(q_dec[:, hs]), jnp.concatenate(starts, axis=1), nt,
                                         preferred_element_type=F32))
        for h, hs in enumerate(heads):
            oh = o[h] + inter[h]
            ms = jnp.mean(oh * oh, axis=-1, keepdims=True)
            gate = gh_ref[rows, hs]
            o_ref[rows, hs] = oh * lax.rsqrt(ms + LN_EPS) * ng_ref[:, hs] * (gate * _sigmoid(gate))
        return carry

    lax.fori_loop(0, rows_per_step // grp, group, 0)

    @pl.when(tb == pl.num_programs(1) - 1)
    def _():
        for h in range(HG_HEADS):
            s_ref[0, h] = st_ref[h].T


HG_ROWS = 1024


def hgrn_prompt(qh, kh, ih, lf, gh, norm_g, batch, seq):
    n_tb = seq // HG_ROWS
    blk = pl.BlockSpec((HG_ROWS, HG_WIDTH), lambda b, t: (b * n_tb + t, 0))
    return pl.pallas_call(
        functools.partial(_hgrn_prompt_body, rows_per_step=HG_ROWS),
        grid=(batch, n_tb),
        in_specs=[blk, blk, blk, blk, blk, _const_spec((1, HG_WIDTH))],
        out_specs=[blk, pl.BlockSpec((1, HG_HEADS, HG_DK, HG_DV), lambda b, t: (b, 0, 0, 0))],
        out_shape=[jax.ShapeDtypeStruct((batch * seq, HG_WIDTH), F32),
                   jax.ShapeDtypeStruct((batch, HG_HEADS, HG_DK, HG_DV), F32)],
        scratch_shapes=[pltpu.VMEM((HG_HEADS, HG_DV, HG_DK), F32)],
        compiler_params=pltpu.CompilerParams(dimension_semantics=("arbitrary", "arbitrary"),
                                             vmem_limit_bytes=VMEM_LIMIT),
        name="hgrn_prompt",
    )(qh, kh, ih, lf, gh, norm_g)


FA_ROWS = 128
FA_HEADS = 4


def _ffn_attn_body(*refs, n_new, past, b_off, mixed):
    n_x = 7 if mixed else 1
    wg_ref, wu_ref, wd_ref, g_ref, b_ref, q_ref, kn_ref, vn_ref, kc_ref, vc_ref = refs[n_x:n_x + 10]
    y_ref, o_ref, nk_ref, nv_ref = refs[-4:]
    halves = ATT_HEADS // FA_HEADS
    odd = ((b_off + pl.program_id(0) // halves) % 2) == 1

    def new_rows(ref):
        both = ref[...]
        return jnp.where(odd, both[n_new:2 * n_new], both[0:n_new])

    q_new, k_new, v_new = new_rows(q_ref), new_rows(kn_ref), new_rows(vn_ref)
    head_cols = [slice(h * ATT_HEAD_DIM, (h + 1) * ATT_HEAD_DIM) for h in range(FA_HEADS)]
    srow = lax.broadcasted_iota(jnp.int32, (n_new, past), 0)
    tcol = lax.broadcasted_iota(jnp.int32, (n_new, past), 1)
    dist = past + srow - tcol
    cnt = jnp.zeros((n_new, past), F32)
    for window, dil in DILATIONS:
        cnt = cnt + jnp.where((dist % dil == 0) & (dist <= window), 1.0, 0.0)
    si = lax.broadcasted_iota(jnp.int32, (n_new, n_new), 0)
    sj = lax.broadcasted_iota(jnp.int32, (n_new, n_new), 1)
    dn = si - sj
    cnt_new = jnp.zeros((n_new, n_new), F32)
    for window, dil in DILATIONS:
        cnt_new = cnt_new + jnp.where((dn >= 0) & (dn % dil == 0) & (dn <= window), 1.0, 0.0)
    lane = lax.broadcasted_iota(jnp.int32, (1, LANES), 1)
    ones = jnp.ones((8, past), BF16)
    ones_new = jnp.ones((8, n_new), BF16)
    nt = (((1,), (1,)), ((), ()))
    shift = past - n_new
    hs = range(FA_HEADS)

    def scores():
        qs = [q_new[:, c].astype(BF16) for c in head_cols]
        s_c = [jnp.dot(qs[h], kc_ref[0, h].astype(BF16), preferred_element_type=F32) for h in hs]
        s_n = [lax.dot_general(qs[h], k_new[:, head_cols[h]].astype(BF16), nt, preferred_element_type=F32)
               for h in hs]
        return s_c, s_n

    if mixed:
        x1_ref, oa_ref, oh_ref, wa_ref, wh_ref, g2_ref, b2_ref = refs[:n_x]
        mix = jnp.dot(oa_ref[...].astype(BF16), wa_ref[...], preferred_element_type=F32)
        mix = mix + jnp.dot(oh_ref[...].astype(BF16), wh_ref[...], preferred_element_type=F32)
        s_c, s_n = scores()
        x = _layer_norm(DN_ALPHA * x1_ref[...] + mix, g2_ref[...], b2_ref[...])
    else:
        x = refs[0][...]
    xb = x.astype(BF16)
    gate = jnp.dot(xb, wg_ref[...], preferred_element_type=F32)
    up = jnp.dot(xb, wu_ref[...], preferred_element_type=F32)
    if not mixed:
        s_c, s_n = scores()
    knt = [k_new[:, c].T for c in head_cols]
    vnt = [v_new[:, c].T for c in head_cols]

    hmid = (gate * _sigmoid(gate) * up).astype(BF16)
    y = jnp.dot(hmid, wd_ref[...], preferred_element_type=F32)
    y_ref[...] = _layer_norm(DN_ALPHA * x + 0.5 * y, g_ref[...], b_ref[...])

    p_c, p_n = [], []
    for h in hs:
        sc = jnp.where(cnt > 0, s_c[h], NEG_INF)
        sn = jnp.where(cnt_new > 0, s_n[h], NEG_INF)
        m = jnp.maximum(jnp.max(sc, axis=-1, keepdims=True), jnp.max(sn, axis=-1, keepdims=True))
        p_c.append((cnt * jnp.exp(sc - m)).astype(BF16))
        p_n.append((cnt_new * jnp.exp(sn - m)).astype(BF16))
    outs = []
    for h in hs:
        num = (lax.dot_general(vc_ref[0, h].astype(BF16), p_c[h], nt, preferred_element_type=F32)
               + lax.dot_general(vnt[h].astype(BF16), p_n[h], nt, preferred_element_type=F32))
        den = (lax.dot_general(ones, p_c[h], nt, preferred_element_type=F32)
               + lax.dot_general(ones_new, p_n[h], nt, preferred_element_type=F32))
        outs.append((num / den[0:1, :]).T)
    o_ref[0] = jnp.concatenate(outs, axis=1)

    for h in hs:
        for src_ref, new, dst in ((kc_ref, knt[h], nk_ref), (vc_ref, vnt[h], nv_ref)):
            rolled = pltpu.roll(src_ref[0, h], shift, 1)
            dst[0, h, :, 0:past - LANES] = rolled[:, 0:past - LANES]
            last = rolled[:, past - LANES:past]
            for s in range(n_new):
                col = jnp.broadcast_to(new[:, s:s + 1], last.shape)
                last = jnp.where(lane == LANES - n_new + s, col, last)
            dst[0, h, :, past - LANES:past] = last


def ffn_attn(x, wg, wu, wd, g, b, q, kn, vn, kc, vc, n_new, b_off, prev=None, mix=None):
    m = x.shape[0]
    bsz, n_heads, _, past = kc.shape
    halves = n_heads // FA_HEADS
    steps = m // FA_ROWS
    assert b_off % 2 == 0 and bsz % 2 == 0, "new-token rows are fetched per pair of batch rows"

    def samp(i):
        return (b_off + i // halves, i % halves, 0, 0)

    row = pl.BlockSpec((FA_ROWS, D_MODEL), lambda i: (i, 0))
    pair = pl.BlockSpec((2 * n_new, FA_HEADS * ATT_HEAD_DIM), lambda i: ((b_off + i // halves) // 2, i % halves))
    o_blk = pl.BlockSpec((1, n_new, FA_HEADS * ATT_HEAD_DIM), lambda i: (b_off + i // halves, 0, i % halves))
    big = pl.BlockSpec((1, FA_HEADS, ATT_HEAD_DIM, past), samp)
    in_specs, args = [row], [x]
    if mix is not None:
        half = pl.BlockSpec((FA_ROWS, ATT_WIDTH), lambda i: (i, 0))
        in_specs += [half, half, _const_spec((ATT_WIDTH, D_MODEL)), _const_spec((HG_WIDTH, D_MODEL)),
                     _const_spec((1, D_MODEL)), _const_spec((1, D_MODEL))]
        args += list(mix)
    in_specs += [_const_spec((D_MODEL, D_FF)), _const_spec((D_MODEL, D_FF)), _const_spec((D_FF, D_MODEL)),
                 _const_spec((1, D_MODEL)), _const_spec((1, D_MODEL)), pair, pair, pair, big, big]
    args += [wg, wu, wd, g, b, q, kn, vn, kc, vc]
    aliases = {}
    if prev is not None:
        for j, a in enumerate(prev):
            aliases[len(args)] = 1 + j
            in_specs.append(pl.BlockSpec(memory_space=pl.ANY))
            args.append(a)
    big_sds = jax.ShapeDtypeStruct(kc.shape, F32)
    return pl.pallas_call(
        functools.partial(_ffn_attn_body, n_new=n_new, past=past, b_off=b_off, mixed=mix is not None),
        grid=(steps,),
        in_specs=in_specs,
        out_specs=[row, o_blk, big, big],
        out_shape=[jax.ShapeDtypeStruct((m, D_MODEL), F32),
                   jax.ShapeDtypeStruct((bsz, n_new, n_heads * ATT_HEAD_DIM), F32), big_sds, big_sds],
        input_output_aliases=aliases,
        compiler_params=pltpu.CompilerParams(dimension_semantics=("arbitrary",),
                                             vmem_limit_bytes=VMEM_LIMIT),
        name="ffn_attn",
    )(*args)


HS_BATCH = 8


def _hgrn_sample_body(q_ref, k_ref, v_ref, lf_ref, gh_ref, ng_ref, s0_ref, o_ref, s_ref, *, n_new):
    row = lax.broadcasted_iota(jnp.int32, (n_new, 1), 0)
    tn = (((0,), (0,)), ((), ()))
    ones12 = jnp.ones((3 * n_new, HG_DV), BF16)

    sub = 8
    per_tile = sub // n_new

    def tile(t, carry):
        rows = pl.ds(pl.multiple_of(t * sub, sub), sub)
        units = []
        for h in range(HG_HEADS):
            cols = slice(h * LANES, (h + 1) * LANES)
            q8, k8, v8, g8 = q_ref[rows, cols], k_ref[rows, cols], v_ref[rows, cols], lf_ref[rows, cols]
            for i in range(per_tile):
                part = slice(i * n_new, (i + 1) * n_new)
                q, k, v, g = q8[part], k8[part], v8[part], g8[part]
                cum = jnp.zeros((n_new, LANES), F32)
                for j in range(n_new):
                    cum = cum + jnp.where(row >= j, g[j:j + 1, :], 0.0)
                cum_end = cum[n_new - 1:n_new, :]
                units.append(dict(h=h, b=t * per_tile + i, v=v, g=g, q_dec=q * jnp.exp(cum),
                                  k_dec=k * jnp.exp(-cum), k_end=k * jnp.exp(cum_end - cum)))
        for u in units:
            u["s0"] = s0_ref[u["b"], u["h"]]
            u["o"] = jnp.dot(u["q_dec"].astype(BF16), u["s0"].astype(BF16), preferred_element_type=F32)
        for u in units:
            dec = lax.dot_general(jnp.concatenate(_split3(u["g"]), axis=0), ones12, tn,
                                  preferred_element_type=F32)
            upd = lax.dot_general(u["k_end"].astype(BF16), u["v"].astype(BF16), tn,
                                  preferred_element_type=F32)
            s_ref[u["b"], u["h"]] = jnp.exp(dec) * u["s0"] + upd
        for h in range(HG_HEADS):
            cols = slice(h * LANES, (h + 1) * LANES)
            outs = []
            for u in units[h * per_tile:(h + 1) * per_tile]:
                o = u["o"]
                for j in range(n_new):
                    a_j = jnp.sum(u["q_dec"] * u["k_dec"][j:j + 1, :], axis=-1, keepdims=True)
                    o = o + jnp.where(row >= j, a_j, 0.0) * u["v"][j:j + 1, :]
                outs.append(o * lax.rsqrt(jnp.mean(o * o, axis=-1, keepdims=True) + LN_EPS))
            gate = gh_ref[rows, cols]
            o_ref[rows, cols] = jnp.concatenate(outs, axis=0) * ng_ref[:, cols] * (gate * _sigmoid(gate))
        return carry

    lax.fori_loop(0, HS_BATCH // per_tile, tile, 0)


def hgrn_sample(qh, kh, ih, lf, gh, norm_g, s0, n_new):
    bsz = s0.shape[0]
    blk = pl.BlockSpec((HS_BATCH * n_new, HG_WIDTH), lambda i: (i, 0))
    sblk = pl.BlockSpec((HS_BATCH, HG_HEADS, HG_DK, HG_DV), lambda i: (i, 0, 0, 0))
    return pl.pallas_call(
        functools.partial(_hgrn_sample_body, n_new=n_new),
        grid=(bsz // HS_BATCH,),
        in_specs=[blk, blk, blk, blk, blk, _const_spec((1, HG_WIDTH)), sblk],
        out_specs=[blk, sblk],
        out_shape=[jax.ShapeDtypeStruct((bsz * n_new, HG_WIDTH), F32),
                   jax.ShapeDtypeStruct(s0.shape, F32)],
        compiler_params=pltpu.CompilerParams(dimension_semantics=("arbitrary",),
                                             vmem_limit_bytes=VMEM_LIMIT),
        name="hgrn_sample",
    )(qh, kh, ih, lf, gh, norm_g, s0)


def _rope_tables(pos):
    half = ATT_HEAD_DIM // 2
    inv = ROPE_THETA ** (-jnp.arange(half, dtype=F32) / half)
    ang = pos.astype(F32)[:, None] * inv[None, :]
    cos, sin = jnp.cos(ang), jnp.sin(ang)
    return jnp.tile(cos, (1, 4)), jnp.tile(jnp.concatenate([-sin, sin], axis=1), (1, 2))


def kernel(x_prompt, x_sample, cache_k, cache_v, state_hgrn, ffn1_w_gate, ffn1_w_up, ffn1_w_down, ln1_g, ln1_b, w_in, hg_lower_bound, hg_norm_g, w_out, ln2_g, ln2_b, ffn2_w_gate, ffn2_w_up, ffn2_w_down, ln3_g, ln3_b):
    batch, seq, _ = x_prompt.shape
    dec_batch, n_new, _ = x_sample.shape
    assert ffn1_w_gate.shape[0] == 1, "single layer"
    win_p = min(ATT_WINDOW_MAX, seq)

    w1 = (ffn1_w_gate[0].astype(BF16), ffn1_w_up[0].astype(BF16), ffn1_w_down[0].astype(BF16))
    w2 = (ffn2_w_gate[0].astype(BF16), ffn2_w_up[0].astype(BF16), ffn2_w_down[0].astype(BF16))
    w_in_b = w_in[0].astype(BF16)
    w_att = w_out[0, :ATT_WIDTH].astype(BF16)
    w_hg = w_out[0, ATT_WIDTH:].astype(BF16)

    cos_p, sin_p = _rope_tables(jnp.arange(seq, dtype=jnp.int32))
    pos_s = PAST_LEN + (jnp.arange(dec_batch * n_new, dtype=jnp.int32) % n_new)
    cos_s, sin_s = _rope_tables(pos_s)

    xp = x_prompt.reshape(batch * seq, D_MODEL)
    xs = x_sample.reshape(dec_batch * n_new, D_MODEL)

    xs1 = ffn_ln(xs, *w1, ln1_g, ln1_b)
    qs, ks, vs, qhs, khs, ihs, lfs, ghs = proj(xs1, w_in_b, cos_s, sin_s, hg_lower_bound, 1)

    samp = (qs, ks, vs,
            cache_k[0].transpose(0, 2, 3, 1),
            cache_v[0].transpose(0, 2, 3, 1), n_new)
    steps = (batch * seq) // FA_ROWS
    per_call = steps // (ATT_HEADS // FA_HEADS)
    assert 2 * per_call == dec_batch, "the two prompt FFN calls must cover the sample batch"

    xp1, o_t, nk, nv = ffn_attn(xp, *w1, ln1_g, ln1_b, *samp, 0)
    q, k, v, qh, kh, ih, lf, gh = proj(xp1, w_in_b, cos_p, sin_p, hg_lower_bound, seq // ROW_TILE)
    o_att, kt_p, vt_p = attn_prompt(q, k, v, batch, seq)
    o_hg, s_p = hgrn_prompt(qh, kh, ih, lf, gh, hg_norm_g, batch, seq)
    yp, o_t, nk, nv = ffn_attn(xp1, *w2, ln3_g, ln3_b, *samp, per_call, prev=(o_t, nk, nv),
                               mix=(o_att, o_hg, w_att, w_hg, ln2_g, ln2_b))

    o_att_s = o_t.reshape(dec_batch * n_new, ATT_WIDTH)

    o_hg_s, s_s = hgrn_sample(qhs, khs, ihs, lfs, ghs, hg_norm_g, state_hgrn[0], n_new)
    xs2 = mix_ln(xs1, o_att_s, o_hg_s, w_att, w_hg, ln2_g, ln2_b)
    ys = ffn_ln(xs2, *w2, ln3_g, ln3_b)

    return (yp.reshape(batch, seq, D_MODEL),
            ys.reshape(dec_batch, n_new, D_MODEL),
            kt_p.transpose(0, 3, 1, 2)[None],
            vt_p.transpose(0, 3, 1, 2)[None],
            s_p[None],
            nk.transpose(0, 3, 1, 2)[None],
            nv.transpose(0, 3, 1, 2)[None],
            s_s[None])
```

```python
import functools

import jax
import jax.numpy as jnp
from jax import lax
from jax.experimental import pallas as pl
from jax.experimental.pallas import tpu as pltpu

F32 = jnp.float32
BF16 = jnp.bfloat16

D_MODEL = 1024
ATT_WIDTH = 512
HG_WIDTH = 512
ATT_HEAD_DIM = 64
ATT_HEADS = 8
DILATIONS = ((128, 1), (512, 4), (2048, 16))
ATT_WINDOW_MAX = 2048
ATT_BLOCK = 128
ROPE_THETA = 10000.0
HG_HEADS = 4
HG_DK = 128
HG_DV = 128
HG_CHUNK = 32
D_FF = 2816
IN_COLS = 3 * ATT_WIDTH + 4 * HG_WIDTH
PAST_LEN = 8192
DN_ALPHA = 2.0 ** 0.25
LN_EPS = 1e-5
NEG_INF = -1e30

LANES = 128
ROW_TILE = 512
VMEM_LIMIT = 56 * 1024 * 1024


def _const_spec(shape):
    return pl.BlockSpec(shape, lambda *_: (0,) * len(shape), pipeline_mode=pl.Buffered(1))


def _layer_norm(r, g, b):
    mu = jnp.mean(r, axis=-1, keepdims=True)
    c = r - mu
    var = jnp.mean(c * c, axis=-1, keepdims=True)
    return c * lax.rsqrt(var + LN_EPS) * g + b


def _sigmoid(x):
    return 1.0 / (1.0 + jnp.exp(-x))


def _split3(x):
    hi = x.astype(BF16)
    r1 = x - hi.astype(F32)
    mid = r1.astype(BF16)
    lo = (r1 - mid.astype(F32)).astype(BF16)
    return hi, mid, lo


def _ffn_ln_body(x_ref, wg_ref, wu_ref, wd_ref, g_ref, b_ref, o_ref):
    x = x_ref[...]
    xb = x.astype(BF16)
    gate = jnp.dot(xb, wg_ref[...], preferred_element_type=F32)
    up = jnp.dot(xb, wu_ref[...], preferred_element_type=F32)
    h = (gate * _sigmoid(gate) * up).astype(BF16)
    y = jnp.dot(h, wd_ref[...], preferred_element_type=F32)
    o_ref[...] = _layer_norm(DN_ALPHA * x + 0.5 * y, g_ref[...], b_ref[...])


def ffn_ln(x, wg, wu, wd, g, b):
    m = x.shape[0]
    row = pl.BlockSpec((ROW_TILE, D_MODEL), lambda i: (i, 0))
    return pl.pallas_call(
        _ffn_ln_body,
        grid=(m // ROW_TILE,),
        in_specs=[row, _const_spec((D_MODEL, D_FF)), _const_spec((D_MODEL, D_FF)),
                  _const_spec((D_FF, D_MODEL)), _const_spec((1, D_MODEL)), _const_spec((1, D_MODEL))],
        out_specs=row,
        out_shape=jax.ShapeDtypeStruct((m, D_MODEL), F32),
        compiler_params=pltpu.CompilerParams(dimension_semantics=("arbitrary",),
                                             vmem_limit_bytes=VMEM_LIMIT),
        name="ffn_ln",
    )(x, wg, wu, wd, g, b)


def _rope(z, cos_t, sin_t):
    lane = lax.broadcasted_iota(jnp.int32, (1, LANES), 1)
    first_half = (lane % ATT_HEAD_DIM) < (ATT_HEAD_DIM // 2)
    outs = []
    for c in range(z.shape[1] // LANES):
        zc = z[:, c * LANES:(c + 1) * LANES]
        partner = jnp.where(first_half, pltpu.roll(zc, LANES - 32, 1), pltpu.roll(zc, 32, 1))
        outs.append(zc * cos_t + partner * sin_t)
    return outs


def _proj_body(x_ref, w_ref, cos_ref, sin_ref, lbraw_ref,
               q_ref, k_ref, v_ref, qh_ref, kh_ref, ih_ref, lf_ref, gh_ref):
    xb = x_ref[...].astype(BF16)
    z = jnp.dot(xb, w_ref[...], preferred_element_type=F32)
    cos_t = cos_ref[...]
    sin_t = sin_ref[...]
    aw = ATT_WIDTH
    for c, val in enumerate(_rope(z[:, 0:aw], cos_t, sin_t)):
        q_ref[:, c * LANES:(c + 1) * LANES] = val * (ATT_HEAD_DIM ** -0.5)
    for c, val in enumerate(_rope(z[:, aw:2 * aw], cos_t, sin_t)):
        k_ref[:, c * LANES:(c + 1) * LANES] = val
    v_ref[...] = z[:, 2 * aw:3 * aw]
    base = 3 * aw
    zq = z[:, base:base + HG_WIDTH]
    zf = z[:, base + HG_WIDTH:base + 2 * HG_WIDTH]
    lbr = lbraw_ref[...]
    mx = jnp.max(lbr, axis=0, keepdims=True)
    ex = jnp.exp(lbr - mx)
    lb = ex[0:1, :] / jnp.sum(ex, axis=0, keepdims=True)
    sg = _sigmoid(zf)
    qh_ref[...] = zq * _sigmoid(zq)
    lf_ref[...] = jnp.log(lb + (1.0 - lb) * sg)
    kh_ref[...] = (1.0 - lb) * _sigmoid(-zf)
    ih_ref[...] = z[:, base + 2 * HG_WIDTH:base + 3 * HG_WIDTH]
    gh_ref[...] = z[:, base + 3 * HG_WIDTH:base + 4 * HG_WIDTH]


def proj(x, w_in, cos_t, sin_t, lb_raw, table_tiles):
    m = x.shape[0]
    row = pl.BlockSpec((ROW_TILE, D_MODEL), lambda i: (i, 0))
    tab = pl.BlockSpec((ROW_TILE, LANES), lambda i: (i % table_tiles, 0))
    out = pl.BlockSpec((ROW_TILE, ATT_WIDTH), lambda i: (i, 0))
    sds = jax.ShapeDtypeStruct((m, ATT_WIDTH), F32)
    return pl.pallas_call(
        _proj_body,
        grid=(m // ROW_TILE,),
        in_specs=[row, _const_spec((D_MODEL, IN_COLS)), tab, tab, _const_spec((2, HG_WIDTH))],
        out_specs=[out] * 8,
        out_shape=[sds] * 8,
        compiler_params=pltpu.CompilerParams(dimension_semantics=("arbitrary",),
                                             vmem_limit_bytes=VMEM_LIMIT),
        name="proj",
    )(x, w_in, cos_t, sin_t, lb_raw)


def _mix_ln_body(x_ref, oa_ref, oh_ref, wa_ref, wh_ref, g_ref, b_ref, o_ref):
    mix = jnp.dot(oa_ref[...].astype(BF16), wa_ref[...], preferred_element_type=F32)
    mix = mix + jnp.dot(oh_ref[...].astype(BF16), wh_ref[...], preferred_element_type=F32)
    o_ref[...] = _layer_norm(DN_ALPHA * x_ref[...] + mix, g_ref[...], b_ref[...])


def mix_ln(x, o_att, o_hg, w_att, w_hg, g, b):
    m = x.shape[0]
    row = pl.BlockSpec((ROW_TILE, D_MODEL), lambda i: (i, 0))
    half = pl.BlockSpec((ROW_TILE, ATT_WIDTH), lambda i: (i, 0))
    return pl.pallas_call(
        _mix_ln_body,
        grid=(m // ROW_TILE,),
        in_specs=[row, half, half, _const_spec((ATT_WIDTH, D_MODEL)), _const_spec((HG_WIDTH, D_MODEL)),
                  _const_spec((1, D_MODEL)), _const_spec((1, D_MODEL))],
        out_specs=row,
        out_shape=jax.ShapeDtypeStruct((m, D_MODEL), F32),
        compiler_params=pltpu.CompilerParams(dimension_semantics=("arbitrary",),
                                             vmem_limit_bytes=VMEM_LIMIT),
        name="mix_ln",
    )(x, o_att, o_hg, w_att, w_hg, g, b)


ATT_UNROLL = 4


def _attn_prompt_body(q_ref, k_ref, v_ref, o_ref, kt_ref, vt_ref, acc_ref, m_ref, l_ref, *, seq):
    blk = ATT_BLOCK
    lane = lax.broadcasted_iota(jnp.int32, (1, LANES), 1)
    head0 = lane < ATT_HEAD_DIM
    qi = lax.broadcasted_iota(jnp.int32, (blk, 2 * blk), 0)
    kc = lax.broadcasted_iota(jnp.int32, (blk, 2 * blk), 1)
    dist = qi + blk - kc
    first_mask = (lax.broadcasted_iota(jnp.int32, (blk, blk), 1)
                  <= lax.broadcasted_iota(jnp.int32, (blk, blk), 0))
    nt = (((1,), (1,)), ((), ()))
    log2e = 1.4426950408889634

    for br, (window, dil) in enumerate(DILATIONS):
        w_sub = window // dil
        rel = (dist >= 0) & (dist <= w_sub)
        span = blk * dil
        n_blk = seq // span
        run = min(ATT_UNROLL, n_blk)
        n_res = ATT_UNROLL // run
        runs_per_res = n_blk // run

        def units(it, carry, br=br, dil=dil, rel=rel, span=span, run=run, n_res=n_res,
                  runs_per_res=runs_per_res):
            starts, masks, vbs, scores = [], [], [], []
            for rr in range(n_res):
                if runs_per_res == 1:
                    r, n0 = it * n_res + rr, 0
                else:
                    r, n0 = it // runs_per_res, (it % runs_per_res) * run
                base = r + n0 * span

                def kv_block(j):
                    rows = pl.ds(jnp.maximum(base + j * span, r), blk, stride=dil)
                    return k_ref[rows, :].astype(BF16), v_ref[rows, :].astype(BF16)

                prev = None if runs_per_res == 1 else kv_block(-1)
                for j in range(run):
                    cur = kv_block(j)
                    start = base + j * span
                    q = q_ref[pl.ds(start, blk, stride=dil), :] * log2e
                    if prev is None:
                        kb, vb, mask = cur[0], cur[1], first_mask
                    else:
                        kb = jnp.concatenate([prev[0], cur[0]], axis=0)
                        vb = jnp.concatenate([prev[1], cur[1]], axis=0)
                        mask = rel & (kc >= jnp.where(n0 > 0, 0, blk)) if (j == 0 and runs_per_res > 1) else rel
                    starts.append(start)
                    masks.append(mask)
                    vbs.append(vb)
                    for hsel in (head0, ~head0):
                        qh = jnp.where(hsel, q, 0.0).astype(BF16)
                        scores.append(lax.dot_general(qh, kb, nt, preferred_element_type=F32))
                    prev = cur
            probs, ls, ms = [], [], []
            for i, s in enumerate(scores):
                s = jnp.where(masks[i // 2], s, NEG_INF)
                m = jnp.max(s, axis=-1, keepdims=True)
                p = jnp.exp2(s - m)
                ls.append(jnp.sum(p, axis=-1, keepdims=True))
                ms.append(m)
                probs.append(p.astype(BF16))
            outs = [jnp.dot(p, vbs[i // 2], preferred_element_type=F32) for i, p in enumerate(probs)]
            for uu in range(ATT_UNROLL):
                i0, i1 = 2 * uu, 2 * uu + 1
                rows = pl.ds(starts[uu], blk, stride=dil)
                acc_ref[br, rows, :] = jnp.where(head0, outs[i0], outs[i1])
                m_ref[br, rows, :] = jnp.broadcast_to(jnp.where(head0, ms[i0], ms[i1]), (blk, LANES))
                l_ref[br, rows, :] = jnp.broadcast_to(jnp.where(head0, ls[i0], ls[i1]), (blk, LANES))
            return carry

        lax.fori_loop(0, (dil * n_blk) // ATT_UNROLL, units, 0)

    def combine(c, carry):
        rows = pl.ds(pl.multiple_of(c * blk, blk), blk)
        m0, m1, m2 = m_ref[0, rows, :], m_ref[1, rows, :], m_ref[2, rows, :]
        mx = jnp.maximum(jnp.maximum(m0, m1), m2)
        w0, w1, w2 = jnp.exp2(m0 - mx), jnp.exp2(m1 - mx), jnp.exp2(m2 - mx)
        num = w0 * acc_ref[0, rows, :] + w1 * acc_ref[1, rows, :] + w2 * acc_ref[2, rows, :]
        den = w0 * l_ref[0, rows, :] + w1 * l_ref[1, rows, :] + w2 * l_ref[2, rows, :]
        o_ref[rows, :] = num / den
        return carry

    lax.fori_loop(0, seq // blk, combine, 0)

    win = kt_ref.shape[-1]
    for j in range(win // LANES):
        rows = pl.ds(seq - win + j * LANES, LANES)
        kt_ref[0, :, :, j * LANES:(j + 1) * LANES] = k_ref[rows, :].T.reshape(2, ATT_HEAD_DIM, LANES)
        vt_ref[0, :, :, j * LANES:(j + 1) * LANES] = v_ref[rows, :].T.reshape(2, ATT_HEAD_DIM, LANES)


def attn_prompt(q, k, v, batch, seq):
    win = min(ATT_WINDOW_MAX, seq)
    n_hp = ATT_WIDTH // LANES
    blk = pl.BlockSpec((seq, LANES), lambda b, h: (b, h))
    tblk = pl.BlockSpec((1, 2, ATT_HEAD_DIM, win), lambda b, h: (b, h, 0, 0))
    tsds = jax.ShapeDtypeStruct((batch, ATT_HEADS, ATT_HEAD_DIM, win), F32)
    return pl.pallas_call(
        functools.partial(_attn_prompt_body, seq=seq),
        grid=(batch, n_hp),
        in_specs=[blk, blk, blk],
        out_specs=[blk, tblk, tblk],
        out_shape=[jax.ShapeDtypeStruct((batch * seq, ATT_WIDTH), F32), tsds, tsds],
        scratch_shapes=[pltpu.VMEM((3, seq, LANES), F32)] * 3,
        compiler_params=pltpu.CompilerParams(dimension_semantics=("arbitrary", "arbitrary"),
                                             vmem_limit_bytes=VMEM_LIMIT),
        name="attn_prompt",
    )(q, k, v)


def _hgrn_prompt_body(q_ref, k_ref, v_ref, lf_ref, gh_ref, ng_ref, o_ref, s_ref, st_ref, *, rows_per_step):
    grp = LANES
    n_ch = grp // HG_CHUNK
    ri = lax.broadcasted_iota(jnp.int32, (grp, grp), 0)
    ci = lax.broadcasted_iota(jnp.int32, (grp, grp), 1)
    same_chunk = (ri // HG_CHUNK) == (ci // HG_CHUNK)
    causal = same_chunk & (ci <= ri)
    summat = jnp.concatenate([jnp.where(causal, 1.0, 0.0), jnp.where(same_chunk, 1.0, 0.0)],
                             axis=0).astype(BF16)
    nt = (((1,), (1,)), ((), ()))
    tb = pl.program_id(1)

    @pl.when(tb == 0)
    def _():
        st_ref[...] = jnp.zeros(st_ref.shape, F32)

    heads = [slice(h * LANES, (h + 1) * LANES) for h in range(HG_HEADS)]
    in_chunk = [(ri // HG_CHUNK) == c for c in range(n_ch)]

    def by_chunk(x):
        return jnp.concatenate([jnp.where(m, x, 0.0).astype(BF16) for m in in_chunk], axis=1)

    def group(gi, carry):
        rows = pl.ds(pl.multiple_of(gi * grp, grp), grp)
        parts = _split3(lf_ref[rows, :])
        sums = (jnp.dot(summat, parts[0], preferred_element_type=F32)
                + jnp.dot(summat, parts[1], preferred_element_type=F32)
                + jnp.dot(summat, parts[2], preferred_element_type=F32))
        cum = sums[0:grp]
        cum_end = sums[grp:2 * grp]
        q, k, v = q_ref[rows, :], k_ref[rows, :], v_ref[rows, :]
        q_dec = q * jnp.exp(cum)
        k_dec = (k * jnp.exp(-cum)).astype(BF16)
        k_end = k * jnp.exp(cum_end - cum)
        decay = jnp.exp(cum_end)
        q_decb = q_dec.astype(BF16)
        vb = v.astype(BF16)
        a = [lax.dot_general(q_decb[:, hs], k_dec[:, hs], nt, preferred_element_type=F32) for hs in heads]
        o = [jnp.dot(jnp.where(causal, a[h], 0.0).astype(BF16), vb[:, hs], preferred_element_type=F32)
             for h, hs in enumerate(heads)]
        ut = [jnp.dot(v[:, hs].T.astype(BF16), by_chunk(k_end[:, hs]), preferred_element_type=F32)
              for hs in heads]
        inter = []
        for h, hs in enumerate(heads):
            st = st_ref[h]
            starts = []
            for c in range(n_ch):
                starts.append(st.astype(BF16))
                st = st * decay[c * HG_CHUNK:c * HG_CHUNK + 1, hs] + ut[h][:, c * LANES:(c + 1) * LANES]
            st_ref[h] = st
            inter.append(lax.dot_general(by_chunk(q_dec[:, hs]), jnp.concatenate(starts, axis=1), nt,
                                         preferred_element_type=F32))
        for h, hs in enumerate(heads):
            oh = o[h] + inter[h]
            ms = jnp.mean(oh * oh, axis=-1, keepdims=True)
            gate = gh_ref[rows, hs]
            o_ref[rows, hs] = oh * lax.rsqrt(ms + LN_EPS) * ng_ref[:, hs] * (gate * _sigmoid(gate))
        return carry

    lax.fori_loop(0, rows_per_step // grp, group, 0)

    @pl.when(tb == pl.num_programs(1) - 1)
    def _():
        for h in range(HG_HEADS):
            s_ref[0, h] = st_ref[h].T


HG_ROWS = 1024


def hgrn_prompt(qh, kh, ih, lf, gh, norm_g, batch, seq):
    n_tb = seq // HG_ROWS
    blk = pl.BlockSpec((HG_ROWS, HG_WIDTH), lambda b, t: (b * n_tb + t, 0))
    return pl.pallas_call(
        functools.partial(_hgrn_prompt_body, rows_per_step=HG_ROWS),
        grid=(batch, n_tb),
        in_specs=[blk, blk, blk, blk, blk, _const_spec((1, HG_WIDTH))],
        out_specs=[blk, pl.BlockSpec((1, HG_HEADS, HG_DK, HG_DV), lambda b, t: (b, 0, 0, 0))],
        out_shape=[jax.ShapeDtypeStruct((batch * seq, HG_WIDTH), F32),
                   jax.ShapeDtypeStruct((batch, HG_HEADS, HG_DK, HG_DV), F32)],
        scratch_shapes=[pltpu.VMEM((HG_HEADS, HG_DV, HG_DK), F32)],
        compiler_params=pltpu.CompilerParams(dimension_semantics=("arbitrary", "arbitrary"),
                                             vmem_limit_bytes=VMEM_LIMIT),
        name="hgrn_prompt",
    )(qh, kh, ih, lf, gh, norm_g)


FA_ROWS = 128
FA_HEADS = 4
FA_SLOTS = 3


def _ffn_attn_body(*refs, n_new, past, b_off, mixed, n_steps):
    n_x = 7 if mixed else 1
    wg_ref, wu_ref, wd_ref, g_ref, b_ref, q_ref, kn_ref, vn_ref, kc_hbm, vc_hbm = refs[n_x:n_x + 10]
    y_ref, o_ref, nk_ref, nv_ref, kbuf, vbuf, sems = refs[-7:]
    halves = ATT_HEADS // FA_HEADS
    step = pl.program_id(0)
    odd = ((b_off + step // halves) % 2) == 1

    def window_copies(s):
        slot = s % FA_SLOTS
        src = (b_off + s // halves, pl.ds((s % halves) * FA_HEADS, FA_HEADS))
        return (pltpu.make_async_copy(kc_hbm.at[src], kbuf.at[slot], sems.at[slot, 0]),
                pltpu.make_async_copy(vc_hbm.at[src], vbuf.at[slot], sems.at[slot, 1]))

    @pl.when(step == 0)
    def _():
        for s in range(min(FA_SLOTS - 1, n_steps)):
            for c in window_copies(s):
                c.start()

    @pl.when(step + (FA_SLOTS - 1) < n_steps)
    def _():
        for c in window_copies(step + (FA_SLOTS - 1)):
            c.start()

    for c in window_copies(step):
        c.wait()
    slot = step % FA_SLOTS
    kc_ref, vc_ref = kbuf.at[slot], vbuf.at[slot]

    def new_rows(ref):
        both = ref[...]
        return jnp.where(odd, both[n_new:2 * n_new], both[0:n_new])

    q_new, k_new, v_new = new_rows(q_ref), new_rows(kn_ref), new_rows(vn_ref)
    head_cols = [slice(h * ATT_HEAD_DIM, (h + 1) * ATT_HEAD_DIM) for h in range(FA_HEADS)]
    srow = lax.broadcasted_iota(jnp.int32, (n_new, past), 0)
    tcol = lax.broadcasted_iota(jnp.int32, (n_new, past), 1)
    dist = past + srow - tcol
    cnt = jnp.zeros((n_new, past), F32)
    for window, dil in DILATIONS:
        cnt = cnt + jnp.where((dist % dil == 0) & (dist <= window), 1.0, 0.0)
    si = lax.broadcasted_iota(jnp.int32, (n_new, n_new), 0)
    sj = lax.broadcasted_iota(jnp.int32, (n_new, n_new), 1)
    dn = si - sj
    cnt_new = jnp.zeros((n_new, n_new), F32)
    for window, dil in DILATIONS:
        cnt_new = cnt_new + jnp.where((dn >= 0) & (dn % dil == 0) & (dn <= window), 1.0, 0.0)
    lane = lax.broadcasted_iota(jnp.int32, (1, LANES), 1)
    ones = jnp.ones((8, past), BF16)
    ones_new = jnp.ones((8, n_new), BF16)
    nt = (((1,), (1,)), ((), ()))
    shift = past - n_new
    hs = range(FA_HEADS)

    def scores():
        qs = [q_new[:, c].astype(BF16) for c in head_cols]
        s_c = [jnp.dot(qs[h], kc_ref[h].astype(BF16), preferred_element_type=F32) for h in hs]
        s_n = [lax.dot_general(qs[h], k_new[:, head_cols[h]].astype(BF16), nt, preferred_element_type=F32)
               for h in hs]
        return s_c, s_n

    if mixed:
        x1_ref, oa_ref, oh_ref, wa_ref, wh_ref, g2_ref, b2_ref = refs[:n_x]
        mix = jnp.dot(oa_ref[...].astype(BF16), wa_ref[...], preferred_element_type=F32)
        mix = mix + jnp.dot(oh_ref[...].astype(BF16), wh_ref[...], preferred_element_type=F32)
        s_c, s_n = scores()
        x = _layer_norm(DN_ALPHA * x1_ref[...] + mix, g2_ref[...], b2_ref[...])
    else:
        x = refs[0][...]
    xb = x.astype(BF16)
    gate = jnp.dot(xb, wg_ref[...], preferred_element_type=F32)
    up = jnp.dot(xb, wu_ref[...], preferred_element_type=F32)
    if not mixed:
        s_c, s_n = scores()
    knt = [k_new[:, c].T for c in head_cols]
    vnt = [v_new[:, c].T for c in head_cols]

    hmid = (gate * _sigmoid(gate) * up).astype(BF16)
    y = jnp.dot(hmid, wd_ref[...], preferred_element_type=F32)
    y_ref[...] = _layer_norm(DN_ALPHA * x + 0.5 * y, g_ref[...], b_ref[...])

    p_c, p_n = [], []
    for h in hs:
        sc = jnp.where(cnt > 0, s_c[h], NEG_INF)
        sn = jnp.where(cnt_new > 0, s_n[h], NEG_INF)
        m = jnp.maximum(jnp.max(sc, axis=-1, keepdims=True), jnp.max(sn, axis=-1, keepdims=True))
        p_c.append((cnt * jnp.exp(sc - m)).astype(BF16))
        p_n.append((cnt_new * jnp.exp(sn - m)).astype(BF16))
    outs = []
    for h in hs:
        num = (lax.dot_general(vc_ref[h].astype(BF16), p_c[h], nt, preferred_element_type=F32)
               + lax.dot_general(vnt[h].astype(BF16), p_n[h], nt, preferred_element_type=F32))
        den = (lax.dot_general(ones, p_c[h], nt, preferred_element_type=F32)
               + lax.dot_general(ones_new, p_n[h], nt, preferred_element_type=F32))
        outs.append((num / den[0:1, :]).T)
    o_ref[0] = jnp.concatenate(outs, axis=1)

    for h in hs:
        for src_ref, new, dst in ((kc_ref, knt[h], nk_ref), (vc_ref, vnt[h], nv_ref)):
            rolled = pltpu.roll(src_ref[h], shift, 1)
            dst[0, h, :, 0:past - LANES] = rolled[:, 0:past - LANES]
            last = rolled[:, past - LANES:past]
            for s in range(n_new):
                col = jnp.broadcast_to(new[:, s:s + 1], last.shape)
                last = jnp.where(lane == LANES - n_new + s, col, last)
            dst[0, h, :, past - LANES:past] = last


def ffn_attn(x, wg, wu, wd, g, b, q, kn, vn, kc, vc, n_new, b_off, prev=None, mix=None):
    m = x.shape[0]
    bsz, n_heads, _, past = kc.shape
    halves = n_heads // FA_HEADS
    steps = m // FA_ROWS
    assert b_off % 2 == 0 and bsz % 2 == 0, "new-token rows are fetched per pair of batch rows"

    def samp(i):
        return (b_off + i // halves, i % halves, 0, 0)

    row = pl.BlockSpec((FA_ROWS, D_MODEL), lambda i: (i, 0))
    pair = pl.BlockSpec((2 * n_new, FA_HEADS * ATT_HEAD_DIM), lambda i: ((b_off + i // halves) // 2, i % halves))
    o_blk = pl.BlockSpec((1, n_new, FA_HEADS * ATT_HEAD_DIM), lambda i: (b_off + i // halves, 0, i % halves))
    big = pl.BlockSpec((1, FA_HEADS, ATT_HEAD_DIM, past), samp)
    in_specs, args = [row], [x]
    if mix is not None:
        half = pl.BlockSpec((FA_ROWS, ATT_WIDTH), lambda i: (i, 0))
        in_specs += [half, half, _const_spec((ATT_WIDTH, D_MODEL)), _const_spec((HG_WIDTH, D_MODEL)),
                     _const_spec((1, D_MODEL)), _const_spec((1, D_MODEL))]
        args += list(mix)
    hbm = pl.BlockSpec(memory_space=pl.ANY)
    in_specs += [_const_spec((D_MODEL, D_FF)), _const_spec((D_MODEL, D_FF)), _const_spec((D_FF, D_MODEL)),
                 _const_spec((1, D_MODEL)), _const_spec((1, D_MODEL)), pair, pair, pair, hbm, hbm]
    args += [wg, wu, wd, g, b, q, kn, vn, kc, vc]
    aliases = {}
    if prev is not None:
        for j, a in enumerate(prev):
            aliases[len(args)] = 1 + j
            in_specs.append(pl.BlockSpec(memory_space=pl.ANY))
            args.append(a)
    big_sds = jax.ShapeDtypeStruct(kc.shape, F32)
    return pl.pallas_call(
        functools.partial(_ffn_attn_body, n_new=n_new, past=past, b_off=b_off, mixed=mix is not None,
                          n_steps=steps),
        grid=(steps,),
        in_specs=in_specs,
        out_specs=[row, o_blk, big, big],
        out_shape=[jax.ShapeDtypeStruct((m, D_MODEL), F32),
                   jax.ShapeDtypeStruct((bsz, n_new, n_heads * ATT_HEAD_DIM), F32), big_sds, big_sds],
        scratch_shapes=[pltpu.VMEM((FA_SLOTS, FA_HEADS, ATT_HEAD_DIM, past), F32),
                        pltpu.VMEM((FA_SLOTS, FA_HEADS, ATT_HEAD_DIM, past), F32),
                        pltpu.SemaphoreType.DMA((FA_SLOTS, 2))],
        input_output_aliases=aliases,
        compiler_params=pltpu.CompilerParams(dimension_semantics=("arbitrary",),
                                             vmem_limit_bytes=VMEM_LIMIT),
        name="ffn_attn",
    )(*args)


HS_BATCH = 8


def _hgrn_sample_body(q_ref, k_ref, v_ref, lf_ref, gh_ref, ng_ref, s0_ref, o_ref, s_ref, *, n_new):
    row = lax.broadcasted_iota(jnp.int32, (n_new, 1), 0)
    tn = (((0,), (0,)), ((), ()))
    ones12 = jnp.ones((3 * n_new, HG_DV), BF16)

    sub = 8
    per_tile = sub // n_new

    def tile(t, carry):
        rows = pl.ds(pl.multiple_of(t * sub, sub), sub)
        units = []
        for h in range(HG_HEADS):
            cols = slice(h * LANES, (h + 1) * LANES)
            q8, k8, v8, g8 = q_ref[rows, cols], k_ref[rows, cols], v_ref[rows, cols], lf_ref[rows, cols]
            for i in range(per_tile):
                part = slice(i * n_new, (i + 1) * n_new)
                q, k, v, g = q8[part], k8[part], v8[part], g8[part]
                cum = jnp.zeros((n_new, LANES), F32)
                for j in range(n_new):
                    cum = cum + jnp.where(row >= j, g[j:j + 1, :], 0.0)
                cum_end = cum[n_new - 1:n_new, :]
                units.append(dict(h=h, b=t * per_tile + i, v=v, g=g, q_dec=q * jnp.exp(cum),
                                  k_dec=k * jnp.exp(-cum), k_end=k * jnp.exp(cum_end - cum)))
        for u in units:
            u["s0"] = s0_ref[u["b"], u["h"]]
            u["o"] = jnp.dot(u["q_dec"].astype(BF16), u["s0"].astype(BF16), preferred_element_type=F32)
        for u in units:
            dec = lax.dot_general(jnp.concatenate(_split3(u["g"]), axis=0), ones12, tn,
                                  preferred_element_type=F32)
            upd = lax.dot_general(u["k_end"].astype(BF16), u["v"].astype(BF16), tn,
                                  preferred_element_type=F32)
            s_ref[u["b"], u["h"]] = jnp.exp(dec) * u["s0"] + upd
        for h in range(HG_HEADS):
            cols = slice(h * LANES, (h + 1) * LANES)
            outs = []
            for u in units[h * per_tile:(h + 1) * per_tile]:
                o = u["o"]
                for j in range(n_new):
                    a_j = jnp.sum(u["q_dec"] * u["k_dec"][j:j + 1, :], axis=-1, keepdims=True)
                    o = o + jnp.where(row >= j, a_j, 0.0) * u["v"][j:j + 1, :]
                outs.append(o * lax.rsqrt(jnp.mean(o * o, axis=-1, keepdims=True) + LN_EPS))
            gate = gh_ref[rows, cols]
            o_ref[rows, cols] = jnp.concatenate(outs, axis=0) * ng_ref[:, cols] * (gate * _sigmoid(gate))
        return carry

    lax.fori_loop(0, HS_BATCH // per_tile, tile, 0)


def hgrn_sample(qh, kh, ih, lf, gh, norm_g, s0, n_new):
    bsz = s0.shape[0]
    blk = pl.BlockSpec((HS_BATCH * n_new, HG_WIDTH), lambda i: (i, 0))
    sblk = pl.BlockSpec((HS_BATCH, HG_HEADS, HG_DK, HG_DV), lambda i: (i, 0, 0, 0))
    return pl.pallas_call(
        functools.partial(_hgrn_sample_body, n_new=n_new),
        grid=(bsz // HS_BATCH,),
        in_specs=[blk, blk, blk, blk, blk, _const_spec((1, HG_WIDTH)), sblk],
        out_specs=[blk, sblk],
        out_shape=[jax.ShapeDtypeStruct((bsz * n_new, HG_WIDTH), F32),
                   jax.ShapeDtypeStruct(s0.shape, F32)],
        compiler_params=pltpu.CompilerParams(dimension_semantics=("arbitrary",),
                                             vmem_limit_bytes=VMEM_LIMIT),
        name="hgrn_sample",
    )(qh, kh, ih, lf, gh, norm_g, s0)


def _rope_tables(pos):
    half = ATT_HEAD_DIM // 2
    inv = ROPE_THETA ** (-jnp.arange(half, dtype=F32) / half)
    ang = pos.astype(F32)[:, None] * inv[None, :]
    cos, sin = jnp.cos(ang), jnp.sin(ang)
    return jnp.tile(cos, (1, 4)), jnp.tile(jnp.concatenate([-sin, sin], axis=1), (1, 2))


def kernel(x_prompt, x_sample, cache_k, cache_v, state_hgrn, ffn1_w_gate, ffn1_w_up, ffn1_w_down, ln1_g, ln1_b, w_in, hg_lower_bound, hg_norm_g, w_out, ln2_g, ln2_b, ffn2_w_gate, ffn2_w_up, ffn2_w_down, ln3_g, ln3_b):
    batch, seq, _ = x_prompt.shape
    dec_batch, n_new, _ = x_sample.shape
    assert ffn1_w_gate.shape[0] == 1, "single layer"
    win_p = min(ATT_WINDOW_MAX, seq)

    w1 = (ffn1_w_gate[0].astype(BF16), ffn1_w_up[0].astype(BF16), ffn1_w_down[0].astype(BF16))
    w2 = (ffn2_w_gate[0].astype(BF16), ffn2_w_up[0].astype(BF16), ffn2_w_down[0].astype(BF16))
    w_in_b = w_in[0].astype(BF16)
    w_att = w_out[0, :ATT_WIDTH].astype(BF16)
    w_hg = w_out[0, ATT_WIDTH:].astype(BF16)

    cos_p, sin_p = _rope_tables(jnp.arange(seq, dtype=jnp.int32))
    pos_s = PAST_LEN + (jnp.arange(dec_batch * n_new, dtype=jnp.int32) % n_new)
    cos_s, sin_s = _rope_tables(pos_s)

    xp = x_prompt.reshape(batch * seq, D_MODEL)
    xs = x_sample.reshape(dec_batch * n_new, D_MODEL)

    xs1 = ffn_ln(xs, *w1, ln1_g, ln1_b)
    qs, ks, vs, qhs, khs, ihs, lfs, ghs = proj(xs1, w_in_b, cos_s, sin_s, hg_lower_bound, 1)

    samp = (qs, ks, vs,
            cache_k[0].transpose(0, 2, 3, 1),
            cache_v[0].transpose(0, 2, 3, 1), n_new)
    steps = (batch * seq) // FA_ROWS
    per_call = steps // (ATT_HEADS // FA_HEADS)
    assert 2 * per_call == dec_batch, "the two prompt FFN calls must cover the sample batch"

    xp1, o_t, nk, nv = ffn_attn(xp, *w1, ln1_g, ln1_b, *samp, 0)
    q, k, v, qh, kh, ih, lf, gh = proj(xp1, w_in_b, cos_p, sin_p, hg_lower_bound, seq // ROW_TILE)
    o_att, kt_p, vt_p = attn_prompt(q, k, v, batch, seq)
    o_hg, s_p = hgrn_prompt(qh, kh, ih, lf, gh, hg_norm_g, batch, seq)
    yp, o_t, nk, nv = ffn_attn(xp1, *w2, ln3_g, ln3_b, *samp, per_call, prev=(o_t, nk, nv),
                               mix=(o_att, o_hg, w_att, w_hg, ln2_g, ln2_b))

    o_att_s = o_t.reshape(dec_batch * n_new, ATT_WIDTH)

    o_hg_s, s_s = hgrn_sample(qhs, khs, ihs, lfs, ghs, hg_norm_g, state_hgrn[0], n_new)
    xs2 = mix_ln(xs1, o_att_s, o_hg_s, w_att, w_hg, ln2_g, ln2_b)
    ys = ffn_ln(xs2, *w2, ln3_g, ln3_b)

    return (yp.reshape(batch, seq, D_MODEL),
            ys.reshape(dec_batch, n_new, D_MODEL),
            kt_p.transpose(0, 3, 1, 2)[None],
            vt_p.transpose(0, 3, 1, 2)[None],
            s_p[None],
            nk.transpose(0, 3, 1, 2)[None],
            nv.transpose(0, 3, 1, 2)[None],
            s_s[None])
```

```python
import functools

import jax
import jax.numpy as jnp
from jax import lax
from jax.experimental import pallas as pl
from jax.experimental.pallas import tpu as pltpu

F32 = jnp.float32
BF16 = jnp.bfloat16

D_MODEL = 1024
ATT_WIDTH = 512
HG_WIDTH = 512
ATT_HEAD_DIM = 64
ATT_HEADS = 8
DILATIONS = ((128, 1), (512, 4), (2048, 16))
ATT_WINDOW_MAX = 2048
ATT_BLOCK = 128
ROPE_THETA = 10000.0
HG_HEADS = 4
HG_DK = 128
HG_DV = 128
HG_CHUNK = 32
D_FF = 2816
IN_COLS = 3 * ATT_WIDTH + 4 * HG_WIDTH
PAST_LEN = 8192
DN_ALPHA = 2.0 ** 0.25
LN_EPS = 1e-5
NEG_INF = -1e30

LANES = 128
ROW_TILE = 512
VMEM_LIMIT = 56 * 1024 * 1024


def _const_spec(shape):
    return pl.BlockSpec(shape, lambda *_: (0,) * len(shape), pipeline_mode=pl.Buffered(1))


def _layer_norm(r, g, b):
    mu = jnp.mean(r, axis=-1, keepdims=True)
    c = r - mu
    var = jnp.mean(c * c, axis=-1, keepdims=True)
    return c * lax.rsqrt(var + LN_EPS) * g + b


def _sigmoid(x):
    return 1.0 / (1.0 + jnp.exp(-x))


def _split3(x):
    hi = x.astype(BF16)
    r1 = x - hi.astype(F32)
    mid = r1.astype(BF16)
    lo = (r1 - mid.astype(F32)).astype(BF16)
    return hi, mid, lo


def _ffn_ln_body(x_ref, wg_ref, wu_ref, wd_ref, g_ref, b_ref, o_ref):
    x = x_ref[...]
    xb = x.astype(BF16)
    gate = jnp.dot(xb, wg_ref[...], preferred_element_type=F32)
    up = jnp.dot(xb, wu_ref[...], preferred_element_type=F32)
    h = (gate * _sigmoid(gate) * up).astype(BF16)
    y = jnp.dot(h, wd_ref[...], preferred_element_type=F32)
    o_ref[...] = _layer_norm(DN_ALPHA * x + 0.5 * y, g_ref[...], b_ref[...])


def ffn_ln(x, wg, wu, wd, g, b):
    m = x.shape[0]
    row = pl.BlockSpec((ROW_TILE, D_MODEL), lambda i: (i, 0))
    return pl.pallas_call(
        _ffn_ln_body,
        grid=(m // ROW_TILE,),
        in_specs=[row, _const_spec((D_MODEL, D_FF)), _const_spec((D_MODEL, D_FF)),
                  _const_spec((D_FF, D_MODEL)), _const_spec((1, D_MODEL)), _const_spec((1, D_MODEL))],
        out_specs=row,
        out_shape=jax.ShapeDtypeStruct((m, D_MODEL), F32),
        compiler_params=pltpu.CompilerParams(dimension_semantics=("arbitrary",),
                                             vmem_limit_bytes=VMEM_LIMIT),
        name="ffn_ln",
    )(x, wg, wu, wd, g, b)


def _rope(z, cos_t, sin_t):
    lane = lax.broadcasted_iota(jnp.int32, (1, LANES), 1)
    first_half = (lane % ATT_HEAD_DIM) < (ATT_HEAD_DIM // 2)
    outs = []
    for c in range(z.shape[1] // LANES):
        zc = z[:, c * LANES:(c + 1) * LANES]
        partner = jnp.where(first_half, pltpu.roll(zc, LANES - 32, 1), pltpu.roll(zc, 32, 1))
        outs.append(zc * cos_t + partner * sin_t)
    return outs


def _proj_body(x_ref, w_ref, cos_ref, sin_ref, lbraw_ref,
               q_ref, k_ref, v_ref, qh_ref, kh_ref, ih_ref, lf_ref, gh_ref):
    xb = x_ref[...].astype(BF16)
    z = jnp.dot(xb, w_ref[...], preferred_element_type=F32)
    cos_t = cos_ref[...]
    sin_t = sin_ref[...]
    aw = ATT_WIDTH
    for c, val in enumerate(_rope(z[:, 0:aw], cos_t, sin_t)):
        q_ref[:, c * LANES:(c + 1) * LANES] = val * (ATT_HEAD_DIM ** -0.5)
    for c, val in enumerate(_rope(z[:, aw:2 * aw], cos_t, sin_t)):
        k_ref[:, c * LANES:(c + 1) * LANES] = val
    v_ref[...] = z[:, 2 * aw:3 * aw]
    base = 3 * aw
    zq = z[:, base:base + HG_WIDTH]
    zf = z[:, base + HG_WIDTH:base + 2 * HG_WIDTH]
    lbr = lbraw_ref[...]
    mx = jnp.max(lbr, axis=0, keepdims=True)
    ex = jnp.exp(lbr - mx)
    lb = ex[0:1, :] / jnp.sum(ex, axis=0, keepdims=True)
    sg = _sigmoid(zf)
    qh_ref[...] = zq * _sigmoid(zq)
    lf_ref[...] = jnp.log(lb + (1.0 - lb) * sg)
    kh_ref[...] = (1.0 - lb) * _sigmoid(-zf)
    ih_ref[...] = z[:, base + 2 * HG_WIDTH:base + 3 * HG_WIDTH]
    gh_ref[...] = z[:, base + 3 * HG_WIDTH:base + 4 * HG_WIDTH]


def proj(x, w_in, cos_t, sin_t, lb_raw, table_tiles):
    m = x.shape[0]
    row = pl.BlockSpec((ROW_TILE, D_MODEL), lambda i: (i, 0))
    tab = pl.BlockSpec((ROW_TILE, LANES), lambda i: (i % table_tiles, 0))
    out = pl.BlockSpec((ROW_TILE, ATT_WIDTH), lambda i: (i, 0))
    sds = jax.ShapeDtypeStruct((m, ATT_WIDTH), F32)
    return pl.pallas_call(
        _proj_body,
        grid=(m // ROW_TILE,),
        in_specs=[row, _const_spec((D_MODEL, IN_COLS)), tab, tab, _const_spec((2, HG_WIDTH))],
        out_specs=[out] * 8,
        out_shape=[sds] * 8,
        compiler_params=pltpu.CompilerParams(dimension_semantics=("arbitrary",),
                                             vmem_limit_bytes=VMEM_LIMIT),
        name="proj",
    )(x, w_in, cos_t, sin_t, lb_raw)


def _mix_ln_body(x_ref, oa_ref, oh_ref, wa_ref, wh_ref, g_ref, b_ref, o_ref):
    mix = jnp.dot(oa_ref[...].astype(BF16), wa_ref[...], preferred_element_type=F32)
    mix = mix + jnp.dot(oh_ref[...].astype(BF16), wh_ref[...], preferred_element_type=F32)
    o_ref[...] = _layer_norm(DN_ALPHA * x_ref[...] + mix, g_ref[...], b_ref[...])


def mix_ln(x, o_att, o_hg, w_att, w_hg, g, b):
    m = x.shape[0]
    row = pl.BlockSpec((ROW_TILE, D_MODEL), lambda i: (i, 0))
    half = pl.BlockSpec((ROW_TILE, ATT_WIDTH), lambda i: (i, 0))
    return pl.pallas_call(
        _mix_ln_body,
        grid=(m // ROW_TILE,),
        in_specs=[row, half, half, _const_spec((ATT_WIDTH, D_MODEL)), _const_spec((HG_WIDTH, D_MODEL)),
                  _const_spec((1, D_MODEL)), _const_spec((1, D_MODEL))],
        out_specs=row,
        out_shape=jax.ShapeDtypeStruct((m, D_MODEL), F32),
        compiler_params=pltpu.CompilerParams(dimension_semantics=("arbitrary",),
                                             vmem_limit_bytes=VMEM_LIMIT),
        name="mix_ln",
    )(x, o_att, o_hg, w_att, w_hg, g, b)


ATT_UNROLL = (8, 8, 4)


def _attn_prompt_body(q_ref, k_ref, v_ref, o_ref, kt_ref, vt_ref, acc_ref, m_ref, l_ref, *, seq):
    blk = ATT_BLOCK
    lane = lax.broadcasted_iota(jnp.int32, (1, LANES), 1)
    head0 = lane < ATT_HEAD_DIM
    qi = lax.broadcasted_iota(jnp.int32, (blk, 2 * blk), 0)
    kc = lax.broadcasted_iota(jnp.int32, (blk, 2 * blk), 1)
    dist = qi + blk - kc
    first_mask = (lax.broadcasted_iota(jnp.int32, (blk, blk), 1)
                  <= lax.broadcasted_iota(jnp.int32, (blk, blk), 0))
    nt = (((1,), (1,)), ((), ()))
    log2e = 1.4426950408889634

    for br, (window, dil) in enumerate(DILATIONS):
        w_sub = window // dil
        rel = (dist >= 0) & (dist <= w_sub)
        span = blk * dil
        n_blk = seq // span
        n_units = ATT_UNROLL[br]
        run = min(n_units, n_blk)
        n_res = n_units // run
        runs_per_res = n_blk // run

        def units(it, carry, br=br, dil=dil, rel=rel, span=span, run=run, n_res=n_res,
                  runs_per_res=runs_per_res, n_units=n_units):
            starts, masks, vbs, scores = [], [], [], []
            for rr in range(n_res):
                if runs_per_res == 1:
                    r, n0 = it * n_res + rr, 0
                else:
                    r, n0 = it // runs_per_res, (it % runs_per_res) * run
                base = r + n0 * span

                def kv_block(j):
                    rows = pl.ds(jnp.maximum(base + j * span, r), blk, stride=dil)
                    return k_ref[rows, :].astype(BF16), v_ref[rows, :].astype(BF16)

                prev = None if runs_per_res == 1 else kv_block(-1)
                for j in range(run):
                    cur = kv_block(j)
                    start = base + j * span
                    q = q_ref[pl.ds(start, blk, stride=dil), :] * log2e
                    if prev is None:
                        kb, vb, mask = cur[0], cur[1], first_mask
                    else:
                        kb = jnp.concatenate([prev[0], cur[0]], axis=0)
                        vb = jnp.concatenate([prev[1], cur[1]], axis=0)
                        mask = rel & (kc >= jnp.where(n0 > 0, 0, blk)) if (j == 0 and runs_per_res > 1) else rel
                    starts.append(start)
                    masks.append(mask)
                    vbs.append(vb)
                    for hsel in (head0, ~head0):
                        qh = jnp.where(hsel, q, 0.0).astype(BF16)
                        scores.append(lax.dot_general(qh, kb, nt, preferred_element_type=F32))
                    prev = cur
            probs, ls, ms = [], [], []
            for i, s in enumerate(scores):
                s = jnp.where(masks[i // 2], s, NEG_INF)
                m = jnp.max(s, axis=-1, keepdims=True)
                p = jnp.exp2(s - m)
                ls.append(jnp.sum(p, axis=-1, keepdims=True))
                ms.append(m)
                probs.append(p.astype(BF16))
            outs = [jnp.dot(p, vbs[i // 2], preferred_element_type=F32) for i, p in enumerate(probs)]
            for uu in range(n_units):
                i0, i1 = 2 * uu, 2 * uu + 1
                rows = pl.ds(starts[uu], blk, stride=dil)
                acc_ref[br, rows, :] = jnp.where(head0, outs[i0], outs[i1])
                m_ref[br, rows, :] = jnp.broadcast_to(jnp.where(head0, ms[i0], ms[i1]), (blk, LANES))
                l_ref[br, rows, :] = jnp.broadcast_to(jnp.where(head0, ls[i0], ls[i1]), (blk, LANES))
            return carry

        lax.fori_loop(0, (dil * n_blk) // n_units, units, 0)

    def combine(c, carry):
        rows = pl.ds(pl.multiple_of(c * blk, blk), blk)
        m0, m1, m2 = m_ref[0, rows, :], m_ref[1, rows, :], m_ref[2, rows, :]
        mx = jnp.maximum(jnp.maximum(m0, m1), m2)
        w0, w1, w2 = jnp.exp2(m0 - mx), jnp.exp2(m1 - mx), jnp.exp2(m2 - mx)
        num = w0 * acc_ref[0, rows, :] + w1 * acc_ref[1, rows, :] + w2 * acc_ref[2, rows, :]
        den = w0 * l_ref[0, rows, :] + w1 * l_ref[1, rows, :] + w2 * l_ref[2, rows, :]
        o_ref[rows, :] = num / den
        return carry

    lax.fori_loop(0, seq // blk, combine, 0)

    win = kt_ref.shape[-1]
    for j in range(win // LANES):
        rows = pl.ds(seq - win + j * LANES, LANES)
        kt_ref[0, :, :, j * LANES:(j + 1) * LANES] = k_ref[rows, :].T.reshape(2, ATT_HEAD_DIM, LANES)
        vt_ref[0, :, :, j * LANES:(j + 1) * LANES] = v_ref[rows, :].T.reshape(2, ATT_HEAD_DIM, LANES)


def attn_prompt(q, k, v, batch, seq):
    win = min(ATT_WINDOW_MAX, seq)
    n_hp = ATT_WIDTH // LANES
    blk = pl.BlockSpec((seq, LANES), lambda b, h: (b, h))
    tblk = pl.BlockSpec((1, 2, ATT_HEAD_DIM, win), lambda b, h: (b, h, 0, 0))
    tsds = jax.ShapeDtypeStruct((batch, ATT_HEADS, ATT_HEAD_DIM, win), F32)
    return pl.pallas_call(
        functools.partial(_attn_prompt_body, seq=seq),
        grid=(batch, n_hp),
        in_specs=[blk, blk, blk],
        out_specs=[blk, tblk, tblk],
        out_shape=[jax.ShapeDtypeStruct((batch * seq, ATT_WIDTH), F32), tsds, tsds],
        scratch_shapes=[pltpu.VMEM((3, seq, LANES), F32)] * 3,
        compiler_params=pltpu.CompilerParams(dimension_semantics=("arbitrary", "arbitrary"),
                                             vmem_limit_bytes=VMEM_LIMIT),
        name="attn_prompt",
    )(q, k, v)


def _hgrn_prompt_body(q_ref, k_ref, v_ref, lf_ref, gh_ref, ng_ref, o_ref, s_ref, st_ref, *, rows_per_step):
    grp = LANES
    n_ch = grp // HG_CHUNK
    ri = lax.broadcasted_iota(jnp.int32, (grp, grp), 0)
    ci = lax.broadcasted_iota(jnp.int32, (grp, grp), 1)
    same_chunk = (ri // HG_CHUNK) == (ci // HG_CHUNK)
    causal = same_chunk & (ci <= ri)
    summat = jnp.where(causal, 1.0, 0.0).astype(BF16)
    nt = (((1,), (1,)), ((), ()))
    tb = pl.program_id(1)

    @pl.when(tb == 0)
    def _():
        st_ref[...] = jnp.zeros(st_ref.shape, F32)

    heads = [slice(h * LANES, (h + 1) * LANES) for h in range(HG_HEADS)]
    in_chunk = [(ri // HG_CHUNK) == c for c in range(n_ch)]

    def by_chunk(x):
        return jnp.concatenate([jnp.where(m, x, 0.0).astype(BF16) for m in in_chunk], axis=1)

    def groups(it, carry):
        gs = range(HG_GROUPS)
        rows = [pl.ds(pl.multiple_of((it * HG_GROUPS + g) * grp, grp), grp) for g in gs]
        cum = []
        for g in gs:
            parts = _split3(lf_ref[rows[g], :])
            cum.append(jnp.dot(summat, parts[0], preferred_element_type=F32)
                       + jnp.dot(summat, parts[1], preferred_element_type=F32)
                       + jnp.dot(summat, parts[2], preferred_element_type=F32))
        q_dec, decay, a, vs, k_end = [], [], [], [], []
        for g in gs:
            cum_end = jnp.concatenate(
                [jnp.broadcast_to(cum[g][(c + 1) * HG_CHUNK - 1:(c + 1) * HG_CHUNK, :], (HG_CHUNK, HG_WIDTH))
                 for c in range(n_ch)], axis=0)
            q, k, v = q_ref[rows[g], :], k_ref[rows[g], :], v_ref[rows[g], :]
            q_dec.append(q * jnp.exp(cum[g]))
            k_dec = (k * jnp.exp(-cum[g])).astype(BF16)
            k_end.append(k * jnp.exp(cum_end - cum[g]))
            decay.append(jnp.exp(cum_end))
            q_decb = q_dec[g].astype(BF16)
            vs.append(v)
            a.append([lax.dot_general(q_decb[:, hs], k_dec[:, hs], nt, preferred_element_type=F32)
                      for hs in heads])
        o = [[jnp.dot(jnp.where(causal, a[g][h], 0.0).astype(BF16), vs[g][:, hs].astype(BF16),
                      preferred_element_type=F32) for h, hs in enumerate(heads)] for g in gs]
        ut = [[jnp.dot(vs[g][:, hs].T.astype(BF16), by_chunk(k_end[g][:, hs]), preferred_element_type=F32)
               for hs in heads] for g in gs]
        starts = [[[] for _ in heads] for _ in gs]
        for h, hs in enumerate(heads):
            st = st_ref[h]
            for g in gs:
                for c in range(n_ch):
                    starts[g][h].append(st.astype(BF16))
                    st = (st * decay[g][c * HG_CHUNK:c * HG_CHUNK + 1, hs]
                          + ut[g][h][:, c * LANES:(c + 1) * LANES])
            st_ref[h] = st
        inter = [[lax.dot_general(by_chunk(q_dec[g][:, hs]), jnp.concatenate(starts[g][h], axis=1), nt,
                                  preferred_element_type=F32) for h, hs in enumerate(heads)] for g in gs]
        for g in gs:
            for h, hs in enumerate(heads):
                oh = o[g][h] + inter[g][h]
                ms = jnp.mean(oh * oh, axis=-1, keepdims=True)
                gate = gh_ref[rows[g], hs]
                o_ref[rows[g], hs] = oh * lax.rsqrt(ms + LN_EPS) * ng_ref[:, hs] * (gate * _sigmoid(gate))
        return carry

    lax.fori_loop(0, rows_per_step // (grp * HG_GROUPS), groups, 0)

    @pl.when(tb == pl.num_programs(1) - 1)
    def _():
        for h in range(HG_HEADS):
            s_ref[0, h] = st_ref[h].T


HG_ROWS = 1024
HG_GROUPS = 2


def hgrn_prompt(qh, kh, ih, lf, gh, norm_g, batch, seq):
    n_tb = seq // HG_ROWS
    blk = pl.BlockSpec((HG_ROWS, HG_WIDTH), lambda b, t: (b * n_tb + t, 0))
    return pl.pallas_call(
        functools.partial(_hgrn_prompt_body, rows_per_step=HG_ROWS),
        grid=(batch, n_tb),
        in_specs=[blk, blk, blk, blk, blk, _const_spec((1, HG_WIDTH))],
        out_specs=[blk, pl.BlockSpec((1, HG_HEADS, HG_DK, HG_DV), lambda b, t: (b, 0, 0, 0))],
        out_shape=[jax.ShapeDtypeStruct((batch * seq, HG_WIDTH), F32),
                   jax.ShapeDtypeStruct((batch, HG_HEADS, HG_DK, HG_DV), F32)],
        scratch_shapes=[pltpu.VMEM((HG_HEADS, HG_DV, HG_DK), F32)],
        compiler_params=pltpu.CompilerParams(dimension_semantics=("arbitrary", "arbitrary"),
                                             vmem_limit_bytes=VMEM_LIMIT),
        name="hgrn_prompt",
    )(qh, kh, ih, lf, gh, norm_g)


FA_ROWS = 128
FA_HEADS = 4
FA_SLOTS = 3


def _ffn_attn_body(*refs, n_new, past, b_off, mixed, n_steps):
    n_x = 7 if mixed else 1
    wg_ref, wu_ref, wd_ref, g_ref, b_ref, q_ref, kn_ref, vn_ref, kc_hbm, vc_hbm = refs[n_x:n_x + 10]
    y_ref, o_ref, nk_ref, nv_ref, kbuf, vbuf, sems = refs[-7:]
    halves = ATT_HEADS // FA_HEADS
    step = pl.program_id(0)
    odd = ((b_off + step // halves) % 2) == 1

    def window_copies(s):
        slot = s % FA_SLOTS
        src = (b_off + s // halves, pl.ds((s % halves) * FA_HEADS, FA_HEADS))
        return (pltpu.make_async_copy(kc_hbm.at[src], kbuf.at[slot], sems.at[slot, 0]),
                pltpu.make_async_copy(vc_hbm.at[src], vbuf.at[slot], sems.at[slot, 1]))

    @pl.when(step == 0)
    def _():
        for s in range(min(FA_SLOTS - 1, n_steps)):
            for c in window_copies(s):
                c.start()

    @pl.when(step + (FA_SLOTS - 1) < n_steps)
    def _():
        for c in window_copies(step + (FA_SLOTS - 1)):
            c.start()

    for c in window_copies(step):
        c.wait()
    slot = step % FA_SLOTS
    kc_ref, vc_ref = kbuf.at[slot], vbuf.at[slot]

    def new_rows(ref):
        both = ref[...]
        return jnp.where(odd, both[n_new:2 * n_new], both[0:n_new])

    q_new, k_new, v_new = new_rows(q_ref), new_rows(kn_ref), new_rows(vn_ref)
    head_cols = [slice(h * ATT_HEAD_DIM, (h + 1) * ATT_HEAD_DIM) for h in range(FA_HEADS)]
    srow = lax.broadcasted_iota(jnp.int32, (n_new, past), 0)
    tcol = lax.broadcasted_iota(jnp.int32, (n_new, past), 1)
    dist = past + srow - tcol
    cnt = jnp.zeros((n_new, past), F32)
    for window, dil in DILATIONS:
        cnt = cnt + jnp.where((dist % dil == 0) & (dist <= window), 1.0, 0.0)
    si = lax.broadcasted_iota(jnp.int32, (n_new, n_new), 0)
    sj = lax.broadcasted_iota(jnp.int32, (n_new, n_new), 1)
    dn = si - sj
    cnt_new = jnp.zeros((n_new, n_new), F32)
    for window, dil in DILATIONS:
        cnt_new = cnt_new + jnp.where((dn >= 0) & (dn % dil == 0) & (dn <= window), 1.0, 0.0)
    lane = lax.broadcasted_iota(jnp.int32, (1, LANES), 1)
    ones = jnp.ones((8, past), BF16)
    ones_new = jnp.ones((8, n_new), BF16)
    nt = (((1,), (1,)), ((), ()))
    shift = past - n_new
    hs = range(FA_HEADS)

    def scores():
        qs = [q_new[:, c].astype(BF16) for c in head_cols]
        s_c = [jnp.dot(qs[h], kc_ref[h].astype(BF16), preferred_element_type=F32) for h in hs]
        s_n = [lax.dot_general(qs[h], k_new[:, head_cols[h]].astype(BF16), nt, preferred_element_type=F32)
               for h in hs]
        return s_c, s_n

    if mixed:
        x1_ref, oa_ref, oh_ref, wa_ref, wh_ref, g2_ref, b2_ref = refs[:n_x]
        mix = jnp.dot(oa_ref[...].astype(BF16), wa_ref[...], preferred_element_type=F32)
        mix = mix + jnp.dot(oh_ref[...].astype(BF16), wh_ref[...], preferred_element_type=F32)
        s_c, s_n = scores()
        x = _layer_norm(DN_ALPHA * x1_ref[...] + mix, g2_ref[...], b2_ref[...])
    else:
        x = refs[0][...]
    xb = x.astype(BF16)
    gate = jnp.dot(xb, wg_ref[...], preferred_element_type=F32)
    up = jnp.dot(xb, wu_ref[...], preferred_element_type=F32)
    if not mixed:
        s_c, s_n = scores()
    knt = [k_new[:, c].T for c in head_cols]
    vnt = [v_new[:, c].T for c in head_cols]

    hmid = (gate * _sigmoid(gate) * up).astype(BF16)
    y = jnp.dot(hmid, wd_ref[...], preferred_element_type=F32)
    y_ref[...] = _layer_norm(DN_ALPHA * x + 0.5 * y, g_ref[...], b_ref[...])

    p_c, p_n = [], []
    for h in hs:
        sc = jnp.where(cnt > 0, s_c[h], NEG_INF)
        sn = jnp.where(cnt_new > 0, s_n[h], NEG_INF)
        m = jnp.maximum(jnp.max(sc, axis=-1, keepdims=True), jnp.max(sn, axis=-1, keepdims=True))
        p_c.append((cnt * jnp.exp(sc - m)).astype(BF16))
        p_n.append((cnt_new * jnp.exp(sn - m)).astype(BF16))
    outs = []
    for h in hs:
        num = (lax.dot_general(vc_ref[h].astype(BF16), p_c[h], nt, preferred_element_type=F32)
               + lax.dot_general(vnt[h].astype(BF16), p_n[h], nt, preferred_element_type=F32))
        den = (lax.dot_general(ones, p_c[h], nt, preferred_element_type=F32)
               + lax.dot_general(ones_new, p_n[h], nt, preferred_element_type=F32))
        outs.append((num / den[0:1, :]).T)
    o_ref[0] = jnp.concatenate(outs, axis=1)

    for h in hs:
        for src_ref, new, dst in ((kc_ref, knt[h], nk_ref), (vc_ref, vnt[h], nv_ref)):
            rolled = pltpu.roll(src_ref[h], shift, 1)
            dst[0, h, :, 0:past - LANES] = rolled[:, 0:past - LANES]
            last = rolled[:, past - LANES:past]
            for s in range(n_new):
                col = jnp.broadcast_to(new[:, s:s + 1], last.shape)
                last = jnp.where(lane == LANES - n_new + s, col, last)
            dst[0, h, :, past - LANES:past] = last


def ffn_attn(x, wg, wu, wd, g, b, q, kn, vn, kc, vc, n_new, b_off, prev=None, mix=None):
    m = x.shape[0]
    bsz, n_heads, _, past = kc.shape
    halves = n_heads // FA_HEADS
    steps = m // FA_ROWS
    assert b_off % 2 == 0 and bsz % 2 == 0, "new-token rows are fetched per pair of batch rows"

    def samp(i):
        return (b_off + i // halves, i % halves, 0, 0)

    row = pl.BlockSpec((FA_ROWS, D_MODEL), lambda i: (i, 0))
    pair = pl.BlockSpec((2 * n_new, FA_HEADS * ATT_HEAD_DIM), lambda i: ((b_off + i // halves) // 2, i % halves))
    o_blk = pl.BlockSpec((1, n_new, FA_HEADS * ATT_HEAD_DIM), lambda i: (b_off + i // halves, 0, i % halves))
    big = pl.BlockSpec((1, FA_HEADS, ATT_HEAD_DIM, past), samp)
    in_specs, args = [row], [x]
    if mix is not None:
        half = pl.BlockSpec((FA_ROWS, ATT_WIDTH), lambda i: (i, 0))
        in_specs += [half, half, _const_spec((ATT_WIDTH, D_MODEL)), _const_spec((HG_WIDTH, D_MODEL)),
                     _const_spec((1, D_MODEL)), _const_spec((1, D_MODEL))]
        args += list(mix)
    hbm = pl.BlockSpec(memory_space=pl.ANY)
    in_specs += [_const_spec((D_MODEL, D_FF)), _const_spec((D_MODEL, D_FF)), _const_spec((D_FF, D_MODEL)),
                 _const_spec((1, D_MODEL)), _const_spec((1, D_MODEL)), pair, pair, pair, hbm, hbm]
    args += [wg, wu, wd, g, b, q, kn, vn, kc, vc]
    aliases = {}
    if prev is not None:
        for j, a in enumerate(prev):
            aliases[len(args)] = 1 + j
            in_specs.append(pl.BlockSpec(memory_space=pl.ANY))
            args.append(a)
    big_sds = jax.ShapeDtypeStruct(kc.shape, F32)
    return pl.pallas_call(
        functools.partial(_ffn_attn_body, n_new=n_new, past=past, b_off=b_off, mixed=mix is not None,
                          n_steps=steps),
        grid=(steps,),
        in_specs=in_specs,
        out_specs=[row, o_blk, big, big],
        out_shape=[jax.ShapeDtypeStruct((m, D_MODEL), F32),
                   jax.ShapeDtypeStruct((bsz, n_new, n_heads * ATT_HEAD_DIM), F32), big_sds, big_sds],
        scratch_shapes=[pltpu.VMEM((FA_SLOTS, FA_HEADS, ATT_HEAD_DIM, past), F32),
                        pltpu.VMEM((FA_SLOTS, FA_HEADS, ATT_HEAD_DIM, past), F32),
                        pltpu.SemaphoreType.DMA((FA_SLOTS, 2))],
        input_output_aliases=aliases,
        compiler_params=pltpu.CompilerParams(dimension_semantics=("arbitrary",),
                                             vmem_limit_bytes=VMEM_LIMIT),
        name="ffn_attn",
    )(*args)


HS_BATCH = 8


def _hgrn_sample_body(q_ref, k_ref, v_ref, lf_ref, gh_ref, ng_ref, s0_ref, o_ref, s_ref, *, n_new):
    row = lax.broadcasted_iota(jnp.int32, (n_new, 1), 0)
    tn = (((0,), (0,)), ((), ()))
    ones12 = jnp.ones((3 * n_new, HG_DV), BF16)

    sub = 8
    per_tile = sub // n_new

    def tile(t, carry):
        rows = pl.ds(pl.multiple_of(t * sub, sub), sub)
        units = []
        for h in range(HG_HEADS):
            cols = slice(h * LANES, (h + 1) * LANES)
            q8, k8, v8, g8 = q_ref[rows, cols], k_ref[rows, cols], v_ref[rows, cols], lf_ref[rows, cols]
            for i in range(per_tile):
                part = slice(i * n_new, (i + 1) * n_new)
                q, k, v, g = q8[part], k8[part], v8[part], g8[part]
                cum = jnp.zeros((n_new, LANES), F32)
                for j in range(n_new):
                    cum = cum + jnp.where(row >= j, g[j:j + 1, :], 0.0)
                cum_end = cum[n_new - 1:n_new, :]
                units.append(dict(h=h, b=t * per_tile + i, v=v, g=g, q_dec=q * jnp.exp(cum),
                                  k_dec=k * jnp.exp(-cum), k_end=k * jnp.exp(cum_end - cum)))
        for u in units:
            u["s0"] = s0_ref[u["b"], u["h"]]
            u["o"] = jnp.dot(u["q_dec"].astype(BF16), u["s0"].astype(BF16), preferred_element_type=F32)
        for u in units:
            dec = lax.dot_general(jnp.concatenate(_split3(u["g"]), axis=0), ones12, tn,
                                  preferred_element_type=F32)
            upd = lax.dot_general(u["k_end"].astype(BF16), u["v"].astype(BF16), tn,
                                  preferred_element_type=F32)
            s_ref[u["b"], u["h"]] = jnp.exp(dec) * u["s0"] + upd
        for h in range(HG_HEADS):
            cols = slice(h * LANES, (h + 1) * LANES)
            outs = []
            for u in units[h * per_tile:(h + 1) * per_tile]:
                o = u["o"]
                for j in range(n_new):
                    a_j = jnp.sum(u["q_dec"] * u["k_dec"][j:j + 1, :], axis=-1, keepdims=True)
                    o = o + jnp.where(row >= j, a_j, 0.0) * u["v"][j:j + 1, :]
                outs.append(o * lax.rsqrt(jnp.mean(o * o, axis=-1, keepdims=True) + LN_EPS))
            gate = gh_ref[rows, cols]
            o_ref[rows, cols] = jnp.concatenate(outs, axis=0) * ng_ref[:, cols] * (gate * _sigmoid(gate))
        return carry

    lax.fori_loop(0, HS_BATCH // per_tile, tile, 0)


def hgrn_sample(qh, kh, ih, lf, gh, norm_g, s0, n_new):
    bsz = s0.shape[0]
    blk = pl.BlockSpec((HS_BATCH * n_new, HG_WIDTH), lambda i: (i, 0))
    sblk = pl.BlockSpec((HS_BATCH, HG_HEADS, HG_DK, HG_DV), lambda i: (i, 0, 0, 0))
    return pl.pallas_call(
        functools.partial(_hgrn_sample_body, n_new=n_new),
        grid=(bsz // HS_BATCH,),
        in_specs=[blk, blk, blk, blk, blk, _const_spec((1, HG_WIDTH)), sblk],
        out_specs=[blk, sblk],
        out_shape=[jax.ShapeDtypeStruct((bsz * n_new, HG_WIDTH), F32),
                   jax.ShapeDtypeStruct(s0.shape, F32)],
        compiler_params=pltpu.CompilerParams(dimension_semantics=("arbitrary",),
                                             vmem_limit_bytes=VMEM_LIMIT),
        name="hgrn_sample",
    )(qh, kh, ih, lf, gh, norm_g, s0)


def _rope_tables(pos):
    half = ATT_HEAD_DIM // 2
    inv = ROPE_THETA ** (-jnp.arange(half, dtype=F32) / half)
    ang = pos.astype(F32)[:, None] * inv[None, :]
    cos, sin = jnp.cos(ang), jnp.sin(ang)
    return jnp.tile(cos, (1, 4)), jnp.tile(jnp.concatenate([-sin, sin], axis=1), (1, 2))


def kernel(x_prompt, x_sample, cache_k, cache_v, state_hgrn, ffn1_w_gate, ffn1_w_up, ffn1_w_down, ln1_g, ln1_b, w_in, hg_lower_bound, hg_norm_g, w_out, ln2_g, ln2_b, ffn2_w_gate, ffn2_w_up, ffn2_w_down, ln3_g, ln3_b):
    batch, seq, _ = x_prompt.shape
    dec_batch, n_new, _ = x_sample.shape
    assert ffn1_w_gate.shape[0] == 1, "single layer"
    win_p = min(ATT_WINDOW_MAX, seq)

    w1 = (ffn1_w_gate[0].astype(BF16), ffn1_w_up[0].astype(BF16), ffn1_w_down[0].astype(BF16))
    w2 = (ffn2_w_gate[0].astype(BF16), ffn2_w_up[0].astype(BF16), ffn2_w_down[0].astype(BF16))
    w_in_b = w_in[0].astype(BF16)
    w_att = w_out[0, :ATT_WIDTH].astype(BF16)
    w_hg = w_out[0, ATT_WIDTH:].astype(BF16)

    cos_p, sin_p = _rope_tables(jnp.arange(seq, dtype=jnp.int32))
    pos_s = PAST_LEN + (jnp.arange(dec_batch * n_new, dtype=jnp.int32) % n_new)
    cos_s, sin_s = _rope_tables(pos_s)

    xp = x_prompt.reshape(batch * seq, D_MODEL)
    xs = x_sample.reshape(dec_batch * n_new, D_MODEL)

    xs1 = ffn_ln(xs, *w1, ln1_g, ln1_b)
    qs, ks, vs, qhs, khs, ihs, lfs, ghs = proj(xs1, w_in_b, cos_s, sin_s, hg_lower_bound, 1)

    samp = (qs, ks, vs,
            cache_k[0].transpose(0, 2, 3, 1),
            cache_v[0].transpose(0, 2, 3, 1), n_new)
    steps = (batch * seq) // FA_ROWS
    per_call = steps // (ATT_HEADS // FA_HEADS)
    assert 2 * per_call == dec_batch, "the two prompt FFN calls must cover the sample batch"

    xp1, o_t, nk, nv = ffn_attn(xp, *w1, ln1_g, ln1_b, *samp, 0)
    q, k, v, qh, kh, ih, lf, gh = proj(xp1, w_in_b, cos_p, sin_p, hg_lower_bound, seq // ROW_TILE)
    o_att, kt_p, vt_p = attn_prompt(q, k, v, batch, seq)
    o_hg, s_p = hgrn_prompt(qh, kh, ih, lf, gh, hg_norm_g, batch, seq)
    yp, o_t, nk, nv = ffn_attn(xp1, *w2, ln3_g, ln3_b, *samp, per_call, prev=(o_t, nk, nv),
                               mix=(o_att, o_hg, w_att, w_hg, ln2_g, ln2_b))

    o_att_s = o_t.reshape(dec_batch * n_new, ATT_WIDTH)

    o_hg_s, s_s = hgrn_sample(qhs, khs, ihs, lfs, ghs, hg_norm_g, state_hgrn[0], n_new)
    xs2 = mix_ln(xs1, o_att_s, o_hg_s, w_att, w_hg, ln2_g, ln2_b)
    ys = ffn_ln(xs2, *w2, ln3_g, ln3_b)

    return (yp.reshape(batch, seq, D_MODEL),
            ys.reshape(dec_batch, n_new, D_MODEL),
            kt_p.transpose(0, 3, 1, 2)[None],
            vt_p.transpose(0, 3, 1, 2)[None],
            s_p[None],
            nk.transpose(0, 3, 1, 2)[None],
            nv.transpose(0, 3, 1, 2)[None],
            s_s[None])
```

```python
import functools

import jax
import jax.numpy as jnp
from jax import lax
from jax.experimental import pallas as pl
from jax.experimental.pallas import tpu as pltpu

F32 = jnp.float32
BF16 = jnp.bfloat16

D_MODEL = 1024
ATT_WIDTH = 512
HG_WIDTH = 512
ATT_HEAD_DIM = 64
ATT_HEADS = 8
DILATIONS = ((128, 1), (512, 4), (2048, 16))
ATT_WINDOW_MAX = 2048
ATT_BLOCK = 128
ROPE_THETA = 10000.0
HG_HEADS = 4
HG_DK = 128
HG_DV = 128
HG_CHUNK = 32
D_FF = 2816
IN_COLS = 3 * ATT_WIDTH + 4 * HG_WIDTH
PAST_LEN = 8192
DN_ALPHA = 2.0 ** 0.25
LN_EPS = 1e-5
NEG_INF = -1e30

LANES = 128
ROW_TILE = 512
VMEM_LIMIT = 56 * 1024 * 1024


def _const_spec(shape):
    return pl.BlockSpec(shape, lambda *_: (0,) * len(shape), pipeline_mode=pl.Buffered(1))


def _layer_norm(r, g, b):
    mu = jnp.mean(r, axis=-1, keepdims=True)
    c = r - mu
    var = jnp.mean(c * c, axis=-1, keepdims=True)
    return c * lax.rsqrt(var + LN_EPS) * g + b


def _sigmoid(x):
    return 1.0 / (1.0 + jnp.exp(-x))


def _split3(x):
    hi = x.astype(BF16)
    r1 = x - hi.astype(F32)
    mid = r1.astype(BF16)
    lo = (r1 - mid.astype(F32)).astype(BF16)
    return hi, mid, lo


def _ffn_ln_body(x_ref, wg_ref, wu_ref, wd_ref, g_ref, b_ref, o_ref):
    x = x_ref[...]
    xb = x.astype(BF16)
    gate = jnp.dot(xb, wg_ref[...], preferred_element_type=F32)
    up = jnp.dot(xb, wu_ref[...], preferred_element_type=F32)
    h = (gate * _sigmoid(gate) * up).astype(BF16)
    y = jnp.dot(h, wd_ref[...], preferred_element_type=F32)
    o_ref[...] = _layer_norm(DN_ALPHA * x + 0.5 * y, g_ref[...], b_ref[...])


def ffn_ln(x, wg, wu, wd, g, b):
    m = x.shape[0]
    row = pl.BlockSpec((ROW_TILE, D_MODEL), lambda i: (i, 0))
    return pl.pallas_call(
        _ffn_ln_body,
        grid=(m // ROW_TILE,),
        in_specs=[row, _const_spec((D_MODEL, D_FF)), _const_spec((D_MODEL, D_FF)),
                  _const_spec((D_FF, D_MODEL)), _const_spec((1, D_MODEL)), _const_spec((1, D_MODEL))],
        out_specs=row,
        out_shape=jax.ShapeDtypeStruct((m, D_MODEL), F32),
        compiler_params=pltpu.CompilerParams(dimension_semantics=("arbitrary",),
                                             vmem_limit_bytes=VMEM_LIMIT),
        name="ffn_ln",
    )(x, wg, wu, wd, g, b)


def _rope(z, cos_t, sin_t):
    lane = lax.broadcasted_iota(jnp.int32, (1, LANES), 1)
    first_half = (lane % ATT_HEAD_DIM) < (ATT_HEAD_DIM // 2)
    outs = []
    for c in range(z.shape[1] // LANES):
        zc = z[:, c * LANES:(c + 1) * LANES]
        partner = jnp.where(first_half, pltpu.roll(zc, LANES - 32, 1), pltpu.roll(zc, 32, 1))
        outs.append(zc * cos_t + partner * sin_t)
    return outs


def _proj_body(x_ref, w_ref, cos_ref, sin_ref, lbraw_ref,
               q_ref, k_ref, v_ref, qh_ref, kh_ref, ih_ref, lf_ref, gh_ref):
    xb = x_ref[...].astype(BF16)
    z = jnp.dot(xb, w_ref[...], preferred_element_type=F32)
    cos_t = cos_ref[...]
    sin_t = sin_ref[...]
    aw = ATT_WIDTH
    for c, val in enumerate(_rope(z[:, 0:aw], cos_t, sin_t)):
        q_ref[:, c * LANES:(c + 1) * LANES] = val * (ATT_HEAD_DIM ** -0.5)
    for c, val in enumerate(_rope(z[:, aw:2 * aw], cos_t, sin_t)):
        k_ref[:, c * LANES:(c + 1) * LANES] = val
    v_ref[...] = z[:, 2 * aw:3 * aw]
    base = 3 * aw
    zq = z[:, base:base + HG_WIDTH]
    zf = z[:, base + HG_WIDTH:base + 2 * HG_WIDTH]
    lbr = lbraw_ref[...]
    mx = jnp.max(lbr, axis=0, keepdims=True)
    ex = jnp.exp(lbr - mx)
    lb = ex[0:1, :] / jnp.sum(ex, axis=0, keepdims=True)
    sg = _sigmoid(zf)
    qh_ref[...] = zq * _sigmoid(zq)
    lf_ref[...] = jnp.log(lb + (1.0 - lb) * sg)
    kh_ref[...] = (1.0 - lb) * _sigmoid(-zf)
    ih_ref[...] = z[:, base + 2 * HG_WIDTH:base + 3 * HG_WIDTH]
    gh_ref[...] = z[:, base + 3 * HG_WIDTH:base + 4 * HG_WIDTH]


def proj(x, w_in, cos_t, sin_t, lb_raw, table_tiles):
    m = x.shape[0]
    row = pl.BlockSpec((ROW_TILE, D_MODEL), lambda i: (i, 0))
    tab = pl.BlockSpec((ROW_TILE, LANES), lambda i: (i % table_tiles, 0))
    out = pl.BlockSpec((ROW_TILE, ATT_WIDTH), lambda i: (i, 0))
    sds = jax.ShapeDtypeStruct((m, ATT_WIDTH), F32)
    return pl.pallas_call(
        _proj_body,
        grid=(m // ROW_TILE,),
        in_specs=[row, _const_spec((D_MODEL, IN_COLS)), tab, tab, _const_spec((2, HG_WIDTH))],
        out_specs=[out] * 8,
        out_shape=[sds] * 8,
        compiler_params=pltpu.CompilerParams(dimension_semantics=("arbitrary",),
                                             vmem_limit_bytes=VMEM_LIMIT),
        name="proj",
    )(x, w_in, cos_t, sin_t, lb_raw)


def _mix_ln_body(x_ref, oa_ref, oh_ref, wa_ref, wh_ref, g_ref, b_ref, o_ref):
    mix = jnp.dot(oa_ref[...].astype(BF16), wa_ref[...], preferred_element_type=F32)
    mix = mix + jnp.dot(oh_ref[...].astype(BF16), wh_ref[...], preferred_element_type=F32)
    o_ref[...] = _layer_norm(DN_ALPHA * x_ref[...] + mix, g_ref[...], b_ref[...])


def mix_ln(x, o_att, o_hg, w_att, w_hg, g, b):
    m = x.shape[0]
    row = pl.BlockSpec((ROW_TILE, D_MODEL), lambda i: (i, 0))
    half = pl.BlockSpec((ROW_TILE, ATT_WIDTH), lambda i: (i, 0))
    return pl.pallas_call(
        _mix_ln_body,
        grid=(m // ROW_TILE,),
        in_specs=[row, half, half, _const_spec((ATT_WIDTH, D_MODEL)), _const_spec((HG_WIDTH, D_MODEL)),
                  _const_spec((1, D_MODEL)), _const_spec((1, D_MODEL))],
        out_specs=row,
        out_shape=jax.ShapeDtypeStruct((m, D_MODEL), F32),
        compiler_params=pltpu.CompilerParams(dimension_semantics=("arbitrary",),
                                             vmem_limit_bytes=VMEM_LIMIT),
        name="mix_ln",
    )(x, o_att, o_hg, w_att, w_hg, g, b)


ATT_UNROLL = (8, 8, 4)


def _attn_prompt_body(q_ref, k_ref, v_ref, o_ref, kt_ref, vt_ref, acc_ref, m_ref, l_ref, *, seq):
    blk = ATT_BLOCK
    lane = lax.broadcasted_iota(jnp.int32, (1, LANES), 1)
    head0 = lane < ATT_HEAD_DIM
    qi = lax.broadcasted_iota(jnp.int32, (blk, 2 * blk), 0)
    kc = lax.broadcasted_iota(jnp.int32, (blk, 2 * blk), 1)
    dist = qi + blk - kc
    first_mask = (lax.broadcasted_iota(jnp.int32, (blk, blk), 1)
                  <= lax.broadcasted_iota(jnp.int32, (blk, blk), 0))
    nt = (((1,), (1,)), ((), ()))
    log2e = 1.4426950408889634

    for br, (window, dil) in enumerate(DILATIONS):
        w_sub = window // dil
        rel = (dist >= 0) & (dist <= w_sub)
        span = blk * dil
        n_blk = seq // span
        n_units = ATT_UNROLL[br]
        run = min(n_units, n_blk)
        n_res = n_units // run
        runs_per_res = n_blk // run

        def units(it, carry, br=br, dil=dil, rel=rel, span=span, run=run, n_res=n_res,
                  runs_per_res=runs_per_res, n_units=n_units):
            starts, masks, vbs, scores = [], [], [], []
            for rr in range(n_res):
                if runs_per_res == 1:
                    r, n0 = it * n_res + rr, 0
                else:
                    r, n0 = it // runs_per_res, (it % runs_per_res) * run
                base = r + n0 * span

                def kv_block(j):
                    rows = pl.ds(jnp.maximum(base + j * span, r), blk, stride=dil)
                    return k_ref[rows, :].astype(BF16), v_ref[rows, :].astype(BF16)

                prev = None if runs_per_res == 1 else kv_block(-1)
                for j in range(run):
                    cur = kv_block(j)
                    start = base + j * span
                    q = q_ref[pl.ds(start, blk, stride=dil), :] * log2e
                    if prev is None:
                        kb, vb, mask = cur[0], cur[1], first_mask
                    else:
                        kb = jnp.concatenate([prev[0], cur[0]], axis=0)
                        vb = jnp.concatenate([prev[1], cur[1]], axis=0)
                        mask = rel & (kc >= jnp.where(n0 > 0, 0, blk)) if (j == 0 and runs_per_res > 1) else rel
                    starts.append(start)
                    masks.append(mask)
                    vbs.append(vb)
                    for hsel in (head0, ~head0):
                        qh = jnp.where(hsel, q, 0.0).astype(BF16)
                        scores.append(lax.dot_general(qh, kb, nt, preferred_element_type=F32))
                    prev = cur
            probs, ls, ms = [], [], []
            for i, s in enumerate(scores):
                s = jnp.where(masks[i // 2], s, NEG_INF)
                m = jnp.max(s, axis=-1, keepdims=True)
                p = jnp.exp2(s - m)
                ls.append(jnp.sum(p, axis=-1, keepdims=True))
                ms.append(m)
                probs.append(p.astype(BF16))
            outs = [jnp.dot(p, vbs[i // 2], preferred_element_type=F32) for i, p in enumerate(probs)]
            for uu in range(n_units):
                i0, i1 = 2 * uu, 2 * uu + 1
                rows = pl.ds(starts[uu], blk, stride=dil)
                acc_ref[br, rows, :] = jnp.where(head0, outs[i0], outs[i1])
                m_ref[br, rows, :] = jnp.broadcast_to(jnp.where(head0, ms[i0], ms[i1]), (blk, LANES))
                l_ref[br, rows, :] = jnp.broadcast_to(jnp.where(head0, ls[i0], ls[i1]), (blk, LANES))
            return carry

        lax.fori_loop(0, (dil * n_blk) // n_units, units, 0)

    def combine(c, carry):
        rows = pl.ds(pl.multiple_of(c * blk, blk), blk)
        m0, m1, m2 = m_ref[0, rows, :], m_ref[1, rows, :], m_ref[2, rows, :]
        mx = jnp.maximum(jnp.maximum(m0, m1), m2)
        w0, w1, w2 = jnp.exp2(m0 - mx), jnp.exp2(m1 - mx), jnp.exp2(m2 - mx)
        num = w0 * acc_ref[0, rows, :] + w1 * acc_ref[1, rows, :] + w2 * acc_ref[2, rows, :]
        den = w0 * l_ref[0, rows, :] + w1 * l_ref[1, rows, :] + w2 * l_ref[2, rows, :]
        o_ref[rows, :] = num / den
        return carry

    lax.fori_loop(0, seq // blk, combine, 0)

    win = kt_ref.shape[-1]
    for j in range(win // LANES):
        rows = pl.ds(seq - win + j * LANES, LANES)
        kt_ref[0, :, :, j * LANES:(j + 1) * LANES] = k_ref[rows, :].T.reshape(2, ATT_HEAD_DIM, LANES)
        vt_ref[0, :, :, j * LANES:(j + 1) * LANES] = v_ref[rows, :].T.reshape(2, ATT_HEAD_DIM, LANES)


def attn_prompt(q, k, v, batch, seq):
    win = min(ATT_WINDOW_MAX, seq)
    n_hp = ATT_WIDTH // LANES
    blk = pl.BlockSpec((seq, LANES), lambda b, h: (b, h))
    tblk = pl.BlockSpec((1, 2, ATT_HEAD_DIM, win), lambda b, h: (b, h, 0, 0))
    tsds = jax.ShapeDtypeStruct((batch, ATT_HEADS, ATT_HEAD_DIM, win), F32)
    return pl.pallas_call(
        functools.partial(_attn_prompt_body, seq=seq),
        grid=(batch, n_hp),
        in_specs=[blk, blk, blk],
        out_specs=[blk, tblk, tblk],
        out_shape=[jax.ShapeDtypeStruct((batch * seq, ATT_WIDTH), F32), tsds, tsds],
        scratch_shapes=[pltpu.VMEM((3, seq, LANES), F32)] * 3,
        compiler_params=pltpu.CompilerParams(dimension_semantics=("arbitrary", "arbitrary"),
                                             vmem_limit_bytes=VMEM_LIMIT),
        name="attn_prompt",
    )(q, k, v)


def _hgrn_prompt_body(q_ref, k_ref, v_ref, lf_ref, gh_ref, ng_ref, o_ref, s_ref, st_ref, *, rows_per_step):
    grp = LANES
    n_ch = grp // HG_CHUNK
    ri = lax.broadcasted_iota(jnp.int32, (grp, grp), 0)
    ci = lax.broadcasted_iota(jnp.int32, (grp, grp), 1)
    same_chunk = (ri // HG_CHUNK) == (ci // HG_CHUNK)
    causal = same_chunk & (ci <= ri)
    summat = jnp.where(causal, 1.0, 0.0).astype(BF16)
    nt = (((1,), (1,)), ((), ()))
    tb = pl.program_id(1)

    @pl.when(tb == 0)
    def _():
        st_ref[...] = jnp.zeros(st_ref.shape, F32)

    heads = [slice(h * LANES, (h + 1) * LANES) for h in range(HG_HEADS)]
    in_chunk = [(ri // HG_CHUNK) == c for c in range(n_ch)]

    def by_chunk(x):
        return jnp.concatenate([jnp.where(m, x, 0.0).astype(BF16) for m in in_chunk], axis=1)

    def groups(it, carry):
        gs = range(HG_GROUPS)
        rows = [pl.ds(pl.multiple_of((it * HG_GROUPS + g) * grp, grp), grp) for g in gs]
        cum = []
        for g in gs:
            parts = _split3(lf_ref[rows[g], :])
            cum.append(jnp.dot(summat, parts[0], preferred_element_type=F32)
                       + jnp.dot(summat, parts[1], preferred_element_type=F32)
                       + jnp.dot(summat, parts[2], preferred_element_type=F32))
        q_dec, decay, a, vs, k_end = [], [], [], [], []
        for g in gs:
            cum_end = jnp.concatenate(
                [jnp.broadcast_to(cum[g][(c + 1) * HG_CHUNK - 1:(c + 1) * HG_CHUNK, :], (HG_CHUNK, HG_WIDTH))
                 for c in range(n_ch)], axis=0)
            q, k, v = q_ref[rows[g], :], k_ref[rows[g], :], v_ref[rows[g], :]
            q_dec.append(q * jnp.exp(cum[g]))
            k_dec = (k * jnp.exp(-cum[g])).astype(BF16)
            k_end.append(k * jnp.exp(cum_end - cum[g]))
            decay.append(jnp.exp(cum_end))
            q_decb = q_dec[g].astype(BF16)
            vs.append(v)
            a.append([lax.dot_general(q_decb[:, hs], k_dec[:, hs], nt, preferred_element_type=F32)
                      for hs in heads])
        o = [[jnp.dot(jnp.where(causal, a[g][h], 0.0).astype(BF16), vs[g][:, hs].astype(BF16),
                      preferred_element_type=F32) for h, hs in enumerate(heads)] for g in gs]
        ut = [[jnp.dot(vs[g][:, hs].T.astype(BF16), by_chunk(k_end[g][:, hs]), preferred_element_type=F32)
               for hs in heads] for g in gs]
        starts = [[[] for _ in heads] for _ in gs]
        for h, hs in enumerate(heads):
            st = st_ref[h]
            for g in gs:
                for c in range(n_ch):
                    starts[g][h].append(st.astype(BF16))
                    st = (st * decay[g][c * HG_CHUNK:c * HG_CHUNK + 1, hs]
                          + ut[g][h][:, c * LANES:(c + 1) * LANES])
            st_ref[h] = st
        inter = [[lax.dot_general(by_chunk(q_dec[g][:, hs]), jnp.concatenate(starts[g][h], axis=1), nt,
                                  preferred_element_type=F32) for h, hs in enumerate(heads)] for g in gs]
        for g in gs:
            for h, hs in enumerate(heads):
                oh = o[g][h] + inter[g][h]
                ms = jnp.mean(oh * oh, axis=-1, keepdims=True)
                gate = gh_ref[rows[g], hs]
                o_ref[rows[g], hs] = oh * lax.rsqrt(ms + LN_EPS) * ng_ref[:, hs] * (gate * _sigmoid(gate))
        return carry

    lax.fori_loop(0, rows_per_step // (grp * HG_GROUPS), groups, 0)

    @pl.when(tb == pl.num_programs(1) - 1)
    def _():
        for h in range(HG_HEADS):
            s_ref[0, h] = st_ref[h].T


HG_ROWS = 1024
HG_GROUPS = 2


def hgrn_prompt(qh, kh, ih, lf, gh, norm_g, batch, seq):
    n_tb = seq // HG_ROWS
    blk = pl.BlockSpec((HG_ROWS, HG_WIDTH), lambda b, t: (b * n_tb + t, 0))
    return pl.pallas_call(
        functools.partial(_hgrn_prompt_body, rows_per_step=HG_ROWS),
        grid=(batch, n_tb),
        in_specs=[blk, blk, blk, blk, blk, _const_spec((1, HG_WIDTH))],
        out_specs=[blk, pl.BlockSpec((1, HG_HEADS, HG_DK, HG_DV), lambda b, t: (b, 0, 0, 0))],
        out_shape=[jax.ShapeDtypeStruct((batch * seq, HG_WIDTH), F32),
                   jax.ShapeDtypeStruct((batch, HG_HEADS, HG_DK, HG_DV), F32)],
        scratch_shapes=[pltpu.VMEM((HG_HEADS, HG_DV, HG_DK), F32)],
        compiler_params=pltpu.CompilerParams(dimension_semantics=("arbitrary", "arbitrary"),
                                             vmem_limit_bytes=VMEM_LIMIT),
        name="hgrn_prompt",
    )(qh, kh, ih, lf, gh, norm_g)


FA_ROWS = 128
FA_HEADS = 4
FA_SLOTS = 3


def _ffn_attn_body(*refs, n_new, past, b_off, mixed, n_steps):
    n_x = 7 if mixed else 1
    wg_ref, wu_ref, wd_ref, g_ref, b_ref, q_ref, kn_ref, vn_ref, kc_hbm, vc_hbm = refs[n_x:n_x + 10]
    y_ref, o_ref, nk_ref, nv_ref, kbuf, vbuf, sems = refs[-7:]
    halves = ATT_HEADS // FA_HEADS
    step = pl.program_id(0)
    odd = ((b_off + step // halves) % 2) == 1

    def window_copies(s):
        slot = s % FA_SLOTS
        src = (b_off + s // halves, pl.ds((s % halves) * FA_HEADS, FA_HEADS))
        return (pltpu.make_async_copy(kc_hbm.at[src], kbuf.at[slot], sems.at[slot, 0]),
                pltpu.make_async_copy(vc_hbm.at[src], vbuf.at[slot], sems.at[slot, 1]))

    @pl.when(step == 0)
    def _():
        for s in range(min(FA_SLOTS - 1, n_steps)):
            for c in window_copies(s):
                c.start()

    @pl.when(step + (FA_SLOTS - 1) < n_steps)
    def _():
        for c in window_copies(step + (FA_SLOTS - 1)):
            c.start()

    for c in window_copies(step):
        c.wait()
    slot = step % FA_SLOTS
    kc_ref, vc_ref = kbuf.at[slot], vbuf.at[slot]

    def new_rows(ref):
        both = ref[...]
        return jnp.where(odd, both[n_new:2 * n_new], both[0:n_new])

    q_new, k_new, v_new = new_rows(q_ref), new_rows(kn_ref), new_rows(vn_ref)
    head_cols = [slice(h * ATT_HEAD_DIM, (h + 1) * ATT_HEAD_DIM) for h in range(FA_HEADS)]
    srow = lax.broadcasted_iota(jnp.int32, (n_new, past), 0)
    tcol = lax.broadcasted_iota(jnp.int32, (n_new, past), 1)
    dist = past + srow - tcol
    cnt = jnp.zeros((n_new, past), F32)
    for window, dil in DILATIONS:
        cnt = cnt + jnp.where((dist % dil == 0) & (dist <= window), 1.0, 0.0)
    si = lax.broadcasted_iota(jnp.int32, (n_new, n_new), 0)
    sj = lax.broadcasted_iota(jnp.int32, (n_new, n_new), 1)
    dn = si - sj
    cnt_new = jnp.zeros((n_new, n_new), F32)
    for window, dil in DILATIONS:
        cnt_new = cnt_new + jnp.where((dn >= 0) & (dn % dil == 0) & (dn <= window), 1.0, 0.0)
    lane = lax.broadcasted_iota(jnp.int32, (1, LANES), 1)
    nt = (((1,), (1,)), ((), ()))
    shift = past - n_new
    hs = range(FA_HEADS)

    def scores():
        qs = [q_new[:, c].astype(BF16) for c in head_cols]
        s_c = [jnp.dot(qs[h], kc_ref[h].astype(BF16), preferred_element_type=F32) for h in hs]
        s_n = [lax.dot_general(qs[h], k_new[:, head_cols[h]].astype(BF16), nt, preferred_element_type=F32)
               for h in hs]
        return s_c, s_n

    if mixed:
        x1_ref, oa_ref, oh_ref, wa_ref, wh_ref, g2_ref, b2_ref = refs[:n_x]
        mix = jnp.dot(oa_ref[...].astype(BF16), wa_ref[...], preferred_element_type=F32)
        mix = mix + jnp.dot(oh_ref[...].astype(BF16), wh_ref[...], preferred_element_type=F32)
        s_c, s_n = scores()
        x = _layer_norm(DN_ALPHA * x1_ref[...] + mix, g2_ref[...], b2_ref[...])
    else:
        x = refs[0][...]
    xb = x.astype(BF16)
    gate = jnp.dot(xb, wg_ref[...], preferred_element_type=F32)
    up = jnp.dot(xb, wu_ref[...], preferred_element_type=F32)
    if not mixed:
        s_c, s_n = scores()
    knt = [k_new[:, c].T for c in head_cols]
    vnt = [v_new[:, c].T for c in head_cols]

    hmid = (gate * _sigmoid(gate) * up).astype(BF16)
    y = jnp.dot(hmid, wd_ref[...], preferred_element_type=F32)
    y_ref[...] = _layer_norm(DN_ALPHA * x + 0.5 * y, g_ref[...], b_ref[...])

    p_c, p_n, den = [], [], []
    for h in hs:
        sc = jnp.where(cnt > 0, s_c[h], NEG_INF)
        sn = jnp.where(cnt_new > 0, s_n[h], NEG_INF)
        m = jnp.maximum(jnp.max(sc, axis=-1, keepdims=True), jnp.max(sn, axis=-1, keepdims=True))
        p_c.append((cnt * jnp.exp(sc - m)).astype(BF16))
        p_n.append((cnt_new * jnp.exp(sn - m)).astype(BF16))
        den.append(jnp.sum(p_c[h].astype(F32), axis=-1, keepdims=True)
                   + jnp.sum(p_n[h].astype(F32), axis=-1, keepdims=True))
    outs = []
    for h in hs:
        num = (lax.dot_general(vc_ref[h].astype(BF16), p_c[h], nt, preferred_element_type=F32)
               + lax.dot_general(vnt[h].astype(BF16), p_n[h], nt, preferred_element_type=F32))
        outs.append(num.T / den[h])
    o_ref[0] = jnp.concatenate(outs, axis=1)

    for h in hs:
        for src_ref, new, dst in ((kc_ref, knt[h], nk_ref), (vc_ref, vnt[h], nv_ref)):
            rolled = pltpu.roll(src_ref[h], shift, 1)
            dst[0, h, :, 0:past - LANES] = rolled[:, 0:past - LANES]
            last = rolled[:, past - LANES:past]
            for s in range(n_new):
                col = jnp.broadcast_to(new[:, s:s + 1], last.shape)
                last = jnp.where(lane == LANES - n_new + s, col, last)
            dst[0, h, :, past - LANES:past] = last


def ffn_attn(x, wg, wu, wd, g, b, q, kn, vn, kc, vc, n_new, b_off, prev=None, mix=None):
    m = x.shape[0]
    bsz, n_heads, _, past = kc.shape
    halves = n_heads // FA_HEADS
    steps = m // FA_ROWS
    assert b_off % 2 == 0 and bsz % 2 == 0, "new-token rows are fetched per pair of batch rows"

    def samp(i):
        return (b_off + i // halves, i % halves, 0, 0)

    row = pl.BlockSpec((FA_ROWS, D_MODEL), lambda i: (i, 0))
    pair = pl.BlockSpec((2 * n_new, FA_HEADS * ATT_HEAD_DIM), lambda i: ((b_off + i // halves) // 2, i % halves))
    o_blk = pl.BlockSpec((1, n_new, FA_HEADS * ATT_HEAD_DIM), lambda i: (b_off + i // halves, 0, i % halves))
    big = pl.BlockSpec((1, FA_HEADS, ATT_HEAD_DIM, past), samp)
    in_specs, args = [row], [x]
    if mix is not None:
        half = pl.BlockSpec((FA_ROWS, ATT_WIDTH), lambda i: (i, 0))
        in_specs += [half, half, _const_spec((ATT_WIDTH, D_MODEL)), _const_spec((HG_WIDTH, D_MODEL)),
                     _const_spec((1, D_MODEL)), _const_spec((1, D_MODEL))]
        args += list(mix)
    hbm = pl.BlockSpec(memory_space=pl.ANY)
    in_specs += [_const_spec((D_MODEL, D_FF)), _const_spec((D_MODEL, D_FF)), _const_spec((D_FF, D_MODEL)),
                 _const_spec((1, D_MODEL)), _const_spec((1, D_MODEL)), pair, pair, pair, hbm, hbm]
    args += [wg, wu, wd, g, b, q, kn, vn, kc, vc]
    aliases = {}
    if prev is not None:
        for j, a in enumerate(prev):
            aliases[len(args)] = 1 + j
            in_specs.append(pl.BlockSpec(memory_space=pl.ANY))
            args.append(a)
    big_sds = jax.ShapeDtypeStruct(kc.shape, F32)
    return pl.pallas_call(
        functools.partial(_ffn_attn_body, n_new=n_new, past=past, b_off=b_off, mixed=mix is not None,
                          n_steps=steps),
        grid=(steps,),
        in_specs=in_specs,
        out_specs=[row, o_blk, big, big],
        out_shape=[jax.ShapeDtypeStruct((m, D_MODEL), F32),
                   jax.ShapeDtypeStruct((bsz, n_new, n_heads * ATT_HEAD_DIM), F32), big_sds, big_sds],
        scratch_shapes=[pltpu.VMEM((FA_SLOTS, FA_HEADS, ATT_HEAD_DIM, past), F32),
                        pltpu.VMEM((FA_SLOTS, FA_HEADS, ATT_HEAD_DIM, past), F32),
                        pltpu.SemaphoreType.DMA((FA_SLOTS, 2))],
        input_output_aliases=aliases,
        compiler_params=pltpu.CompilerParams(dimension_semantics=("arbitrary",),
                                             vmem_limit_bytes=VMEM_LIMIT),
        name="ffn_attn",
    )(*args)


HS_BATCH = 8


def _hgrn_sample_body(q_ref, k_ref, v_ref, lf_ref, gh_ref, ng_ref, s0_ref, o_ref, s_ref, *, n_new):
    row = lax.broadcasted_iota(jnp.int32, (n_new, 1), 0)
    tn = (((0,), (0,)), ((), ()))
    ones12 = jnp.ones((3 * n_new, HG_DV), BF16)

    sub = 8
    per_tile = sub // n_new

    def tile(t, carry):
        rows = pl.ds(pl.multiple_of(t * sub, sub), sub)
        units = []
        for h in range(HG_HEADS):
            cols = slice(h * LANES, (h + 1) * LANES)
            q8, k8, v8, g8 = q_ref[rows, cols], k_ref[rows, cols], v_ref[rows, cols], lf_ref[rows, cols]
            for i in range(per_tile):
                part = slice(i * n_new, (i + 1) * n_new)
                q, k, v, g = q8[part], k8[part], v8[part], g8[part]
                cum = jnp.zeros((n_new, LANES), F32)
                for j in range(n_new):
                    cum = cum + jnp.where(row >= j, g[j:j + 1, :], 0.0)
                cum_end = cum[n_new - 1:n_new, :]
                units.append(dict(h=h, b=t * per_tile + i, v=v, g=g, q_dec=q * jnp.exp(cum),
                                  k_dec=k * jnp.exp(-cum), k_end=k * jnp.exp(cum_end - cum)))
        for u in units:
            u["s0"] = s0_ref[u["b"], u["h"]]
            u["o"] = jnp.dot(u["q_dec"].astype(BF16), u["s0"].astype(BF16), preferred_element_type=F32)
        for u in units:
            dec = lax.dot_general(jnp.concatenate(_split3(u["g"]), axis=0), ones12, tn,
                                  preferred_element_type=F32)
            upd = lax.dot_general(u["k_end"].astype(BF16), u["v"].astype(BF16), tn,
                                  preferred_element_type=F32)
            s_ref[u["b"], u["h"]] = jnp.exp(dec) * u["s0"] + upd
        for h in range(HG_HEADS):
            cols = slice(h * LANES, (h + 1) * LANES)
            outs = []
            for u in units[h * per_tile:(h + 1) * per_tile]:
                o = u["o"]
                for j in range(n_new):
                    a_j = jnp.sum(u["q_dec"] * u["k_dec"][j:j + 1, :], axis=-1, keepdims=True)
                    o = o + jnp.where(row >= j, a_j, 0.0) * u["v"][j:j + 1, :]
                outs.append(o * lax.rsqrt(jnp.mean(o * o, axis=-1, keepdims=True) + LN_EPS))
            gate = gh_ref[rows, cols]
            o_ref[rows, cols] = jnp.concatenate(outs, axis=0) * ng_ref[:, cols] * (gate * _sigmoid(gate))
        return carry

    lax.fori_loop(0, HS_BATCH // per_tile, tile, 0)


def hgrn_sample(qh, kh, ih, lf, gh, norm_g, s0, n_new):
    bsz = s0.shape[0]
    blk = pl.BlockSpec((HS_BATCH * n_new, HG_WIDTH), lambda i: (i, 0))
    sblk = pl.BlockSpec((HS_BATCH, HG_HEADS, HG_DK, HG_DV), lambda i: (i, 0, 0, 0))
    return pl.pallas_call(
        functools.partial(_hgrn_sample_body, n_new=n_new),
        grid=(bsz // HS_BATCH,),
        in_specs=[blk, blk, blk, blk, blk, _const_spec((1, HG_WIDTH)), sblk],
        out_specs=[blk, sblk],
        out_shape=[jax.ShapeDtypeStruct((bsz * n_new, HG_WIDTH), F32),
                   jax.ShapeDtypeStruct(s0.shape, F32)],
        compiler_params=pltpu.CompilerParams(dimension_semantics=("arbitrary",),
                                             vmem_limit_bytes=VMEM_LIMIT),
        name="hgrn_sample",
    )(qh, kh, ih, lf, gh, norm_g, s0)


def _rope_tables(pos):
    half = ATT_HEAD_DIM // 2
    inv = ROPE_THETA ** (-jnp.arange(half, dtype=F32) / half)
    ang = pos.astype(F32)[:, None] * inv[None, :]
    cos, sin = jnp.cos(ang), jnp.sin(ang)
    return jnp.tile(cos, (1, 4)), jnp.tile(jnp.concatenate([-sin, sin], axis=1), (1, 2))


def kernel(x_prompt, x_sample, cache_k, cache_v, state_hgrn, ffn1_w_gate, ffn1_w_up, ffn1_w_down, ln1_g, ln1_b, w_in, hg_lower_bound, hg_norm_g, w_out, ln2_g, ln2_b, ffn2_w_gate, ffn2_w_up, ffn2_w_down, ln3_g, ln3_b):
    batch, seq, _ = x_prompt.shape
    dec_batch, n_new, _ = x_sample.shape
    assert ffn1_w_gate.shape[0] == 1, "single layer"
    win_p = min(ATT_WINDOW_MAX, seq)

    w1 = (ffn1_w_gate[0].astype(BF16), ffn1_w_up[0].astype(BF16), ffn1_w_down[0].astype(BF16))
    w2 = (ffn2_w_gate[0].astype(BF16), ffn2_w_up[0].astype(BF16), ffn2_w_down[0].astype(BF16))
    w_in_b = w_in[0].astype(BF16)
    w_att = w_out[0, :ATT_WIDTH].astype(BF16)
    w_hg = w_out[0, ATT_WIDTH:].astype(BF16)

    cos_p, sin_p = _rope_tables(jnp.arange(seq, dtype=jnp.int32))
    pos_s = PAST_LEN + (jnp.arange(dec_batch * n_new, dtype=jnp.int32) % n_new)
    cos_s, sin_s = _rope_tables(pos_s)

    xp = x_prompt.reshape(batch * seq, D_MODEL)
    xs = x_sample.reshape(dec_batch * n_new, D_MODEL)

    xs1 = ffn_ln(xs, *w1, ln1_g, ln1_b)
    qs, ks, vs, qhs, khs, ihs, lfs, ghs = proj(xs1, w_in_b, cos_s, sin_s, hg_lower_bound, 1)

    samp = (qs, ks, vs,
            cache_k[0].transpose(0, 2, 3, 1),
            cache_v[0].transpose(0, 2, 3, 1), n_new)
    steps = (batch * seq) // FA_ROWS
    per_call = steps // (ATT_HEADS // FA_HEADS)
    assert 2 * per_call == dec_batch, "the two prompt FFN calls must cover the sample batch"

    xp1, o_t, nk, nv = ffn_attn(xp, *w1, ln1_g, ln1_b, *samp, 0)
    q, k, v, qh, kh, ih, lf, gh = proj(xp1, w_in_b, cos_p, sin_p, hg_lower_bound, seq // ROW_TILE)
    o_att, kt_p, vt_p = attn_prompt(q, k, v, batch, seq)
    o_hg, s_p = hgrn_prompt(qh, kh, ih, lf, gh, hg_norm_g, batch, seq)
    yp, o_t, nk, nv = ffn_attn(xp1, *w2, ln3_g, ln3_b, *samp, per_call, prev=(o_t, nk, nv),
                               mix=(o_att, o_hg, w_att, w_hg, ln2_g, ln2_b))

    o_att_s = o_t.reshape(dec_batch * n_new, ATT_WIDTH)

    o_hg_s, s_s = hgrn_sample(qhs, khs, ihs, lfs, ghs, hg_norm_g, state_hgrn[0], n_new)
    xs2 = mix_ln(xs1, o_att_s, o_hg_s, w_att, w_hg, ln2_g, ln2_b)
    ys = ffn_ln(xs2, *w2, ln3_g, ln3_b)

    return (yp.reshape(batch, seq, D_MODEL),
            ys.reshape(dec_batch, n_new, D_MODEL),
            kt_p.transpose(0, 3, 1, 2)[None],
            vt_p.transpose(0, 3, 1, 2)[None],
            s_p[None],
            nk.transpose(0, 3, 1, 2)[None],
            nv.transpose(0, 3, 1, 2)[None],
            s_s[None])
```

```python
import functools

import jax
import jax.numpy as jnp
from jax import lax
from jax.experimental import pallas as pl
from jax.experimental.pallas import tpu as pltpu

F32 = jnp.float32
BF16 = jnp.bfloat16

D_MODEL = 1024
ATT_WIDTH = 512
HG_WIDTH = 512
ATT_HEAD_DIM = 64
ATT_HEADS = 8
DILATIONS = ((128, 1), (512, 4), (2048, 16))
ATT_WINDOW_MAX = 2048
ATT_BLOCK = 128
ROPE_THETA = 10000.0
HG_HEADS = 4
HG_DK = 128
HG_DV = 128
HG_CHUNK = 32
D_FF = 2816
IN_COLS = 3 * ATT_WIDTH + 4 * HG_WIDTH
PAST_LEN = 8192
DN_ALPHA = 2.0 ** 0.25
LN_EPS = 1e-5
NEG_INF = -1e30

LANES = 128
ROW_TILE = 512
VMEM_LIMIT = 56 * 1024 * 1024


def _const_spec(shape, block=None):
    index = tuple(block) if block is not None else (0,) * len(shape)
    return pl.BlockSpec(shape, lambda *_: index, pipeline_mode=pl.Buffered(1))


def _layer_norm(r, g, b):
    mu = jnp.mean(r, axis=-1, keepdims=True)
    c = r - mu
    var = jnp.mean(c * c, axis=-1, keepdims=True)
    return c * lax.rsqrt(var + LN_EPS) * g + b


def _sigmoid(x):
    return 1.0 / (1.0 + jnp.exp(-x))


def _split3(x):
    hi = x.astype(BF16)
    r1 = x - hi.astype(F32)
    mid = r1.astype(BF16)
    lo = (r1 - mid.astype(F32)).astype(BF16)
    return hi, mid, lo


def _ffn_ln_body(x_ref, wg_ref, wu_ref, wd_ref, g_ref, b_ref, o_ref):
    x = x_ref[...]
    xb = x.astype(BF16)
    gate = jnp.dot(xb, wg_ref[...], preferred_element_type=F32)
    up = jnp.dot(xb, wu_ref[...], preferred_element_type=F32)
    h = (gate * _sigmoid(gate) * up).astype(BF16)
    y = jnp.dot(h, wd_ref[...], preferred_element_type=F32)
    o_ref[...] = _layer_norm(DN_ALPHA * x + 0.5 * y, g_ref[...], b_ref[...])


def ffn_ln(x, wg, wu, wd, g, b):
    m = x.shape[0]
    row = pl.BlockSpec((ROW_TILE, D_MODEL), lambda i: (i, 0))
    return pl.pallas_call(
        _ffn_ln_body,
        grid=(m // ROW_TILE,),
        in_specs=[row, _const_spec((D_MODEL, D_FF)), _const_spec((D_MODEL, D_FF)),
                  _const_spec((D_FF, D_MODEL)), _const_spec((1, D_MODEL)), _const_spec((1, D_MODEL))],
        out_specs=row,
        out_shape=jax.ShapeDtypeStruct((m, D_MODEL), F32),
        compiler_params=pltpu.CompilerParams(dimension_semantics=("arbitrary",),
                                             vmem_limit_bytes=VMEM_LIMIT),
        name="ffn_ln",
    )(x, wg, wu, wd, g, b)


FFN_CHUNK = 256


def _ffn_ln_cast_body(x_ref, wg_ref, wu_ref, wd_ref, g_ref, b_ref, o_ref, wgb_ref, wub_ref, wdb_ref, acc_ref):
    j = pl.program_id(0)

    @pl.when(j == 0)
    def _():
        acc_ref[...] = jnp.zeros(acc_ref.shape, F32)

    wgb, wub, wdb = wg_ref[...].astype(BF16), wu_ref[...].astype(BF16), wd_ref[...].astype(BF16)
    wgb_ref[...], wub_ref[...], wdb_ref[...] = wgb, wub, wdb
    xb = x_ref[...].astype(BF16)
    gate = jnp.dot(xb, wgb, preferred_element_type=F32)
    up = jnp.dot(xb, wub, preferred_element_type=F32)
    h = (gate * _sigmoid(gate) * up).astype(BF16)
    acc_ref[...] += jnp.dot(h, wdb, preferred_element_type=F32)

    @pl.when(j == pl.num_programs(0) - 1)
    def _():
        o_ref[...] = _layer_norm(DN_ALPHA * x_ref[...] + 0.5 * acc_ref[...], g_ref[...], b_ref[...])


def ffn_ln_cast(x, wg, wu, wd, g, b):
    m = x.shape[0]
    col = pl.BlockSpec((D_MODEL, FFN_CHUNK), lambda j: (0, j))
    rowc = pl.BlockSpec((FFN_CHUNK, D_MODEL), lambda j: (j, 0))
    xs = pl.BlockSpec((m, D_MODEL), lambda j: (0, 0))
    return pl.pallas_call(
        _ffn_ln_cast_body,
        grid=(D_FF // FFN_CHUNK,),
        in_specs=[xs, col, col, rowc, _const_spec((1, D_MODEL)), _const_spec((1, D_MODEL))],
        out_specs=[xs, col, col, rowc],
        out_shape=[jax.ShapeDtypeStruct((m, D_MODEL), F32), jax.ShapeDtypeStruct(wg.shape, BF16),
                   jax.ShapeDtypeStruct(wu.shape, BF16), jax.ShapeDtypeStruct(wd.shape, BF16)],
        scratch_shapes=[pltpu.VMEM((m, D_MODEL), F32)],
        compiler_params=pltpu.CompilerParams(dimension_semantics=("arbitrary",),
                                             vmem_limit_bytes=VMEM_LIMIT),
        name="ffn_ln_cast",
    )(x, wg, wu, wd, g, b)


def _rope(z, cos_t, sin_t):
    lane = lax.broadcasted_iota(jnp.int32, (1, LANES), 1)
    first_half = (lane % ATT_HEAD_DIM) < (ATT_HEAD_DIM // 2)
    outs = []
    for c in range(z.shape[1] // LANES):
        zc = z[:, c * LANES:(c + 1) * LANES]
        partner = jnp.where(first_half, pltpu.roll(zc, LANES - 32, 1), pltpu.roll(zc, 32, 1))
        outs.append(zc * cos_t + partner * sin_t)
    return outs


def _proj_body(x_ref, w_ref, cos_ref, sin_ref, lbraw_ref,
               q_ref, k_ref, v_ref, qh_ref, kh_ref, ih_ref, lf_ref, gh_ref, wb_ref=None):
    xb = x_ref[...].astype(BF16)
    aw = ATT_WIDTH
    base = 3 * aw

    def cols(start, width):
        w = w_ref[:, start:start + width]
        if wb_ref is not None:
            w = w.astype(BF16)
            wb_ref[:, start:start + width] = w
        return jnp.dot(xb, w, preferred_element_type=F32)

    cos_t = cos_ref[...]
    sin_t = sin_ref[...]
    for c, val in enumerate(_rope(cols(0, aw), cos_t, sin_t)):
        q_ref[:, c * LANES:(c + 1) * LANES] = val * (ATT_HEAD_DIM ** -0.5)
    for c, val in enumerate(_rope(cols(aw, aw), cos_t, sin_t)):
        k_ref[:, c * LANES:(c + 1) * LANES] = val
    lbr = lbraw_ref[...]
    mx = jnp.max(lbr, axis=0, keepdims=True)
    ex = jnp.exp(lbr - mx)
    lb = ex[0:1, :] / jnp.sum(ex, axis=0, keepdims=True)
    zf = cols(base + HG_WIDTH, HG_WIDTH)
    lf_ref[...] = jnp.log(lb + (1.0 - lb) * _sigmoid(zf))
    kh_ref[...] = (1.0 - lb) * _sigmoid(-zf)
    zq = cols(base, HG_WIDTH)
    qh_ref[...] = zq * _sigmoid(zq)
    v_ref[...] = cols(2 * aw, aw)
    ih_ref[...] = cols(base + 2 * HG_WIDTH, HG_WIDTH)
    gh_ref[...] = cols(base + 3 * HG_WIDTH, HG_WIDTH)


def proj(x, w_in, cos_t, sin_t, lb_raw, table_tiles):
    m = x.shape[0]
    emit = w_in.dtype == F32
    assert not emit or m == ROW_TILE, "the weight copy is written once, by a single grid step"
    row = pl.BlockSpec((ROW_TILE, D_MODEL), lambda i: (i, 0))
    tab = pl.BlockSpec((ROW_TILE, LANES), lambda i: (i % table_tiles, 0))
    out = pl.BlockSpec((ROW_TILE, ATT_WIDTH), lambda i: (i, 0))
    sds = jax.ShapeDtypeStruct((m, ATT_WIDTH), F32)
    return pl.pallas_call(
        _proj_body,
        grid=(m // ROW_TILE,),
        in_specs=[row, _const_spec((D_MODEL, IN_COLS)), tab, tab, _const_spec((2, HG_WIDTH))],
        out_specs=[out] * 8 + ([pl.BlockSpec((D_MODEL, IN_COLS), lambda i: (0, 0))] if emit else []),
        out_shape=[sds] * 8 + ([jax.ShapeDtypeStruct(w_in.shape, BF16)] if emit else []),
        compiler_params=pltpu.CompilerParams(dimension_semantics=("arbitrary",),
                                             vmem_limit_bytes=VMEM_LIMIT),
        name="proj",
    )(x, w_in, cos_t, sin_t, lb_raw)


def _mix_ln_body(x_ref, oa_ref, oh_ref, wa_ref, wh_ref, g_ref, b_ref, o_ref):
    mix = jnp.dot(oa_ref[...].astype(BF16), wa_ref[...], preferred_element_type=F32)
    mix = mix + jnp.dot(oh_ref[...].astype(BF16), wh_ref[...], preferred_element_type=F32)
    o_ref[...] = _layer_norm(DN_ALPHA * x_ref[...] + mix, g_ref[...], b_ref[...])


def _w_out_specs():
    assert ATT_WIDTH == HG_WIDTH
    return [_const_spec((ATT_WIDTH, D_MODEL), (0, 0)), _const_spec((HG_WIDTH, D_MODEL), (1, 0))]


def mix_ln(x, o_att, o_hg, w_out, g, b):
    m = x.shape[0]
    row = pl.BlockSpec((ROW_TILE, D_MODEL), lambda i: (i, 0))
    half = pl.BlockSpec((ROW_TILE, ATT_WIDTH), lambda i: (i, 0))
    return pl.pallas_call(
        _mix_ln_body,
        grid=(m // ROW_TILE,),
        in_specs=[row, half, half] + _w_out_specs() + [_const_spec((1, D_MODEL)), _const_spec((1, D_MODEL))],
        out_specs=row,
        out_shape=jax.ShapeDtypeStruct((m, D_MODEL), F32),
        compiler_params=pltpu.CompilerParams(dimension_semantics=("arbitrary",),
                                             vmem_limit_bytes=VMEM_LIMIT),
        name="mix_ln",
    )(x, o_att, o_hg, w_out, w_out, g, b)


ATT_UNROLL = (8, 8, 4)


def _attn_prompt_body(q_ref, k_ref, v_ref, o_ref, kt_ref, vt_ref, acc_ref, m_ref, l_ref, *, seq):
    blk = ATT_BLOCK
    lane = lax.broadcasted_iota(jnp.int32, (1, LANES), 1)
    head0 = lane < ATT_HEAD_DIM
    qi = lax.broadcasted_iota(jnp.int32, (blk, 2 * blk), 0)
    kc = lax.broadcasted_iota(jnp.int32, (blk, 2 * blk), 1)
    dist = qi + blk - kc
    first_mask = (lax.broadcasted_iota(jnp.int32, (blk, blk), 1)
                  <= lax.broadcasted_iota(jnp.int32, (blk, blk), 0))
    nt = (((1,), (1,)), ((), ()))
    log2e = 1.4426950408889634

    for br, (window, dil) in enumerate(DILATIONS):
        w_sub = window // dil
        rel = (dist >= 0) & (dist <= w_sub)
        span = blk * dil
        n_blk = seq // span
        n_units = ATT_UNROLL[br]
        run = min(n_units, n_blk)
        n_res = n_units // run
        runs_per_res = n_blk // run

        def units(it, carry, br=br, dil=dil, rel=rel, span=span, run=run, n_res=n_res,
                  runs_per_res=runs_per_res, n_units=n_units):
            starts, masks, vbs, scores = [], [], [], []
            for rr in range(n_res):
                if runs_per_res == 1:
                    r, n0 = it * n_res + rr, 0
                else:
                    r, n0 = it // runs_per_res, (it % runs_per_res) * run
                base = r + n0 * span

                def kv_block(j):
                    rows = pl.ds(jnp.maximum(base + j * span, r), blk, stride=dil)
                    return k_ref[rows, :].astype(BF16), v_ref[rows, :].astype(BF16)

                prev = None if runs_per_res == 1 else kv_block(-1)
                for j in range(run):
                    cur = kv_block(j)
                    start = base + j * span
                    q = q_ref[pl.ds(start, blk, stride=dil), :] * log2e
                    if prev is None:
                        kb, vb, mask = cur[0], cur[1], first_mask
                    else:
                        kb = jnp.concatenate([prev[0], cur[0]], axis=0)
                        vb = jnp.concatenate([prev[1], cur[1]], axis=0)
                        mask = rel & (kc >= jnp.where(n0 > 0, 0, blk)) if (j == 0 and runs_per_res > 1) else rel
                    starts.append(start)
                    masks.append(mask)
                    vbs.append(vb)
                    for hsel in (head0, ~head0):
                        qh = jnp.where(hsel, q, 0.0).astype(BF16)
                        scores.append(lax.dot_general(qh, kb, nt, preferred_element_type=F32))
                    prev = cur
            probs, ls, ms = [], [], []
            for i, s in enumerate(scores):
                s = jnp.where(masks[i // 2], s, NEG_INF)
                m = jnp.max(s, axis=-1, keepdims=True)
                p = jnp.exp2(s - m)
                ls.append(jnp.sum(p, axis=-1, keepdims=True))
                ms.append(m)
                probs.append(p.astype(BF16))
            outs = [jnp.dot(p, vbs[i // 2], preferred_element_type=F32) for i, p in enumerate(probs)]
            for uu in range(n_units):
                i0, i1 = 2 * uu, 2 * uu + 1
                rows = pl.ds(starts[uu], blk, stride=dil)
                acc_ref[br, rows, :] = jnp.where(head0, outs[i0], outs[i1])
                m_ref[br, rows, :] = jnp.broadcast_to(jnp.where(head0, ms[i0], ms[i1]), (blk, LANES))
                l_ref[br, rows, :] = jnp.broadcast_to(jnp.where(head0, ls[i0], ls[i1]), (blk, LANES))
            return carry

        lax.fori_loop(0, (dil * n_blk) // n_units, units, 0)

    def combine(c, carry):
        rows = pl.ds(pl.multiple_of(c * blk, blk), blk)
        m0, m1, m2 = m_ref[0, rows, :], m_ref[1, rows, :], m_ref[2, rows, :]
        mx = jnp.maximum(jnp.maximum(m0, m1), m2)
        w0, w1, w2 = jnp.exp2(m0 - mx), jnp.exp2(m1 - mx), jnp.exp2(m2 - mx)
        num = w0 * acc_ref[0, rows, :] + w1 * acc_ref[1, rows, :] + w2 * acc_ref[2, rows, :]
        den = w0 * l_ref[0, rows, :] + w1 * l_ref[1, rows, :] + w2 * l_ref[2, rows, :]
        o_ref[rows, :] = num / den
        return carry

    lax.fori_loop(0, seq // blk, combine, 0)

    win = kt_ref.shape[-1]
    for j in range(win // LANES):
        rows = pl.ds(seq - win + j * LANES, LANES)
        kt_ref[0, :, :, j * LANES:(j + 1) * LANES] = k_ref[rows, :].T.reshape(2, ATT_HEAD_DIM, LANES)
        vt_ref[0, :, :, j * LANES:(j + 1) * LANES] = v_ref[rows, :].T.reshape(2, ATT_HEAD_DIM, LANES)


def attn_prompt(q, k, v, batch, seq):
    win = min(ATT_WINDOW_MAX, seq)
    n_hp = ATT_WIDTH // LANES
    blk = pl.BlockSpec((seq, LANES), lambda b, h: (b, h))
    tblk = pl.BlockSpec((1, 2, ATT_HEAD_DIM, win), lambda b, h: (b, h, 0, 0))
    tsds = jax.ShapeDtypeStruct((batch, ATT_HEADS, ATT_HEAD_DIM, win), F32)
    return pl.pallas_call(
        functools.partial(_attn_prompt_body, seq=seq),
        grid=(batch, n_hp),
        in_specs=[blk, blk, blk],
        out_specs=[blk, tblk, tblk],
        out_shape=[jax.ShapeDtypeStruct((batch * seq, ATT_WIDTH), F32), tsds, tsds],
        scratch_shapes=[pltpu.VMEM((3, seq, LANES), F32)] * 3,
        compiler_params=pltpu.CompilerParams(dimension_semantics=("arbitrary", "arbitrary"),
                                             vmem_limit_bytes=VMEM_LIMIT),
        name="attn_prompt",
    )(q, k, v)


def _hgrn_prompt_body(*refs, rows_per_step, n_cast):
    q_ref, k_ref, v_ref, lf_ref, gh_ref, ng_ref = refs[:6]
    o_ref, s_ref = refs[6 + n_cast:8 + n_cast]
    st_ref = refs[-1]
    for w_ref, wb_ref in zip(refs[6:6 + n_cast], refs[8 + n_cast:8 + 2 * n_cast]):
        wb_ref[...] = w_ref[...].astype(BF16)
    grp = LANES
    n_ch = grp // HG_CHUNK
    ri = lax.broadcasted_iota(jnp.int32, (grp, grp), 0)
    ci = lax.broadcasted_iota(jnp.int32, (grp, grp), 1)
    same_chunk = (ri // HG_CHUNK) == (ci // HG_CHUNK)
    causal = same_chunk & (ci <= ri)
    summat = jnp.where(causal, 1.0, 0.0).astype(BF16)
    nt = (((1,), (1,)), ((), ()))
    tb = pl.program_id(1)

    @pl.when(tb == 0)
    def _():
        st_ref[...] = jnp.zeros(st_ref.shape, F32)

    heads = [slice(h * LANES, (h + 1) * LANES) for h in range(HG_HEADS)]
    in_chunk = [(ri // HG_CHUNK) == c for c in range(n_ch)]

    def by_chunk(x):
        return jnp.concatenate([jnp.where(m, x, 0.0).astype(BF16) for m in in_chunk], axis=1)

    def groups(it, carry):
        gs = range(HG_GROUPS)
        rows = [pl.ds(pl.multiple_of((it * HG_GROUPS + g) * grp, grp), grp) for g in gs]
        cum = []
        for g in gs:
            parts = _split3(lf_ref[rows[g], :])
            cum.append(jnp.dot(summat, parts[0], preferred_element_type=F32)
                       + jnp.dot(summat, parts[1], preferred_element_type=F32)
                       + jnp.dot(summat, parts[2], preferred_element_type=F32))
        q_dec, decay, a, vs, k_end = [], [], [], [], []
        for g in gs:
            cum_end = jnp.concatenate(
                [jnp.broadcast_to(cum[g][(c + 1) * HG_CHUNK - 1:(c + 1) * HG_CHUNK, :], (HG_CHUNK, HG_WIDTH))
                 for c in range(n_ch)], axis=0)
            q, k, v = q_ref[rows[g], :], k_ref[rows[g], :], v_ref[rows[g], :]
            q_dec.append(q * jnp.exp(cum[g]))
            k_dec = (k * jnp.exp(-cum[g])).astype(BF16)
            k_end.append(k * jnp.exp(cum_end - cum[g]))
            decay.append(jnp.exp(cum_end))
            q_decb = q_dec[g].astype(BF16)
            vs.append(v)
            a.append([lax.dot_general(q_decb[:, hs], k_dec[:, hs], nt, preferred_element_type=F32)
                      for hs in heads])
        o = [[jnp.dot(jnp.where(causal, a[g][h], 0.0).astype(BF16), vs[g][:, hs].astype(BF16),
                      preferred_element_type=F32) for h, hs in enumerate(heads)] for g in gs]
        ut = [[jnp.dot(vs[g][:, hs].T.astype(BF16), by_chunk(k_end[g][:, hs]), preferred_element_type=F32)
               for hs in heads] for g in gs]
        starts = [[[] for _ in heads] for _ in gs]
        for h, hs in enumerate(heads):
            st = st_ref[h]
            for g in gs:
                for c in range(n_ch):
                    starts[g][h].append(st.astype(BF16))
                    st = (st * decay[g][c * HG_CHUNK:c * HG_CHUNK + 1, hs]
                          + ut[g][h][:, c * LANES:(c + 1) * LANES])
            st_ref[h] = st
        inter = [[lax.dot_general(by_chunk(q_dec[g][:, hs]), jnp.concatenate(starts[g][h], axis=1), nt,
                                  preferred_element_type=F32) for h, hs in enumerate(heads)] for g in gs]
        for g in gs:
            for h, hs in enumerate(heads):
                oh = o[g][h] + inter[g][h]
                ms = jnp.mean(oh * oh, axis=-1, keepdims=True)
                gate = gh_ref[rows[g], hs]
                o_ref[rows[g], hs] = oh * lax.rsqrt(ms + LN_EPS) * ng_ref[:, hs] * (gate * _sigmoid(gate))
        return carry

    lax.fori_loop(0, rows_per_step // (grp * HG_GROUPS), groups, 0)

    @pl.when(tb == pl.num_programs(1) - 1)
    def _():
        for h in range(HG_HEADS):
            s_ref[0, h] = st_ref[h].T


HG_ROWS = 1024
HG_GROUPS = 2


def hgrn_prompt(qh, kh, ih, lf, gh, norm_g, batch, seq, cast=()):
    n_tb = seq // HG_ROWS
    steps = batch * n_tb

    def rows(b, t):
        return (b * n_tb + t, 0)

    blk = pl.BlockSpec((HG_ROWS, HG_WIDTH), rows)
    cast_specs = [pl.BlockSpec((w.shape[0] // steps, w.shape[1]), rows) for w in cast]
    return pl.pallas_call(
        functools.partial(_hgrn_prompt_body, rows_per_step=HG_ROWS, n_cast=len(cast)),
        grid=(batch, n_tb),
        in_specs=[blk, blk, blk, blk, blk, _const_spec((1, HG_WIDTH))] + cast_specs,
        out_specs=[blk, pl.BlockSpec((1, HG_HEADS, HG_DK, HG_DV), lambda b, t: (b, 0, 0, 0))] + cast_specs,
        out_shape=[jax.ShapeDtypeStruct((batch * seq, HG_WIDTH), F32),
                   jax.ShapeDtypeStruct((batch, HG_HEADS, HG_DK, HG_DV), F32)]
                  + [jax.ShapeDtypeStruct(w.shape, BF16) for w in cast],
        scratch_shapes=[pltpu.VMEM((HG_HEADS, HG_DV, HG_DK), F32)],
        compiler_params=pltpu.CompilerParams(dimension_semantics=("arbitrary", "arbitrary"),
                                             vmem_limit_bytes=VMEM_LIMIT),
        name="hgrn_prompt",
    )(qh, kh, ih, lf, gh, norm_g, *cast)


FA_ROWS = 128
FA_HEADS = 4
FA_SLOTS = 3


def _ffn_attn_body(*refs, n_new, past, b_off, mixed, n_steps):
    n_x = 7 if mixed else 1
    wg_ref, wu_ref, wd_ref, g_ref, b_ref, q_ref, kn_ref, vn_ref, kc_hbm, vc_hbm = refs[n_x:n_x + 10]
    y_ref, o_ref, nk_ref, nv_ref, kbuf, vbuf, sems = refs[-7:]
    halves = ATT_HEADS // FA_HEADS
    step = pl.program_id(0)
    odd = ((b_off + step // halves) % 2) == 1

    def window_copies(s):
        slot = s % FA_SLOTS
        src = (b_off + s // halves, pl.ds((s % halves) * FA_HEADS, FA_HEADS))
        return (pltpu.make_async_copy(kc_hbm.at[src], kbuf.at[slot], sems.at[slot, 0]),
                pltpu.make_async_copy(vc_hbm.at[src], vbuf.at[slot], sems.at[slot, 1]))

    @pl.when(step == 0)
    def _():
        for s in range(min(FA_SLOTS - 1, n_steps)):
            for c in window_copies(s):
                c.start()

    @pl.when(step + (FA_SLOTS - 1) < n_steps)
    def _():
        for c in window_copies(step + (FA_SLOTS - 1)):
            c.start()

    for c in window_copies(step):
        c.wait()
    slot = step % FA_SLOTS
    kc_ref, vc_ref = kbuf.at[slot], vbuf.at[slot]

    def new_rows(ref):
        both = ref[...]
        return jnp.where(odd, both[n_new:2 * n_new], both[0:n_new])

    q_new, k_new, v_new = new_rows(q_ref), new_rows(kn_ref), new_rows(vn_ref)
    head_cols = [slice(h * ATT_HEAD_DIM, (h + 1) * ATT_HEAD_DIM) for h in range(FA_HEADS)]
    srow = lax.broadcasted_iota(jnp.int32, (n_new, past), 0)
    tcol = lax.broadcasted_iota(jnp.int32, (n_new, past), 1)
    dist = past + srow - tcol
    def on_stride(d, dil):
        return (d & (dil - 1)) == 0 if dil & (dil - 1) == 0 else d % dil == 0

    cnt = jnp.zeros((n_new, past), F32)
    for window, dil in DILATIONS:
        cnt = cnt + jnp.where(on_stride(dist, dil) & (dist <= window), 1.0, 0.0)
    si = lax.broadcasted_iota(jnp.int32, (n_new, n_new), 0)
    sj = lax.broadcasted_iota(jnp.int32, (n_new, n_new), 1)
    dn = si - sj
    cnt_new = jnp.zeros((n_new, n_new), F32)
    for window, dil in DILATIONS:
        cnt_new = cnt_new + jnp.where((dn >= 0) & on_stride(dn, dil) & (dn <= window), 1.0, 0.0)
    lane = lax.broadcasted_iota(jnp.int32, (1, LANES), 1)
    nt = (((1,), (1,)), ((), ()))
    shift = past - n_new
    hs = range(FA_HEADS)

    def scores():
        qs = [q_new[:, c].astype(BF16) for c in head_cols]
        s_c = [jnp.dot(qs[h], kc_ref[h].astype(BF16), preferred_element_type=F32) for h in hs]
        s_n = [lax.dot_general(qs[h], k_new[:, head_cols[h]].astype(BF16), nt, preferred_element_type=F32)
               for h in hs]
        return s_c, s_n

    if mixed:
        x1_ref, oa_ref, oh_ref, wa_ref, wh_ref, g2_ref, b2_ref = refs[:n_x]
        mix = jnp.dot(oa_ref[...].astype(BF16), wa_ref[...], preferred_element_type=F32)
        mix = mix + jnp.dot(oh_ref[...].astype(BF16), wh_ref[...], preferred_element_type=F32)
        s_c, s_n = scores()
        x = _layer_norm(DN_ALPHA * x1_ref[...] + mix, g2_ref[...], b2_ref[...])
    else:
        x = refs[0][...]
    xb = x.astype(BF16)
    gate = jnp.dot(xb, wg_ref[...], preferred_element_type=F32)
    up = jnp.dot(xb, wu_ref[...], preferred_element_type=F32)
    if not mixed:
        s_c, s_n = scores()
    knt = [k_new[:, c].T for c in head_cols]
    vnt = [v_new[:, c].T for c in head_cols]

    hmid = (gate * _sigmoid(gate) * up).astype(BF16)
    y = jnp.dot(hmid, wd_ref[...], preferred_element_type=F32)
    y_ref[...] = _layer_norm(DN_ALPHA * x + 0.5 * y, g_ref[...], b_ref[...])

    p_c, p_n, den = [], [], []
    for h in hs:
        sc = jnp.where(cnt > 0, s_c[h], NEG_INF)
        sn = jnp.where(cnt_new > 0, s_n[h], NEG_INF)
        m = jnp.maximum(jnp.max(sc, axis=-1, keepdims=True), jnp.max(sn, axis=-1, keepdims=True))
        p_c.append((cnt * jnp.exp(sc - m)).astype(BF16))
        p_n.append((cnt_new * jnp.exp(sn - m)).astype(BF16))
        den.append(jnp.sum(p_c[h].astype(F32), axis=-1, keepdims=True)
                   + jnp.sum(p_n[h].astype(F32), axis=-1, keepdims=True))
    outs = []
    for h in hs:
        num = (lax.dot_general(vc_ref[h].astype(BF16), p_c[h], nt, preferred_element_type=F32)
               + lax.dot_general(vnt[h].astype(BF16), p_n[h], nt, preferred_element_type=F32))
        outs.append(num.T / den[h])
    o_ref[0] = jnp.concatenate(outs, axis=1)

    for h in hs:
        for src_ref, new, dst in ((kc_ref, knt[h], nk_ref), (vc_ref, vnt[h], nv_ref)):
            rolled = pltpu.roll(src_ref[h], shift, 1)
            dst[0, h, :, 0:past - LANES] = rolled[:, 0:past - LANES]
            last = rolled[:, past - LANES:past]
            for s in range(n_new):
                col = jnp.broadcast_to(new[:, s:s + 1], last.shape)
                last = jnp.where(lane == LANES - n_new + s, col, last)
            dst[0, h, :, past - LANES:past] = last


def ffn_attn(x, wg, wu, wd, g, b, q, kn, vn, kc, vc, n_new, b_off, prev=None, mix=None):
    m = x.shape[0]
    bsz, n_heads, _, past = kc.shape
    halves = n_heads // FA_HEADS
    steps = m // FA_ROWS
    assert b_off % 2 == 0 and bsz % 2 == 0, "new-token rows are fetched per pair of batch rows"

    def samp(i):
        return (b_off + i // halves, i % halves, 0, 0)

    row = pl.BlockSpec((FA_ROWS, D_MODEL), lambda i: (i, 0))
    pair = pl.BlockSpec((2 * n_new, FA_HEADS * ATT_HEAD_DIM), lambda i: ((b_off + i // halves) // 2, i % halves))
    o_blk = pl.BlockSpec((1, n_new, FA_HEADS * ATT_HEAD_DIM), lambda i: (b_off + i // halves, 0, i % halves))
    big = pl.BlockSpec((1, FA_HEADS, ATT_HEAD_DIM, past), samp)
    in_specs, args = [row], [x]
    if mix is not None:
        half = pl.BlockSpec((FA_ROWS, ATT_WIDTH), lambda i: (i, 0))
        in_specs += [half, half] + _w_out_specs() + [_const_spec((1, D_MODEL)), _const_spec((1, D_MODEL))]
        o_att, o_hg, w_out, ln_g, ln_b = mix
        args += [o_att, o_hg, w_out, w_out, ln_g, ln_b]
    hbm = pl.BlockSpec(memory_space=pl.ANY)
    in_specs += [_const_spec((D_MODEL, D_FF)), _const_spec((D_MODEL, D_FF)), _const_spec((D_FF, D_MODEL)),
                 _const_spec((1, D_MODEL)), _const_spec((1, D_MODEL)), pair, pair, pair, hbm, hbm]
    args += [wg, wu, wd, g, b, q, kn, vn, kc, vc]
    aliases = {}
    if prev is not None:
        for j, a in enumerate(prev):
            aliases[len(args)] = 1 + j
            in_specs.append(pl.BlockSpec(memory_space=pl.ANY))
            args.append(a)
    big_sds = jax.ShapeDtypeStruct(kc.shape, F32)
    return pl.pallas_call(
        functools.partial(_ffn_attn_body, n_new=n_new, past=past, b_off=b_off, mixed=mix is not None,
                          n_steps=steps),
        grid=(steps,),
        in_specs=in_specs,
        out_specs=[row, o_blk, big, big],
        out_shape=[jax.ShapeDtypeStruct((m, D_MODEL), F32),
                   jax.ShapeDtypeStruct((bsz, n_new, n_heads * ATT_HEAD_DIM), F32), big_sds, big_sds],
        scratch_shapes=[pltpu.VMEM((FA_SLOTS, FA_HEADS, ATT_HEAD_DIM, past), F32),
                        pltpu.VMEM((FA_SLOTS, FA_HEADS, ATT_HEAD_DIM, past), F32),
                        pltpu.SemaphoreType.DMA((FA_SLOTS, 2))],
        input_output_aliases=aliases,
        compiler_params=pltpu.CompilerParams(dimension_semantics=("arbitrary",),
                                             vmem_limit_bytes=VMEM_LIMIT),
        name="ffn_attn",
    )(*args)


HS_BATCH = 8


def _hgrn_sample_body(q_ref, k_ref, v_ref, lf_ref, gh_ref, ng_ref, s0_ref, o_ref, s_ref, *, n_new):
    row = lax.broadcasted_iota(jnp.int32, (n_new, 1), 0)
    tn = (((0,), (0,)), ((), ()))
    ones12 = jnp.ones((3 * n_new, HG_DV), BF16)

    sub = 8
    per_tile = sub // n_new

    def tile(t, carry):
        rows = pl.ds(pl.multiple_of(t * sub, sub), sub)
        units = []
        for h in range(HG_HEADS):
            cols = slice(h * LANES, (h + 1) * LANES)
            q8, k8, v8, g8 = q_ref[rows, cols], k_ref[rows, cols], v_ref[rows, cols], lf_ref[rows, cols]
            for i in range(per_tile):
                part = slice(i * n_new, (i + 1) * n_new)
                q, k, v, g = q8[part], k8[part], v8[part], g8[part]
                cum = jnp.zeros((n_new, LANES), F32)
                for j in range(n_new):
                    cum = cum + jnp.where(row >= j, g[j:j + 1, :], 0.0)
                cum_end = cum[n_new - 1:n_new, :]
                units.append(dict(h=h, b=t * per_tile + i, v=v, g=g, q_dec=q * jnp.exp(cum),
                                  k_dec=k * jnp.exp(-cum), k_end=k * jnp.exp(cum_end - cum)))
        for u in units:
            u["s0"] = s0_ref[u["b"], u["h"]]
            u["o"] = jnp.dot(u["q_dec"].astype(BF16), u["s0"].astype(BF16), preferred_element_type=F32)
        for u in units:
            dec = lax.dot_general(jnp.concatenate(_split3(u["g"]), axis=0), ones12, tn,
                                  preferred_element_type=F32)
            upd = lax.dot_general(u["k_end"].astype(BF16), u["v"].astype(BF16), tn,
                                  preferred_element_type=F32)
            s_ref[u["b"], u["h"]] = jnp.exp(dec) * u["s0"] + upd
        for h in range(HG_HEADS):
            cols = slice(h * LANES, (h + 1) * LANES)
            outs = []
            for u in units[h * per_tile:(h + 1) * per_tile]:
                o = u["o"]
                for j in range(n_new):
                    a_j = jnp.sum(u["q_dec"] * u["k_dec"][j:j + 1, :], axis=-1, keepdims=True)
                    o = o + jnp.where(row >= j, a_j, 0.0) * u["v"][j:j + 1, :]
                outs.append(o * lax.rsqrt(jnp.mean(o * o, axis=-1, keepdims=True) + LN_EPS))
            gate = gh_ref[rows, cols]
            o_ref[rows, cols] = jnp.concatenate(outs, axis=0) * ng_ref[:, cols] * (gate * _sigmoid(gate))
        return carry

    lax.fori_loop(0, HS_BATCH // per_tile, tile, 0)


def hgrn_sample(qh, kh, ih, lf, gh, norm_g, s0, n_new):
    bsz = s0.shape[0]
    blk = pl.BlockSpec((HS_BATCH * n_new, HG_WIDTH), lambda i: (i, 0))
    sblk = pl.BlockSpec((HS_BATCH, HG_HEADS, HG_DK, HG_DV), lambda i: (i, 0, 0, 0))
    return pl.pallas_call(
        functools.partial(_hgrn_sample_body, n_new=n_new),
        grid=(bsz // HS_BATCH,),
        in_specs=[blk, blk, blk, blk, blk, _const_spec((1, HG_WIDTH)), sblk],
        out_specs=[blk, sblk],
        out_shape=[jax.ShapeDtypeStruct((bsz * n_new, HG_WIDTH), F32),
                   jax.ShapeDtypeStruct(s0.shape, F32)],
        compiler_params=pltpu.CompilerParams(dimension_semantics=("arbitrary",),
                                             vmem_limit_bytes=VMEM_LIMIT),
        name="hgrn_sample",
    )(qh, kh, ih, lf, gh, norm_g, s0)


def _rope_tables(pos):
    half = ATT_HEAD_DIM // 2
    inv = ROPE_THETA ** (-jnp.arange(half, dtype=F32) / half)
    ang = pos.astype(F32)[:, None] * inv[None, :]
    cos, sin = jnp.cos(ang), jnp.sin(ang)
    return jnp.tile(cos, (1, 4)), jnp.tile(jnp.concatenate([-sin, sin], axis=1), (1, 2))


def kernel(x_prompt, x_sample, cache_k, cache_v, state_hgrn, ffn1_w_gate, ffn1_w_up, ffn1_w_down, ln1_g, ln1_b, w_in, hg_lower_bound, hg_norm_g, w_out, ln2_g, ln2_b, ffn2_w_gate, ffn2_w_up, ffn2_w_down, ln3_g, ln3_b):
    batch, seq, _ = x_prompt.shape
    dec_batch, n_new, _ = x_sample.shape
    assert ffn1_w_gate.shape[0] == 1, "single layer"
    win_p = min(ATT_WINDOW_MAX, seq)


    cos_p, sin_p = _rope_tables(jnp.arange(seq, dtype=jnp.int32))
    pos_s = PAST_LEN + (jnp.arange(dec_batch * n_new, dtype=jnp.int32) % n_new)
    cos_s, sin_s = _rope_tables(pos_s)

    xp = x_prompt.reshape(batch * seq, D_MODEL)
    xs = x_sample.reshape(dec_batch * n_new, D_MODEL)

    xs1, *w1 = ffn_ln_cast(xs, ffn1_w_gate[0], ffn1_w_up[0], ffn1_w_down[0], ln1_g, ln1_b)
    qs, ks, vs, qhs, khs, ihs, lfs, ghs, w_in_b = proj(xs1, w_in[0], cos_s, sin_s, hg_lower_bound, 1)

    samp = (qs, ks, vs,
            cache_k[0].transpose(0, 2, 3, 1),
            cache_v[0].transpose(0, 2, 3, 1), n_new)
    steps = (batch * seq) // FA_ROWS
    per_call = steps // (ATT_HEADS // FA_HEADS)
    assert 2 * per_call == dec_batch, "the two prompt FFN calls must cover the sample batch"

    xp1, o_t, nk, nv = ffn_attn(xp, *w1, ln1_g, ln1_b, *samp, 0)
    q, k, v, qh, kh, ih, lf, gh = proj(xp1, w_in_b, cos_p, sin_p, hg_lower_bound, seq // ROW_TILE)
    o_att, kt_p, vt_p = attn_prompt(q, k, v, batch, seq)
    o_hg, s_p, *w2, w_out_b = hgrn_prompt(qh, kh, ih, lf, gh, hg_norm_g, batch, seq,
                                          cast=(ffn2_w_gate[0], ffn2_w_up[0], ffn2_w_down[0], w_out[0]))
    yp, o_t, nk, nv = ffn_attn(xp1, *w2, ln3_g, ln3_b, *samp, per_call, prev=(o_t, nk, nv),
                               mix=(o_att, o_hg, w_out_b, ln2_g, ln2_b))

    o_att_s = o_t.reshape(dec_batch * n_new, ATT_WIDTH)

    o_hg_s, s_s = hgrn_sample(qhs, khs, ihs, lfs, ghs, hg_norm_g, state_hgrn[0], n_new)
    xs2 = mix_ln(xs1, o_att_s, o_hg_s, w_out_b, ln2_g, ln2_b)
    ys = ffn_ln(xs2, *w2, ln3_g, ln3_b)

    return (yp.reshape(batch, seq, D_MODEL),
            ys.reshape(dec_batch, n_new, D_MODEL),
            kt_p.transpose(0, 3, 1, 2)[None],
            vt_p.transpose(0, 3, 1, 2)[None],
            s_p[None],
            nk.transpose(0, 3, 1, 2)[None],
            nv.transpose(0, 3, 1, 2)[None],
            s_s[None])
```

```python
import functools

import jax
import jax.numpy as jnp
from jax import lax
from jax.experimental import pallas as pl
from jax.experimental.pallas import tpu as pltpu

F32 = jnp.float32
BF16 = jnp.bfloat16

D_MODEL = 1024
ATT_WIDTH = 512
HG_WIDTH = 512
ATT_HEAD_DIM = 64
ATT_HEADS = 8
DILATIONS = ((128, 1), (512, 4), (2048, 16))
ATT_WINDOW_MAX = 2048
ATT_BLOCK = 128
ROPE_THETA = 10000.0
HG_HEADS = 4
HG_DK = 128
HG_DV = 128
HG_CHUNK = 32
D_FF = 2816
IN_COLS = 3 * ATT_WIDTH + 4 * HG_WIDTH
PAST_LEN = 8192
DN_ALPHA = 2.0 ** 0.25
LN_EPS = 1e-5
NEG_INF = -1e30

LANES = 128
ROW_TILE = 512
VMEM_LIMIT = 56 * 1024 * 1024


def _const_spec(shape, block=None):
    index = tuple(block) if block is not None else (0,) * len(shape)
    return pl.BlockSpec(shape, lambda *_: index, pipeline_mode=pl.Buffered(1))


def _layer_norm(r, g, b):
    mu = jnp.mean(r, axis=-1, keepdims=True)
    c = r - mu
    var = jnp.mean(c * c, axis=-1, keepdims=True)
    return c * lax.rsqrt(var + LN_EPS) * g + b


def _sigmoid(x):
    return 1.0 / (1.0 + jnp.exp(-x))


def _split3(x):
    hi = x.astype(BF16)
    r1 = x - hi.astype(F32)
    mid = r1.astype(BF16)
    lo = (r1 - mid.astype(F32)).astype(BF16)
    return hi, mid, lo


def _ffn_ln_body(x_ref, wg_ref, wu_ref, wd_ref, g_ref, b_ref, o_ref):
    x = x_ref[...]
    xb = x.astype(BF16)
    gate = jnp.dot(xb, wg_ref[...], preferred_element_type=F32)
    up = jnp.dot(xb, wu_ref[...], preferred_element_type=F32)
    h = (gate * _sigmoid(gate) * up).astype(BF16)
    y = jnp.dot(h, wd_ref[...], preferred_element_type=F32)
    o_ref[...] = _layer_norm(DN_ALPHA * x + 0.5 * y, g_ref[...], b_ref[...])


def ffn_ln(x, wg, wu, wd, g, b):
    m = x.shape[0]
    row = pl.BlockSpec((ROW_TILE, D_MODEL), lambda i: (i, 0))
    return pl.pallas_call(
        _ffn_ln_body,
        grid=(m // ROW_TILE,),
        in_specs=[row, _const_spec((D_MODEL, D_FF)), _const_spec((D_MODEL, D_FF)),
                  _const_spec((D_FF, D_MODEL)), _const_spec((1, D_MODEL)), _const_spec((1, D_MODEL))],
        out_specs=row,
        out_shape=jax.ShapeDtypeStruct((m, D_MODEL), F32),
        compiler_params=pltpu.CompilerParams(dimension_semantics=("arbitrary",),
                                             vmem_limit_bytes=VMEM_LIMIT),
        name="ffn_ln",
    )(x, wg, wu, wd, g, b)


FFN_CHUNK = 256


def _ffn_ln_cast_body(x_ref, wg_ref, wu_ref, wd_ref, g_ref, b_ref, o_ref, wgb_ref, wub_ref, wdb_ref, acc_ref):
    j = pl.program_id(0)

    @pl.when(j == 0)
    def _():
        acc_ref[...] = jnp.zeros(acc_ref.shape, F32)

    wgb, wub, wdb = wg_ref[...].astype(BF16), wu_ref[...].astype(BF16), wd_ref[...].astype(BF16)
    wgb_ref[...], wub_ref[...], wdb_ref[...] = wgb, wub, wdb
    xb = x_ref[...].astype(BF16)
    gate = jnp.dot(xb, wgb, preferred_element_type=F32)
    up = jnp.dot(xb, wub, preferred_element_type=F32)
    h = (gate * _sigmoid(gate) * up).astype(BF16)
    acc_ref[...] += jnp.dot(h, wdb, preferred_element_type=F32)

    @pl.when(j == pl.num_programs(0) - 1)
    def _():
        o_ref[...] = _layer_norm(DN_ALPHA * x_ref[...] + 0.5 * acc_ref[...], g_ref[...], b_ref[...])


def ffn_ln_cast(x, wg, wu, wd, g, b):
    m = x.shape[0]
    col = pl.BlockSpec((D_MODEL, FFN_CHUNK), lambda j: (0, j))
    rowc = pl.BlockSpec((FFN_CHUNK, D_MODEL), lambda j: (j, 0))
    xs = pl.BlockSpec((m, D_MODEL), lambda j: (0, 0))
    return pl.pallas_call(
        _ffn_ln_cast_body,
        grid=(D_FF // FFN_CHUNK,),
        in_specs=[xs, col, col, rowc, _const_spec((1, D_MODEL)), _const_spec((1, D_MODEL))],
        out_specs=[xs, col, col, rowc],
        out_shape=[jax.ShapeDtypeStruct((m, D_MODEL), F32), jax.ShapeDtypeStruct(wg.shape, BF16),
                   jax.ShapeDtypeStruct(wu.shape, BF16), jax.ShapeDtypeStruct(wd.shape, BF16)],
        scratch_shapes=[pltpu.VMEM((m, D_MODEL), F32)],
        compiler_params=pltpu.CompilerParams(dimension_semantics=("arbitrary",),
                                             vmem_limit_bytes=VMEM_LIMIT),
        name="ffn_ln_cast",
    )(x, wg, wu, wd, g, b)


def _rope(z, cos_t, sin_t):
    lane = lax.broadcasted_iota(jnp.int32, (1, LANES), 1)
    first_half = (lane % ATT_HEAD_DIM) < (ATT_HEAD_DIM // 2)
    outs = []
    for c in range(z.shape[1] // LANES):
        zc = z[:, c * LANES:(c + 1) * LANES]
        partner = jnp.where(first_half, pltpu.roll(zc, LANES - 32, 1), pltpu.roll(zc, 32, 1))
        outs.append(zc * cos_t + partner * sin_t)
    return outs


def _proj_body(x_ref, w_ref, cos_ref, sin_ref, lbraw_ref,
               q_ref, k_ref, v_ref, qh_ref, kh_ref, ih_ref, lf_ref, gh_ref, wb_ref=None):
    xb = x_ref[...].astype(BF16)
    aw = ATT_WIDTH
    base = 3 * aw

    def cols(start, width):
        w = w_ref[:, start:start + width]
        if wb_ref is not None:
            w = w.astype(BF16)
            wb_ref[:, start:start + width] = w
        return jnp.dot(xb, w, preferred_element_type=F32)

    cos_t = cos_ref[...]
    sin_t = sin_ref[...]
    for c, val in enumerate(_rope(cols(0, aw), cos_t, sin_t)):
        q_ref[:, c * LANES:(c + 1) * LANES] = val * (ATT_HEAD_DIM ** -0.5)
    for c, val in enumerate(_rope(cols(aw, aw), cos_t, sin_t)):
        k_ref[:, c * LANES:(c + 1) * LANES] = val
    lbr = lbraw_ref[...]
    mx = jnp.max(lbr, axis=0, keepdims=True)
    ex = jnp.exp(lbr - mx)
    lb = ex[0:1, :] / jnp.sum(ex, axis=0, keepdims=True)
    zf = cols(base + HG_WIDTH, HG_WIDTH)
    lf_ref[...] = jnp.log(lb + (1.0 - lb) * _sigmoid(zf))
    kh_ref[...] = (1.0 - lb) * _sigmoid(-zf)
    zq = cols(base, HG_WIDTH)
    qh_ref[...] = zq * _sigmoid(zq)
    v_ref[...] = cols(2 * aw, aw)
    ih_ref[...] = cols(base + 2 * HG_WIDTH, HG_WIDTH)
    gh_ref[...] = cols(base + 3 * HG_WIDTH, HG_WIDTH)


def proj(x, w_in, cos_t, sin_t, lb_raw, table_tiles):
    m = x.shape[0]
    emit = w_in.dtype == F32
    assert not emit or m == ROW_TILE, "the weight copy is written once, by a single grid step"
    row = pl.BlockSpec((ROW_TILE, D_MODEL), lambda i: (i, 0))
    tab = pl.BlockSpec((ROW_TILE, LANES), lambda i: (i % table_tiles, 0))
    out = pl.BlockSpec((ROW_TILE, ATT_WIDTH), lambda i: (i, 0))
    sds = jax.ShapeDtypeStruct((m, ATT_WIDTH), F32)
    return pl.pallas_call(
        _proj_body,
        grid=(m // ROW_TILE,),
        in_specs=[row, _const_spec((D_MODEL, IN_COLS)), tab, tab, _const_spec((2, HG_WIDTH))],
        out_specs=[out] * 8 + ([pl.BlockSpec((D_MODEL, IN_COLS), lambda i: (0, 0))] if emit else []),
        out_shape=[sds] * 8 + ([jax.ShapeDtypeStruct(w_in.shape, BF16)] if emit else []),
        compiler_params=pltpu.CompilerParams(dimension_semantics=("arbitrary",),
                                             vmem_limit_bytes=VMEM_LIMIT),
        name="proj",
    )(x, w_in, cos_t, sin_t, lb_raw)


def _mix_ln_body(x_ref, oa_ref, oh_ref, wa_ref, wh_ref, g_ref, b_ref, o_ref):
    mix = jnp.dot(oa_ref[...].astype(BF16), wa_ref[...], preferred_element_type=F32)
    mix = mix + jnp.dot(oh_ref[...].astype(BF16), wh_ref[...], preferred_element_type=F32)
    o_ref[...] = _layer_norm(DN_ALPHA * x_ref[...] + mix, g_ref[...], b_ref[...])


def _w_out_specs():
    assert ATT_WIDTH == HG_WIDTH
    return [_const_spec((ATT_WIDTH, D_MODEL), (0, 0)), _const_spec((HG_WIDTH, D_MODEL), (1, 0))]


def mix_ln(x, o_att, o_hg, w_out, g, b):
    m = x.shape[0]
    row = pl.BlockSpec((ROW_TILE, D_MODEL), lambda i: (i, 0))
    half = pl.BlockSpec((ROW_TILE, ATT_WIDTH), lambda i: (i, 0))
    return pl.pallas_call(
        _mix_ln_body,
        grid=(m // ROW_TILE,),
        in_specs=[row, half, half] + _w_out_specs() + [_const_spec((1, D_MODEL)), _const_spec((1, D_MODEL))],
        out_specs=row,
        out_shape=jax.ShapeDtypeStruct((m, D_MODEL), F32),
        compiler_params=pltpu.CompilerParams(dimension_semantics=("arbitrary",),
                                             vmem_limit_bytes=VMEM_LIMIT),
        name="mix_ln",
    )(x, o_att, o_hg, w_out, w_out, g, b)


ATT_UNROLL = (8, 8, 4)


def _attn_prompt_body(q_ref, k_ref, v_ref, o_ref, kt_ref, vt_ref, acc_ref, m_ref, l_ref, *, seq):
    blk = ATT_BLOCK
    lane = lax.broadcasted_iota(jnp.int32, (1, LANES), 1)
    head0 = lane < ATT_HEAD_DIM
    qi = lax.broadcasted_iota(jnp.int32, (blk, 2 * blk), 0)
    kc = lax.broadcasted_iota(jnp.int32, (blk, 2 * blk), 1)
    dist = qi + blk - kc
    first_mask = (lax.broadcasted_iota(jnp.int32, (blk, blk), 1)
                  <= lax.broadcasted_iota(jnp.int32, (blk, blk), 0))
    nt = (((1,), (1,)), ((), ()))
    log2e = 1.4426950408889634

    for br, (window, dil) in enumerate(DILATIONS):
        w_sub = window // dil
        rel = (dist >= 0) & (dist <= w_sub)
        span = blk * dil
        n_blk = seq // span
        n_units = ATT_UNROLL[br]
        run = min(n_units, n_blk)
        n_res = n_units // run
        runs_per_res = n_blk // run

        def units(it, carry, br=br, dil=dil, rel=rel, span=span, run=run, n_res=n_res,
                  runs_per_res=runs_per_res, n_units=n_units):
            starts, masks, vbs, scores = [], [], [], []
            for rr in range(n_res):
                if runs_per_res == 1:
                    r, n0 = it * n_res + rr, 0
                else:
                    r, n0 = it // runs_per_res, (it % runs_per_res) * run
                base = r + n0 * span

                def kv_block(j):
                    rows = pl.ds(jnp.maximum(base + j * span, r), blk, stride=dil)
                    return k_ref[rows, :].astype(BF16), v_ref[rows, :].astype(BF16)

                prev = None if runs_per_res == 1 else kv_block(-1)
                for j in range(run):
                    cur = kv_block(j)
                    start = base + j * span
                    q = q_ref[pl.ds(start, blk, stride=dil), :] * log2e
                    if prev is None:
                        kb, vb, mask = cur[0], cur[1], first_mask
                    else:
                        kb = jnp.concatenate([prev[0], cur[0]], axis=0)
                        vb = jnp.concatenate([prev[1], cur[1]], axis=0)
                        mask = rel & (kc >= jnp.where(n0 > 0, 0, blk)) if (j == 0 and runs_per_res > 1) else rel
                    starts.append(start)
                    masks.append(mask)
                    vbs.append(vb)
                    for hsel in (head0, ~head0):
                        qh = jnp.where(hsel, q, 0.0).astype(BF16)
                        scores.append(lax.dot_general(qh, kb, nt, preferred_element_type=F32))
                    prev = cur
            probs, ls, ms = [], [], []
            for i, s in enumerate(scores):
                s = jnp.where(masks[i // 2], s, NEG_INF)
                m = jnp.max(s, axis=-1, keepdims=True)
                p = jnp.exp2(s - m)
                ls.append(jnp.sum(p, axis=-1, keepdims=True))
                ms.append(m)
                probs.append(p.astype(BF16))
            outs = [jnp.dot(p, vbs[i // 2], preferred_element_type=F32) for i, p in enumerate(probs)]
            for uu in range(n_units):
                i0, i1 = 2 * uu, 2 * uu + 1
                rows = pl.ds(starts[uu], blk, stride=dil)
                acc_ref[br, rows, :] = jnp.where(head0, outs[i0], outs[i1])
                m_ref[br, rows, :] = jnp.broadcast_to(jnp.where(head0, ms[i0], ms[i1]), (blk, LANES))
                l_ref[br, rows, :] = jnp.broadcast_to(jnp.where(head0, ls[i0], ls[i1]), (blk, LANES))
            return carry

        lax.fori_loop(0, (dil * n_blk) // n_units, units, 0)

    def combine(c, carry):
        rows = pl.ds(pl.multiple_of(c * blk, blk), blk)
        m0, m1, m2 = m_ref[0, rows, :], m_ref[1, rows, :], m_ref[2, rows, :]
        mx = jnp.maximum(jnp.maximum(m0, m1), m2)
        w0, w1, w2 = jnp.exp2(m0 - mx), jnp.exp2(m1 - mx), jnp.exp2(m2 - mx)
        num = w0 * acc_ref[0, rows, :] + w1 * acc_ref[1, rows, :] + w2 * acc_ref[2, rows, :]
        den = w0 * l_ref[0, rows, :] + w1 * l_ref[1, rows, :] + w2 * l_ref[2, rows, :]
        o_ref[rows, :] = num / den
        return carry

    lax.fori_loop(0, seq // blk, combine, 0)

    win = kt_ref.shape[-1]
    for j in range(win // LANES):
        rows = pl.ds(seq - win + j * LANES, LANES)
        kt_ref[0, :, :, j * LANES:(j + 1) * LANES] = k_ref[rows, :].T.reshape(2, ATT_HEAD_DIM, LANES)
        vt_ref[0, :, :, j * LANES:(j + 1) * LANES] = v_ref[rows, :].T.reshape(2, ATT_HEAD_DIM, LANES)


def attn_prompt(q, k, v, batch, seq):
    win = min(ATT_WINDOW_MAX, seq)
    n_hp = ATT_WIDTH // LANES
    blk = pl.BlockSpec((seq, LANES), lambda b, h: (b, h))
    tblk = pl.BlockSpec((1, 2, ATT_HEAD_DIM, win), lambda b, h: (b, h, 0, 0))
    tsds = jax.ShapeDtypeStruct((batch, ATT_HEADS, ATT_HEAD_DIM, win), F32)
    return pl.pallas_call(
        functools.partial(_attn_prompt_body, seq=seq),
        grid=(batch, n_hp),
        in_specs=[blk, blk, blk],
        out_specs=[blk, tblk, tblk],
        out_shape=[jax.ShapeDtypeStruct((batch * seq, ATT_WIDTH), F32), tsds, tsds],
        scratch_shapes=[pltpu.VMEM((3, seq, LANES), F32)] * 3,
        compiler_params=pltpu.CompilerParams(dimension_semantics=("arbitrary", "arbitrary"),
                                             vmem_limit_bytes=VMEM_LIMIT),
        name="attn_prompt",
    )(q, k, v)


def _hgrn_prompt_body(*refs, rows_per_step, n_cast):
    q_ref, k_ref, v_ref, lf_ref, gh_ref, ng_ref = refs[:6]
    o_ref, s_ref = refs[6 + n_cast:8 + n_cast]
    st_ref = refs[-1]
    for w_ref, wb_ref in zip(refs[6:6 + n_cast], refs[8 + n_cast:8 + 2 * n_cast]):
        wb_ref[...] = w_ref[...].astype(BF16)
    grp = LANES
    n_ch = grp // HG_CHUNK
    ri = lax.broadcasted_iota(jnp.int32, (grp, grp), 0)
    ci = lax.broadcasted_iota(jnp.int32, (grp, grp), 1)
    same_chunk = (ri // HG_CHUNK) == (ci // HG_CHUNK)
    causal = same_chunk & (ci <= ri)
    summat = jnp.where(causal, 1.0, 0.0).astype(BF16)
    nt = (((1,), (1,)), ((), ()))
    tb = pl.program_id(1)

    @pl.when(tb == 0)
    def _():
        st_ref[...] = jnp.zeros(st_ref.shape, F32)

    heads = [slice(h * LANES, (h + 1) * LANES) for h in range(HG_HEADS)]
    in_chunk = [(ri // HG_CHUNK) == c for c in range(n_ch)]

    def by_chunk(x):
        return jnp.concatenate([jnp.where(m, x, 0.0).astype(BF16) for m in in_chunk], axis=1)

    def groups(it, carry):
        gs = range(HG_GROUPS)
        rows = [pl.ds(pl.multiple_of((it * HG_GROUPS + g) * grp, grp), grp) for g in gs]
        cum = []
        for g in gs:
            parts = _split3(lf_ref[rows[g], :])
            cum.append(jnp.dot(summat, parts[0], preferred_element_type=F32)
                       + jnp.dot(summat, parts[1], preferred_element_type=F32)
                       + jnp.dot(summat, parts[2], preferred_element_type=F32))
        q_dec, decay, a, vs, k_end = [], [], [], [], []
        for g in gs:
            cum_end = jnp.concatenate(
                [jnp.broadcast_to(cum[g][(c + 1) * HG_CHUNK - 1:(c + 1) * HG_CHUNK, :], (HG_CHUNK, HG_WIDTH))
                 for c in range(n_ch)], axis=0)
            q, k, v = q_ref[rows[g], :], k_ref[rows[g], :], v_ref[rows[g], :]
            q_dec.append(q * jnp.exp(cum[g]))
            k_dec = (k * jnp.exp(-cum[g])).astype(BF16)
            k_end.append(k * jnp.exp(cum_end - cum[g]))
            decay.append(jnp.exp(cum_end))
            q_decb = q_dec[g].astype(BF16)
            vs.append(v)
            a.append([lax.dot_general(q_decb[:, hs], k_dec[:, hs], nt, preferred_element_type=F32)
                      for hs in heads])
        o = [[jnp.dot(jnp.where(causal, a[g][h], 0.0).astype(BF16), vs[g][:, hs].astype(BF16),
                      preferred_element_type=F32) for h, hs in enumerate(heads)] for g in gs]
        ut = [[jnp.dot(vs[g][:, hs].T.astype(BF16), by_chunk(k_end[g][:, hs]), preferred_element_type=F32)
               for hs in heads] for g in gs]
        starts = [[[] for _ in heads] for _ in gs]
        for h, hs in enumerate(heads):
            st = st_ref[h]
            for g in gs:
                for c in range(n_ch):
                    starts[g][h].append(st.astype(BF16))
                    st = (st * decay[g][c * HG_CHUNK:c * HG_CHUNK + 1, hs]
                          + ut[g][h][:, c * LANES:(c + 1) * LANES])
            st_ref[h] = st
        inter = [[lax.dot_general(by_chunk(q_dec[g][:, hs]), jnp.concatenate(starts[g][h], axis=1), nt,
                                  preferred_element_type=F32) for h, hs in enumerate(heads)] for g in gs]
        for g in gs:
            for h, hs in enumerate(heads):
                oh = o[g][h] + inter[g][h]
                ms = jnp.mean(oh * oh, axis=-1, keepdims=True)
                gate = gh_ref[rows[g], hs]
                o_ref[rows[g], hs] = oh * lax.rsqrt(ms + LN_EPS) * ng_ref[:, hs] * (gate * _sigmoid(gate))
        return carry

    lax.fori_loop(0, rows_per_step // (grp * HG_GROUPS), groups, 0)

    @pl.when(tb == pl.num_programs(1) - 1)
    def _():
        for h in range(HG_HEADS):
            s_ref[0, h] = st_ref[h].T


HG_ROWS = 1024
HG_GROUPS = 4


def hgrn_prompt(qh, kh, ih, lf, gh, norm_g, batch, seq, cast=()):
    n_tb = seq // HG_ROWS
    steps = batch * n_tb

    def rows(b, t):
        return (b * n_tb + t, 0)

    blk = pl.BlockSpec((HG_ROWS, HG_WIDTH), rows)
    cast_specs = [pl.BlockSpec((w.shape[0] // steps, w.shape[1]), rows) for w in cast]
    return pl.pallas_call(
        functools.partial(_hgrn_prompt_body, rows_per_step=HG_ROWS, n_cast=len(cast)),
        grid=(batch, n_tb),
        in_specs=[blk, blk, blk, blk, blk, _const_spec((1, HG_WIDTH))] + cast_specs,
        out_specs=[blk, pl.BlockSpec((1, HG_HEADS, HG_DK, HG_DV), lambda b, t: (b, 0, 0, 0))] + cast_specs,
        out_shape=[jax.ShapeDtypeStruct((batch * seq, HG_WIDTH), F32),
                   jax.ShapeDtypeStruct((batch, HG_HEADS, HG_DK, HG_DV), F32)]
                  + [jax.ShapeDtypeStruct(w.shape, BF16) for w in cast],
        scratch_shapes=[pltpu.VMEM((HG_HEADS, HG_DV, HG_DK), F32)],
        compiler_params=pltpu.CompilerParams(dimension_semantics=("arbitrary", "arbitrary"),
                                             vmem_limit_bytes=VMEM_LIMIT),
        name="hgrn_prompt",
    )(qh, kh, ih, lf, gh, norm_g, *cast)


FA_ROWS = 128
FA_HEADS = 4
FA_SLOTS = 4


def _divmod_pow2(s, n):
    assert n & (n - 1) == 0
    if isinstance(s, int):
        return s // n, s % n
    return lax.shift_right_logical(s, n.bit_length() - 1), s & (n - 1)


def _ffn_attn_body(*refs, n_new, past, b_off, mixed, n_steps):
    n_x = 7 if mixed else 1
    wg_ref, wu_ref, wd_ref, g_ref, b_ref, q_ref, kn_ref, vn_ref, kc_hbm, vc_hbm = refs[n_x:n_x + 10]
    y_ref, o_ref, nk_ref, nv_ref, kbuf, vbuf, sems = refs[-7:]
    halves = ATT_HEADS // FA_HEADS
    step = pl.program_id(0)

    split = _divmod_pow2
    odd = (b_off + split(step, halves)[0]) & 1 == 1

    def window_copies(s):
        slot = split(s, FA_SLOTS)[1]
        b_rel, half = split(s, halves)
        src = (b_off + b_rel, pl.ds(half * FA_HEADS, FA_HEADS))
        return (pltpu.make_async_copy(kc_hbm.at[src], kbuf.at[slot], sems.at[slot, 0]),
                pltpu.make_async_copy(vc_hbm.at[src], vbuf.at[slot], sems.at[slot, 1]))

    @pl.when(step == 0)
    def _():
        for s in range(min(FA_SLOTS - 1, n_steps)):
            for c in window_copies(s):
                c.start()

    @pl.when(step + (FA_SLOTS - 1) < n_steps)
    def _():
        for c in window_copies(step + (FA_SLOTS - 1)):
            c.start()

    for c in window_copies(step):
        c.wait()
    slot = split(step, FA_SLOTS)[1]
    kc_ref, vc_ref = kbuf.at[slot], vbuf.at[slot]

    def new_rows(ref):
        both = ref[...]
        return jnp.where(odd, both[n_new:2 * n_new], both[0:n_new])

    q_new, k_new, v_new = new_rows(q_ref), new_rows(kn_ref), new_rows(vn_ref)
    head_cols = [slice(h * ATT_HEAD_DIM, (h + 1) * ATT_HEAD_DIM) for h in range(FA_HEADS)]
    srow = lax.broadcasted_iota(jnp.int32, (n_new, past), 0)
    tcol = lax.broadcasted_iota(jnp.int32, (n_new, past), 1)
    dist = past + srow - tcol
    def on_stride(d, dil):
        return (d & (dil - 1)) == 0 if dil & (dil - 1) == 0 else d % dil == 0

    cnt = jnp.zeros((n_new, past), F32)
    for window, dil in DILATIONS:
        cnt = cnt + jnp.where(on_stride(dist, dil) & (dist <= window), 1.0, 0.0)
    si = lax.broadcasted_iota(jnp.int32, (n_new, n_new), 0)
    sj = lax.broadcasted_iota(jnp.int32, (n_new, n_new), 1)
    dn = si - sj
    cnt_new = jnp.zeros((n_new, n_new), F32)
    for window, dil in DILATIONS:
        cnt_new = cnt_new + jnp.where((dn >= 0) & on_stride(dn, dil) & (dn <= window), 1.0, 0.0)
    lane = lax.broadcasted_iota(jnp.int32, (1, LANES), 1)
    nt = (((1,), (1,)), ((), ()))
    shift = past - n_new
    hs = range(FA_HEADS)

    def scores():
        qs = [q_new[:, c].astype(BF16) for c in head_cols]
        s_c = [jnp.dot(qs[h], kc_ref[h].astype(BF16), preferred_element_type=F32) for h in hs]
        s_n = [lax.dot_general(qs[h], k_new[:, head_cols[h]].astype(BF16), nt, preferred_element_type=F32)
               for h in hs]
        return s_c, s_n

    if mixed:
        x1_ref, oa_ref, oh_ref, wa_ref, wh_ref, g2_ref, b2_ref = refs[:n_x]
        mix = jnp.dot(oa_ref[...].astype(BF16), wa_ref[...], preferred_element_type=F32)
        mix = mix + jnp.dot(oh_ref[...].astype(BF16), wh_ref[...], preferred_element_type=F32)
        s_c, s_n = scores()
        x = _layer_norm(DN_ALPHA * x1_ref[...] + mix, g2_ref[...], b2_ref[...])
    else:
        x = refs[0][...]
    xb = x.astype(BF16)
    gate = jnp.dot(xb, wg_ref[...], preferred_element_type=F32)
    up = jnp.dot(xb, wu_ref[...], preferred_element_type=F32)
    if not mixed:
        s_c, s_n = scores()
    knt = [k_new[:, c].T for c in head_cols]
    vnt = [v_new[:, c].T for c in head_cols]

    hmid = (gate * _sigmoid(gate) * up).astype(BF16)
    y = jnp.dot(hmid, wd_ref[...], preferred_element_type=F32)
    y_ref[...] = _layer_norm(DN_ALPHA * x + 0.5 * y, g_ref[...], b_ref[...])

    p_c, p_n, den = [], [], []
    for h in hs:
        sc = jnp.where(cnt > 0, s_c[h], NEG_INF)
        sn = jnp.where(cnt_new > 0, s_n[h], NEG_INF)
        m = jnp.maximum(jnp.max(sc, axis=-1, keepdims=True), jnp.max(sn, axis=-1, keepdims=True))
        p_c.append((cnt * jnp.exp(sc - m)).astype(BF16))
        p_n.append((cnt_new * jnp.exp(sn - m)).astype(BF16))
        den.append(jnp.sum(p_c[h].astype(F32), axis=-1, keepdims=True)
                   + jnp.sum(p_n[h].astype(F32), axis=-1, keepdims=True))
    outs = []
    for h in hs:
        num = (lax.dot_general(vc_ref[h].astype(BF16), p_c[h], nt, preferred_element_type=F32)
               + lax.dot_general(vnt[h].astype(BF16), p_n[h], nt, preferred_element_type=F32))
        outs.append(num.T / den[h])
    o_ref[0] = jnp.concatenate(outs, axis=1)

    for h in hs:
        for src_ref, new, dst in ((kc_ref, knt[h], nk_ref), (vc_ref, vnt[h], nv_ref)):
            rolled = pltpu.roll(src_ref[h], shift, 1)
            dst[0, h, :, 0:past - LANES] = rolled[:, 0:past - LANES]
            last = rolled[:, past - LANES:past]
            for s in range(n_new):
                col = jnp.broadcast_to(new[:, s:s + 1], last.shape)
                last = jnp.where(lane == LANES - n_new + s, col, last)
            dst[0, h, :, past - LANES:past] = last


def ffn_attn(x, wg, wu, wd, g, b, q, kn, vn, kc, vc, n_new, b_off, prev=None, mix=None):
    m = x.shape[0]
    bsz, n_heads, _, past = kc.shape
    halves = n_heads // FA_HEADS
    steps = m // FA_ROWS
    assert b_off % 2 == 0 and bsz % 2 == 0, "new-token rows are fetched per pair of batch rows"

    def where(i):
        b_rel, half = _divmod_pow2(i, halves)
        return b_off + b_rel, half

    def samp(i):
        return (*where(i), 0, 0)

    row = pl.BlockSpec((FA_ROWS, D_MODEL), lambda i: (i, 0))
    pair = pl.BlockSpec((2 * n_new, FA_HEADS * ATT_HEAD_DIM),
                        lambda i: (_divmod_pow2(where(i)[0], 2)[0], where(i)[1]))
    o_blk = pl.BlockSpec((1, n_new, FA_HEADS * ATT_HEAD_DIM), lambda i: (where(i)[0], 0, where(i)[1]))
    big = pl.BlockSpec((1, FA_HEADS, ATT_HEAD_DIM, past), samp)
    in_specs, args = [row], [x]
    if mix is not None:
        half = pl.BlockSpec((FA_ROWS, ATT_WIDTH), lambda i: (i, 0))
        in_specs += [half, half] + _w_out_specs() + [_const_spec((1, D_MODEL)), _const_spec((1, D_MODEL))]
        o_att, o_hg, w_out, ln_g, ln_b = mix
        args += [o_att, o_hg, w_out, w_out, ln_g, ln_b]
    hbm = pl.BlockSpec(memory_space=pl.ANY)
    in_specs += [_const_spec((D_MODEL, D_FF)), _const_spec((D_MODEL, D_FF)), _const_spec((D_FF, D_MODEL)),
                 _const_spec((1, D_MODEL)), _const_spec((1, D_MODEL)), pair, pair, pair, hbm, hbm]
    args += [wg, wu, wd, g, b, q, kn, vn, kc, vc]
    aliases = {}
    if prev is not None:
        for j, a in enumerate(prev):
            aliases[len(args)] = 1 + j
            in_specs.append(pl.BlockSpec(memory_space=pl.ANY))
            args.append(a)
    big_sds = jax.ShapeDtypeStruct(kc.shape, F32)
    return pl.pallas_call(
        functools.partial(_ffn_attn_body, n_new=n_new, past=past, b_off=b_off, mixed=mix is not None,
                          n_steps=steps),
        grid=(steps,),
        in_specs=in_specs,
        out_specs=[row, o_blk, big, big],
        out_shape=[jax.ShapeDtypeStruct((m, D_MODEL), F32),
                   jax.ShapeDtypeStruct((bsz, n_new, n_heads * ATT_HEAD_DIM), F32), big_sds, big_sds],
        scratch_shapes=[pltpu.VMEM((FA_SLOTS, FA_HEADS, ATT_HEAD_DIM, past), F32),
                        pltpu.VMEM((FA_SLOTS, FA_HEADS, ATT_HEAD_DIM, past), F32),
                        pltpu.SemaphoreType.DMA((FA_SLOTS, 2))],
        input_output_aliases=aliases,
        compiler_params=pltpu.CompilerParams(dimension_semantics=("arbitrary",),
                                             vmem_limit_bytes=VMEM_LIMIT),
        name="ffn_attn",
    )(*args)


HS_BATCH = 8


def _hgrn_sample_body(q_ref, k_ref, v_ref, lf_ref, gh_ref, ng_ref, s0_ref, o_ref, s_ref, *, n_new):
    row = lax.broadcasted_iota(jnp.int32, (n_new, 1), 0)
    tn = (((0,), (0,)), ((), ()))
    ones12 = jnp.ones((3 * n_new, HG_DV), BF16)

    sub = 8
    per_tile = sub // n_new

    def tile(t, carry):
        rows = pl.ds(pl.multiple_of(t * sub, sub), sub)
        units = []
        for h in range(HG_HEADS):
            cols = slice(h * LANES, (h + 1) * LANES)
            q8, k8, v8, g8 = q_ref[rows, cols], k_ref[rows, cols], v_ref[rows, cols], lf_ref[rows, cols]
            for i in range(per_tile):
                part = slice(i * n_new, (i + 1) * n_new)
                q, k, v, g = q8[part], k8[part], v8[part], g8[part]
                cum = jnp.zeros((n_new, LANES), F32)
                for j in range(n_new):
                    cum = cum + jnp.where(row >= j, g[j:j + 1, :], 0.0)
                cum_end = cum[n_new - 1:n_new, :]
                units.append(dict(h=h, b=t * per_tile + i, v=v, g=g, q_dec=q * jnp.exp(cum),
                                  k_dec=k * jnp.exp(-cum), k_end=k * jnp.exp(cum_end - cum)))
        for u in units:
            u["s0"] = s0_ref[u["b"], u["h"]]
            u["o"] = jnp.dot(u["q_dec"].astype(BF16), u["s0"].astype(BF16), preferred_element_type=F32)
        for u in units:
            dec = lax.dot_general(jnp.concatenate(_split3(u["g"]), axis=0), ones12, tn,
                                  preferred_element_type=F32)
            upd = lax.dot_general(u["k_end"].astype(BF16), u["v"].astype(BF16), tn,
                                  preferred_element_type=F32)
            s_ref[u["b"], u["h"]] = jnp.exp(dec) * u["s0"] + upd
        for h in range(HG_HEADS):
            cols = slice(h * LANES, (h + 1) * LANES)
            outs = []
            for u in units[h * per_tile:(h + 1) * per_tile]:
                o = u["o"]
                for j in range(n_new):
                    a_j = jnp.sum(u["q_dec"] * u["k_dec"][j:j + 1, :], axis=-1, keepdims=True)
                    o = o + jnp.where(row >= j, a_j, 0.0) * u["v"][j:j + 1, :]
                outs.append(o * lax.rsqrt(jnp.mean(o * o, axis=-1, keepdims=True) + LN_EPS))
            gate = gh_ref[rows, cols]
            o_ref[rows, cols] = jnp.concatenate(outs, axis=0) * ng_ref[:, cols] * (gate * _sigmoid(gate))
        return carry

    lax.fori_loop(0, HS_BATCH // per_tile, tile, 0)


def hgrn_sample(qh, kh, ih, lf, gh, norm_g, s0, n_new):
    bsz = s0.shape[0]
    blk = pl.BlockSpec((HS_BATCH * n_new, HG_WIDTH), lambda i: (i, 0))
    sblk = pl.BlockSpec((HS_BATCH, HG_HEADS, HG_DK, HG_DV), lambda i: (i, 0, 0, 0))
    return pl.pallas_call(
        functools.partial(_hgrn_sample_body, n_new=n_new),
        grid=(bsz // HS_BATCH,),
        in_specs=[blk, blk, blk, blk, blk, _const_spec((1, HG_WIDTH)), sblk],
        out_specs=[blk, sblk],
        out_shape=[jax.ShapeDtypeStruct((bsz * n_new, HG_WIDTH), F32),
                   jax.ShapeDtypeStruct(s0.shape, F32)],
        compiler_params=pltpu.CompilerParams(dimension_semantics=("arbitrary",),
                                             vmem_limit_bytes=VMEM_LIMIT),
        name="hgrn_sample",
    )(qh, kh, ih, lf, gh, norm_g, s0)


def _rope_tables(pos):
    half = ATT_HEAD_DIM // 2
    inv = ROPE_THETA ** (-jnp.arange(half, dtype=F32) / half)
    ang = pos.astype(F32)[:, None] * inv[None, :]
    cos, sin = jnp.cos(ang), jnp.sin(ang)
    return jnp.tile(cos, (1, 4)), jnp.tile(jnp.concatenate([-sin, sin], axis=1), (1, 2))


def kernel(x_prompt, x_sample, cache_k, cache_v, state_hgrn, ffn1_w_gate, ffn1_w_up, ffn1_w_down, ln1_g, ln1_b, w_in, hg_lower_bound, hg_norm_g, w_out, ln2_g, ln2_b, ffn2_w_gate, ffn2_w_up, ffn2_w_down, ln3_g, ln3_b):
    batch, seq, _ = x_prompt.shape
    dec_batch, n_new, _ = x_sample.shape
    assert ffn1_w_gate.shape[0] == 1, "single layer"
    win_p = min(ATT_WINDOW_MAX, seq)


    cos_p, sin_p = _rope_tables(jnp.arange(seq, dtype=jnp.int32))
    pos_s = PAST_LEN + (jnp.arange(dec_batch * n_new, dtype=jnp.int32) % n_new)
    cos_s, sin_s = _rope_tables(pos_s)

    xp = x_prompt.reshape(batch * seq, D_MODEL)
    xs = x_sample.reshape(dec_batch * n_new, D_MODEL)

    xs1, *w1 = ffn_ln_cast(xs, ffn1_w_gate[0], ffn1_w_up[0], ffn1_w_down[0], ln1_g, ln1_b)
    qs, ks, vs, qhs, khs, ihs, lfs, ghs, w_in_b = proj(xs1, w_in[0], cos_s, sin_s, hg_lower_bound, 1)

    samp = (qs, ks, vs,
            cache_k[0].transpose(0, 2, 3, 1),
            cache_v[0].transpose(0, 2, 3, 1), n_new)
    steps = (batch * seq) // FA_ROWS
    per_call = steps // (ATT_HEADS // FA_HEADS)
    assert 2 * per_call == dec_batch, "the two prompt FFN calls must cover the sample batch"

    xp1, o_t, nk, nv = ffn_attn(xp, *w1, ln1_g, ln1_b, *samp, 0)
    q, k, v, qh, kh, ih, lf, gh = proj(xp1, w_in_b, cos_p, sin_p, hg_lower_bound, seq // ROW_TILE)
    o_att, kt_p, vt_p = attn_prompt(q, k, v, batch, seq)
    o_hg, s_p, *w2, w_out_b = hgrn_prompt(qh, kh, ih, lf, gh, hg_norm_g, batch, seq,
                                          cast=(ffn2_w_gate[0], ffn2_w_up[0], ffn2_w_down[0], w_out[0]))
    yp, o_t, nk, nv = ffn_attn(xp1, *w2, ln3_g, ln3_b, *samp, per_call, prev=(o_t, nk, nv),
                               mix=(o_att, o_hg, w_out_b, ln2_g, ln2_b))

    o_att_s = o_t.reshape(dec_batch * n_new, ATT_WIDTH)

    o_hg_s, s_s = hgrn_sample(qhs, khs, ihs, lfs, ghs, hg_norm_g, state_hgrn[0], n_new)
    xs2 = mix_ln(xs1, o_att_s, o_hg_s, w_out_b, ln2_g, ln2_b)
    ys = ffn_ln(xs2, *w2, ln3_g, ln3_b)

    return (yp.reshape(batch, seq, D_MODEL),
            ys.reshape(dec_batch, n_new, D_MODEL),
            kt_p.transpose(0, 3, 1, 2)[None],
            vt_p.transpose(0, 3, 1, 2)[None],
            s_p[None],
            nk.transpose(0, 3, 1, 2)[None],
            nv.transpose(0, 3, 1, 2)[None],
            s_s[None])
```

```python
import functools

import jax
import jax.numpy as jnp
from jax import lax
from jax.experimental import pallas as pl
from jax.experimental.pallas import tpu as pltpu

F32 = jnp.float32
BF16 = jnp.bfloat16

D_MODEL = 1024
ATT_WIDTH = 512
HG_WIDTH = 512
ATT_HEAD_DIM = 64
ATT_HEADS = 8
DILATIONS = ((128, 1), (512, 4), (2048, 16))
ATT_WINDOW_MAX = 2048
ATT_BLOCK = 128
ROPE_THETA = 10000.0
HG_HEADS = 4
HG_DK = 128
HG_DV = 128
HG_CHUNK = 32
D_FF = 2816
IN_COLS = 3 * ATT_WIDTH + 4 * HG_WIDTH
PAST_LEN = 8192
DN_ALPHA = 2.0 ** 0.25
LN_EPS = 1e-5
NEG_INF = -1e30

LANES = 128
ROW_TILE = 512
VMEM_LIMIT = 56 * 1024 * 1024


def _const_spec(shape, block=None):
    index = tuple(block) if block is not None else (0,) * len(shape)
    return pl.BlockSpec(shape, lambda *_: index, pipeline_mode=pl.Buffered(1))


def _layer_norm(r, g, b):
    mu = jnp.mean(r, axis=-1, keepdims=True)
    c = r - mu
    var = jnp.mean(c * c, axis=-1, keepdims=True)
    return c * lax.rsqrt(var + LN_EPS) * g + b


def _sigmoid(x):
    return 1.0 / (1.0 + jnp.exp(-x))


def _split3(x):
    hi = x.astype(BF16)
    r1 = x - hi.astype(F32)
    mid = r1.astype(BF16)
    lo = (r1 - mid.astype(F32)).astype(BF16)
    return hi, mid, lo


def _w_out_specs():
    assert ATT_WIDTH == HG_WIDTH
    return [_const_spec((ATT_WIDTH, D_MODEL), (0, 0)), _const_spec((HG_WIDTH, D_MODEL), (1, 0))]


FFN_CHUNK = 256


def _ffn_ln_body(*refs, cast, mixed):
    n_x = 7 if mixed else 1
    wg_ref, wu_ref, wd_ref, g_ref, b_ref, o_ref = refs[n_x:n_x + 6]
    xin_ref, xb_ref, acc_ref = refs[-3:]
    j = pl.program_id(0)

    @pl.when(j == 0)
    def _():
        if mixed:
            x1_ref, oa_ref, oh_ref, wa_ref, wh_ref, g2_ref, b2_ref = refs[:n_x]
            mix = jnp.dot(oa_ref[...].astype(BF16), wa_ref[...], preferred_element_type=F32)
            mix = mix + jnp.dot(oh_ref[...].astype(BF16), wh_ref[...], preferred_element_type=F32)
            x = _layer_norm(DN_ALPHA * x1_ref[...] + mix, g2_ref[...], b2_ref[...])
        else:
            x = refs[0][...]
        xin_ref[...] = x
        xb_ref[...] = x.astype(BF16)
        acc_ref[...] = jnp.zeros(acc_ref.shape, F32)

    wgb, wub, wdb = wg_ref[...], wu_ref[...], wd_ref[...]
    if cast:
        wgb, wub, wdb = wgb.astype(BF16), wub.astype(BF16), wdb.astype(BF16)
        for dst, w in zip(refs[n_x + 6:n_x + 9], (wgb, wub, wdb)):
            dst[...] = w
    xb = xb_ref[...]
    gate = jnp.dot(xb, wgb, preferred_element_type=F32)
    up = jnp.dot(xb, wub, preferred_element_type=F32)
    h = (gate * _sigmoid(gate) * up).astype(BF16)
    acc_ref[...] += jnp.dot(h, wdb, preferred_element_type=F32)

    @pl.when(j == pl.num_programs(0) - 1)
    def _():
        o_ref[...] = _layer_norm(DN_ALPHA * xin_ref[...] + 0.5 * acc_ref[...], g_ref[...], b_ref[...])


def ffn_ln(x, wg, wu, wd, g, b, mix=None):
    m = x.shape[0]
    cast = wg.dtype == F32
    col = pl.BlockSpec((D_MODEL, FFN_CHUNK), lambda j: (0, j))
    rowc = pl.BlockSpec((FFN_CHUNK, D_MODEL), lambda j: (j, 0))
    xs = pl.BlockSpec((m, D_MODEL), lambda j: (0, 0))
    in_specs, args = [xs], [x]
    if mix is not None:
        half = pl.BlockSpec((m, ATT_WIDTH), lambda j: (0, 0))
        in_specs += [half, half] + _w_out_specs() + [_const_spec((1, D_MODEL)), _const_spec((1, D_MODEL))]
        o_att, o_hg, w_out, ln_g, ln_b = mix
        args += [o_att, o_hg, w_out, w_out, ln_g, ln_b]
    in_specs += [col, col, rowc, _const_spec((1, D_MODEL)), _const_spec((1, D_MODEL))]
    args += [wg, wu, wd, g, b]
    out_specs = [xs] + ([col, col, rowc] if cast else [])
    out_shape = [jax.ShapeDtypeStruct((m, D_MODEL), F32)]
    if cast:
        out_shape += [jax.ShapeDtypeStruct(w.shape, BF16) for w in (wg, wu, wd)]
    return pl.pallas_call(
        functools.partial(_ffn_ln_body, cast=cast, mixed=mix is not None),
        grid=(D_FF // FFN_CHUNK,),
        in_specs=in_specs,
        out_specs=out_specs,
        out_shape=out_shape,
        scratch_shapes=[pltpu.VMEM((m, D_MODEL), F32), pltpu.VMEM((m, D_MODEL), BF16),
                        pltpu.VMEM((m, D_MODEL), F32)],
        compiler_params=pltpu.CompilerParams(dimension_semantics=("arbitrary",),
                                             vmem_limit_bytes=VMEM_LIMIT),
        name="ffn_ln",
    )(*args)


def _rope(z, cos_t, sin_t):
    lane = lax.broadcasted_iota(jnp.int32, (1, LANES), 1)
    first_half = (lane % ATT_HEAD_DIM) < (ATT_HEAD_DIM // 2)
    outs = []
    for c in range(z.shape[1] // LANES):
        zc = z[:, c * LANES:(c + 1) * LANES]
        partner = jnp.where(first_half, pltpu.roll(zc, LANES - 32, 1), pltpu.roll(zc, 32, 1))
        outs.append(zc * cos_t + partner * sin_t)
    return outs


def _proj_body(x_ref, w_ref, cos_ref, sin_ref, lbraw_ref,
               q_ref, k_ref, v_ref, qh_ref, kh_ref, ih_ref, lf_ref, gh_ref, wb_ref=None):
    xb = x_ref[...].astype(BF16)
    aw = ATT_WIDTH
    base = 3 * aw

    def cols(start, width):
        w = w_ref[:, start:start + width]
        if wb_ref is not None:
            w = w.astype(BF16)
            wb_ref[:, start:start + width] = w
        return jnp.dot(xb, w, preferred_element_type=F32)

    cos_t = cos_ref[...]
    sin_t = sin_ref[...]
    for c, val in enumerate(_rope(cols(0, aw), cos_t, sin_t)):
        q_ref[:, c * LANES:(c + 1) * LANES] = val * (ATT_HEAD_DIM ** -0.5)
    for c, val in enumerate(_rope(cols(aw, aw), cos_t, sin_t)):
        k_ref[:, c * LANES:(c + 1) * LANES] = val
    lbr = lbraw_ref[...]
    mx = jnp.max(lbr, axis=0, keepdims=True)
    ex = jnp.exp(lbr - mx)
    lb = ex[0:1, :] / jnp.sum(ex, axis=0, keepdims=True)
    zf = cols(base + HG_WIDTH, HG_WIDTH)
    lf_ref[...] = jnp.log(lb + (1.0 - lb) * _sigmoid(zf))
    kh_ref[...] = (1.0 - lb) * _sigmoid(-zf)
    zq = cols(base, HG_WIDTH)
    qh_ref[...] = zq * _sigmoid(zq)
    v_ref[...] = cols(2 * aw, aw)
    ih_ref[...] = cols(base + 2 * HG_WIDTH, HG_WIDTH)
    gh_ref[...] = cols(base + 3 * HG_WIDTH, HG_WIDTH)


def proj(x, w_in, cos_t, sin_t, lb_raw, table_tiles):
    m = x.shape[0]
    emit = w_in.dtype == F32
    assert not emit or m == ROW_TILE, "the weight copy is written once, by a single grid step"
    row = pl.BlockSpec((ROW_TILE, D_MODEL), lambda i: (i, 0))
    tab = pl.BlockSpec((ROW_TILE, LANES), lambda i: (i % table_tiles, 0))
    out = pl.BlockSpec((ROW_TILE, ATT_WIDTH), lambda i: (i, 0))
    sds = jax.ShapeDtypeStruct((m, ATT_WIDTH), F32)
    return pl.pallas_call(
        _proj_body,
        grid=(m // ROW_TILE,),
        in_specs=[row, _const_spec((D_MODEL, IN_COLS)), tab, tab, _const_spec((2, HG_WIDTH))],
        out_specs=[out] * 8 + ([pl.BlockSpec((D_MODEL, IN_COLS), lambda i: (0, 0))] if emit else []),
        out_shape=[sds] * 8 + ([jax.ShapeDtypeStruct(w_in.shape, BF16)] if emit else []),
        compiler_params=pltpu.CompilerParams(dimension_semantics=("arbitrary",),
                                             vmem_limit_bytes=VMEM_LIMIT),
        name="proj",
    )(x, w_in, cos_t, sin_t, lb_raw)


ATT_UNROLL = (8, 8, 4)


def _attn_prompt_body(q_ref, k_ref, v_ref, o_ref, kt_ref, vt_ref, acc_ref, m_ref, l_ref, *, seq):
    blk = ATT_BLOCK
    lane = lax.broadcasted_iota(jnp.int32, (1, LANES), 1)
    head0 = lane < ATT_HEAD_DIM
    qi = lax.broadcasted_iota(jnp.int32, (blk, 2 * blk), 0)
    kc = lax.broadcasted_iota(jnp.int32, (blk, 2 * blk), 1)
    dist = qi + blk - kc
    first_mask = (lax.broadcasted_iota(jnp.int32, (blk, blk), 1)
                  <= lax.broadcasted_iota(jnp.int32, (blk, blk), 0))
    nt = (((1,), (1,)), ((), ()))
    log2e = 1.4426950408889634

    for br, (window, dil) in enumerate(DILATIONS):
        w_sub = window // dil
        rel = (dist >= 0) & (dist <= w_sub)
        span = blk * dil
        n_blk = seq // span
        n_units = ATT_UNROLL[br]
        run = min(n_units, n_blk)
        n_res = n_units // run
        runs_per_res = n_blk // run

        def units(it, carry, br=br, dil=dil, rel=rel, span=span, run=run, n_res=n_res,
                  runs_per_res=runs_per_res, n_units=n_units):
            starts, masks, vbs, scores = [], [], [], []
            for rr in range(n_res):
                if runs_per_res == 1:
                    r, n0 = it * n_res + rr, 0
                else:
                    r, n0 = it // runs_per_res, (it % runs_per_res) * run
                base = r + n0 * span

                def kv_block(j):
                    rows = pl.ds(jnp.maximum(base + j * span, r), blk, stride=dil)
                    return k_ref[rows, :].astype(BF16), v_ref[rows, :].astype(BF16)

                prev = None if runs_per_res == 1 else kv_block(-1)
                for j in range(run):
                    cur = kv_block(j)
                    start = base + j * span
                    q = q_ref[pl.ds(start, blk, stride=dil), :] * log2e
                    if prev is None:
                        kb, vb, mask = cur[0], cur[1], first_mask
                    else:
                        kb = jnp.concatenate([prev[0], cur[0]], axis=0)
                        vb = jnp.concatenate([prev[1], cur[1]], axis=0)
                        mask = rel & (kc >= jnp.where(n0 > 0, 0, blk)) if (j == 0 and runs_per_res > 1) else rel
                    starts.append(start)
                    masks.append(mask)
                    vbs.append(vb)
                    for hsel in (head0, ~head0):
                        qh = jnp.where(hsel, q, 0.0).astype(BF16)
                        scores.append(lax.dot_general(qh, kb, nt, preferred_element_type=F32))
                    prev = cur
            probs, ls, ms = [], [], []
            for i, s in enumerate(scores):
                s = jnp.where(masks[i // 2], s, NEG_INF)
                m = jnp.max(s, axis=-1, keepdims=True)
                p = jnp.exp2(s - m)
                ls.append(jnp.sum(p, axis=-1, keepdims=True))
                ms.append(m)
                probs.append(p.astype(BF16))
            outs = [jnp.dot(p, vbs[i // 2], preferred_element_type=F32) for i, p in enumerate(probs)]
            for uu in range(n_units):
                i0, i1 = 2 * uu, 2 * uu + 1
                rows = pl.ds(starts[uu], blk, stride=dil)
                acc_ref[br, rows, :] = jnp.where(head0, outs[i0], outs[i1])
                m_ref[br, rows, :] = jnp.broadcast_to(jnp.where(head0, ms[i0], ms[i1]), (blk, LANES))
                l_ref[br, rows, :] = jnp.broadcast_to(jnp.where(head0, ls[i0], ls[i1]), (blk, LANES))
            return carry

        lax.fori_loop(0, (dil * n_blk) // n_units, units, 0)

    def combine(c, carry):
        rows = pl.ds(pl.multiple_of(c * blk, blk), blk)
        m0, m1, m2 = m_ref[0, rows, :], m_ref[1, rows, :], m_ref[2, rows, :]
        mx = jnp.maximum(jnp.maximum(m0, m1), m2)
        w0, w1, w2 = jnp.exp2(m0 - mx), jnp.exp2(m1 - mx), jnp.exp2(m2 - mx)
        num = w0 * acc_ref[0, rows, :] + w1 * acc_ref[1, rows, :] + w2 * acc_ref[2, rows, :]
        den = w0 * l_ref[0, rows, :] + w1 * l_ref[1, rows, :] + w2 * l_ref[2, rows, :]
        o_ref[rows, :] = num / den
        return carry

    lax.fori_loop(0, seq // blk, combine, 0)

    win = kt_ref.shape[-1]
    for j in range(win // LANES):
        rows = pl.ds(seq - win + j * LANES, LANES)
        kt_ref[0, :, :, j * LANES:(j + 1) * LANES] = k_ref[rows, :].T.reshape(2, ATT_HEAD_DIM, LANES)
        vt_ref[0, :, :, j * LANES:(j + 1) * LANES] = v_ref[rows, :].T.reshape(2, ATT_HEAD_DIM, LANES)


def attn_prompt(q, k, v, batch, seq):
    win = min(ATT_WINDOW_MAX, seq)
    n_hp = ATT_WIDTH // LANES
    blk = pl.BlockSpec((seq, LANES), lambda b, h: (b, h))
    tblk = pl.BlockSpec((1, 2, ATT_HEAD_DIM, win), lambda b, h: (b, h, 0, 0))
    tsds = jax.ShapeDtypeStruct((batch, ATT_HEADS, ATT_HEAD_DIM, win), F32)
    return pl.pallas_call(
        functools.partial(_attn_prompt_body, seq=seq),
        grid=(batch, n_hp),
        in_specs=[blk, blk, blk],
        out_specs=[blk, tblk, tblk],
        out_shape=[jax.ShapeDtypeStruct((batch * seq, ATT_WIDTH), F32), tsds, tsds],
        scratch_shapes=[pltpu.VMEM((3, seq, LANES), F32)] * 3,
        compiler_params=pltpu.CompilerParams(dimension_semantics=("arbitrary", "arbitrary"),
                                             vmem_limit_bytes=VMEM_LIMIT),
        name="attn_prompt",
    )(q, k, v)


def _hgrn_prompt_body(*refs, rows_per_step, n_cast):
    q_ref, k_ref, v_ref, lf_ref, gh_ref, ng_ref = refs[:6]
    o_ref, s_ref = refs[6 + n_cast:8 + n_cast]
    st_ref = refs[-1]
    for w_ref, wb_ref in zip(refs[6:6 + n_cast], refs[8 + n_cast:8 + 2 * n_cast]):
        wb_ref[...] = w_ref[...].astype(BF16)
    grp = LANES
    n_ch = grp // HG_CHUNK
    ri = lax.broadcasted_iota(jnp.int32, (grp, grp), 0)
    ci = lax.broadcasted_iota(jnp.int32, (grp, grp), 1)
    same_chunk = (ri // HG_CHUNK) == (ci // HG_CHUNK)
    causal = same_chunk & (ci <= ri)
    summat = jnp.where(causal, 1.0, 0.0).astype(BF16)
    nt = (((1,), (1,)), ((), ()))
    tb = pl.program_id(1)

    @pl.when(tb == 0)
    def _():
        st_ref[...] = jnp.zeros(st_ref.shape, F32)

    heads = [slice(h * LANES, (h + 1) * LANES) for h in range(HG_HEADS)]
    in_chunk = [(ri // HG_CHUNK) == c for c in range(n_ch)]

    def by_chunk(x):
        return jnp.concatenate([jnp.where(m, x, 0.0).astype(BF16) for m in in_chunk], axis=1)

    def groups(it, carry):
        gs = range(HG_GROUPS)
        rows = [pl.ds(pl.multiple_of((it * HG_GROUPS + g) * grp, grp), grp) for g in gs]
        cum = []
        for g in gs:
            parts = _split3(lf_ref[rows[g], :])
            cum.append(jnp.dot(summat, parts[0], preferred_element_type=F32)
                       + jnp.dot(summat, parts[1], preferred_element_type=F32)
                       + jnp.dot(summat, parts[2], preferred_element_type=F32))
        q_dec, decay, a, vs, k_end = [], [], [], [], []
        for g in gs:
            cum_end = jnp.concatenate(
                [jnp.broadcast_to(cum[g][(c + 1) * HG_CHUNK - 1:(c + 1) * HG_CHUNK, :], (HG_CHUNK, HG_WIDTH))
                 for c in range(n_ch)], axis=0)
            q, k, v = q_ref[rows[g], :], k_ref[rows[g], :], v_ref[rows[g], :]
            q_dec.append(q * jnp.exp(cum[g]))
            k_dec = (k * jnp.exp(-cum[g])).astype(BF16)
            k_end.append(k * jnp.exp(cum_end - cum[g]))
            decay.append(jnp.exp(cum_end))
            q_decb = q_dec[g].astype(BF16)
            vs.append(v)
            a.append([lax.dot_general(q_decb[:, hs], k_dec[:, hs], nt, preferred_element_type=F32)
                      for hs in heads])
        o = [[jnp.dot(jnp.where(causal, a[g][h], 0.0).astype(BF16), vs[g][:, hs].astype(BF16),
                      preferred_element_type=F32) for h, hs in enumerate(heads)] for g in gs]
        ut = [[jnp.dot(vs[g][:, hs].T.astype(BF16), by_chunk(k_end[g][:, hs]), preferred_element_type=F32)
               for hs in heads] for g in gs]
        starts = [[[] for _ in heads] for _ in gs]
        for h, hs in enumerate(heads):
            st = st_ref[h]
            for g in gs:
                for c in range(n_ch):
                    starts[g][h].append(st.astype(BF16))
                    st = (st * decay[g][c * HG_CHUNK:c * HG_CHUNK + 1, hs]
                          + ut[g][h][:, c * LANES:(c + 1) * LANES])
            st_ref[h] = st
        inter = [[lax.dot_general(by_chunk(q_dec[g][:, hs]), jnp.concatenate(starts[g][h], axis=1), nt,
                                  preferred_element_type=F32) for h, hs in enumerate(heads)] for g in gs]
        for g in gs:
            for h, hs in enumerate(heads):
                oh = o[g][h] + inter[g][h]
                ms = jnp.mean(oh * oh, axis=-1, keepdims=True)
                gate = gh_ref[rows[g], hs]
                o_ref[rows[g], hs] = oh * lax.rsqrt(ms + LN_EPS) * ng_ref[:, hs] * (gate * _sigmoid(gate))
        return carry

    lax.fori_loop(0, rows_per_step // (grp * HG_GROUPS), groups, 0)

    @pl.when(tb == pl.num_programs(1) - 1)
    def _():
        for h in range(HG_HEADS):
            s_ref[0, h] = st_ref[h].T


HG_ROWS = 1024
HG_GROUPS = 4


def hgrn_prompt(qh, kh, ih, lf, gh, norm_g, batch, seq, cast=()):
    n_tb = seq // HG_ROWS
    steps = batch * n_tb

    def rows(b, t):
        return (b * n_tb + t, 0)

    blk = pl.BlockSpec((HG_ROWS, HG_WIDTH), rows)
    cast_specs = [pl.BlockSpec((w.shape[0] // steps, w.shape[1]), rows) for w in cast]
    return pl.pallas_call(
        functools.partial(_hgrn_prompt_body, rows_per_step=HG_ROWS, n_cast=len(cast)),
        grid=(batch, n_tb),
        in_specs=[blk, blk, blk, blk, blk, _const_spec((1, HG_WIDTH))] + cast_specs,
        out_specs=[blk, pl.BlockSpec((1, HG_HEADS, HG_DK, HG_DV), lambda b, t: (b, 0, 0, 0))] + cast_specs,
        out_shape=[jax.ShapeDtypeStruct((batch * seq, HG_WIDTH), F32),
                   jax.ShapeDtypeStruct((batch, HG_HEADS, HG_DK, HG_DV), F32)]
                  + [jax.ShapeDtypeStruct(w.shape, BF16) for w in cast],
        scratch_shapes=[pltpu.VMEM((HG_HEADS, HG_DV, HG_DK), F32)],
        compiler_params=pltpu.CompilerParams(dimension_semantics=("arbitrary", "arbitrary"),
                                             vmem_limit_bytes=VMEM_LIMIT),
        name="hgrn_prompt",
    )(qh, kh, ih, lf, gh, norm_g, *cast)


FA_ROWS = 128
FA_HEADS = 4
FA_SLOTS = 4


def _divmod_pow2(s, n):
    assert n & (n - 1) == 0
    if isinstance(s, int):
        return s // n, s % n
    return lax.shift_right_logical(s, n.bit_length() - 1), s & (n - 1)


def _ffn_attn_body(*refs, n_new, past, b_off, mixed, n_steps):
    n_x = 7 if mixed else 1
    wg_ref, wu_ref, wd_ref, g_ref, b_ref, q_ref, kn_ref, vn_ref, kc_hbm, vc_hbm = refs[n_x:n_x + 10]
    y_ref, o_ref, nk_ref, nv_ref, kbuf, vbuf, sems = refs[-7:]
    halves = ATT_HEADS // FA_HEADS
    step = pl.program_id(0)

    split = _divmod_pow2
    odd = (b_off + split(step, halves)[0]) & 1 == 1

    def window_copies(s):
        slot = split(s, FA_SLOTS)[1]
        b_rel, half = split(s, halves)
        src = (b_off + b_rel, pl.ds(half * FA_HEADS, FA_HEADS))
        return (pltpu.make_async_copy(kc_hbm.at[src], kbuf.at[slot], sems.at[slot, 0]),
                pltpu.make_async_copy(vc_hbm.at[src], vbuf.at[slot], sems.at[slot, 1]))

    @pl.when(step == 0)
    def _():
        for s in range(min(FA_SLOTS - 1, n_steps)):
            for c in window_copies(s):
                c.start()

    @pl.when(step + (FA_SLOTS - 1) < n_steps)
    def _():
        for c in window_copies(step + (FA_SLOTS - 1)):
            c.start()

    for c in window_copies(step):
        c.wait()
    slot = split(step, FA_SLOTS)[1]
    kc_ref, vc_ref = kbuf.at[slot], vbuf.at[slot]

    def new_rows(ref):
        both = ref[...]
        return jnp.where(odd, both[n_new:2 * n_new], both[0:n_new])

    q_new, k_new, v_new = new_rows(q_ref), new_rows(kn_ref), new_rows(vn_ref)
    head_cols = [slice(h * ATT_HEAD_DIM, (h + 1) * ATT_HEAD_DIM) for h in range(FA_HEADS)]
    srow = lax.broadcasted_iota(jnp.int32, (n_new, past), 0)
    tcol = lax.broadcasted_iota(jnp.int32, (n_new, past), 1)
    dist = past + srow - tcol
    def on_stride(d, dil):
        return (d & (dil - 1)) == 0 if dil & (dil - 1) == 0 else d % dil == 0

    cnt = jnp.zeros((n_new, past), F32)
    for window, dil in DILATIONS:
        cnt = cnt + jnp.where(on_stride(dist, dil) & (dist <= window), 1.0, 0.0)
    si = lax.broadcasted_iota(jnp.int32, (n_new, n_new), 0)
    sj = lax.broadcasted_iota(jnp.int32, (n_new, n_new), 1)
    dn = si - sj
    cnt_new = jnp.zeros((n_new, n_new), F32)
    for window, dil in DILATIONS:
        cnt_new = cnt_new + jnp.where((dn >= 0) & on_stride(dn, dil) & (dn <= window), 1.0, 0.0)
    lane = lax.broadcasted_iota(jnp.int32, (1, LANES), 1)
    nt = (((1,), (1,)), ((), ()))
    shift = past - n_new
    hs = range(FA_HEADS)

    def scores():
        qs = [q_new[:, c].astype(BF16) for c in head_cols]
        s_c = [jnp.dot(qs[h], kc_ref[h].astype(BF16), preferred_element_type=F32) for h in hs]
        s_n = [lax.dot_general(qs[h], k_new[:, head_cols[h]].astype(BF16), nt, preferred_element_type=F32)
               for h in hs]
        return s_c, s_n

    if mixed:
        x1_ref, oa_ref, oh_ref, wa_ref, wh_ref, g2_ref, b2_ref = refs[:n_x]
        mix = jnp.dot(oa_ref[...].astype(BF16), wa_ref[...], preferred_element_type=F32)
        mix = mix + jnp.dot(oh_ref[...].astype(BF16), wh_ref[...], preferred_element_type=F32)
        s_c, s_n = scores()
        x = _layer_norm(DN_ALPHA * x1_ref[...] + mix, g2_ref[...], b2_ref[...])
    else:
        x = refs[0][...]
    xb = x.astype(BF16)
    gate = jnp.dot(xb, wg_ref[...], preferred_element_type=F32)
    up = jnp.dot(xb, wu_ref[...], preferred_element_type=F32)
    if not mixed:
        s_c, s_n = scores()
    knt = [k_new[:, c].T for c in head_cols]
    vnt = [v_new[:, c].T for c in head_cols]

    hmid = (gate * _sigmoid(gate) * up).astype(BF16)
    y = jnp.dot(hmid, wd_ref[...], preferred_element_type=F32)
    y_ref[...] = _layer_norm(DN_ALPHA * x + 0.5 * y, g_ref[...], b_ref[...])

    p_c, p_n, den = [], [], []
    for h in hs:
        sc = jnp.where(cnt > 0, s_c[h], NEG_INF)
        sn = jnp.where(cnt_new > 0, s_n[h], NEG_INF)
        m = jnp.maximum(jnp.max(sc, axis=-1, keepdims=True), jnp.max(sn, axis=-1, keepdims=True))
        p_c.append((cnt * jnp.exp(sc - m)).astype(BF16))
        p_n.append((cnt_new * jnp.exp(sn - m)).astype(BF16))
        den.append(jnp.sum(p_c[h].astype(F32), axis=-1, keepdims=True)
                   + jnp.sum(p_n[h].astype(F32), axis=-1, keepdims=True))
    outs = []
    for h in hs:
        num = (lax.dot_general(vc_ref[h].astype(BF16), p_c[h], nt, preferred_element_type=F32)
               + lax.dot_general(vnt[h].astype(BF16), p_n[h], nt, preferred_element_type=F32))
        outs.append(num.T / den[h])
    o_ref[0] = jnp.concatenate(outs, axis=1)

    for h in hs:
        for src_ref, new, dst in ((kc_ref, knt[h], nk_ref), (vc_ref, vnt[h], nv_ref)):
            rolled = pltpu.roll(src_ref[h], shift, 1)
            dst[0, h, :, 0:past - LANES] = rolled[:, 0:past - LANES]
            last = rolled[:, past - LANES:past]
            for s in range(n_new):
                col = jnp.broadcast_to(new[:, s:s + 1], last.shape)
                last = jnp.where(lane == LANES - n_new + s, col, last)
            dst[0, h, :, past - LANES:past] = last


def ffn_attn(x, wg, wu, wd, g, b, q, kn, vn, kc, vc, n_new, b_off, prev=None, mix=None):
    m = x.shape[0]
    bsz, n_heads, _, past = kc.shape
    halves = n_heads // FA_HEADS
    steps = m // FA_ROWS
    assert b_off % 2 == 0 and bsz % 2 == 0, "new-token rows are fetched per pair of batch rows"

    def where(i):
        b_rel, half = _divmod_pow2(i, halves)
        return b_off + b_rel, half

    def samp(i):
        return (*where(i), 0, 0)

    row = pl.BlockSpec((FA_ROWS, D_MODEL), lambda i: (i, 0))
    pair = pl.BlockSpec((2 * n_new, FA_HEADS * ATT_HEAD_DIM),
                        lambda i: (_divmod_pow2(where(i)[0], 2)[0], where(i)[1]))
    o_blk = pl.BlockSpec((1, n_new, FA_HEADS * ATT_HEAD_DIM), lambda i: (where(i)[0], 0, where(i)[1]))
    big = pl.BlockSpec((1, FA_HEADS, ATT_HEAD_DIM, past), samp)
    in_specs, args = [row], [x]
    if mix is not None:
        half = pl.BlockSpec((FA_ROWS, ATT_WIDTH), lambda i: (i, 0))
        in_specs += [half, half] + _w_out_specs() + [_const_spec((1, D_MODEL)), _const_spec((1, D_MODEL))]
        o_att, o_hg, w_out, ln_g, ln_b = mix
        args += [o_att, o_hg, w_out, w_out, ln_g, ln_b]
    hbm = pl.BlockSpec(memory_space=pl.ANY)
    in_specs += [_const_spec((D_MODEL, D_FF)), _const_spec((D_MODEL, D_FF)), _const_spec((D_FF, D_MODEL)),
                 _const_spec((1, D_MODEL)), _const_spec((1, D_MODEL)), pair, pair, pair, hbm, hbm]
    args += [wg, wu, wd, g, b, q, kn, vn, kc, vc]
    aliases = {}
    if prev is not None:
        for j, a in enumerate(prev):
            aliases[len(args)] = 1 + j
            in_specs.append(pl.BlockSpec(memory_space=pl.ANY))
            args.append(a)
    big_sds = jax.ShapeDtypeStruct(kc.shape, F32)
    return pl.pallas_call(
        functools.partial(_ffn_attn_body, n_new=n_new, past=past, b_off=b_off, mixed=mix is not None,
                          n_steps=steps),
        grid=(steps,),
        in_specs=in_specs,
        out_specs=[row, o_blk, big, big],
        out_shape=[jax.ShapeDtypeStruct((m, D_MODEL), F32),
                   jax.ShapeDtypeStruct((bsz, n_new, n_heads * ATT_HEAD_DIM), F32), big_sds, big_sds],
        scratch_shapes=[pltpu.VMEM((FA_SLOTS, FA_HEADS, ATT_HEAD_DIM, past), F32),
                        pltpu.VMEM((FA_SLOTS, FA_HEADS, ATT_HEAD_DIM, past), F32),
                        pltpu.SemaphoreType.DMA((FA_SLOTS, 2))],
        input_output_aliases=aliases,
        compiler_params=pltpu.CompilerParams(dimension_semantics=("arbitrary",),
                                             vmem_limit_bytes=VMEM_LIMIT),
        name="ffn_attn",
    )(*args)


HS_BATCH = 8


def _hgrn_sample_body(q_ref, k_ref, v_ref, lf_ref, gh_ref, ng_ref, s0_ref, o_ref, s_ref, *, n_new):
    row = lax.broadcasted_iota(jnp.int32, (n_new, 1), 0)
    tn = (((0,), (0,)), ((), ()))
    ones12 = jnp.ones((3 * n_new, HG_DV), BF16)

    sub = 8
    per_tile = sub // n_new

    def tile(t, carry):
        rows = pl.ds(pl.multiple_of(t * sub, sub), sub)
        units = []
        for h in range(HG_HEADS):
            cols = slice(h * LANES, (h + 1) * LANES)
            q8, k8, v8, g8 = q_ref[rows, cols], k_ref[rows, cols], v_ref[rows, cols], lf_ref[rows, cols]
            for i in range(per_tile):
                part = slice(i * n_new, (i + 1) * n_new)
                q, k, v, g = q8[part], k8[part], v8[part], g8[part]
                cum = jnp.zeros((n_new, LANES), F32)
                for j in range(n_new):
                    cum = cum + jnp.where(row >= j, g[j:j + 1, :], 0.0)
                cum_end = cum[n_new - 1:n_new, :]
                units.append(dict(h=h, b=t * per_tile + i, v=v, g=g, q_dec=q * jnp.exp(cum),
                                  k_dec=k * jnp.exp(-cum), k_end=k * jnp.exp(cum_end - cum)))
        for u in units:
            u["s0"] = s0_ref[u["b"], u["h"]]
            u["o"] = jnp.dot(u["q_dec"].astype(BF16), u["s0"].astype(BF16), preferred_element_type=F32)
        for u in units:
            dec = lax.dot_general(jnp.concatenate(_split3(u["g"]), axis=0), ones12, tn,
                                  preferred_element_type=F32)
            upd = lax.dot_general(u["k_end"].astype(BF16), u["v"].astype(BF16), tn,
                                  preferred_element_type=F32)
            s_ref[u["b"], u["h"]] = jnp.exp(dec) * u["s0"] + upd
        for h in range(HG_HEADS):
            cols = slice(h * LANES, (h + 1) * LANES)
            outs = []
            for u in units[h * per_tile:(h + 1) * per_tile]:
                o = u["o"]
                for j in range(n_new):
                    a_j = jnp.sum(u["q_dec"] * u["k_dec"][j:j + 1, :], axis=-1, keepdims=True)
                    o = o + jnp.where(row >= j, a_j, 0.0) * u["v"][j:j + 1, :]
                outs.append(o * lax.rsqrt(jnp.mean(o * o, axis=-1, keepdims=True) + LN_EPS))
            gate = gh_ref[rows, cols]
            o_ref[rows, cols] = jnp.concatenate(outs, axis=0) * ng_ref[:, cols] * (gate * _sigmoid(gate))
        return carry

    lax.fori_loop(0, HS_BATCH // per_tile, tile, 0)


def hgrn_sample(qh, kh, ih, lf, gh, norm_g, s0, n_new):
    bsz = s0.shape[0]
    blk = pl.BlockSpec((HS_BATCH * n_new, HG_WIDTH), lambda i: (i, 0))
    sblk = pl.BlockSpec((HS_BATCH, HG_HEADS, HG_DK, HG_DV), lambda i: (i, 0, 0, 0))
    return pl.pallas_call(
        functools.partial(_hgrn_sample_body, n_new=n_new),
        grid=(bsz // HS_BATCH,),
        in_specs=[blk, blk, blk, blk, blk, _const_spec((1, HG_WIDTH)), sblk],
        out_specs=[blk, sblk],
        out_shape=[jax.ShapeDtypeStruct((bsz * n_new, HG_WIDTH), F32),
                   jax.ShapeDtypeStruct(s0.shape, F32)],
        compiler_params=pltpu.CompilerParams(dimension_semantics=("arbitrary",),
                                             vmem_limit_bytes=VMEM_LIMIT),
        name="hgrn_sample",
    )(qh, kh, ih, lf, gh, norm_g, s0)


def _rope_tables(pos):
    half = ATT_HEAD_DIM // 2
    inv = ROPE_THETA ** (-jnp.arange(half, dtype=F32) / half)
    ang = pos.astype(F32)[:, None] * inv[None, :]
    cos, sin = jnp.cos(ang), jnp.sin(ang)
    return jnp.tile(cos, (1, 4)), jnp.tile(jnp.concatenate([-sin, sin], axis=1), (1, 2))


def kernel(x_prompt, x_sample, cache_k, cache_v, state_hgrn, ffn1_w_gate, ffn1_w_up, ffn1_w_down, ln1_g, ln1_b, w_in, hg_lower_bound, hg_norm_g, w_out, ln2_g, ln2_b, ffn2_w_gate, ffn2_w_up, ffn2_w_down, ln3_g, ln3_b):
    batch, seq, _ = x_prompt.shape
    dec_batch, n_new, _ = x_sample.shape
    assert ffn1_w_gate.shape[0] == 1, "single layer"


    cos_p, sin_p = _rope_tables(jnp.arange(seq, dtype=jnp.int32))
    pos_s = PAST_LEN + (jnp.arange(dec_batch * n_new, dtype=jnp.int32) % n_new)
    cos_s, sin_s = _rope_tables(pos_s)

    xp = x_prompt.reshape(batch * seq, D_MODEL)
    xs = x_sample.reshape(dec_batch * n_new, D_MODEL)

    xs1, *w1 = ffn_ln(xs, ffn1_w_gate[0], ffn1_w_up[0], ffn1_w_down[0], ln1_g, ln1_b)
    qs, ks, vs, qhs, khs, ihs, lfs, ghs, w_in_b = proj(xs1, w_in[0], cos_s, sin_s, hg_lower_bound, 1)

    samp = (qs, ks, vs,
            cache_k[0].transpose(0, 2, 3, 1),
            cache_v[0].transpose(0, 2, 3, 1), n_new)
    steps = (batch * seq) // FA_ROWS
    per_call = steps // (ATT_HEADS // FA_HEADS)
    assert 2 * per_call == dec_batch, "the two prompt FFN calls must cover the sample batch"

    xp1, o_t, nk, nv = ffn_attn(xp, *w1, ln1_g, ln1_b, *samp, 0)
    q, k, v, qh, kh, ih, lf, gh = proj(xp1, w_in_b, cos_p, sin_p, hg_lower_bound, seq // ROW_TILE)
    o_att, kt_p, vt_p = attn_prompt(q, k, v, batch, seq)
    o_hg, s_p, *w2, w_out_b = hgrn_prompt(qh, kh, ih, lf, gh, hg_norm_g, batch, seq,
                                          cast=(ffn2_w_gate[0], ffn2_w_up[0], ffn2_w_down[0], w_out[0]))
    yp, o_t, nk, nv = ffn_attn(xp1, *w2, ln3_g, ln3_b, *samp, per_call, prev=(o_t, nk, nv),
                               mix=(o_att, o_hg, w_out_b, ln2_g, ln2_b))

    o_att_s = o_t.reshape(dec_batch * n_new, ATT_WIDTH)

    o_hg_s, s_s = hgrn_sample(qhs, khs, ihs, lfs, ghs, hg_norm_g, state_hgrn[0], n_new)
    ys, = ffn_ln(xs1, *w2, ln3_g, ln3_b, mix=(o_att_s, o_hg_s, w_out_b, ln2_g, ln2_b))

    return (yp.reshape(batch, seq, D_MODEL),
            ys.reshape(dec_batch, n_new, D_MODEL),
            kt_p.transpose(0, 3, 1, 2)[None],
            vt_p.transpose(0, 3, 1, 2)[None],
            s_p[None],
            nk.transpose(0, 3, 1, 2)[None],
            nv.transpose(0, 3, 1, 2)[None],
            s_s[None])
```

```python
import functools

import jax
import jax.numpy as jnp
from jax import lax
from jax.experimental import pallas as pl
from jax.experimental.pallas import tpu as pltpu

F32 = jnp.float32
BF16 = jnp.bfloat16

D_MODEL = 1024
ATT_WIDTH = 512
HG_WIDTH = 512
ATT_HEAD_DIM = 64
ATT_HEADS = 8
DILATIONS = ((128, 1), (512, 4), (2048, 16))
ATT_WINDOW_MAX = 2048
ATT_BLOCK = 128
ROPE_THETA = 10000.0
HG_HEADS = 4
HG_DK = 128
HG_DV = 128
HG_CHUNK = 32
D_FF = 2816
IN_COLS = 3 * ATT_WIDTH + 4 * HG_WIDTH
PAST_LEN = 8192
DN_ALPHA = 2.0 ** 0.25
LN_EPS = 1e-5
NEG_INF = -1e30

LANES = 128
ROW_TILE = 512
VMEM_LIMIT = 56 * 1024 * 1024


def _const_spec(shape, block=None):
    index = tuple(block) if block is not None else (0,) * len(shape)
    return pl.BlockSpec(shape, lambda *_: index, pipeline_mode=pl.Buffered(1))


def _layer_norm(r, g, b):
    mu = jnp.mean(r, axis=-1, keepdims=True)
    c = r - mu
    var = jnp.mean(c * c, axis=-1, keepdims=True)
    return c * lax.rsqrt(var + LN_EPS) * g + b


def _sigmoid(x):
    return 1.0 / (1.0 + jnp.exp(-x))


def _split3(x):
    hi = x.astype(BF16)
    r1 = x - hi.astype(F32)
    mid = r1.astype(BF16)
    lo = (r1 - mid.astype(F32)).astype(BF16)
    return hi, mid, lo


def _w_out_specs():
    assert ATT_WIDTH == HG_WIDTH
    return [_const_spec((ATT_WIDTH, D_MODEL), (0, 0)), _const_spec((HG_WIDTH, D_MODEL), (1, 0))]


FFN_CHUNK = 256


def _ffn_ln_body(*refs, cast, mixed):
    n_x = 7 if mixed else 1
    wg_ref, wu_ref, wd_ref, g_ref, b_ref, o_ref = refs[n_x:n_x + 6]
    xin_ref, xb_ref, acc_ref = refs[-3:]
    j = pl.program_id(0)

    @pl.when(j == 0)
    def _():
        if mixed:
            x1_ref, oa_ref, oh_ref, wa_ref, wh_ref, g2_ref, b2_ref = refs[:n_x]
            mix = jnp.dot(oa_ref[...].astype(BF16), wa_ref[...], preferred_element_type=F32)
            mix = mix + jnp.dot(oh_ref[...].astype(BF16), wh_ref[...], preferred_element_type=F32)
            x = _layer_norm(DN_ALPHA * x1_ref[...] + mix, g2_ref[...], b2_ref[...])
        else:
            x = refs[0][...]
        xin_ref[...] = x
        xb_ref[...] = x.astype(BF16)
        acc_ref[...] = jnp.zeros(acc_ref.shape, F32)

    wgb, wub, wdb = wg_ref[...], wu_ref[...], wd_ref[...]
    if cast:
        wgb, wub, wdb = wgb.astype(BF16), wub.astype(BF16), wdb.astype(BF16)
        for dst, w in zip(refs[n_x + 6:n_x + 9], (wgb, wub, wdb)):
            dst[...] = w
    xb = xb_ref[...]
    gate = jnp.dot(xb, wgb, preferred_element_type=F32)
    up = jnp.dot(xb, wub, preferred_element_type=F32)
    h = (gate * _sigmoid(gate) * up).astype(BF16)
    acc_ref[...] += jnp.dot(h, wdb, preferred_element_type=F32)

    @pl.when(j == pl.num_programs(0) - 1)
    def _():
        o_ref[...] = _layer_norm(DN_ALPHA * xin_ref[...] + 0.5 * acc_ref[...], g_ref[...], b_ref[...])


def ffn_ln(x, wg, wu, wd, g, b, mix=None):
    m = x.shape[0]
    cast = wg.dtype == F32
    col = pl.BlockSpec((D_MODEL, FFN_CHUNK), lambda j: (0, j))
    rowc = pl.BlockSpec((FFN_CHUNK, D_MODEL), lambda j: (j, 0))
    xs = pl.BlockSpec((m, D_MODEL), lambda j: (0, 0))
    in_specs, args = [xs], [x]
    if mix is not None:
        half = pl.BlockSpec((m, ATT_WIDTH), lambda j: (0, 0))
        in_specs += [half, half] + _w_out_specs() + [_const_spec((1, D_MODEL)), _const_spec((1, D_MODEL))]
        o_att, o_hg, w_out, ln_g, ln_b = mix
        args += [o_att, o_hg, w_out, w_out, ln_g, ln_b]
    in_specs += [col, col, rowc, _const_spec((1, D_MODEL)), _const_spec((1, D_MODEL))]
    args += [wg, wu, wd, g, b]
    out_specs = [xs] + ([col, col, rowc] if cast else [])
    out_shape = [jax.ShapeDtypeStruct((m, D_MODEL), F32)]
    if cast:
        out_shape += [jax.ShapeDtypeStruct(w.shape, BF16) for w in (wg, wu, wd)]
    return pl.pallas_call(
        functools.partial(_ffn_ln_body, cast=cast, mixed=mix is not None),
        grid=(D_FF // FFN_CHUNK,),
        in_specs=in_specs,
        out_specs=out_specs,
        out_shape=out_shape,
        scratch_shapes=[pltpu.VMEM((m, D_MODEL), F32), pltpu.VMEM((m, D_MODEL), BF16),
                        pltpu.VMEM((m, D_MODEL), F32)],
        compiler_params=pltpu.CompilerParams(dimension_semantics=("arbitrary",),
                                             vmem_limit_bytes=VMEM_LIMIT),
        name="ffn_ln",
    )(*args)


def _rope(z, cos_t, sin_t):
    lane = lax.broadcasted_iota(jnp.int32, (1, LANES), 1)
    first_half = (lane % ATT_HEAD_DIM) < (ATT_HEAD_DIM // 2)
    outs = []
    for c in range(z.shape[1] // LANES):
        zc = z[:, c * LANES:(c + 1) * LANES]
        partner = jnp.where(first_half, pltpu.roll(zc, LANES - 32, 1), pltpu.roll(zc, 32, 1))
        outs.append(zc * cos_t + partner * sin_t)
    return outs


def _proj_body(x_ref, w_ref, cos_ref, sin_ref, lbraw_ref,
               q_ref, k_ref, v_ref, qh_ref, kh_ref, ih_ref, lf_ref, gh_ref, wb_ref=None):
    xb = x_ref[...].astype(BF16)
    aw = ATT_WIDTH
    base = 3 * aw

    def cols(start, width):
        w = w_ref[:, start:start + width]
        if wb_ref is not None:
            w = w.astype(BF16)
            wb_ref[:, start:start + width] = w
        return jnp.dot(xb, w, preferred_element_type=F32)

    cos_t = cos_ref[...]
    sin_t = sin_ref[...]
    for c, val in enumerate(_rope(cols(0, aw), cos_t, sin_t)):
        q_ref[:, c * LANES:(c + 1) * LANES] = val * (ATT_HEAD_DIM ** -0.5)
    for c, val in enumerate(_rope(cols(aw, aw), cos_t, sin_t)):
        k_ref[:, c * LANES:(c + 1) * LANES] = val
    lbr = lbraw_ref[...]
    mx = jnp.max(lbr, axis=0, keepdims=True)
    ex = jnp.exp(lbr - mx)
    lb = ex[0:1, :] / jnp.sum(ex, axis=0, keepdims=True)
    zf = cols(base + HG_WIDTH, HG_WIDTH)
    lf_ref[...] = jnp.log(lb + (1.0 - lb) * _sigmoid(zf))
    kh_ref[...] = (1.0 - lb) * _sigmoid(-zf)
    zq = cols(base, HG_WIDTH)
    qh_ref[...] = zq * _sigmoid(zq)
    v_ref[...] = cols(2 * aw, aw)
    ih_ref[...] = cols(base + 2 * HG_WIDTH, HG_WIDTH)
    gh_ref[...] = cols(base + 3 * HG_WIDTH, HG_WIDTH)


def proj(x, w_in, cos_t, sin_t, lb_raw, table_tiles):
    m = x.shape[0]
    emit = w_in.dtype == F32
    assert not emit or m == ROW_TILE, "the weight copy is written once, by a single grid step"
    row = pl.BlockSpec((ROW_TILE, D_MODEL), lambda i: (i, 0))
    tab = pl.BlockSpec((ROW_TILE, LANES), lambda i: (i % table_tiles, 0))
    out = pl.BlockSpec((ROW_TILE, ATT_WIDTH), lambda i: (i, 0))
    sds = jax.ShapeDtypeStruct((m, ATT_WIDTH), F32)
    return pl.pallas_call(
        _proj_body,
        grid=(m // ROW_TILE,),
        in_specs=[row, _const_spec((D_MODEL, IN_COLS)), tab, tab, _const_spec((2, HG_WIDTH))],
        out_specs=[out] * 8 + ([pl.BlockSpec((D_MODEL, IN_COLS), lambda i: (0, 0))] if emit else []),
        out_shape=[sds] * 8 + ([jax.ShapeDtypeStruct(w_in.shape, BF16)] if emit else []),
        compiler_params=pltpu.CompilerParams(dimension_semantics=("arbitrary",),
                                             vmem_limit_bytes=VMEM_LIMIT),
        name="proj",
    )(x, w_in, cos_t, sin_t, lb_raw)


ATT_UNROLL = (8, 8, 4)


def _attn_prompt_body(q_ref, k_ref, v_ref, o_ref, kt_ref, vt_ref, acc_ref, m_ref, l_ref, *, seq):
    blk = ATT_BLOCK
    lane = lax.broadcasted_iota(jnp.int32, (1, LANES), 1)
    head0 = lane < ATT_HEAD_DIM
    qi = lax.broadcasted_iota(jnp.int32, (blk, 2 * blk), 0)
    kc = lax.broadcasted_iota(jnp.int32, (blk, 2 * blk), 1)
    dist = qi + blk - kc
    first_mask = (lax.broadcasted_iota(jnp.int32, (blk, blk), 1)
                  <= lax.broadcasted_iota(jnp.int32, (blk, blk), 0))
    nt = (((1,), (1,)), ((), ()))
    log2e = 1.4426950408889634

    for br, (window, dil) in enumerate(DILATIONS):
        w_sub = window // dil
        rel = (dist >= 0) & (dist <= w_sub)
        span = blk * dil
        n_blk = seq // span
        n_units = ATT_UNROLL[br]
        run = min(n_units, n_blk)
        n_res = n_units // run
        runs_per_res = n_blk // run

        def units(it, carry, br=br, dil=dil, rel=rel, span=span, run=run, n_res=n_res,
                  runs_per_res=runs_per_res, n_units=n_units):
            starts, masks, vbs, scores = [], [], [], []
            for rr in range(n_res):
                if runs_per_res == 1:
                    r, n0 = it * n_res + rr, 0
                else:
                    r, n0 = it // runs_per_res, (it % runs_per_res) * run
                base = r + n0 * span

                def kv_block(j):
                    rows = pl.ds(jnp.maximum(base + j * span, r), blk, stride=dil)
                    return k_ref[rows, :].astype(BF16), v_ref[rows, :].astype(BF16)

                prev = None if runs_per_res == 1 else kv_block(-1)
                for j in range(run):
                    cur = kv_block(j)
                    start = base + j * span
                    q = q_ref[pl.ds(start, blk, stride=dil), :] * log2e
                    if prev is None:
                        kb, vb, mask = cur[0], cur[1], first_mask
                    else:
                        kb = jnp.concatenate([prev[0], cur[0]], axis=0)
                        vb = jnp.concatenate([prev[1], cur[1]], axis=0)
                        mask = rel & (kc >= jnp.where(n0 > 0, 0, blk)) if (j == 0 and runs_per_res > 1) else rel
                    starts.append(start)
                    masks.append(mask)
                    vbs.append(vb)
                    for hsel in (head0, ~head0):
                        qh = jnp.where(hsel, q, 0.0).astype(BF16)
                        scores.append(lax.dot_general(qh, kb, nt, preferred_element_type=F32))
                    prev = cur
            probs, ls, ms = [], [], []
            for i, s in enumerate(scores):
                s = jnp.where(masks[i // 2], s, NEG_INF)
                m = jnp.max(s, axis=-1, keepdims=True)
                p = jnp.exp2(s - m)
                ls.append(jnp.sum(p, axis=-1, keepdims=True))
                ms.append(m)
                probs.append(p.astype(BF16))
            outs = [jnp.dot(p, vbs[i // 2], preferred_element_type=F32) for i, p in enumerate(probs)]
            for uu in range(n_units):
                i0, i1 = 2 * uu, 2 * uu + 1
                rows = pl.ds(starts[uu], blk, stride=dil)
                acc_ref[br, rows, :] = jnp.where(head0, outs[i0], outs[i1])
                m_ref[br, rows, :] = jnp.broadcast_to(jnp.where(head0, ms[i0], ms[i1]), (blk, LANES))
                l_ref[br, rows, :] = jnp.broadcast_to(jnp.where(head0, ls[i0], ls[i1]), (blk, LANES))
            return carry

        lax.fori_loop(0, (dil * n_blk) // n_units, units, 0)

    def combine(c, carry):
        rows = pl.ds(pl.multiple_of(c * blk, blk), blk)
        m0, m1, m2 = m_ref[0, rows, :], m_ref[1, rows, :], m_ref[2, rows, :]
        mx = jnp.maximum(jnp.maximum(m0, m1), m2)
        w0, w1, w2 = jnp.exp2(m0 - mx), jnp.exp2(m1 - mx), jnp.exp2(m2 - mx)
        num = w0 * acc_ref[0, rows, :] + w1 * acc_ref[1, rows, :] + w2 * acc_ref[2, rows, :]
        den = w0 * l_ref[0, rows, :] + w1 * l_ref[1, rows, :] + w2 * l_ref[2, rows, :]
        o_ref[rows, :] = num / den
        return carry

    lax.fori_loop(0, seq // blk, combine, 0)

    win = kt_ref.shape[-1]
    for j in range(win // LANES):
        rows = pl.ds(seq - win + j * LANES, LANES)
        kt_ref[0, :, :, j * LANES:(j + 1) * LANES] = k_ref[rows, :].T.reshape(2, ATT_HEAD_DIM, LANES)
        vt_ref[0, :, :, j * LANES:(j + 1) * LANES] = v_ref[rows, :].T.reshape(2, ATT_HEAD_DIM, LANES)


def attn_prompt(q, k, v, batch, seq):
    win = min(ATT_WINDOW_MAX, seq)
    n_hp = ATT_WIDTH // LANES
    blk = pl.BlockSpec((seq, LANES), lambda b, h: (b, h))
    tblk = pl.BlockSpec((1, 2, ATT_HEAD_DIM, win), lambda b, h: (b, h, 0, 0))
    tsds = jax.ShapeDtypeStruct((batch, ATT_HEADS, ATT_HEAD_DIM, win), F32)
    return pl.pallas_call(
        functools.partial(_attn_prompt_body, seq=seq),
        grid=(batch, n_hp),
        in_specs=[blk, blk, blk],
        out_specs=[blk, tblk, tblk],
        out_shape=[jax.ShapeDtypeStruct((batch * seq, ATT_WIDTH), F32), tsds, tsds],
        scratch_shapes=[pltpu.VMEM((3, seq, LANES), F32)] * 3,
        compiler_params=pltpu.CompilerParams(dimension_semantics=("arbitrary", "arbitrary"),
                                             vmem_limit_bytes=VMEM_LIMIT),
        name="attn_prompt",
    )(q, k, v)


def _hgrn_prompt_body(*refs, rows_per_step, n_cast):
    q_ref, k_ref, v_ref, lf_ref, gh_ref, ng_ref = refs[:6]
    o_ref, s_ref = refs[6 + n_cast:8 + n_cast]
    st_ref = refs[-1]
    for w_ref, wb_ref in zip(refs[6:6 + n_cast], refs[8 + n_cast:8 + 2 * n_cast]):
        wb_ref[...] = w_ref[...].astype(BF16)
    grp = LANES
    n_ch = grp // HG_CHUNK
    ri = lax.broadcasted_iota(jnp.int32, (grp, grp), 0)
    ci = lax.broadcasted_iota(jnp.int32, (grp, grp), 1)
    same_chunk = (ri // HG_CHUNK) == (ci // HG_CHUNK)
    causal = same_chunk & (ci <= ri)
    summat = jnp.where(causal, 1.0, 0.0).astype(BF16)
    nt = (((1,), (1,)), ((), ()))
    tb = pl.program_id(1)

    @pl.when(tb == 0)
    def _():
        st_ref[...] = jnp.zeros(st_ref.shape, F32)

    heads = [slice(h * LANES, (h + 1) * LANES) for h in range(HG_HEADS)]
    in_chunk = [(ri // HG_CHUNK) == c for c in range(n_ch)]

    def by_chunk(x):
        return jnp.concatenate([jnp.where(m, x, 0.0).astype(BF16) for m in in_chunk], axis=1)

    def groups(it, carry):
        gs = range(HG_GROUPS)
        rows = [pl.ds(pl.multiple_of((it * HG_GROUPS + g) * grp, grp), grp) for g in gs]
        cum = []
        for g in gs:
            parts = _split3(lf_ref[rows[g], :])
            cum.append(jnp.dot(summat, parts[0], preferred_element_type=F32)
                       + jnp.dot(summat, parts[1], preferred_element_type=F32)
                       + jnp.dot(summat, parts[2], preferred_element_type=F32))
        q_dec, decay, a, vs, k_end = [], [], [], [], []
        for g in gs:
            cum_end = jnp.concatenate(
                [jnp.broadcast_to(cum[g][(c + 1) * HG_CHUNK - 1:(c + 1) * HG_CHUNK, :], (HG_CHUNK, HG_WIDTH))
                 for c in range(n_ch)], axis=0)
            q, k, v = q_ref[rows[g], :], k_ref[rows[g], :], v_ref[rows[g], :]
            q_dec.append(q * jnp.exp(cum[g]))
            k_dec = (k * jnp.exp(-cum[g])).astype(BF16)
            k_end.append(k * jnp.exp(cum_end - cum[g]))
            decay.append(jnp.exp(cum_end))
            q_decb = q_dec[g].astype(BF16)
            vs.append(v)
            a.append([lax.dot_general(q_decb[:, hs], k_dec[:, hs], nt, preferred_element_type=F32)
                      for hs in heads])
        o = [[jnp.dot(jnp.where(causal, a[g][h], 0.0).astype(BF16), vs[g][:, hs].astype(BF16),
                      preferred_element_type=F32) for h, hs in enumerate(heads)] for g in gs]
        ut = [[jnp.dot(vs[g][:, hs].T.astype(BF16), by_chunk(k_end[g][:, hs]), preferred_element_type=F32)
               for hs in heads] for g in gs]
        starts = [[[] for _ in heads] for _ in gs]
        for h, hs in enumerate(heads):
            st = st_ref[h]
            for g in gs:
                for c in range(n_ch):
                    starts[g][h].append(st.astype(BF16))
                    st = (st * decay[g][c * HG_CHUNK:c * HG_CHUNK + 1, hs]
                          + ut[g][h][:, c * LANES:(c + 1) * LANES])
            st_ref[h] = st
        inter = [[lax.dot_general(by_chunk(q_dec[g][:, hs]), jnp.concatenate(starts[g][h], axis=1), nt,
                                  preferred_element_type=F32) for h, hs in enumerate(heads)] for g in gs]
        for g in gs:
            for h, hs in enumerate(heads):
                oh = o[g][h] + inter[g][h]
                ms = jnp.mean(oh * oh, axis=-1, keepdims=True)
                gate = gh_ref[rows[g], hs]
                o_ref[rows[g], hs] = oh * lax.rsqrt(ms + LN_EPS) * ng_ref[:, hs] * (gate * _sigmoid(gate))
        return carry

    lax.fori_loop(0, rows_per_step // (grp * HG_GROUPS), groups, 0)

    @pl.when(tb == pl.num_programs(1) - 1)
    def _():
        for h in range(HG_HEADS):
            s_ref[0, h] = st_ref[h].T


HG_ROWS = 1024
HG_GROUPS = 4


def hgrn_prompt(qh, kh, ih, lf, gh, norm_g, batch, seq, cast=()):
    n_tb = seq // HG_ROWS
    steps = batch * n_tb

    def rows(b, t):
        return (b * n_tb + t, 0)

    blk = pl.BlockSpec((HG_ROWS, HG_WIDTH), rows)
    cast_specs = [pl.BlockSpec((w.shape[0] // steps, w.shape[1]), rows) for w in cast]
    return pl.pallas_call(
        functools.partial(_hgrn_prompt_body, rows_per_step=HG_ROWS, n_cast=len(cast)),
        grid=(batch, n_tb),
        in_specs=[blk, blk, blk, blk, blk, _const_spec((1, HG_WIDTH))] + cast_specs,
        out_specs=[blk, pl.BlockSpec((1, HG_HEADS, HG_DK, HG_DV), lambda b, t: (b, 0, 0, 0))] + cast_specs,
        out_shape=[jax.ShapeDtypeStruct((batch * seq, HG_WIDTH), F32),
                   jax.ShapeDtypeStruct((batch, HG_HEADS, HG_DK, HG_DV), F32)]
                  + [jax.ShapeDtypeStruct(w.shape, BF16) for w in cast],
        scratch_shapes=[pltpu.VMEM((HG_HEADS, HG_DV, HG_DK), F32)],
        compiler_params=pltpu.CompilerParams(dimension_semantics=("arbitrary", "arbitrary"),
                                             vmem_limit_bytes=VMEM_LIMIT),
        name="hgrn_prompt",
    )(qh, kh, ih, lf, gh, norm_g, *cast)


FA_ROWS = 128
FA_HEADS = 4
FA_SLOTS = 4


def _divmod_pow2(s, n):
    assert n & (n - 1) == 0
    if isinstance(s, int):
        return s // n, s % n
    return lax.shift_right_logical(s, n.bit_length() - 1), s & (n - 1)


def _ffn_attn_body(*refs, n_new, past, b_off, mixed, n_steps):
    n_x = 7 if mixed else 1
    wg_ref, wu_ref, wd_ref, g_ref, b_ref, q_ref, kn_ref, vn_ref, kc_hbm, vc_hbm = refs[n_x:n_x + 10]
    y_ref, o_ref, nk_ref, nv_ref, kbuf, vbuf, sems = refs[-7:]
    halves = ATT_HEADS // FA_HEADS
    step = pl.program_id(0)

    split = _divmod_pow2
    odd = (b_off + split(step, halves)[0]) & 1 == 1

    def window_copies(s):
        slot = split(s, FA_SLOTS)[1]
        b_rel, half = split(s, halves)
        src = (b_off + b_rel, pl.ds(half * FA_HEADS, FA_HEADS))
        return (pltpu.make_async_copy(kc_hbm.at[src], kbuf.at[slot], sems.at[slot, 0]),
                pltpu.make_async_copy(vc_hbm.at[src], vbuf.at[slot], sems.at[slot, 1]))

    @pl.when(step == 0)
    def _():
        for s in range(min(FA_SLOTS - 1, n_steps)):
            for c in window_copies(s):
                c.start()

    @pl.when(step + (FA_SLOTS - 1) < n_steps)
    def _():
        for c in window_copies(step + (FA_SLOTS - 1)):
            c.start()

    for c in window_copies(step):
        c.wait()
    slot = split(step, FA_SLOTS)[1]
    kc_ref, vc_ref = kbuf.at[slot], vbuf.at[slot]

    def new_rows(ref):
        both = ref[...]
        return jnp.where(odd, both[n_new:2 * n_new], both[0:n_new])

    q_new, k_new, v_new = new_rows(q_ref), new_rows(kn_ref), new_rows(vn_ref)
    head_cols = [slice(h * ATT_HEAD_DIM, (h + 1) * ATT_HEAD_DIM) for h in range(FA_HEADS)]
    srow = lax.broadcasted_iota(jnp.int32, (n_new, past), 0)
    tcol = lax.broadcasted_iota(jnp.int32, (n_new, past), 1)
    dist = past + srow - tcol
    def on_stride(d, dil):
        return (d & (dil - 1)) == 0 if dil & (dil - 1) == 0 else d % dil == 0

    cnt = jnp.zeros((n_new, past), F32)
    for window, dil in DILATIONS:
        cnt = cnt + jnp.where(on_stride(dist, dil) & (dist <= window), 1.0, 0.0)
    si = lax.broadcasted_iota(jnp.int32, (n_new, n_new), 0)
    sj = lax.broadcasted_iota(jnp.int32, (n_new, n_new), 1)
    dn = si - sj
    cnt_new = jnp.zeros((n_new, n_new), F32)
    for window, dil in DILATIONS:
        cnt_new = cnt_new + jnp.where((dn >= 0) & on_stride(dn, dil) & (dn <= window), 1.0, 0.0)
    lane = lax.broadcasted_iota(jnp.int32, (1, LANES), 1)
    nt = (((1,), (1,)), ((), ()))
    shift = past - n_new
    hs = range(FA_HEADS)

    def scores():
        qs = [q_new[:, c].astype(BF16) for c in head_cols]
        s_c = [jnp.dot(qs[h], kc_ref[h].astype(BF16), preferred_element_type=F32) for h in hs]
        s_n = [lax.dot_general(qs[h], k_new[:, head_cols[h]].astype(BF16), nt, preferred_element_type=F32)
               for h in hs]
        return s_c, s_n

    if mixed:
        x1_ref, oa_ref, oh_ref, wa_ref, wh_ref, g2_ref, b2_ref = refs[:n_x]
        mix = jnp.dot(oa_ref[...].astype(BF16), wa_ref[...], preferred_element_type=F32)
        mix = mix + jnp.dot(oh_ref[...].astype(BF16), wh_ref[...], preferred_element_type=F32)
        s_c, s_n = scores()
        x = _layer_norm(DN_ALPHA * x1_ref[...] + mix, g2_ref[...], b2_ref[...])
    else:
        x = refs[0][...]
    xb = x.astype(BF16)
    gate = jnp.dot(xb, wg_ref[...], preferred_element_type=F32)
    up = jnp.dot(xb, wu_ref[...], preferred_element_type=F32)
    if not mixed:
        s_c, s_n = scores()
    knt = [k_new[:, c].T for c in head_cols]
    vnt = [v_new[:, c].T for c in head_cols]

    hmid = (gate * _sigmoid(gate) * up).astype(BF16)
    y = jnp.dot(hmid, wd_ref[...], preferred_element_type=F32)
    y_ref[...] = _layer_norm(DN_ALPHA * x + 0.5 * y, g_ref[...], b_ref[...])

    p_c, p_n, den = [], [], []
    for h in hs:
        sc = jnp.where(cnt > 0, s_c[h], NEG_INF)
        sn = jnp.where(cnt_new > 0, s_n[h], NEG_INF)
        m = jnp.maximum(jnp.max(sc, axis=-1, keepdims=True), jnp.max(sn, axis=-1, keepdims=True))
        p_c.append((cnt * jnp.exp(sc - m)).astype(BF16))
        p_n.append((cnt_new * jnp.exp(sn - m)).astype(BF16))
        den.append(jnp.sum(p_c[h].astype(F32), axis=-1, keepdims=True)
                   + jnp.sum(p_n[h].astype(F32), axis=-1, keepdims=True))
    outs = []
    for h in hs:
        num = (lax.dot_general(p_c[h], vc_ref[h].astype(BF16), nt, preferred_element_type=F32)
               + jnp.dot(p_n[h], v_new[:, head_cols[h]].astype(BF16), preferred_element_type=F32))
        outs.append(num / den[h])
    o_ref[0] = jnp.concatenate(outs, axis=1)

    for h in hs:
        for src_ref, new, dst in ((kc_ref, knt[h], nk_ref), (vc_ref, vnt[h], nv_ref)):
            rolled = pltpu.roll(src_ref[h], shift, 1)
            dst[0, h, :, 0:past - LANES] = rolled[:, 0:past - LANES]
            last = rolled[:, past - LANES:past]
            for s in range(n_new):
                col = jnp.broadcast_to(new[:, s:s + 1], last.shape)
                last = jnp.where(lane == LANES - n_new + s, col, last)
            dst[0, h, :, past - LANES:past] = last


def ffn_attn(x, wg, wu, wd, g, b, q, kn, vn, kc, vc, n_new, b_off, prev=None, mix=None):
    m = x.shape[0]
    bsz, n_heads, _, past = kc.shape
    halves = n_heads // FA_HEADS
    steps = m // FA_ROWS
    assert b_off % 2 == 0 and bsz % 2 == 0, "new-token rows are fetched per pair of batch rows"

    def where(i):
        b_rel, half = _divmod_pow2(i, halves)
        return b_off + b_rel, half

    def samp(i):
        return (*where(i), 0, 0)

    row = pl.BlockSpec((FA_ROWS, D_MODEL), lambda i: (i, 0))
    pair = pl.BlockSpec((2 * n_new, FA_HEADS * ATT_HEAD_DIM),
                        lambda i: (_divmod_pow2(where(i)[0], 2)[0], where(i)[1]))
    o_blk = pl.BlockSpec((1, n_new, FA_HEADS * ATT_HEAD_DIM), lambda i: (where(i)[0], 0, where(i)[1]))
    big = pl.BlockSpec((1, FA_HEADS, ATT_HEAD_DIM, past), samp)
    in_specs, args = [row], [x]
    if mix is not None:
        half = pl.BlockSpec((FA_ROWS, ATT_WIDTH), lambda i: (i, 0))
        in_specs += [half, half] + _w_out_specs() + [_const_spec((1, D_MODEL)), _const_spec((1, D_MODEL))]
        o_att, o_hg, w_out, ln_g, ln_b = mix
        args += [o_att, o_hg, w_out, w_out, ln_g, ln_b]
    hbm = pl.BlockSpec(memory_space=pl.ANY)
    in_specs += [_const_spec((D_MODEL, D_FF)), _const_spec((D_MODEL, D_FF)), _const_spec((D_FF, D_MODEL)),
                 _const_spec((1, D_MODEL)), _const_spec((1, D_MODEL)), pair, pair, pair, hbm, hbm]
    args += [wg, wu, wd, g, b, q, kn, vn, kc, vc]
    aliases = {}
    if prev is not None:
        for j, a in enumerate(prev):
            aliases[len(args)] = 1 + j
            in_specs.append(pl.BlockSpec(memory_space=pl.ANY))
            args.append(a)
    big_sds = jax.ShapeDtypeStruct(kc.shape, F32)
    return pl.pallas_call(
        functools.partial(_ffn_attn_body, n_new=n_new, past=past, b_off=b_off, mixed=mix is not None,
                          n_steps=steps),
        grid=(steps,),
        in_specs=in_specs,
        out_specs=[row, o_blk, big, big],
        out_shape=[jax.ShapeDtypeStruct((m, D_MODEL), F32),
                   jax.ShapeDtypeStruct((bsz, n_new, n_heads * ATT_HEAD_DIM), F32), big_sds, big_sds],
        scratch_shapes=[pltpu.VMEM((FA_SLOTS, FA_HEADS, ATT_HEAD_DIM, past), F32),
                        pltpu.VMEM((FA_SLOTS, FA_HEADS, ATT_HEAD_DIM, past), F32),
                        pltpu.SemaphoreType.DMA((FA_SLOTS, 2))],
        input_output_aliases=aliases,
        compiler_params=pltpu.CompilerParams(dimension_semantics=("arbitrary",),
                                             vmem_limit_bytes=VMEM_LIMIT),
        name="ffn_attn",
    )(*args)


HS_BATCH = 8


def _hgrn_sample_body(q_ref, k_ref, v_ref, lf_ref, gh_ref, ng_ref, s0_ref, o_ref, s_ref, *, n_new):
    row = lax.broadcasted_iota(jnp.int32, (n_new, 1), 0)
    tn = (((0,), (0,)), ((), ()))
    ones12 = jnp.ones((3 * n_new, HG_DV), BF16)

    sub = 8
    per_tile = sub // n_new

    def tile(t, carry):
        rows = pl.ds(pl.multiple_of(t * sub, sub), sub)
        units = []
        for h in range(HG_HEADS):
            cols = slice(h * LANES, (h + 1) * LANES)
            q8, k8, v8, g8 = q_ref[rows, cols], k_ref[rows, cols], v_ref[rows, cols], lf_ref[rows, cols]
            for i in range(per_tile):
                part = slice(i * n_new, (i + 1) * n_new)
                q, k, v, g = q8[part], k8[part], v8[part], g8[part]
                cum = jnp.zeros((n_new, LANES), F32)
                for j in range(n_new):
                    cum = cum + jnp.where(row >= j, g[j:j + 1, :], 0.0)
                cum_end = cum[n_new - 1:n_new, :]
                units.append(dict(h=h, b=t * per_tile + i, v=v, g=g, q_dec=q * jnp.exp(cum),
                                  k_dec=k * jnp.exp(-cum), k_end=k * jnp.exp(cum_end - cum)))
        for u in units:
            u["s0"] = s0_ref[u["b"], u["h"]]
            u["o"] = jnp.dot(u["q_dec"].astype(BF16), u["s0"].astype(BF16), preferred_element_type=F32)
        for u in units:
            dec = lax.dot_general(jnp.concatenate(_split3(u["g"]), axis=0), ones12, tn,
                                  preferred_element_type=F32)
            upd = lax.dot_general(u["k_end"].astype(BF16), u["v"].astype(BF16), tn,
                                  preferred_element_type=F32)
            s_ref[u["b"], u["h"]] = jnp.exp(dec) * u["s0"] + upd
        for h in range(HG_HEADS):
            cols = slice(h * LANES, (h + 1) * LANES)
            outs = []
            for u in units[h * per_tile:(h + 1) * per_tile]:
                o = u["o"]
                for j in range(n_new):
                    a_j = jnp.sum(u["q_dec"] * u["k_dec"][j:j + 1, :], axis=-1, keepdims=True)
                    o = o + jnp.where(row >= j, a_j, 0.0) * u["v"][j:j + 1, :]
                outs.append(o * lax.rsqrt(jnp.mean(o * o, axis=-1, keepdims=True) + LN_EPS))
            gate = gh_ref[rows, cols]
            o_ref[rows, cols] = jnp.concatenate(outs, axis=0) * ng_ref[:, cols] * (gate * _sigmoid(gate))
        return carry

    lax.fori_loop(0, HS_BATCH // per_tile, tile, 0)


def hgrn_sample(qh, kh, ih, lf, gh, norm_g, s0, n_new):
    bsz = s0.shape[0]
    blk = pl.BlockSpec((HS_BATCH * n_new, HG_WIDTH), lambda i: (i, 0))
    sblk = pl.BlockSpec((HS_BATCH, HG_HEADS, HG_DK, HG_DV), lambda i: (i, 0, 0, 0))
    return pl.pallas_call(
        functools.partial(_hgrn_sample_body, n_new=n_new),
        grid=(bsz // HS_BATCH,),
        in_specs=[blk, blk, blk, blk, blk, _const_spec((1, HG_WIDTH)), sblk],
        out_specs=[blk, sblk],
        out_shape=[jax.ShapeDtypeStruct((bsz * n_new, HG_WIDTH), F32),
                   jax.ShapeDtypeStruct(s0.shape, F32)],
        compiler_params=pltpu.CompilerParams(dimension_semantics=("arbitrary",),
                                             vmem_limit_bytes=VMEM_LIMIT),
        name="hgrn_sample",
    )(qh, kh, ih, lf, gh, norm_g, s0)


def _rope_tables(pos):
    half = ATT_HEAD_DIM // 2
    inv = ROPE_THETA ** (-jnp.arange(half, dtype=F32) / half)
    ang = pos.astype(F32)[:, None] * inv[None, :]
    cos, sin = jnp.cos(ang), jnp.sin(ang)
    return jnp.tile(cos, (1, 4)), jnp.tile(jnp.concatenate([-sin, sin], axis=1), (1, 2))


def kernel(x_prompt, x_sample, cache_k, cache_v, state_hgrn, ffn1_w_gate, ffn1_w_up, ffn1_w_down, ln1_g, ln1_b, w_in, hg_lower_bound, hg_norm_g, w_out, ln2_g, ln2_b, ffn2_w_gate, ffn2_w_up, ffn2_w_down, ln3_g, ln3_b):
    batch, seq, _ = x_prompt.shape
    dec_batch, n_new, _ = x_sample.shape
    assert ffn1_w_gate.shape[0] == 1, "single layer"


    cos_p, sin_p = _rope_tables(jnp.arange(seq, dtype=jnp.int32))
    pos_s = PAST_LEN + (jnp.arange(dec_batch * n_new, dtype=jnp.int32) % n_new)
    cos_s, sin_s = _rope_tables(pos_s)

    xp = x_prompt.reshape(batch * seq, D_MODEL)
    xs = x_sample.reshape(dec_batch * n_new, D_MODEL)

    xs1, *w1 = ffn_ln(xs, ffn1_w_gate[0], ffn1_w_up[0], ffn1_w_down[0], ln1_g, ln1_b)
    qs, ks, vs, qhs, khs, ihs, lfs, ghs, w_in_b = proj(xs1, w_in[0], cos_s, sin_s, hg_lower_bound, 1)

    samp = (qs, ks, vs,
            cache_k[0].transpose(0, 2, 3, 1),
            cache_v[0].transpose(0, 2, 3, 1), n_new)
    steps = (batch * seq) // FA_ROWS
    per_call = steps // (ATT_HEADS // FA_HEADS)
    assert 2 * per_call == dec_batch, "the two prompt FFN calls must cover the sample batch"

    xp1, o_t, nk, nv = ffn_attn(xp, *w1, ln1_g, ln1_b, *samp, 0)
    q, k, v, qh, kh, ih, lf, gh = proj(xp1, w_in_b, cos_p, sin_p, hg_lower_bound, seq // ROW_TILE)
    o_att, kt_p, vt_p = attn_prompt(q, k, v, batch, seq)
    o_hg, s_p, *w2, w_out_b = hgrn_prompt(qh, kh, ih, lf, gh, hg_norm_g, batch, seq,
                                          cast=(ffn2_w_gate[0], ffn2_w_up[0], ffn2_w_down[0], w_out[0]))
    yp, o_t, nk, nv = ffn_attn(xp1, *w2, ln3_g, ln3_b, *samp, per_call, prev=(o_t, nk, nv),
                               mix=(o_att, o_hg, w_out_b, ln2_g, ln2_b))

    o_att_s = o_t.reshape(dec_batch * n_new, ATT_WIDTH)

    o_hg_s, s_s = hgrn_sample(qhs, khs, ihs, lfs, ghs, hg_norm_g, state_hgrn[0], n_new)
    ys, = ffn_ln(xs1, *w2, ln3_g, ln3_b, mix=(o_att_s, o_hg_s, w_out_b, ln2_g, ln2_b))

    return (yp.reshape(batch, seq, D_MODEL),
            ys.reshape(dec_batch, n_new, D_MODEL),
            kt_p.transpose(0, 3, 1, 2)[None],
            vt_p.transpose(0, 3, 1, 2)[None],
            s_p[None],
            nk.transpose(0, 3, 1, 2)[None],
            nv.transpose(0, 3, 1, 2)[None],
            s_s[None])
```

```python
import functools

import jax
import jax.numpy as jnp
from jax import lax
from jax.experimental import pallas as pl
from jax.experimental.pallas import tpu as pltpu

F32 = jnp.float32
BF16 = jnp.bfloat16

D_MODEL = 1024
ATT_WIDTH = 512
HG_WIDTH = 512
ATT_HEAD_DIM = 64
ATT_HEADS = 8
DILATIONS = ((128, 1), (512, 4), (2048, 16))
ATT_WINDOW_MAX = 2048
ATT_BLOCK = 128
ROPE_THETA = 10000.0
HG_HEADS = 4
HG_DK = 128
HG_DV = 128
HG_CHUNK = 32
D_FF = 2816
IN_COLS = 3 * ATT_WIDTH + 4 * HG_WIDTH
PAST_LEN = 8192
DN_ALPHA = 2.0 ** 0.25
LN_EPS = 1e-5
NEG_INF = -1e30

LANES = 128
ROW_TILE = 512
VMEM_LIMIT = 56 * 1024 * 1024


def _const_spec(shape, block=None):
    index = tuple(block) if block is not None else (0,) * len(shape)
    return pl.BlockSpec(shape, lambda *_: index, pipeline_mode=pl.Buffered(1))


def _layer_norm(r, g, b):
    mu = jnp.mean(r, axis=-1, keepdims=True)
    c = r - mu
    var = jnp.mean(c * c, axis=-1, keepdims=True)
    return c * lax.rsqrt(var + LN_EPS) * g + b


def _sigmoid(x):
    return 1.0 / (1.0 + jnp.exp(-x))


def _split3(x):
    hi = x.astype(BF16)
    r1 = x - hi.astype(F32)
    mid = r1.astype(BF16)
    lo = (r1 - mid.astype(F32)).astype(BF16)
    return hi, mid, lo


def _w_out_specs():
    assert ATT_WIDTH == HG_WIDTH
    return [_const_spec((ATT_WIDTH, D_MODEL), (0, 0)), _const_spec((HG_WIDTH, D_MODEL), (1, 0))]


FFN_CHUNK = 256


def _ffn_ln_body(*refs, cast, mixed):
    n_x = 7 if mixed else 1
    wg_ref, wu_ref, wd_ref, g_ref, b_ref, o_ref = refs[n_x:n_x + 6]
    xin_ref, xb_ref, acc_ref = refs[-3:]
    j = pl.program_id(0)

    @pl.when(j == 0)
    def _():
        if mixed:
            x1_ref, oa_ref, oh_ref, wa_ref, wh_ref, g2_ref, b2_ref = refs[:n_x]
            mix = jnp.dot(oa_ref[...].astype(BF16), wa_ref[...], preferred_element_type=F32)
            mix = mix + jnp.dot(oh_ref[...].astype(BF16), wh_ref[...], preferred_element_type=F32)
            x = _layer_norm(DN_ALPHA * x1_ref[...] + mix, g2_ref[...], b2_ref[...])
        else:
            x = refs[0][...]
        xin_ref[...] = x
        xb_ref[...] = x.astype(BF16)
        acc_ref[...] = jnp.zeros(acc_ref.shape, F32)

    wgb, wub, wdb = wg_ref[...], wu_ref[...], wd_ref[...]
    if cast:
        wgb, wub, wdb = wgb.astype(BF16), wub.astype(BF16), wdb.astype(BF16)
        for dst, w in zip(refs[n_x + 6:n_x + 9], (wgb, wub, wdb)):
            dst[...] = w
    xb = xb_ref[...]
    gate = jnp.dot(xb, wgb, preferred_element_type=F32)
    up = jnp.dot(xb, wub, preferred_element_type=F32)
    h = (gate * _sigmoid(gate) * up).astype(BF16)
    acc_ref[...] += jnp.dot(h, wdb, preferred_element_type=F32)

    @pl.when(j == pl.num_programs(0) - 1)
    def _():
        o_ref[...] = _layer_norm(DN_ALPHA * xin_ref[...] + 0.5 * acc_ref[...], g_ref[...], b_ref[...])


def ffn_ln(x, wg, wu, wd, g, b, mix=None):
    m = x.shape[0]
    cast = wg.dtype == F32
    col = pl.BlockSpec((D_MODEL, FFN_CHUNK), lambda j: (0, j))
    rowc = pl.BlockSpec((FFN_CHUNK, D_MODEL), lambda j: (j, 0))
    xs = pl.BlockSpec((m, D_MODEL), lambda j: (0, 0))
    in_specs, args = [xs], [x]
    if mix is not None:
        half = pl.BlockSpec((m, ATT_WIDTH), lambda j: (0, 0))
        in_specs += [half, half] + _w_out_specs() + [_const_spec((1, D_MODEL)), _const_spec((1, D_MODEL))]
        o_att, o_hg, w_out, ln_g, ln_b = mix
        args += [o_att, o_hg, w_out, w_out, ln_g, ln_b]
    in_specs += [col, col, rowc, _const_spec((1, D_MODEL)), _const_spec((1, D_MODEL))]
    args += [wg, wu, wd, g, b]
    out_specs = [xs] + ([col, col, rowc] if cast else [])
    out_shape = [jax.ShapeDtypeStruct((m, D_MODEL), F32)]
    if cast:
        out_shape += [jax.ShapeDtypeStruct(w.shape, BF16) for w in (wg, wu, wd)]
    return pl.pallas_call(
        functools.partial(_ffn_ln_body, cast=cast, mixed=mix is not None),
        grid=(D_FF // FFN_CHUNK,),
        in_specs=in_specs,
        out_specs=out_specs,
        out_shape=out_shape,
        scratch_shapes=[pltpu.VMEM((m, D_MODEL), F32), pltpu.VMEM((m, D_MODEL), BF16),
                        pltpu.VMEM((m, D_MODEL), F32)],
        compiler_params=pltpu.CompilerParams(dimension_semantics=("arbitrary",),
                                             vmem_limit_bytes=VMEM_LIMIT),
        name="ffn_ln",
    )(*args)


def _rope(z, cos_t, sin_t):
    lane = lax.broadcasted_iota(jnp.int32, (1, LANES), 1)
    first_half = (lane % ATT_HEAD_DIM) < (ATT_HEAD_DIM // 2)
    outs = []
    for c in range(z.shape[1] // LANES):
        zc = z[:, c * LANES:(c + 1) * LANES]
        partner = jnp.where(first_half, pltpu.roll(zc, LANES - 32, 1), pltpu.roll(zc, 32, 1))
        outs.append(zc * cos_t + partner * sin_t)
    return outs


def _proj_body(x_ref, w_ref, cos_ref, sin_ref, lbraw_ref,
               q_ref, k_ref, v_ref, qh_ref, kh_ref, ih_ref, lf_ref, gh_ref, wb_ref=None):
    xb = x_ref[...].astype(BF16)
    aw = ATT_WIDTH
    base = 3 * aw

    def cols(start, width):
        w = w_ref[:, start:start + width]
        if wb_ref is not None:
            w = w.astype(BF16)
            wb_ref[:, start:start + width] = w
        return jnp.dot(xb, w, preferred_element_type=F32)

    cos_t = cos_ref[...]
    sin_t = sin_ref[...]
    for c, val in enumerate(_rope(cols(0, aw), cos_t, sin_t)):
        q_ref[:, c * LANES:(c + 1) * LANES] = val * (ATT_HEAD_DIM ** -0.5)
    for c, val in enumerate(_rope(cols(aw, aw), cos_t, sin_t)):
        k_ref[:, c * LANES:(c + 1) * LANES] = val
    lbr = lbraw_ref[...]
    mx = jnp.max(lbr, axis=0, keepdims=True)
    ex = jnp.exp(lbr - mx)
    lb = ex[0:1, :] / jnp.sum(ex, axis=0, keepdims=True)
    zf = cols(base + HG_WIDTH, HG_WIDTH)
    lf_ref[...] = jnp.log(lb + (1.0 - lb) * _sigmoid(zf))
    kh_ref[...] = (1.0 - lb) * _sigmoid(-zf)
    zq = cols(base, HG_WIDTH)
    qh_ref[...] = zq * _sigmoid(zq)
    v_ref[...] = cols(2 * aw, aw)
    ih_ref[...] = cols(base + 2 * HG_WIDTH, HG_WIDTH)
    gh_ref[...] = cols(base + 3 * HG_WIDTH, HG_WIDTH)


def proj(x, w_in, cos_t, sin_t, lb_raw, table_tiles):
    m = x.shape[0]
    emit = w_in.dtype == F32
    assert not emit or m == ROW_TILE, "the weight copy is written once, by a single grid step"
    row = pl.BlockSpec((ROW_TILE, D_MODEL), lambda i: (i, 0))
    tab = pl.BlockSpec((ROW_TILE, LANES), lambda i: (i % table_tiles, 0))
    out = pl.BlockSpec((ROW_TILE, ATT_WIDTH), lambda i: (i, 0))
    sds = jax.ShapeDtypeStruct((m, ATT_WIDTH), F32)
    return pl.pallas_call(
        _proj_body,
        grid=(m // ROW_TILE,),
        in_specs=[row, _const_spec((D_MODEL, IN_COLS)), tab, tab, _const_spec((2, HG_WIDTH))],
        out_specs=[out] * 8 + ([pl.BlockSpec((D_MODEL, IN_COLS), lambda i: (0, 0))] if emit else []),
        out_shape=[sds] * 8 + ([jax.ShapeDtypeStruct(w_in.shape, BF16)] if emit else []),
        compiler_params=pltpu.CompilerParams(dimension_semantics=("arbitrary",),
                                             vmem_limit_bytes=VMEM_LIMIT),
        name="proj",
    )(x, w_in, cos_t, sin_t, lb_raw)


ATT_UNROLL = (8, 8, 4)


def _attn_prompt_body(q_ref, k_ref, v_ref, o_ref, kt_ref, vt_ref, acc_ref, m_ref, l_ref, *, seq):
    blk = ATT_BLOCK
    lane = lax.broadcasted_iota(jnp.int32, (1, LANES), 1)
    head0 = lane < ATT_HEAD_DIM
    qi = lax.broadcasted_iota(jnp.int32, (blk, 2 * blk), 0)
    kc = lax.broadcasted_iota(jnp.int32, (blk, 2 * blk), 1)
    dist = qi + blk - kc
    first_mask = (lax.broadcasted_iota(jnp.int32, (blk, blk), 1)
                  <= lax.broadcasted_iota(jnp.int32, (blk, blk), 0))
    nt = (((1,), (1,)), ((), ()))
    log2e = 1.4426950408889634

    for br, (window, dil) in enumerate(DILATIONS):
        w_sub = window // dil
        rel = (dist >= 0) & (dist <= w_sub)
        span = blk * dil
        n_blk = seq // span
        n_units = ATT_UNROLL[br]
        run = min(n_units, n_blk)
        n_res = n_units // run
        runs_per_res = n_blk // run

        def units(it, carry, br=br, dil=dil, rel=rel, span=span, run=run, n_res=n_res,
                  runs_per_res=runs_per_res, n_units=n_units):
            starts, masks, vbs, scores = [], [], [], []
            for rr in range(n_res):
                if runs_per_res == 1:
                    r, n0 = it * n_res + rr, 0
                else:
                    r, n0 = it // runs_per_res, (it % runs_per_res) * run
                base = r + n0 * span

                def kv_block(j):
                    rows = pl.ds(jnp.maximum(base + j * span, r), blk, stride=dil)
                    return k_ref[rows, :].astype(BF16), v_ref[rows, :].astype(BF16)

                prev = None if runs_per_res == 1 else kv_block(-1)
                for j in range(run):
                    cur = kv_block(j)
                    start = base + j * span
                    q = q_ref[pl.ds(start, blk, stride=dil), :] * log2e
                    if prev is None:
                        kb, vb, mask = cur[0], cur[1], first_mask
                    else:
                        kb = jnp.concatenate([prev[0], cur[0]], axis=0)
                        vb = jnp.concatenate([prev[1], cur[1]], axis=0)
                        mask = rel & (kc >= jnp.where(n0 > 0, 0, blk)) if (j == 0 and runs_per_res > 1) else rel
                    starts.append(start)
                    masks.append(mask)
                    vbs.append(vb)
                    for hsel in (head0, ~head0):
                        qh = jnp.where(hsel, q, 0.0).astype(BF16)
                        scores.append(lax.dot_general(qh, kb, nt, preferred_element_type=F32))
                    prev = cur
            probs, ls, ms = [], [], []
            for i, s in enumerate(scores):
                s = jnp.where(masks[i // 2], s, NEG_INF)
                m = jnp.max(s, axis=-1, keepdims=True)
                p = jnp.exp2(s - m)
                ls.append(jnp.sum(p, axis=-1, keepdims=True))
                ms.append(m)
                probs.append(p.astype(BF16))
            outs = [jnp.dot(p, vbs[i // 2], preferred_element_type=F32) for i, p in enumerate(probs)]
            for uu in range(n_units):
                i0, i1 = 2 * uu, 2 * uu + 1
                rows = pl.ds(starts[uu], blk, stride=dil)
                acc_ref[br, rows, :] = jnp.where(head0, outs[i0], outs[i1])
                m_ref[br, rows, :] = jnp.broadcast_to(jnp.where(head0, ms[i0], ms[i1]), (blk, LANES))
                l_ref[br, rows, :] = jnp.broadcast_to(jnp.where(head0, ls[i0], ls[i1]), (blk, LANES))
            return carry

        lax.fori_loop(0, (dil * n_blk) // n_units, units, 0)

    def combine(c, carry):
        rows = pl.ds(pl.multiple_of(c * blk, blk), blk)
        m0, m1, m2 = m_ref[0, rows, :], m_ref[1, rows, :], m_ref[2, rows, :]
        mx = jnp.maximum(jnp.maximum(m0, m1), m2)
        w0, w1, w2 = jnp.exp2(m0 - mx), jnp.exp2(m1 - mx), jnp.exp2(m2 - mx)
        num = w0 * acc_ref[0, rows, :] + w1 * acc_ref[1, rows, :] + w2 * acc_ref[2, rows, :]
        den = w0 * l_ref[0, rows, :] + w1 * l_ref[1, rows, :] + w2 * l_ref[2, rows, :]
        o_ref[rows, :] = num / den
        return carry

    lax.fori_loop(0, seq // blk, combine, 0)

    win = kt_ref.shape[-1]
    for j in range(win // LANES):
        rows = pl.ds(seq - win + j * LANES, LANES)
        kt_ref[0, :, :, j * LANES:(j + 1) * LANES] = k_ref[rows, :].T.reshape(2, ATT_HEAD_DIM, LANES)
        vt_ref[0, :, :, j * LANES:(j + 1) * LANES] = v_ref[rows, :].T.reshape(2, ATT_HEAD_DIM, LANES)


def attn_prompt(q, k, v, batch, seq):
    win = min(ATT_WINDOW_MAX, seq)
    n_hp = ATT_WIDTH // LANES
    blk = pl.BlockSpec((seq, LANES), lambda b, h: (b, h))
    tblk = pl.BlockSpec((1, 2, ATT_HEAD_DIM, win), lambda b, h: (b, h, 0, 0))
    tsds = jax.ShapeDtypeStruct((batch, ATT_HEADS, ATT_HEAD_DIM, win), F32)
    return pl.pallas_call(
        functools.partial(_attn_prompt_body, seq=seq),
        grid=(batch, n_hp),
        in_specs=[blk, blk, blk],
        out_specs=[blk, tblk, tblk],
        out_shape=[jax.ShapeDtypeStruct((batch * seq, ATT_WIDTH), F32), tsds, tsds],
        scratch_shapes=[pltpu.VMEM((3, seq, LANES), F32)] * 3,
        compiler_params=pltpu.CompilerParams(dimension_semantics=("arbitrary", "arbitrary"),
                                             vmem_limit_bytes=VMEM_LIMIT),
        name="attn_prompt",
    )(q, k, v)


def _hgrn_prompt_body(*refs, rows_per_step, n_cast):
    q_ref, k_ref, v_ref, lf_ref, gh_ref, ng_ref = refs[:6]
    o_ref, s_ref = refs[6 + n_cast:8 + n_cast]
    st_ref = refs[-1]
    for w_ref, wb_ref in zip(refs[6:6 + n_cast], refs[8 + n_cast:8 + 2 * n_cast]):
        wb_ref[...] = w_ref[...].astype(BF16)
    grp = LANES
    n_ch = grp // HG_CHUNK
    ri = lax.broadcasted_iota(jnp.int32, (grp, grp), 0)
    ci = lax.broadcasted_iota(jnp.int32, (grp, grp), 1)
    same_chunk = (ri // HG_CHUNK) == (ci // HG_CHUNK)
    causal = same_chunk & (ci <= ri)
    summat = jnp.where(causal, 1.0, 0.0).astype(BF16)
    nt = (((1,), (1,)), ((), ()))
    tb = pl.program_id(1)

    @pl.when(tb == 0)
    def _():
        st_ref[...] = jnp.zeros(st_ref.shape, F32)

    heads = [slice(h * LANES, (h + 1) * LANES) for h in range(HG_HEADS)]
    in_chunk = [(ri // HG_CHUNK) == c for c in range(n_ch)]

    def by_chunk(x):
        return jnp.concatenate([jnp.where(m, x, 0.0).astype(BF16) for m in in_chunk], axis=1)

    def groups(it, carry):
        gs = range(HG_GROUPS)
        rows = [pl.ds(pl.multiple_of((it * HG_GROUPS + g) * grp, grp), grp) for g in gs]
        cum = []
        for g in gs:
            parts = _split3(lf_ref[rows[g], :])
            cum.append(jnp.dot(summat, parts[0], preferred_element_type=F32)
                       + jnp.dot(summat, parts[1], preferred_element_type=F32)
                       + jnp.dot(summat, parts[2], preferred_element_type=F32))
        q_dec, decay, a, vs, k_end = [], [], [], [], []
        for g in gs:
            cum_end = jnp.concatenate(
                [jnp.broadcast_to(cum[g][(c + 1) * HG_CHUNK - 1:(c + 1) * HG_CHUNK, :], (HG_CHUNK, HG_WIDTH))
                 for c in range(n_ch)], axis=0)
            q, k, v = q_ref[rows[g], :], k_ref[rows[g], :], v_ref[rows[g], :]
            q_dec.append(q * jnp.exp(cum[g]))
            k_dec = (k * jnp.exp(-cum[g])).astype(BF16)
            k_end.append(k * jnp.exp(cum_end - cum[g]))
            decay.append(jnp.exp(cum_end))
            q_decb = q_dec[g].astype(BF16)
            vs.append(v)
            a.append([lax.dot_general(q_decb[:, hs], k_dec[:, hs], nt, preferred_element_type=F32)
                      for hs in heads])
        o = [[jnp.dot(jnp.where(causal, a[g][h], 0.0).astype(BF16), vs[g][:, hs].astype(BF16),
                      preferred_element_type=F32) for h, hs in enumerate(heads)] for g in gs]
        ut = [[jnp.dot(vs[g][:, hs].T.astype(BF16), by_chunk(k_end[g][:, hs]), preferred_element_type=F32)
               for hs in heads] for g in gs]
        starts = [[[] for _ in heads] for _ in gs]
        for h, hs in enumerate(heads):
            st = st_ref[h]
            for g in gs:
                for c in range(n_ch):
                    starts[g][h].append(st.astype(BF16))
                    st = (st * decay[g][c * HG_CHUNK:c * HG_CHUNK + 1, hs]
                          + ut[g][h][:, c * LANES:(c + 1) * LANES])
            st_ref[h] = st
        inter = [[lax.dot_general(by_chunk(q_dec[g][:, hs]), jnp.concatenate(starts[g][h], axis=1), nt,
                                  preferred_element_type=F32) for h, hs in enumerate(heads)] for g in gs]
        for g in gs:
            for h, hs in enumerate(heads):
                oh = o[g][h] + inter[g][h]
                ms = jnp.mean(oh * oh, axis=-1, keepdims=True)
                gate = gh_ref[rows[g], hs]
                o_ref[rows[g], hs] = oh * lax.rsqrt(ms + LN_EPS) * ng_ref[:, hs] * (gate * _sigmoid(gate))
        return carry

    lax.fori_loop(0, rows_per_step // (grp * HG_GROUPS), groups, 0)

    @pl.when(tb == pl.num_programs(1) - 1)
    def _():
        for h in range(HG_HEADS):
            s_ref[0, h] = st_ref[h].T


HG_ROWS = 1024
HG_GROUPS = 4


def hgrn_prompt(qh, kh, ih, lf, gh, norm_g, batch, seq, cast=()):
    n_tb = seq // HG_ROWS
    steps = batch * n_tb

    def rows(b, t):
        return (b * n_tb + t, 0)

    blk = pl.BlockSpec((HG_ROWS, HG_WIDTH), rows)
    cast_specs = [pl.BlockSpec((w.shape[0] // steps, w.shape[1]), rows) for w in cast]
    return pl.pallas_call(
        functools.partial(_hgrn_prompt_body, rows_per_step=HG_ROWS, n_cast=len(cast)),
        grid=(batch, n_tb),
        in_specs=[blk, blk, blk, blk, blk, _const_spec((1, HG_WIDTH))] + cast_specs,
        out_specs=[blk, pl.BlockSpec((1, HG_HEADS, HG_DK, HG_DV), lambda b, t: (b, 0, 0, 0))] + cast_specs,
        out_shape=[jax.ShapeDtypeStruct((batch * seq, HG_WIDTH), F32),
                   jax.ShapeDtypeStruct((batch, HG_HEADS, HG_DK, HG_DV), F32)]
                  + [jax.ShapeDtypeStruct(w.shape, BF16) for w in cast],
        scratch_shapes=[pltpu.VMEM((HG_HEADS, HG_DV, HG_DK), F32)],
        compiler_params=pltpu.CompilerParams(dimension_semantics=("arbitrary", "arbitrary"),
                                             vmem_limit_bytes=VMEM_LIMIT),
        name="hgrn_prompt",
    )(qh, kh, ih, lf, gh, norm_g, *cast)


FA_ROWS = 128
FA_HEADS = 4
FA_SLOTS = 4


def _divmod_pow2(s, n):
    assert n & (n - 1) == 0
    if isinstance(s, int):
        return s // n, s % n
    return lax.shift_right_logical(s, n.bit_length() - 1), s & (n - 1)


def _ffn_attn_body(*refs, n_new, past, b_off, mixed, n_steps):
    n_x = 10 if mixed else 1
    wg_ref, wu_ref, wd_ref, g_ref, b_ref, q_ref, kn_ref, vn_ref, kc_hbm, vc_hbm = refs[n_x:n_x + 10]
    y_ref, o_ref, nk_ref, nv_ref, kbuf, vbuf, sems = refs[-8:-1] if mixed else refs[-7:]
    halves = ATT_HEADS // FA_HEADS
    step = pl.program_id(0)

    split = _divmod_pow2
    odd = (b_off + split(step, halves)[0]) & 1 == 1

    def window_copies(s):
        slot = split(s, FA_SLOTS)[1]
        b_rel, half = split(s, halves)
        src = (b_off + b_rel, pl.ds(half * FA_HEADS, FA_HEADS))
        return (pltpu.make_async_copy(kc_hbm.at[src], kbuf.at[slot], sems.at[slot, 0]),
                pltpu.make_async_copy(vc_hbm.at[src], vbuf.at[slot], sems.at[slot, 1]))

    @pl.when(step == 0)
    def _():
        for s in range(min(FA_SLOTS - 1, n_steps)):
            for c in window_copies(s):
                c.start()

    @pl.when(step + (FA_SLOTS - 1) < n_steps)
    def _():
        for c in window_copies(step + (FA_SLOTS - 1)):
            c.start()

    for c in window_copies(step):
        c.wait()
    slot = split(step, FA_SLOTS)[1]
    kc_ref, vc_ref = kbuf.at[slot], vbuf.at[slot]

    def new_rows(ref):
        both = ref[...]
        return jnp.where(odd, both[n_new:2 * n_new], both[0:n_new])

    q_new, k_new, v_new = new_rows(q_ref), new_rows(kn_ref), new_rows(vn_ref)
    head_cols = [slice(h * ATT_HEAD_DIM, (h + 1) * ATT_HEAD_DIM) for h in range(FA_HEADS)]
    srow = lax.broadcasted_iota(jnp.int32, (n_new, past), 0)
    tcol = lax.broadcasted_iota(jnp.int32, (n_new, past), 1)
    dist = past + srow - tcol
    def on_stride(d, dil):
        return (d & (dil - 1)) == 0 if dil & (dil - 1) == 0 else d % dil == 0

    cnt = jnp.zeros((n_new, past), F32)
    for window, dil in DILATIONS:
        cnt = cnt + jnp.where(on_stride(dist, dil) & (dist <= window), 1.0, 0.0)
    si = lax.broadcasted_iota(jnp.int32, (n_new, n_new), 0)
    sj = lax.broadcasted_iota(jnp.int32, (n_new, n_new), 1)
    dn = si - sj
    cnt_new = jnp.zeros((n_new, n_new), F32)
    for window, dil in DILATIONS:
        cnt_new = cnt_new + jnp.where((dn >= 0) & on_stride(dn, dil) & (dn <= window), 1.0, 0.0)
    lane = lax.broadcasted_iota(jnp.int32, (1, LANES), 1)
    nt = (((1,), (1,)), ((), ()))
    shift = past - n_new
    hs = range(FA_HEADS)

    def scores():
        qs = [q_new[:, c].astype(BF16) for c in head_cols]
        s_c = [jnp.dot(qs[h], kc_ref[h].astype(BF16), preferred_element_type=F32) for h in hs]
        s_n = [lax.dot_general(qs[h], k_new[:, head_cols[h]].astype(BF16), nt, preferred_element_type=F32)
               for h in hs]
        return s_c, s_n

    if mixed:
        first, nxt = refs[0:3], refs[3:6]
        wa_ref, wh_ref, g2_ref, b2_ref = refs[6:n_x]
        xin_ref = refs[-1]

        def mixed_rows(x1_ref, oa_ref, oh_ref):
            mix = jnp.dot(oa_ref[...].astype(BF16), wa_ref[...], preferred_element_type=F32)
            mix = mix + jnp.dot(oh_ref[...].astype(BF16), wh_ref[...], preferred_element_type=F32)
            return _layer_norm(DN_ALPHA * x1_ref[...] + mix, g2_ref[...], b2_ref[...])

        @pl.when(step == 0)
        def _():
            xin_ref[...] = mixed_rows(*first)

        x = xin_ref[...]
        x_next = mixed_rows(*nxt)
    else:
        x = refs[0][...]
    xb = x.astype(BF16)
    gate = jnp.dot(xb, wg_ref[...], preferred_element_type=F32)
    up = jnp.dot(xb, wu_ref[...], preferred_element_type=F32)
    s_c, s_n = scores()
    knt = [k_new[:, c].T for c in head_cols]
    vnt = [v_new[:, c].T for c in head_cols]

    hmid = (gate * _sigmoid(gate) * up).astype(BF16)
    y = jnp.dot(hmid, wd_ref[...], preferred_element_type=F32)
    y_ref[...] = _layer_norm(DN_ALPHA * x + 0.5 * y, g_ref[...], b_ref[...])
    if mixed:
        xin_ref[...] = x_next

    p_c, p_n, den = [], [], []
    for h in hs:
        sc = jnp.where(cnt > 0, s_c[h], NEG_INF)
        sn = jnp.where(cnt_new > 0, s_n[h], NEG_INF)
        m = jnp.maximum(jnp.max(sc, axis=-1, keepdims=True), jnp.max(sn, axis=-1, keepdims=True))
        p_c.append((cnt * jnp.exp(sc - m)).astype(BF16))
        p_n.append((cnt_new * jnp.exp(sn - m)).astype(BF16))
        den.append(jnp.sum(p_c[h].astype(F32), axis=-1, keepdims=True)
                   + jnp.sum(p_n[h].astype(F32), axis=-1, keepdims=True))
    outs = []
    for h in hs:
        num = (lax.dot_general(p_c[h], vc_ref[h].astype(BF16), nt, preferred_element_type=F32)
               + jnp.dot(p_n[h], v_new[:, head_cols[h]].astype(BF16), preferred_element_type=F32))
        outs.append(num / den[h])
    o_ref[0] = jnp.concatenate(outs, axis=1)

    for h in hs:
        for src_ref, new, dst in ((kc_ref, knt[h], nk_ref), (vc_ref, vnt[h], nv_ref)):
            rolled = pltpu.roll(src_ref[h], shift, 1)
            dst[0, h, :, 0:past - LANES] = rolled[:, 0:past - LANES]
            last = rolled[:, past - LANES:past]
            for s in range(n_new):
                col = jnp.broadcast_to(new[:, s:s + 1], last.shape)
                last = jnp.where(lane == LANES - n_new + s, col, last)
            dst[0, h, :, past - LANES:past] = last


def ffn_attn(x, wg, wu, wd, g, b, q, kn, vn, kc, vc, n_new, b_off, prev=None, mix=None):
    m = x.shape[0]
    bsz, n_heads, _, past = kc.shape
    halves = n_heads // FA_HEADS
    steps = m // FA_ROWS
    assert b_off % 2 == 0 and bsz % 2 == 0, "new-token rows are fetched per pair of batch rows"

    def where(i):
        b_rel, half = _divmod_pow2(i, halves)
        return b_off + b_rel, half

    def samp(i):
        return (*where(i), 0, 0)

    row = pl.BlockSpec((FA_ROWS, D_MODEL), lambda i: (i, 0))
    pair = pl.BlockSpec((2 * n_new, FA_HEADS * ATT_HEAD_DIM),
                        lambda i: (_divmod_pow2(where(i)[0], 2)[0], where(i)[1]))
    o_blk = pl.BlockSpec((1, n_new, FA_HEADS * ATT_HEAD_DIM), lambda i: (where(i)[0], 0, where(i)[1]))
    big = pl.BlockSpec((1, FA_HEADS, ATT_HEAD_DIM, past), samp)
    if mix is None:
        in_specs, args = [row], [x]
    else:
        o_att, o_hg, w_out, ln_g, ln_b = mix
        in_specs, args = [], []
        for index in (lambda i: (0, 0), lambda i: (jnp.minimum(i + 1, steps - 1), 0)):
            in_specs += [pl.BlockSpec((FA_ROWS, D_MODEL), index), pl.BlockSpec((FA_ROWS, ATT_WIDTH), index),
                         pl.BlockSpec((FA_ROWS, HG_WIDTH), index)]
            args += [x, o_att, o_hg]
        in_specs += _w_out_specs() + [_const_spec((1, D_MODEL)), _const_spec((1, D_MODEL))]
        args += [w_out, w_out, ln_g, ln_b]
    hbm = pl.BlockSpec(memory_space=pl.ANY)
    in_specs += [_const_spec((D_MODEL, D_FF)), _const_spec((D_MODEL, D_FF)), _const_spec((D_FF, D_MODEL)),
                 _const_spec((1, D_MODEL)), _const_spec((1, D_MODEL)), pair, pair, pair, hbm, hbm]
    args += [wg, wu, wd, g, b, q, kn, vn, kc, vc]
    aliases = {}
    if prev is not None:
        for j, a in enumerate(prev):
            aliases[len(args)] = 1 + j
            in_specs.append(pl.BlockSpec(memory_space=pl.ANY))
            args.append(a)
    big_sds = jax.ShapeDtypeStruct(kc.shape, F32)
    return pl.pallas_call(
        functools.partial(_ffn_attn_body, n_new=n_new, past=past, b_off=b_off, mixed=mix is not None,
                          n_steps=steps),
        grid=(steps,),
        in_specs=in_specs,
        out_specs=[row, o_blk, big, big],
        out_shape=[jax.ShapeDtypeStruct((m, D_MODEL), F32),
                   jax.ShapeDtypeStruct((bsz, n_new, n_heads * ATT_HEAD_DIM), F32), big_sds, big_sds],
        scratch_shapes=[pltpu.VMEM((FA_SLOTS, FA_HEADS, ATT_HEAD_DIM, past), F32),
                        pltpu.VMEM((FA_SLOTS, FA_HEADS, ATT_HEAD_DIM, past), F32),
                        pltpu.SemaphoreType.DMA((FA_SLOTS, 2))]
                       + ([pltpu.VMEM((FA_ROWS, D_MODEL), F32)] if mix is not None else []),
        input_output_aliases=aliases,
        compiler_params=pltpu.CompilerParams(dimension_semantics=("arbitrary",),
                                             vmem_limit_bytes=VMEM_LIMIT),
        name="ffn_attn",
    )(*args)


HS_BATCH = 8


def _hgrn_sample_body(q_ref, k_ref, v_ref, lf_ref, gh_ref, ng_ref, s0_ref, o_ref, s_ref, *, n_new):
    row = lax.broadcasted_iota(jnp.int32, (n_new, 1), 0)
    tn = (((0,), (0,)), ((), ()))
    ones12 = jnp.ones((3 * n_new, HG_DV), BF16)

    sub = 8
    per_tile = sub // n_new

    def tile(t, carry):
        rows = pl.ds(pl.multiple_of(t * sub, sub), sub)
        units = []
        for h in range(HG_HEADS):
            cols = slice(h * LANES, (h + 1) * LANES)
            q8, k8, v8, g8 = q_ref[rows, cols], k_ref[rows, cols], v_ref[rows, cols], lf_ref[rows, cols]
            for i in range(per_tile):
                part = slice(i * n_new, (i + 1) * n_new)
                q, k, v, g = q8[part], k8[part], v8[part], g8[part]
                cum = jnp.zeros((n_new, LANES), F32)
                for j in range(n_new):
                    cum = cum + jnp.where(row >= j, g[j:j + 1, :], 0.0)
                cum_end = cum[n_new - 1:n_new, :]
                units.append(dict(h=h, b=t * per_tile + i, v=v, g=g, q_dec=q * jnp.exp(cum),
                                  k_dec=k * jnp.exp(-cum), k_end=k * jnp.exp(cum_end - cum)))
        for u in units:
            u["s0"] = s0_ref[u["b"], u["h"]]
            u["o"] = jnp.dot(u["q_dec"].astype(BF16), u["s0"].astype(BF16), preferred_element_type=F32)
        for u in units:
            dec = lax.dot_general(jnp.concatenate(_split3(u["g"]), axis=0), ones12, tn,
                                  preferred_element_type=F32)
            upd = lax.dot_general(u["k_end"].astype(BF16), u["v"].astype(BF16), tn,
                                  preferred_element_type=F32)
            s_ref[u["b"], u["h"]] = jnp.exp(dec) * u["s0"] + upd
        for h in range(HG_HEADS):
            cols = slice(h * LANES, (h + 1) * LANES)
            outs = []
            for u in units[h * per_tile:(h + 1) * per_tile]:
                o = u["o"]
                for j in range(n_new):
                    a_j = jnp.sum(u["q_dec"] * u["k_dec"][j:j + 1, :], axis=-1, keepdims=True)
                    o = o + jnp.where(row >= j, a_j, 0.0) * u["v"][j:j + 1, :]
                outs.append(o * lax.rsqrt(jnp.mean(o * o, axis=-1, keepdims=True) + LN_EPS))
            gate = gh_ref[rows, cols]
            o_ref[rows, cols] = jnp.concatenate(outs, axis=0) * ng_ref[:, cols] * (gate * _sigmoid(gate))
        return carry

    lax.fori_loop(0, HS_BATCH // per_tile, tile, 0)


def hgrn_sample(qh, kh, ih, lf, gh, norm_g, s0, n_new):
    bsz = s0.shape[0]
    blk = pl.BlockSpec((HS_BATCH * n_new, HG_WIDTH), lambda i: (i, 0))
    sblk = pl.BlockSpec((HS_BATCH, HG_HEADS, HG_DK, HG_DV), lambda i: (i, 0, 0, 0))
    return pl.pallas_call(
        functools.partial(_hgrn_sample_body, n_new=n_new),
        grid=(bsz // HS_BATCH,),
        in_specs=[blk, blk, blk, blk, blk, _const_spec((1, HG_WIDTH)), sblk],
        out_specs=[blk, sblk],
        out_shape=[jax.ShapeDtypeStruct((bsz * n_new, HG_WIDTH), F32),
                   jax.ShapeDtypeStruct(s0.shape, F32)],
        compiler_params=pltpu.CompilerParams(dimension_semantics=("arbitrary",),
                                             vmem_limit_bytes=VMEM_LIMIT),
        name="hgrn_sample",
    )(qh, kh, ih, lf, gh, norm_g, s0)


def _rope_tables(pos):
    half = ATT_HEAD_DIM // 2
    inv = ROPE_THETA ** (-jnp.arange(half, dtype=F32) / half)
    ang = pos.astype(F32)[:, None] * inv[None, :]
    cos, sin = jnp.cos(ang), jnp.sin(ang)
    return jnp.tile(cos, (1, 4)), jnp.tile(jnp.concatenate([-sin, sin], axis=1), (1, 2))


def kernel(x_prompt, x_sample, cache_k, cache_v, state_hgrn, ffn1_w_gate, ffn1_w_up, ffn1_w_down, ln1_g, ln1_b, w_in, hg_lower_bound, hg_norm_g, w_out, ln2_g, ln2_b, ffn2_w_gate, ffn2_w_up, ffn2_w_down, ln3_g, ln3_b):
    batch, seq, _ = x_prompt.shape
    dec_batch, n_new, _ = x_sample.shape
    assert ffn1_w_gate.shape[0] == 1, "single layer"


    cos_p, sin_p = _rope_tables(jnp.arange(seq, dtype=jnp.int32))
    pos_s = PAST_LEN + (jnp.arange(dec_batch * n_new, dtype=jnp.int32) % n_new)
    cos_s, sin_s = _rope_tables(pos_s)

    xp = x_prompt.reshape(batch * seq, D_MODEL)
    xs = x_sample.reshape(dec_batch * n_new, D_MODEL)

    xs1, *w1 = ffn_ln(xs, ffn1_w_gate[0], ffn1_w_up[0], ffn1_w_down[0], ln1_g, ln1_b)
    qs, ks, vs, qhs, khs, ihs, lfs, ghs, w_in_b = proj(xs1, w_in[0], cos_s, sin_s, hg_lower_bound, 1)

    samp = (qs, ks, vs,
            cache_k[0].transpose(0, 2, 3, 1),
            cache_v[0].transpose(0, 2, 3, 1), n_new)
    steps = (batch * seq) // FA_ROWS
    per_call = steps // (ATT_HEADS // FA_HEADS)
    assert 2 * per_call == dec_batch, "the two prompt FFN calls must cover the sample batch"

    xp1, o_t, nk, nv = ffn_attn(xp, *w1, ln1_g, ln1_b, *samp, 0)
    q, k, v, qh, kh, ih, lf, gh = proj(xp1, w_in_b, cos_p, sin_p, hg_lower_bound, seq // ROW_TILE)
    o_att, kt_p, vt_p = attn_prompt(q, k, v, batch, seq)
    o_hg, s_p, *w2, w_out_b = hgrn_prompt(qh, kh, ih, lf, gh, hg_norm_g, batch, seq,
                                          cast=(ffn2_w_gate[0], ffn2_w_up[0], ffn2_w_down[0], w_out[0]))
    yp, o_t, nk, nv = ffn_attn(xp1, *w2, ln3_g, ln3_b, *samp, per_call, prev=(o_t, nk, nv),
                               mix=(o_att, o_hg, w_out_b, ln2_g, ln2_b))

    o_att_s = o_t.reshape(dec_batch * n_new, ATT_WIDTH)

    o_hg_s, s_s = hgrn_sample(qhs, khs, ihs, lfs, ghs, hg_norm_g, state_hgrn[0], n_new)
    ys, = ffn_ln(xs1, *w2, ln3_g, ln3_b, mix=(o_att_s, o_hg_s, w_out_b, ln2_g, ln2_b))

    return (yp.reshape(batch, seq, D_MODEL),
            ys.reshape(dec_batch, n_new, D_MODEL),
            kt_p.transpose(0, 3, 1, 2)[None],
            vt_p.transpose(0, 3, 1, 2)[None],
            s_p[None],
            nk.transpose(0, 3, 1, 2)[None],
            nv.transpose(0, 3, 1, 2)[None],
            s_s[None])
```

```python
import functools

import jax
import jax.numpy as jnp
from jax import lax
from jax.experimental import pallas as pl
from jax.experimental.pallas import tpu as pltpu

F32 = jnp.float32
BF16 = jnp.bfloat16

D_MODEL = 1024
ATT_WIDTH = 512
HG_WIDTH = 512
ATT_HEAD_DIM = 64
ATT_HEADS = 8
DILATIONS = ((128, 1), (512, 4), (2048, 16))
ATT_WINDOW_MAX = 2048
ATT_BLOCK = 128
ROPE_THETA = 10000.0
HG_HEADS = 4
HG_DK = 128
HG_DV = 128
HG_CHUNK = 32
D_FF = 2816
IN_COLS = 3 * ATT_WIDTH + 4 * HG_WIDTH
PAST_LEN = 8192
DN_ALPHA = 2.0 ** 0.25
LN_EPS = 1e-5
NEG_INF = -1e30

LANES = 128
ROW_TILE = 512
VMEM_LIMIT = 56 * 1024 * 1024


def _const_spec(shape, block=None):
    index = tuple(block) if block is not None else (0,) * len(shape)
    return pl.BlockSpec(shape, lambda *_: index, pipeline_mode=pl.Buffered(1))


def _layer_norm(r, g, b):
    mu = jnp.mean(r, axis=-1, keepdims=True)
    c = r - mu
    var = jnp.mean(c * c, axis=-1, keepdims=True)
    return c * lax.rsqrt(var + LN_EPS) * g + b


def _sigmoid(x):
    return 1.0 / (1.0 + jnp.exp(-x))


def _split3(x):
    hi = x.astype(BF16)
    r1 = x - hi.astype(F32)
    mid = r1.astype(BF16)
    lo = (r1 - mid.astype(F32)).astype(BF16)
    return hi, mid, lo


def _w_out_specs():
    assert ATT_WIDTH == HG_WIDTH
    return [_const_spec((ATT_WIDTH, D_MODEL), (0, 0)), _const_spec((HG_WIDTH, D_MODEL), (1, 0))]


FFN_CHUNK = 256


def _ffn_ln_body(*refs, cast, mixed):
    n_x = 7 if mixed else 1
    wg_ref, wu_ref, wd_ref, g_ref, b_ref, o_ref = refs[n_x:n_x + 6]
    xin_ref, xb_ref, acc_ref = refs[-3:]
    j = pl.program_id(0)

    @pl.when(j == 0)
    def _():
        if mixed:
            x1_ref, oa_ref, oh_ref, wa_ref, wh_ref, g2_ref, b2_ref = refs[:n_x]
            mix = jnp.dot(oa_ref[...].astype(BF16), wa_ref[...], preferred_element_type=F32)
            mix = mix + jnp.dot(oh_ref[...].astype(BF16), wh_ref[...], preferred_element_type=F32)
            x = _layer_norm(DN_ALPHA * x1_ref[...] + mix, g2_ref[...], b2_ref[...])
        else:
            x = refs[0][...]
        xin_ref[...] = x
        xb_ref[...] = x.astype(BF16)
        acc_ref[...] = jnp.zeros(acc_ref.shape, F32)

    wgb, wub, wdb = wg_ref[...], wu_ref[...], wd_ref[...]
    if cast:
        wgb, wub, wdb = wgb.astype(BF16), wub.astype(BF16), wdb.astype(BF16)
        for dst, w in zip(refs[n_x + 6:n_x + 9], (wgb, wub, wdb)):
            dst[...] = w
    xb = xb_ref[...]
    gate = jnp.dot(xb, wgb, preferred_element_type=F32)
    up = jnp.dot(xb, wub, preferred_element_type=F32)
    h = (gate * _sigmoid(gate) * up).astype(BF16)
    acc_ref[...] += jnp.dot(h, wdb, preferred_element_type=F32)

    @pl.when(j == pl.num_programs(0) - 1)
    def _():
        o_ref[...] = _layer_norm(DN_ALPHA * xin_ref[...] + 0.5 * acc_ref[...], g_ref[...], b_ref[...])


def ffn_ln(x, wg, wu, wd, g, b, mix=None):
    m = x.shape[0]
    cast = wg.dtype == F32
    col = pl.BlockSpec((D_MODEL, FFN_CHUNK), lambda j: (0, j))
    rowc = pl.BlockSpec((FFN_CHUNK, D_MODEL), lambda j: (j, 0))
    xs = pl.BlockSpec((m, D_MODEL), lambda j: (0, 0))
    in_specs, args = [xs], [x]
    if mix is not None:
        half = pl.BlockSpec((m, ATT_WIDTH), lambda j: (0, 0))
        in_specs += [half, half] + _w_out_specs() + [_const_spec((1, D_MODEL)), _const_spec((1, D_MODEL))]
        o_att, o_hg, w_out, ln_g, ln_b = mix
        args += [o_att, o_hg, w_out, w_out, ln_g, ln_b]
    in_specs += [col, col, rowc, _const_spec((1, D_MODEL)), _const_spec((1, D_MODEL))]
    args += [wg, wu, wd, g, b]
    out_specs = [xs] + ([col, col, rowc] if cast else [])
    out_shape = [jax.ShapeDtypeStruct((m, D_MODEL), F32)]
    if cast:
        out_shape += [jax.ShapeDtypeStruct(w.shape, BF16) for w in (wg, wu, wd)]
    return pl.pallas_call(
        functools.partial(_ffn_ln_body, cast=cast, mixed=mix is not None),
        grid=(D_FF // FFN_CHUNK,),
        in_specs=in_specs,
        out_specs=out_specs,
        out_shape=out_shape,
        scratch_shapes=[pltpu.VMEM((m, D_MODEL), F32), pltpu.VMEM((m, D_MODEL), BF16),
                        pltpu.VMEM((m, D_MODEL), F32)],
        compiler_params=pltpu.CompilerParams(dimension_semantics=("arbitrary",),
                                             vmem_limit_bytes=VMEM_LIMIT),
        name="ffn_ln",
    )(*args)


def _rope(z, cos_t, sin_t):
    lane = lax.broadcasted_iota(jnp.int32, (1, LANES), 1)
    first_half = (lane % ATT_HEAD_DIM) < (ATT_HEAD_DIM // 2)
    outs = []
    for c in range(z.shape[1] // LANES):
        zc = z[:, c * LANES:(c + 1) * LANES]
        partner = jnp.where(first_half, pltpu.roll(zc, LANES - 32, 1), pltpu.roll(zc, 32, 1))
        outs.append(zc * cos_t + partner * sin_t)
    return outs


def _proj_body(x_ref, w_ref, cos_ref, sin_ref, lbraw_ref,
               q_ref, k_ref, v_ref, qh_ref, kh_ref, ih_ref, lf_ref, gh_ref, *extra, emit, window):
    wb_ref = extra[0] if emit else None
    kt_ref, vt_ref = extra[-2:] if window else (None, None)
    xb = x_ref[...].astype(BF16)
    aw = ATT_WIDTH
    base = 3 * aw

    def transposed(ref, c, val):
        ref[0, 2 * c:2 * c + 2] = val.T.reshape(2, ATT_HEAD_DIM, val.shape[0])

    def cols(start, width):
        w = w_ref[:, start:start + width]
        if wb_ref is not None:
            w = w.astype(BF16)
            wb_ref[:, start:start + width] = w
        return jnp.dot(xb, w, preferred_element_type=F32)

    cos_t = cos_ref[...]
    sin_t = sin_ref[...]
    for c, val in enumerate(_rope(cols(0, aw), cos_t, sin_t)):
        q_ref[:, c * LANES:(c + 1) * LANES] = val * (ATT_HEAD_DIM ** -0.5)
    for c, val in enumerate(_rope(cols(aw, aw), cos_t, sin_t)):
        k_ref[:, c * LANES:(c + 1) * LANES] = val
        if window:
            transposed(kt_ref, c, val)
    lbr = lbraw_ref[...]
    mx = jnp.max(lbr, axis=0, keepdims=True)
    ex = jnp.exp(lbr - mx)
    lb = ex[0:1, :] / jnp.sum(ex, axis=0, keepdims=True)
    zf = cols(base + HG_WIDTH, HG_WIDTH)
    lf_ref[...] = jnp.log(lb + (1.0 - lb) * _sigmoid(zf))
    kh_ref[...] = (1.0 - lb) * _sigmoid(-zf)
    zq = cols(base, HG_WIDTH)
    qh_ref[...] = zq * _sigmoid(zq)
    v = cols(2 * aw, aw)
    v_ref[...] = v
    if window:
        for c in range(aw // LANES):
            transposed(vt_ref, c, v[:, c * LANES:(c + 1) * LANES])
    ih_ref[...] = cols(base + 2 * HG_WIDTH, HG_WIDTH)
    gh_ref[...] = cols(base + 3 * HG_WIDTH, HG_WIDTH)


def proj(x, w_in, cos_t, sin_t, lb_raw, table_tiles, win_tiles=0):
    m = x.shape[0]
    emit = w_in.dtype == F32
    assert not emit or m == ROW_TILE, "the weight copy is written once, by a single grid step"
    row = pl.BlockSpec((ROW_TILE, D_MODEL), lambda i: (i, 0))
    tab = pl.BlockSpec((ROW_TILE, LANES), lambda i: (i % table_tiles, 0))
    out = pl.BlockSpec((ROW_TILE, ATT_WIDTH), lambda i: (i, 0))
    sds = jax.ShapeDtypeStruct((m, ATT_WIDTH), F32)
    out_specs, out_shape = [out] * 8, [sds] * 8
    if emit:
        out_specs.append(pl.BlockSpec((D_MODEL, IN_COLS), lambda i: (0, 0)))
        out_shape.append(jax.ShapeDtypeStruct(w_in.shape, BF16))
    if win_tiles:
        skipped = table_tiles - win_tiles
        wblk = pl.BlockSpec((1, ATT_HEADS, ATT_HEAD_DIM, ROW_TILE),
                            lambda i: (i // table_tiles, 0, 0, jnp.maximum(i % table_tiles - skipped, 0)))
        wsds = jax.ShapeDtypeStruct((m // (table_tiles * ROW_TILE), ATT_HEADS, ATT_HEAD_DIM,
                                     win_tiles * ROW_TILE), F32)
        out_specs += [wblk, wblk]
        out_shape += [wsds, wsds]
    return pl.pallas_call(
        functools.partial(_proj_body, emit=emit, window=bool(win_tiles)),
        grid=(m // ROW_TILE,),
        in_specs=[row, _const_spec((D_MODEL, IN_COLS)), tab, tab, _const_spec((2, HG_WIDTH))],
        out_specs=out_specs,
        out_shape=out_shape,
        compiler_params=pltpu.CompilerParams(dimension_semantics=("arbitrary",),
                                             vmem_limit_bytes=VMEM_LIMIT),
        name="proj",
    )(x, w_in, cos_t, sin_t, lb_raw)


ATT_UNROLL = (8, 8, 4)


def _attn_prompt_body(q_ref, k_ref, v_ref, o_ref, acc_ref, m_ref, l_ref, *, seq):
    blk = ATT_BLOCK
    lane = lax.broadcasted_iota(jnp.int32, (1, LANES), 1)
    head0 = lane < ATT_HEAD_DIM
    qi = lax.broadcasted_iota(jnp.int32, (blk, 2 * blk), 0)
    kc = lax.broadcasted_iota(jnp.int32, (blk, 2 * blk), 1)
    dist = qi + blk - kc
    first_mask = (lax.broadcasted_iota(jnp.int32, (blk, blk), 1)
                  <= lax.broadcasted_iota(jnp.int32, (blk, blk), 0))
    nt = (((1,), (1,)), ((), ()))
    log2e = 1.4426950408889634

    for br, (window, dil) in enumerate(DILATIONS):
        w_sub = window // dil
        rel = (dist >= 0) & (dist <= w_sub)
        span = blk * dil
        n_blk = seq // span
        n_units = ATT_UNROLL[br]
        run = min(n_units, n_blk)
        n_res = n_units // run
        runs_per_res = n_blk // run

        def units(it, carry, br=br, dil=dil, rel=rel, span=span, run=run, n_res=n_res,
                  runs_per_res=runs_per_res, n_units=n_units):
            starts, masks, vbs, scores = [], [], [], []
            for rr in range(n_res):
                if runs_per_res == 1:
                    r, n0 = it * n_res + rr, 0
                else:
                    r, n0 = it // runs_per_res, (it % runs_per_res) * run
                base = r + n0 * span

                def kv_block(j):
                    rows = pl.ds(jnp.maximum(base + j * span, r), blk, stride=dil)
                    return k_ref[rows, :].astype(BF16), v_ref[rows, :].astype(BF16)

                prev = None if runs_per_res == 1 else kv_block(-1)
                for j in range(run):
                    cur = kv_block(j)
                    start = base + j * span
                    q = q_ref[pl.ds(start, blk, stride=dil), :] * log2e
                    if prev is None:
                        kb, vb, mask = cur[0], cur[1], first_mask
                    else:
                        kb = jnp.concatenate([prev[0], cur[0]], axis=0)
                        vb = jnp.concatenate([prev[1], cur[1]], axis=0)
                        mask = rel & (kc >= jnp.where(n0 > 0, 0, blk)) if (j == 0 and runs_per_res > 1) else rel
                    starts.append(start)
                    masks.append(mask)
                    vbs.append(vb)
                    for hsel in (head0, ~head0):
                        qh = jnp.where(hsel, q, 0.0).astype(BF16)
                        scores.append(lax.dot_general(qh, kb, nt, preferred_element_type=F32))
                    prev = cur
            probs, ls, ms = [], [], []
            for i, s in enumerate(scores):
                s = jnp.where(masks[i // 2], s, NEG_INF)
                m = jnp.max(s, axis=-1, keepdims=True)
                p = jnp.exp2(s - m)
                ls.append(jnp.sum(p, axis=-1, keepdims=True))
                ms.append(m)
                probs.append(p.astype(BF16))
            outs = [jnp.dot(p, vbs[i // 2], preferred_element_type=F32) for i, p in enumerate(probs)]
            for uu in range(n_units):
                i0, i1 = 2 * uu, 2 * uu + 1
                rows = pl.ds(starts[uu], blk, stride=dil)
                acc_ref[br, rows, :] = jnp.where(head0, outs[i0], outs[i1])
                m_ref[br, rows, :] = jnp.broadcast_to(jnp.where(head0, ms[i0], ms[i1]), (blk, LANES))
                l_ref[br, rows, :] = jnp.broadcast_to(jnp.where(head0, ls[i0], ls[i1]), (blk, LANES))
            return carry

        lax.fori_loop(0, (dil * n_blk) // n_units, units, 0)

    def combine(c, carry):
        rows = pl.ds(pl.multiple_of(c * blk, blk), blk)
        m0, m1, m2 = m_ref[0, rows, :], m_ref[1, rows, :], m_ref[2, rows, :]
        mx = jnp.maximum(jnp.maximum(m0, m1), m2)
        w0, w1, w2 = jnp.exp2(m0 - mx), jnp.exp2(m1 - mx), jnp.exp2(m2 - mx)
        num = w0 * acc_ref[0, rows, :] + w1 * acc_ref[1, rows, :] + w2 * acc_ref[2, rows, :]
        den = w0 * l_ref[0, rows, :] + w1 * l_ref[1, rows, :] + w2 * l_ref[2, rows, :]
        o_ref[rows, :] = num / den
        return carry

    lax.fori_loop(0, seq // blk, combine, 0)


def attn_prompt(q, k, v, batch, seq):
    n_hp = ATT_WIDTH // LANES
    blk = pl.BlockSpec((seq, LANES), lambda b, h: (b, h))
    return pl.pallas_call(
        functools.partial(_attn_prompt_body, seq=seq),
        grid=(batch, n_hp),
        in_specs=[blk, blk, blk],
        out_specs=blk,
        out_shape=jax.ShapeDtypeStruct((batch * seq, ATT_WIDTH), F32),
        scratch_shapes=[pltpu.VMEM((3, seq, LANES), F32)] * 3,
        compiler_params=pltpu.CompilerParams(dimension_semantics=("arbitrary", "arbitrary"),
                                             vmem_limit_bytes=VMEM_LIMIT),
        name="attn_prompt",
    )(q, k, v)


def _hgrn_prompt_body(*refs, rows_per_step, n_cast):
    q_ref, k_ref, v_ref, lf_ref, gh_ref, ng_ref = refs[:6]
    o_ref, s_ref = refs[6 + n_cast:8 + n_cast]
    st_ref = refs[-1]
    for w_ref, wb_ref in zip(refs[6:6 + n_cast], refs[8 + n_cast:8 + 2 * n_cast]):
        wb_ref[...] = w_ref[...].astype(BF16)
    grp = LANES
    n_ch = grp // HG_CHUNK
    ri = lax.broadcasted_iota(jnp.int32, (grp, grp), 0)
    ci = lax.broadcasted_iota(jnp.int32, (grp, grp), 1)
    same_chunk = (ri // HG_CHUNK) == (ci // HG_CHUNK)
    causal = same_chunk & (ci <= ri)
    summat = jnp.where(causal, 1.0, 0.0).astype(BF16)
    nt = (((1,), (1,)), ((), ()))
    tb = pl.program_id(1)

    @pl.when(tb == 0)
    def _():
        st_ref[...] = jnp.zeros(st_ref.shape, F32)

    heads = [slice(h * LANES, (h + 1) * LANES) for h in range(HG_HEADS)]
    in_chunk = [(ri // HG_CHUNK) == c for c in range(n_ch)]

    def by_chunk(x):
        return jnp.concatenate([jnp.where(m, x, 0.0).astype(BF16) for m in in_chunk], axis=1)

    def groups(it, carry):
        gs = range(HG_GROUPS)
        rows = [pl.ds(pl.multiple_of((it * HG_GROUPS + g) * grp, grp), grp) for g in gs]
        cum = []
        for g in gs:
            parts = _split3(lf_ref[rows[g], :])
            cum.append(jnp.dot(summat, parts[0], preferred_element_type=F32)
                       + jnp.dot(summat, parts[1], preferred_element_type=F32)
                       + jnp.dot(summat, parts[2], preferred_element_type=F32))
        q_dec, decay, a, vs, k_end = [], [], [], [], []
        for g in gs:
            cum_end = jnp.concatenate(
                [jnp.broadcast_to(cum[g][(c + 1) * HG_CHUNK - 1:(c + 1) * HG_CHUNK, :], (HG_CHUNK, HG_WIDTH))
                 for c in range(n_ch)], axis=0)
            q, k, v = q_ref[rows[g], :], k_ref[rows[g], :], v_ref[rows[g], :]
            q_dec.append(q * jnp.exp(cum[g]))
            k_dec = (k * jnp.exp(-cum[g])).astype(BF16)
            k_end.append(k * jnp.exp(cum_end - cum[g]))
            decay.append(jnp.exp(cum_end))
            q_decb = q_dec[g].astype(BF16)
            vs.append(v)
            a.append([lax.dot_general(q_decb[:, hs], k_dec[:, hs], nt, preferred_element_type=F32)
                      for hs in heads])
        o = [[jnp.dot(jnp.where(causal, a[g][h], 0.0).astype(BF16), vs[g][:, hs].astype(BF16),
                      preferred_element_type=F32) for h, hs in enumerate(heads)] for g in gs]
        ut = [[jnp.dot(vs[g][:, hs].T.astype(BF16), by_chunk(k_end[g][:, hs]), preferred_element_type=F32)
               for hs in heads] for g in gs]
        starts = [[[] for _ in heads] for _ in gs]
        for h, hs in enumerate(heads):
            st = st_ref[h]
            for g in gs:
                for c in range(n_ch):
                    starts[g][h].append(st.astype(BF16))
                    st = (st * decay[g][c * HG_CHUNK:c * HG_CHUNK + 1, hs]
                          + ut[g][h][:, c * LANES:(c + 1) * LANES])
            st_ref[h] = st
        inter = [[lax.dot_general(by_chunk(q_dec[g][:, hs]), jnp.concatenate(starts[g][h], axis=1), nt,
                                  preferred_element_type=F32) for h, hs in enumerate(heads)] for g in gs]
        for g in gs:
            for h, hs in enumerate(heads):
                oh = o[g][h] + inter[g][h]
                ms = jnp.mean(oh * oh, axis=-1, keepdims=True)
                gate = gh_ref[rows[g], hs]
                o_ref[rows[g], hs] = oh * lax.rsqrt(ms + LN_EPS) * ng_ref[:, hs] * (gate * _sigmoid(gate))
        return carry

    lax.fori_loop(0, rows_per_step // (grp * HG_GROUPS), groups, 0)

    @pl.when(tb == pl.num_programs(1) - 1)
    def _():
        for h in range(HG_HEADS):
            s_ref[0, h] = st_ref[h].T


HG_ROWS = 1024
HG_GROUPS = 4


def hgrn_prompt(qh, kh, ih, lf, gh, norm_g, batch, seq, cast=()):
    n_tb = seq // HG_ROWS
    steps = batch * n_tb

    def rows(b, t):
        return (b * n_tb + t, 0)

    blk = pl.BlockSpec((HG_ROWS, HG_WIDTH), rows)
    cast_specs = [pl.BlockSpec((w.shape[0] // steps, w.shape[1]), rows) for w in cast]
    return pl.pallas_call(
        functools.partial(_hgrn_prompt_body, rows_per_step=HG_ROWS, n_cast=len(cast)),
        grid=(batch, n_tb),
        in_specs=[blk, blk, blk, blk, blk, _const_spec((1, HG_WIDTH))] + cast_specs,
        out_specs=[blk, pl.BlockSpec((1, HG_HEADS, HG_DK, HG_DV), lambda b, t: (b, 0, 0, 0))] + cast_specs,
        out_shape=[jax.ShapeDtypeStruct((batch * seq, HG_WIDTH), F32),
                   jax.ShapeDtypeStruct((batch, HG_HEADS, HG_DK, HG_DV), F32)]
                  + [jax.ShapeDtypeStruct(w.shape, BF16) for w in cast],
        scratch_shapes=[pltpu.VMEM((HG_HEADS, HG_DV, HG_DK), F32)],
        compiler_params=pltpu.CompilerParams(dimension_semantics=("arbitrary", "arbitrary"),
                                             vmem_limit_bytes=VMEM_LIMIT),
        name="hgrn_prompt",
    )(qh, kh, ih, lf, gh, norm_g, *cast)


FA_ROWS = 128
FA_HEADS = 4
FA_SLOTS = 4


def _divmod_pow2(s, n):
    assert n & (n - 1) == 0
    if isinstance(s, int):
        return s // n, s % n
    return lax.shift_right_logical(s, n.bit_length() - 1), s & (n - 1)


def _ffn_attn_body(*refs, n_new, past, b_off, mixed, n_steps):
    n_x = 10 if mixed else 1
    wg_ref, wu_ref, wd_ref, g_ref, b_ref, q_ref, kn_ref, vn_ref, kc_hbm, vc_hbm = refs[n_x:n_x + 10]
    y_ref, o_ref, nk_ref, nv_ref, kbuf, vbuf, sems = refs[-8:-1] if mixed else refs[-7:]
    halves = ATT_HEADS // FA_HEADS
    step = pl.program_id(0)

    split = _divmod_pow2
    odd = (b_off + split(step, halves)[0]) & 1 == 1

    def window_copies(s):
        slot = split(s, FA_SLOTS)[1]
        b_rel, half = split(s, halves)
        src = (b_off + b_rel, pl.ds(half * FA_HEADS, FA_HEADS))
        return (pltpu.make_async_copy(kc_hbm.at[src], kbuf.at[slot], sems.at[slot, 0]),
                pltpu.make_async_copy(vc_hbm.at[src], vbuf.at[slot], sems.at[slot, 1]))

    @pl.when(step == 0)
    def _():
        for s in range(min(FA_SLOTS - 1, n_steps)):
            for c in window_copies(s):
                c.start()

    @pl.when(step + (FA_SLOTS - 1) < n_steps)
    def _():
        for c in window_copies(step + (FA_SLOTS - 1)):
            c.start()

    for c in window_copies(step):
        c.wait()
    slot = split(step, FA_SLOTS)[1]
    kc_ref, vc_ref = kbuf.at[slot], vbuf.at[slot]

    def new_rows(ref):
        both = ref[...]
        return jnp.where(odd, both[n_new:2 * n_new], both[0:n_new])

    q_new, k_new, v_new = new_rows(q_ref), new_rows(kn_ref), new_rows(vn_ref)
    head_cols = [slice(h * ATT_HEAD_DIM, (h + 1) * ATT_HEAD_DIM) for h in range(FA_HEADS)]
    srow = lax.broadcasted_iota(jnp.int32, (n_new, past), 0)
    tcol = lax.broadcasted_iota(jnp.int32, (n_new, past), 1)
    dist = past + srow - tcol
    def on_stride(d, dil):
        return (d & (dil - 1)) == 0 if dil & (dil - 1) == 0 else d % dil == 0

    cnt = jnp.zeros((n_new, past), F32)
    for window, dil in DILATIONS:
        cnt = cnt + jnp.where(on_stride(dist, dil) & (dist <= window), 1.0, 0.0)
    si = lax.broadcasted_iota(jnp.int32, (n_new, n_new), 0)
    sj = lax.broadcasted_iota(jnp.int32, (n_new, n_new), 1)
    dn = si - sj
    cnt_new = jnp.zeros((n_new, n_new), F32)
    for window, dil in DILATIONS:
        cnt_new = cnt_new + jnp.where((dn >= 0) & on_stride(dn, dil) & (dn <= window), 1.0, 0.0)
    lane = lax.broadcasted_iota(jnp.int32, (1, LANES), 1)
    nt = (((1,), (1,)), ((), ()))
    shift = past - n_new
    hs = range(FA_HEADS)

    def scores():
        qs = [q_new[:, c].astype(BF16) for c in head_cols]
        s_c = [jnp.dot(qs[h], kc_ref[h].astype(BF16), preferred_element_type=F32) for h in hs]
        s_n = [lax.dot_general(qs[h], k_new[:, head_cols[h]].astype(BF16), nt, preferred_element_type=F32)
               for h in hs]
        return s_c, s_n

    if mixed:
        first, nxt = refs[0:3], refs[3:6]
        wa_ref, wh_ref, g2_ref, b2_ref = refs[6:n_x]
        xin_ref = refs[-1]

        def mixed_rows(x1_ref, oa_ref, oh_ref):
            mix = jnp.dot(oa_ref[...].astype(BF16), wa_ref[...], preferred_element_type=F32)
            mix = mix + jnp.dot(oh_ref[...].astype(BF16), wh_ref[...], preferred_element_type=F32)
            return _layer_norm(DN_ALPHA * x1_ref[...] + mix, g2_ref[...], b2_ref[...])

        @pl.when(step == 0)
        def _():
            xin_ref[...] = mixed_rows(*first)

        x = xin_ref[...]
        x_next = mixed_rows(*nxt)
    else:
        x = refs[0][...]
    xb = x.astype(BF16)
    gate = jnp.dot(xb, wg_ref[...], preferred_element_type=F32)
    up = jnp.dot(xb, wu_ref[...], preferred_element_type=F32)
    s_c, s_n = scores()
    knt = [k_new[:, c].T for c in head_cols]
    vnt = [v_new[:, c].T for c in head_cols]

    hmid = (gate * _sigmoid(gate) * up).astype(BF16)
    y = jnp.dot(hmid, wd_ref[...], preferred_element_type=F32)
    y_ref[...] = _layer_norm(DN_ALPHA * x + 0.5 * y, g_ref[...], b_ref[...])
    if mixed:
        xin_ref[...] = x_next

    p_c, p_n, den = [], [], []
    for h in hs:
        sc = jnp.where(cnt > 0, s_c[h], NEG_INF)
        sn = jnp.where(cnt_new > 0, s_n[h], NEG_INF)
        m = jnp.maximum(jnp.max(sc, axis=-1, keepdims=True), jnp.max(sn, axis=-1, keepdims=True))
        p_c.append((cnt * jnp.exp(sc - m)).astype(BF16))
        p_n.append((cnt_new * jnp.exp(sn - m)).astype(BF16))
        den.append(jnp.sum(p_c[h].astype(F32), axis=-1, keepdims=True)
                   + jnp.sum(p_n[h].astype(F32), axis=-1, keepdims=True))
    outs = []
    for h in hs:
        num = (lax.dot_general(p_c[h], vc_ref[h].astype(BF16), nt, preferred_element_type=F32)
               + jnp.dot(p_n[h], v_new[:, head_cols[h]].astype(BF16), preferred_element_type=F32))
        outs.append(num / den[h])
    o_ref[0] = jnp.concatenate(outs, axis=1)

    for h in hs:
        for src_ref, new, dst in ((kc_ref, knt[h], nk_ref), (vc_ref, vnt[h], nv_ref)):
            rolled = pltpu.roll(src_ref[h], shift, 1)
            dst[0, h, :, 0:past - LANES] = rolled[:, 0:past - LANES]
            last = rolled[:, past - LANES:past]
            for s in range(n_new):
                col = jnp.broadcast_to(new[:, s:s + 1], last.shape)
                last = jnp.where(lane == LANES - n_new + s, col, last)
            dst[0, h, :, past - LANES:past] = last


def ffn_attn(x, wg, wu, wd, g, b, q, kn, vn, kc, vc, n_new, b_off, prev=None, mix=None):
    m = x.shape[0]
    bsz, n_heads, _, past = kc.shape
    halves = n_heads // FA_HEADS
    steps = m // FA_ROWS
    assert b_off % 2 == 0 and bsz % 2 == 0, "new-token rows are fetched per pair of batch rows"

    def where(i):
        b_rel, half = _divmod_pow2(i, halves)
        return b_off + b_rel, half

    def samp(i):
        return (*where(i), 0, 0)

    row = pl.BlockSpec((FA_ROWS, D_MODEL), lambda i: (i, 0))
    pair = pl.BlockSpec((2 * n_new, FA_HEADS * ATT_HEAD_DIM),
                        lambda i: (_divmod_pow2(where(i)[0], 2)[0], where(i)[1]))
    o_blk = pl.BlockSpec((1, n_new, FA_HEADS * ATT_HEAD_DIM), lambda i: (where(i)[0], 0, where(i)[1]))
    big = pl.BlockSpec((1, FA_HEADS, ATT_HEAD_DIM, past), samp)
    if mix is None:
        in_specs, args = [row], [x]
    else:
        o_att, o_hg, w_out, ln_g, ln_b = mix
        in_specs, args = [], []
        for index in (lambda i: (0, 0), lambda i: (jnp.minimum(i + 1, steps - 1), 0)):
            in_specs += [pl.BlockSpec((FA_ROWS, D_MODEL), index), pl.BlockSpec((FA_ROWS, ATT_WIDTH), index),
                         pl.BlockSpec((FA_ROWS, HG_WIDTH), index)]
            args += [x, o_att, o_hg]
        in_specs += _w_out_specs() + [_const_spec((1, D_MODEL)), _const_spec((1, D_MODEL))]
        args += [w_out, w_out, ln_g, ln_b]
    hbm = pl.BlockSpec(memory_space=pl.ANY)
    in_specs += [_const_spec((D_MODEL, D_FF)), _const_spec((D_MODEL, D_FF)), _const_spec((D_FF, D_MODEL)),
                 _const_spec((1, D_MODEL)), _const_spec((1, D_MODEL)), pair, pair, pair, hbm, hbm]
    args += [wg, wu, wd, g, b, q, kn, vn, kc, vc]
    aliases = {}
    if prev is not None:
        for j, a in enumerate(prev):
            aliases[len(args)] = 1 + j
            in_specs.append(pl.BlockSpec(memory_space=pl.ANY))
            args.append(a)
    big_sds = jax.ShapeDtypeStruct(kc.shape, F32)
    return pl.pallas_call(
        functools.partial(_ffn_attn_body, n_new=n_new, past=past, b_off=b_off, mixed=mix is not None,
                          n_steps=steps),
        grid=(steps,),
        in_specs=in_specs,
        out_specs=[row, o_blk, big, big],
        out_shape=[jax.ShapeDtypeStruct((m, D_MODEL), F32),
                   jax.ShapeDtypeStruct((bsz, n_new, n_heads * ATT_HEAD_DIM), F32), big_sds, big_sds],
        scratch_shapes=[pltpu.VMEM((FA_SLOTS, FA_HEADS, ATT_HEAD_DIM, past), F32),
                        pltpu.VMEM((FA_SLOTS, FA_HEADS, ATT_HEAD_DIM, past), F32),
                        pltpu.SemaphoreType.DMA((FA_SLOTS, 2))]
                       + ([pltpu.VMEM((FA_ROWS, D_MODEL), F32)] if mix is not None else []),
        input_output_aliases=aliases,
        compiler_params=pltpu.CompilerParams(dimension_semantics=("arbitrary",),
                                             vmem_limit_bytes=VMEM_LIMIT),
        name="ffn_attn",
    )(*args)


HS_BATCH = 8


def _hgrn_sample_body(q_ref, k_ref, v_ref, lf_ref, gh_ref, ng_ref, s0_ref, o_ref, s_ref, *, n_new):
    row = lax.broadcasted_iota(jnp.int32, (n_new, 1), 0)
    tn = (((0,), (0,)), ((), ()))
    ones12 = jnp.ones((3 * n_new, HG_DV), BF16)

    sub = 8
    per_tile = sub // n_new

    def tile(t, carry):
        rows = pl.ds(pl.multiple_of(t * sub, sub), sub)
        units = []
        for h in range(HG_HEADS):
            cols = slice(h * LANES, (h + 1) * LANES)
            q8, k8, v8, g8 = q_ref[rows, cols], k_ref[rows, cols], v_ref[rows, cols], lf_ref[rows, cols]
            for i in range(per_tile):
                part = slice(i * n_new, (i + 1) * n_new)
                q, k, v, g = q8[part], k8[part], v8[part], g8[part]
                cum = jnp.zeros((n_new, LANES), F32)
                for j in range(n_new):
                    cum = cum + jnp.where(row >= j, g[j:j + 1, :], 0.0)
                cum_end = cum[n_new - 1:n_new, :]
                units.append(dict(h=h, b=t * per_tile + i, v=v, g=g, q_dec=q * jnp.exp(cum),
                                  k_dec=k * jnp.exp(-cum), k_end=k * jnp.exp(cum_end - cum)))
        for u in units:
            u["s0"] = s0_ref[u["b"], u["h"]]
            u["o"] = jnp.dot(u["q_dec"].astype(BF16), u["s0"].astype(BF16), preferred_element_type=F32)
        for u in units:
            dec = lax.dot_general(jnp.concatenate(_split3(u["g"]), axis=0), ones12, tn,
                                  preferred_element_type=F32)
            upd = lax.dot_general(u["k_end"].astype(BF16), u["v"].astype(BF16), tn,
                                  preferred_element_type=F32)
            s_ref[u["b"], u["h"]] = jnp.exp(dec) * u["s0"] + upd
        for h in range(HG_HEADS):
            cols = slice(h * LANES, (h + 1) * LANES)
            outs = []
            for u in units[h * per_tile:(h + 1) * per_tile]:
                o = u["o"]
                for j in range(n_new):
                    a_j = jnp.sum(u["q_dec"] * u["k_dec"][j:j + 1, :], axis=-1, keepdims=True)
                    o = o + jnp.where(row >= j, a_j, 0.0) * u["v"][j:j + 1, :]
                outs.append(o * lax.rsqrt(jnp.mean(o * o, axis=-1, keepdims=True) + LN_EPS))
            gate = gh_ref[rows, cols]
            o_ref[rows, cols] = jnp.concatenate(outs, axis=0) * ng_ref[:, cols] * (gate * _sigmoid(gate))
        return carry

    lax.fori_loop(0, HS_BATCH // per_tile, tile, 0)


def hgrn_sample(qh, kh, ih, lf, gh, norm_g, s0, n_new):
    bsz = s0.shape[0]
    blk = pl.BlockSpec((HS_BATCH * n_new, HG_WIDTH), lambda i: (i, 0))
    sblk = pl.BlockSpec((HS_BATCH, HG_HEADS, HG_DK, HG_DV), lambda i: (i, 0, 0, 0))
    return pl.pallas_call(
        functools.partial(_hgrn_sample_body, n_new=n_new),
        grid=(bsz // HS_BATCH,),
        in_specs=[blk, blk, blk, blk, blk, _const_spec((1, HG_WIDTH)), sblk],
        out_specs=[blk, sblk],
        out_shape=[jax.ShapeDtypeStruct((bsz * n_new, HG_WIDTH), F32),
                   jax.ShapeDtypeStruct(s0.shape, F32)],
        compiler_params=pltpu.CompilerParams(dimension_semantics=("arbitrary",),
                                             vmem_limit_bytes=VMEM_LIMIT),
        name="hgrn_sample",
    )(qh, kh, ih, lf, gh, norm_g, s0)


def _rope_tables(pos):
    half = ATT_HEAD_DIM // 2
    inv = ROPE_THETA ** (-jnp.arange(half, dtype=F32) / half)
    ang = pos.astype(F32)[:, None] * inv[None, :]
    cos, sin = jnp.cos(ang), jnp.sin(ang)
    return jnp.tile(cos, (1, 4)), jnp.tile(jnp.concatenate([-sin, sin], axis=1), (1, 2))


def kernel(x_prompt, x_sample, cache_k, cache_v, state_hgrn, ffn1_w_gate, ffn1_w_up, ffn1_w_down, ln1_g, ln1_b, w_in, hg_lower_bound, hg_norm_g, w_out, ln2_g, ln2_b, ffn2_w_gate, ffn2_w_up, ffn2_w_down, ln3_g, ln3_b):
    batch, seq, _ = x_prompt.shape
    dec_batch, n_new, _ = x_sample.shape
    assert ffn1_w_gate.shape[0] == 1, "single layer"


    cos_p, sin_p = _rope_tables(jnp.arange(seq, dtype=jnp.int32))
    pos_s = PAST_LEN + (jnp.arange(dec_batch * n_new, dtype=jnp.int32) % n_new)
    cos_s, sin_s = _rope_tables(pos_s)

    xp = x_prompt.reshape(batch * seq, D_MODEL)
    xs = x_sample.reshape(dec_batch * n_new, D_MODEL)

    xs1, *w1 = ffn_ln(xs, ffn1_w_gate[0], ffn1_w_up[0], ffn1_w_down[0], ln1_g, ln1_b)
    qs, ks, vs, qhs, khs, ihs, lfs, ghs, w_in_b = proj(xs1, w_in[0], cos_s, sin_s, hg_lower_bound, 1)

    samp = (qs, ks, vs,
            cache_k[0].transpose(0, 2, 3, 1),
            cache_v[0].transpose(0, 2, 3, 1), n_new)
    steps = (batch * seq) // FA_ROWS
    per_call = steps // (ATT_HEADS // FA_HEADS)
    assert 2 * per_call == dec_batch, "the two prompt FFN calls must cover the sample batch"

    xp1, o_t, nk, nv = ffn_attn(xp, *w1, ln1_g, ln1_b, *samp, 0)
    win_p = min(ATT_WINDOW_MAX, seq)
    q, k, v, qh, kh, ih, lf, gh, kt_p, vt_p = proj(xp1, w_in_b, cos_p, sin_p, hg_lower_bound, seq // ROW_TILE,
                                                   win_tiles=win_p // ROW_TILE)
    o_att = attn_prompt(q, k, v, batch, seq)
    o_hg, s_p, *w2, w_out_b = hgrn_prompt(qh, kh, ih, lf, gh, hg_norm_g, batch, seq,
                                          cast=(ffn2_w_gate[0], ffn2_w_up[0], ffn2_w_down[0], w_out[0]))
    yp, o_t, nk, nv = ffn_attn(xp1, *w2, ln3_g, ln3_b, *samp, per_call, prev=(o_t, nk, nv),
                               mix=(o_att, o_hg, w_out_b, ln2_g, ln2_b))

    o_att_s = o_t.reshape(dec_batch * n_new, ATT_WIDTH)

    o_hg_s, s_s = hgrn_sample(qhs, khs, ihs, lfs, ghs, hg_norm_g, state_hgrn[0], n_new)
    ys, = ffn_ln(xs1, *w2, ln3_g, ln3_b, mix=(o_att_s, o_hg_s, w_out_b, ln2_g, ln2_b))

    return (yp.reshape(batch, seq, D_MODEL),
            ys.reshape(dec_batch, n_new, D_MODEL),
            kt_p.transpose(0, 3, 1, 2)[None],
            vt_p.transpose(0, 3, 1, 2)[None],
            s_p[None],
            nk.transpose(0, 3, 1, 2)[None],
            nv.transpose(0, 3, 1, 2)[None],
            s_s[None])
```

```python
import functools

import jax
import jax.numpy as jnp
from jax import lax
from jax.experimental import pallas as pl
from jax.experimental.pallas import tpu as pltpu

F32 = jnp.float32
BF16 = jnp.bfloat16

D_MODEL = 1024
ATT_WIDTH = 512
HG_WIDTH = 512
ATT_HEAD_DIM = 64
ATT_HEADS = 8
DILATIONS = ((128, 1), (512, 4), (2048, 16))
ATT_WINDOW_MAX = 2048
ATT_BLOCK = 128
ROPE_THETA = 10000.0
HG_HEADS = 4
HG_DK = 128
HG_DV = 128
HG_CHUNK = 32
D_FF = 2816
IN_COLS = 3 * ATT_WIDTH + 4 * HG_WIDTH
PAST_LEN = 8192
DN_ALPHA = 2.0 ** 0.25
LN_EPS = 1e-5
NEG_INF = -1e30

LANES = 128
ROW_TILE = 512
VMEM_LIMIT = 56 * 1024 * 1024


def _const_spec(shape, block=None):
    index = tuple(block) if block is not None else (0,) * len(shape)
    return pl.BlockSpec(shape, lambda *_: index, pipeline_mode=pl.Buffered(1))


def _layer_norm(r, g, b):
    mu = jnp.mean(r, axis=-1, keepdims=True)
    c = r - mu
    var = jnp.mean(c * c, axis=-1, keepdims=True)
    return c * lax.rsqrt(var + LN_EPS) * g + b


def _sigmoid(x):
    return 1.0 / (1.0 + jnp.exp(-x))


def _split3(x):
    hi = x.astype(BF16)
    r1 = x - hi.astype(F32)
    mid = r1.astype(BF16)
    lo = (r1 - mid.astype(F32)).astype(BF16)
    return hi, mid, lo


def _w_out_specs():
    assert ATT_WIDTH == HG_WIDTH
    return [_const_spec((ATT_WIDTH, D_MODEL), (0, 0)), _const_spec((HG_WIDTH, D_MODEL), (1, 0))]


FFN_CHUNK = 256


def _ffn_ln_body(*refs, cast, mixed):
    n_x = 7 if mixed else 1
    wg_ref, wu_ref, wd_ref, g_ref, b_ref, o_ref = refs[n_x:n_x + 6]
    xin_ref, xb_ref, acc_ref = refs[-3:]
    j = pl.program_id(0)

    @pl.when(j == 0)
    def _():
        if mixed:
            x1_ref, oa_ref, oh_ref, wa_ref, wh_ref, g2_ref, b2_ref = refs[:n_x]
            mix = jnp.dot(oa_ref[...].astype(BF16), wa_ref[...], preferred_element_type=F32)
            mix = mix + jnp.dot(oh_ref[...].astype(BF16), wh_ref[...], preferred_element_type=F32)
            x = _layer_norm(DN_ALPHA * x1_ref[...] + mix, g2_ref[...], b2_ref[...])
        else:
            x = refs[0][...]
        xin_ref[...] = x
        xb_ref[...] = x.astype(BF16)
        acc_ref[...] = jnp.zeros(acc_ref.shape, F32)

    wgb, wub, wdb = wg_ref[...], wu_ref[...], wd_ref[...]
    if cast:
        wgb, wub, wdb = wgb.astype(BF16), wub.astype(BF16), wdb.astype(BF16)
        for dst, w in zip(refs[n_x + 6:n_x + 9], (wgb, wub, wdb)):
            dst[...] = w
    xb = xb_ref[...]
    gate = jnp.dot(xb, wgb, preferred_element_type=F32)
    up = jnp.dot(xb, wub, preferred_element_type=F32)
    h = (gate * _sigmoid(gate) * up).astype(BF16)
    acc_ref[...] += jnp.dot(h, wdb, preferred_element_type=F32)

    @pl.when(j == pl.num_programs(0) - 1)
    def _():
        o_ref[...] = _layer_norm(DN_ALPHA * xin_ref[...] + 0.5 * acc_ref[...], g_ref[...], b_ref[...])


def ffn_ln(x, wg, wu, wd, g, b, mix=None):
    m = x.shape[0]
    cast = wg.dtype == F32
    col = pl.BlockSpec((D_MODEL, FFN_CHUNK), lambda j: (0, j))
    rowc = pl.BlockSpec((FFN_CHUNK, D_MODEL), lambda j: (j, 0))
    xs = pl.BlockSpec((m, D_MODEL), lambda j: (0, 0))
    in_specs, args = [xs], [x]
    if mix is not None:
        half = pl.BlockSpec((m, ATT_WIDTH), lambda j: (0, 0))
        in_specs += [half, half] + _w_out_specs() + [_const_spec((1, D_MODEL)), _const_spec((1, D_MODEL))]
        o_att, o_hg, w_out, ln_g, ln_b = mix
        args += [o_att, o_hg, w_out, w_out, ln_g, ln_b]
    in_specs += [col, col, rowc, _const_spec((1, D_MODEL)), _const_spec((1, D_MODEL))]
    args += [wg, wu, wd, g, b]
    out_specs = [xs] + ([col, col, rowc] if cast else [])
    out_shape = [jax.ShapeDtypeStruct((m, D_MODEL), F32)]
    if cast:
        out_shape += [jax.ShapeDtypeStruct(w.shape, BF16) for w in (wg, wu, wd)]
    return pl.pallas_call(
        functools.partial(_ffn_ln_body, cast=cast, mixed=mix is not None),
        grid=(D_FF // FFN_CHUNK,),
        in_specs=in_specs,
        out_specs=out_specs,
        out_shape=out_shape,
        scratch_shapes=[pltpu.VMEM((m, D_MODEL), F32), pltpu.VMEM((m, D_MODEL), BF16),
                        pltpu.VMEM((m, D_MODEL), F32)],
        compiler_params=pltpu.CompilerParams(dimension_semantics=("arbitrary",),
                                             vmem_limit_bytes=VMEM_LIMIT),
        name="ffn_ln",
    )(*args)


def _rope(z, cos_t, sin_t):
    lane = lax.broadcasted_iota(jnp.int32, (1, LANES), 1)
    first_half = (lane % ATT_HEAD_DIM) < (ATT_HEAD_DIM // 2)
    outs = []
    for c in range(z.shape[1] // LANES):
        zc = z[:, c * LANES:(c + 1) * LANES]
        partner = jnp.where(first_half, pltpu.roll(zc, LANES - 32, 1), pltpu.roll(zc, 32, 1))
        outs.append(zc * cos_t + partner * sin_t)
    return outs


def _proj_body(x_ref, w_ref, cos_ref, sin_ref, lbraw_ref,
               q_ref, k_ref, v_ref, qh_ref, kh_ref, ih_ref, lf_ref, gh_ref, *extra, emit, window):
    wb_ref = extra[0] if emit else None
    kt_ref, vt_ref = extra[-2:] if window else (None, None)
    xb = x_ref[...].astype(BF16)
    aw = ATT_WIDTH
    base = 3 * aw

    def transposed(ref, c, val):
        ref[0, 2 * c:2 * c + 2] = val.T.reshape(2, ATT_HEAD_DIM, val.shape[0])

    def cols(start, width):
        w = w_ref[:, start:start + width]
        if wb_ref is not None:
            w = w.astype(BF16)
            wb_ref[:, start:start + width] = w
        return jnp.dot(xb, w, preferred_element_type=F32)

    cos_t = cos_ref[...]
    sin_t = sin_ref[...]
    for c, val in enumerate(_rope(cols(0, aw), cos_t, sin_t)):
        q_ref[:, c * LANES:(c + 1) * LANES] = val * (ATT_HEAD_DIM ** -0.5)
    for c, val in enumerate(_rope(cols(aw, aw), cos_t, sin_t)):
        k_ref[:, c * LANES:(c + 1) * LANES] = val
        if window:
            transposed(kt_ref, c, val)
    lbr = lbraw_ref[...]
    mx = jnp.max(lbr, axis=0, keepdims=True)
    ex = jnp.exp(lbr - mx)
    lb = ex[0:1, :] / jnp.sum(ex, axis=0, keepdims=True)
    zf = cols(base + HG_WIDTH, HG_WIDTH)
    lf_ref[...] = jnp.log(lb + (1.0 - lb) * _sigmoid(zf))
    kh_ref[...] = (1.0 - lb) * _sigmoid(-zf)
    zq = cols(base, HG_WIDTH)
    qh_ref[...] = zq * _sigmoid(zq)
    v = cols(2 * aw, aw)
    v_ref[...] = v
    if window:
        for c in range(aw // LANES):
            transposed(vt_ref, c, v[:, c * LANES:(c + 1) * LANES])
    ih_ref[...] = cols(base + 2 * HG_WIDTH, HG_WIDTH)
    gh_ref[...] = cols(base + 3 * HG_WIDTH, HG_WIDTH)


def proj(x, w_in, cos_t, sin_t, lb_raw, table_tiles, win_tiles=0):
    m = x.shape[0]
    emit = w_in.dtype == F32
    assert not emit or m == ROW_TILE, "the weight copy is written once, by a single grid step"
    row = pl.BlockSpec((ROW_TILE, D_MODEL), lambda i: (i, 0))
    tab = pl.BlockSpec((ROW_TILE, LANES), lambda i: (i % table_tiles, 0))
    out = pl.BlockSpec((ROW_TILE, ATT_WIDTH), lambda i: (i, 0))
    sds = jax.ShapeDtypeStruct((m, ATT_WIDTH), F32)
    out_specs, out_shape = [out] * 8, [sds] * 8
    if emit:
        out_specs.append(pl.BlockSpec((D_MODEL, IN_COLS), lambda i: (0, 0)))
        out_shape.append(jax.ShapeDtypeStruct(w_in.shape, BF16))
    if win_tiles:
        skipped = table_tiles - win_tiles
        wblk = pl.BlockSpec((1, ATT_HEADS, ATT_HEAD_DIM, ROW_TILE),
                            lambda i: (i // table_tiles, 0, 0, jnp.maximum(i % table_tiles - skipped, 0)))
        wsds = jax.ShapeDtypeStruct((m // (table_tiles * ROW_TILE), ATT_HEADS, ATT_HEAD_DIM,
                                     win_tiles * ROW_TILE), F32)
        out_specs += [wblk, wblk]
        out_shape += [wsds, wsds]
    return pl.pallas_call(
        functools.partial(_proj_body, emit=emit, window=bool(win_tiles)),
        grid=(m // ROW_TILE,),
        in_specs=[row, _const_spec((D_MODEL, IN_COLS)), tab, tab, _const_spec((2, HG_WIDTH))],
        out_specs=out_specs,
        out_shape=out_shape,
        compiler_params=pltpu.CompilerParams(dimension_semantics=("arbitrary",),
                                             vmem_limit_bytes=VMEM_LIMIT),
        name="proj",
    )(x, w_in, cos_t, sin_t, lb_raw)


ATT_UNROLL = (8, 8, 4)
ATT_COMBINE_ROWS = 512


def _attn_prompt_body(q_ref, k_ref, v_ref, o_ref, acc_ref, m_ref, l_ref, *, seq):
    blk = ATT_BLOCK
    lane = lax.broadcasted_iota(jnp.int32, (1, LANES), 1)
    head0 = lane < ATT_HEAD_DIM
    qi = lax.broadcasted_iota(jnp.int32, (blk, 2 * blk), 0)
    kc = lax.broadcasted_iota(jnp.int32, (blk, 2 * blk), 1)
    dist = qi + blk - kc
    first_mask = (lax.broadcasted_iota(jnp.int32, (blk, blk), 1)
                  <= lax.broadcasted_iota(jnp.int32, (blk, blk), 0))
    nt = (((1,), (1,)), ((), ()))
    log2e = 1.4426950408889634

    for br, (window, dil) in enumerate(DILATIONS):
        w_sub = window // dil
        rel = (dist >= 0) & (dist <= w_sub)
        span = blk * dil
        n_blk = seq // span
        n_units = ATT_UNROLL[br]
        run = min(n_units, n_blk)
        n_res = n_units // run
        runs_per_res = n_blk // run

        def units(it, carry, br=br, dil=dil, rel=rel, span=span, run=run, n_res=n_res,
                  runs_per_res=runs_per_res, n_units=n_units):
            starts, masks, vbs, scores = [], [], [], []
            for rr in range(n_res):
                if runs_per_res == 1:
                    r, n0 = it * n_res + rr, 0
                else:
                    r, n0 = it // runs_per_res, (it % runs_per_res) * run
                base = r + n0 * span

                def kv_block(j):
                    rows = pl.ds(jnp.maximum(base + j * span, r), blk, stride=dil)
                    return k_ref[rows, :].astype(BF16), v_ref[rows, :].astype(BF16)

                prev = None if runs_per_res == 1 else kv_block(-1)
                for j in range(run):
                    cur = kv_block(j)
                    start = base + j * span
                    q = q_ref[pl.ds(start, blk, stride=dil), :] * log2e
                    if prev is None:
                        kb, vb, mask = cur[0], cur[1], first_mask
                    else:
                        kb = jnp.concatenate([prev[0], cur[0]], axis=0)
                        vb = jnp.concatenate([prev[1], cur[1]], axis=0)
                        mask = rel & (kc >= jnp.where(n0 > 0, 0, blk)) if (j == 0 and runs_per_res > 1) else rel
                    starts.append(start)
                    masks.append(mask)
                    vbs.append(vb)
                    for hsel in (head0, ~head0):
                        qh = jnp.where(hsel, q, 0.0).astype(BF16)
                        scores.append(lax.dot_general(qh, kb, nt, preferred_element_type=F32))
                    prev = cur
            probs, ls, ms = [], [], []
            for i, s in enumerate(scores):
                s = jnp.where(masks[i // 2], s, NEG_INF)
                m = jnp.max(s, axis=-1, keepdims=True)
                p = jnp.exp2(s - m)
                ls.append(jnp.sum(p, axis=-1, keepdims=True))
                ms.append(m)
                probs.append(p.astype(BF16))
            outs = [jnp.dot(p, vbs[i // 2], preferred_element_type=F32) for i, p in enumerate(probs)]
            for uu in range(n_units):
                i0, i1 = 2 * uu, 2 * uu + 1
                rows = pl.ds(starts[uu], blk, stride=dil)
                acc_ref[br, rows, :] = jnp.where(head0, outs[i0], outs[i1])
                m_ref[br, rows, :] = jnp.broadcast_to(jnp.where(head0, ms[i0], ms[i1]), (blk, LANES))
                l_ref[br, rows, :] = jnp.broadcast_to(jnp.where(head0, ls[i0], ls[i1]), (blk, LANES))
            return carry

        lax.fori_loop(0, (dil * n_blk) // n_units, units, 0)

    def combine(c, carry):
        rows = pl.ds(pl.multiple_of(c * ATT_COMBINE_ROWS, ATT_COMBINE_ROWS), ATT_COMBINE_ROWS)
        m0, m1, m2 = m_ref[0, rows, :], m_ref[1, rows, :], m_ref[2, rows, :]
        mx = jnp.maximum(jnp.maximum(m0, m1), m2)
        w0, w1, w2 = jnp.exp2(m0 - mx), jnp.exp2(m1 - mx), jnp.exp2(m2 - mx)
        num = w0 * acc_ref[0, rows, :] + w1 * acc_ref[1, rows, :] + w2 * acc_ref[2, rows, :]
        den = w0 * l_ref[0, rows, :] + w1 * l_ref[1, rows, :] + w2 * l_ref[2, rows, :]
        o_ref[rows, :] = num / den
        return carry

    lax.fori_loop(0, seq // ATT_COMBINE_ROWS, combine, 0)


def attn_prompt(q, k, v, batch, seq):
    n_hp = ATT_WIDTH // LANES
    blk = pl.BlockSpec((seq, LANES), lambda b, h: (b, h))
    return pl.pallas_call(
        functools.partial(_attn_prompt_body, seq=seq),
        grid=(batch, n_hp),
        in_specs=[blk, blk, blk],
        out_specs=blk,
        out_shape=jax.ShapeDtypeStruct((batch * seq, ATT_WIDTH), F32),
        scratch_shapes=[pltpu.VMEM((3, seq, LANES), F32)] * 3,
        compiler_params=pltpu.CompilerParams(dimension_semantics=("arbitrary", "arbitrary"),
                                             vmem_limit_bytes=VMEM_LIMIT),
        name="attn_prompt",
    )(q, k, v)


def _hgrn_prompt_body(*refs, rows_per_step, n_cast):
    q_ref, k_ref, v_ref, lf_ref, gh_ref, ng_ref = refs[:6]
    o_ref, s_ref = refs[6 + n_cast:8 + n_cast]
    st_ref = refs[-1]
    for w_ref, wb_ref in zip(refs[6:6 + n_cast], refs[8 + n_cast:8 + 2 * n_cast]):
        wb_ref[...] = w_ref[...].astype(BF16)
    grp = LANES
    n_ch = grp // HG_CHUNK
    ri = lax.broadcasted_iota(jnp.int32, (grp, grp), 0)
    ci = lax.broadcasted_iota(jnp.int32, (grp, grp), 1)
    same_chunk = (ri // HG_CHUNK) == (ci // HG_CHUNK)
    causal = same_chunk & (ci <= ri)
    summat = jnp.where(causal, 1.0, 0.0).astype(BF16)
    nt = (((1,), (1,)), ((), ()))
    tb = pl.program_id(1)

    @pl.when(tb == 0)
    def _():
        st_ref[...] = jnp.zeros(st_ref.shape, F32)

    heads = [slice(h * LANES, (h + 1) * LANES) for h in range(HG_HEADS)]
    in_chunk = [(ri // HG_CHUNK) == c for c in range(n_ch)]

    def by_chunk(x):
        return jnp.concatenate([jnp.where(m, x, 0.0).astype(BF16) for m in in_chunk], axis=1)

    def groups(it, carry):
        gs = range(HG_GROUPS)
        rows = [pl.ds(pl.multiple_of((it * HG_GROUPS + g) * grp, grp), grp) for g in gs]
        cum = []
        for g in gs:
            parts = _split3(lf_ref[rows[g], :])
            cum.append(jnp.dot(summat, parts[0], preferred_element_type=F32)
                       + jnp.dot(summat, parts[1], preferred_element_type=F32)
                       + jnp.dot(summat, parts[2], preferred_element_type=F32))
        q_dec, decay, a, vs, k_end = [], [], [], [], []
        for g in gs:
            cum_end = jnp.concatenate(
                [jnp.broadcast_to(cum[g][(c + 1) * HG_CHUNK - 1:(c + 1) * HG_CHUNK, :], (HG_CHUNK, HG_WIDTH))
                 for c in range(n_ch)], axis=0)
            q, k, v = q_ref[rows[g], :], k_ref[rows[g], :], v_ref[rows[g], :]
            q_dec.append(q * jnp.exp(cum[g]))
            k_dec = (k * jnp.exp(-cum[g])).astype(BF16)
            k_end.append(k * jnp.exp(cum_end - cum[g]))
            decay.append(jnp.exp(cum_end))
            q_decb = q_dec[g].astype(BF16)
            vs.append(v)
            a.append([lax.dot_general(q_decb[:, hs], k_dec[:, hs], nt, preferred_element_type=F32)
                      for hs in heads])
        o = [[jnp.dot(jnp.where(causal, a[g][h], 0.0).astype(BF16), vs[g][:, hs].astype(BF16),
                      preferred_element_type=F32) for h, hs in enumerate(heads)] for g in gs]
        ut = [[jnp.dot(vs[g][:, hs].T.astype(BF16), by_chunk(k_end[g][:, hs]), preferred_element_type=F32)
               for hs in heads] for g in gs]
        starts = [[[] for _ in heads] for _ in gs]
        for h, hs in enumerate(heads):
            st = st_ref[h]
            for g in gs:
                for c in range(n_ch):
                    starts[g][h].append(st.astype(BF16))
                    st = (st * decay[g][c * HG_CHUNK:c * HG_CHUNK + 1, hs]
                          + ut[g][h][:, c * LANES:(c + 1) * LANES])
            st_ref[h] = st
        inter = [[lax.dot_general(by_chunk(q_dec[g][:, hs]), jnp.concatenate(starts[g][h], axis=1), nt,
                                  preferred_element_type=F32) for h, hs in enumerate(heads)] for g in gs]
        for g in gs:
            for h, hs in enumerate(heads):
                oh = o[g][h] + inter[g][h]
                ms = jnp.mean(oh * oh, axis=-1, keepdims=True)
                gate = gh_ref[rows[g], hs]
                o_ref[rows[g], hs] = oh * lax.rsqrt(ms + LN_EPS) * ng_ref[:, hs] * (gate * _sigmoid(gate))
        return carry

    lax.fori_loop(0, rows_per_step // (grp * HG_GROUPS), groups, 0)

    @pl.when(tb == pl.num_programs(1) - 1)
    def _():
        for h in range(HG_HEADS):
            s_ref[0, h] = st_ref[h].T


HG_ROWS = 1024
HG_GROUPS = 4


def hgrn_prompt(qh, kh, ih, lf, gh, norm_g, batch, seq, cast=()):
    n_tb = seq // HG_ROWS
    steps = batch * n_tb

    def rows(b, t):
        return (b * n_tb + t, 0)

    blk = pl.BlockSpec((HG_ROWS, HG_WIDTH), rows)
    cast_specs = [pl.BlockSpec((w.shape[0] // steps, w.shape[1]), rows) for w in cast]
    return pl.pallas_call(
        functools.partial(_hgrn_prompt_body, rows_per_step=HG_ROWS, n_cast=len(cast)),
        grid=(batch, n_tb),
        in_specs=[blk, blk, blk, blk, blk, _const_spec((1, HG_WIDTH))] + cast_specs,
        out_specs=[blk, pl.BlockSpec((1, HG_HEADS, HG_DK, HG_DV), lambda b, t: (b, 0, 0, 0))] + cast_specs,
        out_shape=[jax.ShapeDtypeStruct((batch * seq, HG_WIDTH), F32),
                   jax.ShapeDtypeStruct((batch, HG_HEADS, HG_DK, HG_DV), F32)]
                  + [jax.ShapeDtypeStruct(w.shape, BF16) for w in cast],
        scratch_shapes=[pltpu.VMEM((HG_HEADS, HG_DV, HG_DK), F32)],
        compiler_params=pltpu.CompilerParams(dimension_semantics=("arbitrary", "arbitrary"),
                                             vmem_limit_bytes=VMEM_LIMIT),
        name="hgrn_prompt",
    )(qh, kh, ih, lf, gh, norm_g, *cast)


FA_ROWS = 128
FA_HEADS = 4
FA_SLOTS = 4


def _divmod_pow2(s, n):
    assert n & (n - 1) == 0
    if isinstance(s, int):
        return s // n, s % n
    return lax.shift_right_logical(s, n.bit_length() - 1), s & (n - 1)


def _ffn_attn_body(*refs, n_new, past, b_off, mixed, n_steps):
    n_x = 10 if mixed else 1
    wg_ref, wu_ref, wd_ref, g_ref, b_ref, q_ref, kn_ref, vn_ref, kc_hbm, vc_hbm = refs[n_x:n_x + 10]
    y_ref, o_ref, nk_ref, nv_ref, kbuf, vbuf, sems = refs[-8:-1] if mixed else refs[-7:]
    halves = ATT_HEADS // FA_HEADS
    step = pl.program_id(0)

    split = _divmod_pow2
    odd = (b_off + split(step, halves)[0]) & 1 == 1

    def window_copies(s):
        slot = split(s, FA_SLOTS)[1]
        b_rel, half = split(s, halves)
        src = (b_off + b_rel, pl.ds(half * FA_HEADS, FA_HEADS))
        return (pltpu.make_async_copy(kc_hbm.at[src], kbuf.at[slot], sems.at[slot, 0]),
                pltpu.make_async_copy(vc_hbm.at[src], vbuf.at[slot], sems.at[slot, 1]))

    @pl.when(step == 0)
    def _():
        for s in range(min(FA_SLOTS - 1, n_steps)):
            for c in window_copies(s):
                c.start()

    @pl.when(step + (FA_SLOTS - 1) < n_steps)
    def _():
        for c in window_copies(step + (FA_SLOTS - 1)):
            c.start()

    for c in window_copies(step):
        c.wait()
    slot = split(step, FA_SLOTS)[1]
    kc_ref, vc_ref = kbuf.at[slot], vbuf.at[slot]

    def new_rows(ref):
        both = ref[...]
        return jnp.where(odd, both[n_new:2 * n_new], both[0:n_new])

    q_new, k_new, v_new = new_rows(q_ref), new_rows(kn_ref), new_rows(vn_ref)
    head_cols = [slice(h * ATT_HEAD_DIM, (h + 1) * ATT_HEAD_DIM) for h in range(FA_HEADS)]
    srow = lax.broadcasted_iota(jnp.int32, (n_new, past), 0)
    tcol = lax.broadcasted_iota(jnp.int32, (n_new, past), 1)
    dist = past + srow - tcol
    def on_stride(d, dil):
        return (d & (dil - 1)) == 0 if dil & (dil - 1) == 0 else d % dil == 0

    cnt = jnp.zeros((n_new, past), F32)
    for window, dil in DILATIONS:
        cnt = cnt + jnp.where(on_stride(dist, dil) & (dist <= window), 1.0, 0.0)
    si = lax.broadcasted_iota(jnp.int32, (n_new, n_new), 0)
    sj = lax.broadcasted_iota(jnp.int32, (n_new, n_new), 1)
    dn = si - sj
    cnt_new = jnp.zeros((n_new, n_new), F32)
    for window, dil in DILATIONS:
        cnt_new = cnt_new + jnp.where((dn >= 0) & on_stride(dn, dil) & (dn <= window), 1.0, 0.0)
    lane = lax.broadcasted_iota(jnp.int32, (1, LANES), 1)
    nt = (((1,), (1,)), ((), ()))
    shift = past - n_new
    hs = range(FA_HEADS)

    def scores():
        qs = [q_new[:, c].astype(BF16) for c in head_cols]
        s_c = [jnp.dot(qs[h], kc_ref[h].astype(BF16), preferred_element_type=F32) for h in hs]
        s_n = [lax.dot_general(qs[h], k_new[:, head_cols[h]].astype(BF16), nt, preferred_element_type=F32)
               for h in hs]
        return s_c, s_n

    if mixed:
        first, nxt = refs[0:3], refs[3:6]
        wa_ref, wh_ref, g2_ref, b2_ref = refs[6:n_x]
        xin_ref = refs[-1]

        def mixed_rows(x1_ref, oa_ref, oh_ref):
            mix = jnp.dot(oa_ref[...].astype(BF16), wa_ref[...], preferred_element_type=F32)
            mix = mix + jnp.dot(oh_ref[...].astype(BF16), wh_ref[...], preferred_element_type=F32)
            return _layer_norm(DN_ALPHA * x1_ref[...] + mix, g2_ref[...], b2_ref[...])

        @pl.when(step == 0)
        def _():
            xin_ref[...] = mixed_rows(*first)

        x = xin_ref[...]
        x_next = mixed_rows(*nxt)
    else:
        x = refs[0][...]
    xb = x.astype(BF16)
    gate = jnp.dot(xb, wg_ref[...], preferred_element_type=F32)
    up = jnp.dot(xb, wu_ref[...], preferred_element_type=F32)
    s_c, s_n = scores()
    knt = [k_new[:, c].T for c in head_cols]
    vnt = [v_new[:, c].T for c in head_cols]

    hmid = (gate * _sigmoid(gate) * up).astype(BF16)
    y = jnp.dot(hmid, wd_ref[...], preferred_element_type=F32)
    y_ref[...] = _layer_norm(DN_ALPHA * x + 0.5 * y, g_ref[...], b_ref[...])
    if mixed:
        xin_ref[...] = x_next

    p_c, p_n, den = [], [], []
    for h in hs:
        sc = jnp.where(cnt > 0, s_c[h], NEG_INF)
        sn = jnp.where(cnt_new > 0, s_n[h], NEG_INF)
        m = jnp.maximum(jnp.max(sc, axis=-1, keepdims=True), jnp.max(sn, axis=-1, keepdims=True))
        p_c.append((cnt * jnp.exp(sc - m)).astype(BF16))
        p_n.append((cnt_new * jnp.exp(sn - m)).astype(BF16))
        den.append(jnp.sum(p_c[h].astype(F32), axis=-1, keepdims=True)
                   + jnp.sum(p_n[h].astype(F32), axis=-1, keepdims=True))
    outs = []
    for h in hs:
        num = (lax.dot_general(p_c[h], vc_ref[h].astype(BF16), nt, preferred_element_type=F32)
               + jnp.dot(p_n[h], v_new[:, head_cols[h]].astype(BF16), preferred_element_type=F32))
        outs.append(num / den[h])
    o_ref[0] = jnp.concatenate(outs, axis=1)

    for h in hs:
        for src_ref, new, dst in ((kc_ref, knt[h], nk_ref), (vc_ref, vnt[h], nv_ref)):
            rolled = pltpu.roll(src_ref[h], shift, 1)
            dst[0, h, :, 0:past - LANES] = rolled[:, 0:past - LANES]
            last = rolled[:, past - LANES:past]
            for s in range(n_new):
                col = jnp.broadcast_to(new[:, s:s + 1], last.shape)
                last = jnp.where(lane == LANES - n_new + s, col, last)
            dst[0, h, :, past - LANES:past] = last


def ffn_attn(x, wg, wu, wd, g, b, q, kn, vn, kc, vc, n_new, b_off, prev=None, mix=None):
    m = x.shape[0]
    bsz, n_heads, _, past = kc.shape
    halves = n_heads // FA_HEADS
    steps = m // FA_ROWS
    assert b_off % 2 == 0 and bsz % 2 == 0, "new-token rows are fetched per pair of batch rows"

    def where(i):
        b_rel, half = _divmod_pow2(i, halves)
        return b_off + b_rel, half

    def samp(i):
        return (*where(i), 0, 0)

    row = pl.BlockSpec((FA_ROWS, D_MODEL), lambda i: (i, 0))
    pair = pl.BlockSpec((2 * n_new, FA_HEADS * ATT_HEAD_DIM),
                        lambda i: (_divmod_pow2(where(i)[0], 2)[0], where(i)[1]))
    o_blk = pl.BlockSpec((1, n_new, FA_HEADS * ATT_HEAD_DIM), lambda i: (where(i)[0], 0, where(i)[1]))
    big = pl.BlockSpec((1, FA_HEADS, ATT_HEAD_DIM, past), samp)
    if mix is None:
        in_specs, args = [row], [x]
    else:
        o_att, o_hg, w_out, ln_g, ln_b = mix
        in_specs, args = [], []
        for index in (lambda i: (0, 0), lambda i: (jnp.minimum(i + 1, steps - 1), 0)):
            in_specs += [pl.BlockSpec((FA_ROWS, D_MODEL), index), pl.BlockSpec((FA_ROWS, ATT_WIDTH), index),
                         pl.BlockSpec((FA_ROWS, HG_WIDTH), index)]
            args += [x, o_att, o_hg]
        in_specs += _w_out_specs() + [_const_spec((1, D_MODEL)), _const_spec((1, D_MODEL))]
        args += [w_out, w_out, ln_g, ln_b]
    hbm = pl.BlockSpec(memory_space=pl.ANY)
    in_specs += [_const_spec((D_MODEL, D_FF)), _const_spec((D_MODEL, D_FF)), _const_spec((D_FF, D_MODEL)),
                 _const_spec((1, D_MODEL)), _const_spec((1, D_MODEL)), pair, pair, pair, hbm, hbm]
    args += [wg, wu, wd, g, b, q, kn, vn, kc, vc]
    aliases = {}
    if prev is not None:
        for j, a in enumerate(prev):
            aliases[len(args)] = 1 + j
            in_specs.append(pl.BlockSpec(memory_space=pl.ANY))
            args.append(a)
    big_sds = jax.ShapeDtypeStruct(kc.shape, F32)
    return pl.pallas_call(
        functools.partial(_ffn_attn_body, n_new=n_new, past=past, b_off=b_off, mixed=mix is not None,
                          n_steps=steps),
        grid=(steps,),
        in_specs=in_specs,
        out_specs=[row, o_blk, big, big],
        out_shape=[jax.ShapeDtypeStruct((m, D_MODEL), F32),
                   jax.ShapeDtypeStruct((bsz, n_new, n_heads * ATT_HEAD_DIM), F32), big_sds, big_sds],
        scratch_shapes=[pltpu.VMEM((FA_SLOTS, FA_HEADS, ATT_HEAD_DIM, past), F32),
                        pltpu.VMEM((FA_SLOTS, FA_HEADS, ATT_HEAD_DIM, past), F32),
                        pltpu.SemaphoreType.DMA((FA_SLOTS, 2))]
                       + ([pltpu.VMEM((FA_ROWS, D_MODEL), F32)] if mix is not None else []),
        input_output_aliases=aliases,
        compiler_params=pltpu.CompilerParams(dimension_semantics=("arbitrary",),
                                             vmem_limit_bytes=VMEM_LIMIT),
        name="ffn_attn",
    )(*args)


HS_BATCH = 8


def _hgrn_sample_body(q_ref, k_ref, v_ref, lf_ref, gh_ref, ng_ref, s0_ref, o_ref, s_ref, *, n_new):
    row = lax.broadcasted_iota(jnp.int32, (n_new, 1), 0)
    tn = (((0,), (0,)), ((), ()))
    ones12 = jnp.ones((3 * n_new, HG_DV), BF16)

    sub = 8
    per_tile = sub // n_new

    def tile(t, carry):
        rows = pl.ds(pl.multiple_of(t * sub, sub), sub)
        units = []
        for h in range(HG_HEADS):
            cols = slice(h * LANES, (h + 1) * LANES)
            q8, k8, v8, g8 = q_ref[rows, cols], k_ref[rows, cols], v_ref[rows, cols], lf_ref[rows, cols]
            for i in range(per_tile):
                part = slice(i * n_new, (i + 1) * n_new)
                q, k, v, g = q8[part], k8[part], v8[part], g8[part]
                cum = jnp.zeros((n_new, LANES), F32)
                for j in range(n_new):
                    cum = cum + jnp.where(row >= j, g[j:j + 1, :], 0.0)
                cum_end = cum[n_new - 1:n_new, :]
                units.append(dict(h=h, b=t * per_tile + i, v=v, g=g, q_dec=q * jnp.exp(cum),
                                  k_dec=k * jnp.exp(-cum), k_end=k * jnp.exp(cum_end - cum)))
        for u in units:
            u["s0"] = s0_ref[u["b"], u["h"]]
            u["o"] = jnp.dot(u["q_dec"].astype(BF16), u["s0"].astype(BF16), preferred_element_type=F32)
        for u in units:
            dec = lax.dot_general(jnp.concatenate(_split3(u["g"]), axis=0), ones12, tn,
                                  preferred_element_type=F32)
            upd = lax.dot_general(u["k_end"].astype(BF16), u["v"].astype(BF16), tn,
                                  preferred_element_type=F32)
            s_ref[u["b"], u["h"]] = jnp.exp(dec) * u["s0"] + upd
        for h in range(HG_HEADS):
            cols = slice(h * LANES, (h + 1) * LANES)
            outs = []
            for u in units[h * per_tile:(h + 1) * per_tile]:
                o = u["o"]
                for j in range(n_new):
                    a_j = jnp.sum(u["q_dec"] * u["k_dec"][j:j + 1, :], axis=-1, keepdims=True)
                    o = o + jnp.where(row >= j, a_j, 0.0) * u["v"][j:j + 1, :]
                outs.append(o * lax.rsqrt(jnp.mean(o * o, axis=-1, keepdims=True) + LN_EPS))
            gate = gh_ref[rows, cols]
            o_ref[rows, cols] = jnp.concatenate(outs, axis=0) * ng_ref[:, cols] * (gate * _sigmoid(gate))
        return carry

    lax.fori_loop(0, HS_BATCH // per_tile, tile, 0)


def hgrn_sample(qh, kh, ih, lf, gh, norm_g, s0, n_new):
    bsz = s0.shape[0]
    blk = pl.BlockSpec((HS_BATCH * n_new, HG_WIDTH), lambda i: (i, 0))
    sblk = pl.BlockSpec((HS_BATCH, HG_HEADS, HG_DK, HG_DV), lambda i: (i, 0, 0, 0))
    return pl.pallas_call(
        functools.partial(_hgrn_sample_body, n_new=n_new),
        grid=(bsz // HS_BATCH,),
        in_specs=[blk, blk, blk, blk, blk, _const_spec((1, HG_WIDTH)), sblk],
        out_specs=[blk, sblk],
        out_shape=[jax.ShapeDtypeStruct((bsz * n_new, HG_WIDTH), F32),
                   jax.ShapeDtypeStruct(s0.shape, F32)],
        compiler_params=pltpu.CompilerParams(dimension_semantics=("arbitrary",),
                                             vmem_limit_bytes=VMEM_LIMIT),
        name="hgrn_sample",
    )(qh, kh, ih, lf, gh, norm_g, s0)


def _rope_tables(pos):
    half = ATT_HEAD_DIM // 2
    inv = ROPE_THETA ** (-jnp.arange(half, dtype=F32) / half)
    ang = pos.astype(F32)[:, None] * inv[None, :]
    cos, sin = jnp.cos(ang), jnp.sin(ang)
    return jnp.tile(cos, (1, 4)), jnp.tile(jnp.concatenate([-sin, sin], axis=1), (1, 2))


def kernel(x_prompt, x_sample, cache_k, cache_v, state_hgrn, ffn1_w_gate, ffn1_w_up, ffn1_w_down, ln1_g, ln1_b, w_in, hg_lower_bound, hg_norm_g, w_out, ln2_g, ln2_b, ffn2_w_gate, ffn2_w_up, ffn2_w_down, ln3_g, ln3_b):
    batch, seq, _ = x_prompt.shape
    dec_batch, n_new, _ = x_sample.shape
    assert ffn1_w_gate.shape[0] == 1, "single layer"


    cos_p, sin_p = _rope_tables(jnp.arange(seq, dtype=jnp.int32))
    pos_s = PAST_LEN + (jnp.arange(dec_batch * n_new, dtype=jnp.int32) % n_new)
    cos_s, sin_s = _rope_tables(pos_s)

    xp = x_prompt.reshape(batch * seq, D_MODEL)
    xs = x_sample.reshape(dec_batch * n_new, D_MODEL)

    xs1, *w1 = ffn_ln(xs, ffn1_w_gate[0], ffn1_w_up[0], ffn1_w_down[0], ln1_g, ln1_b)
    qs, ks, vs, qhs, khs, ihs, lfs, ghs, w_in_b = proj(xs1, w_in[0], cos_s, sin_s, hg_lower_bound, 1)

    samp = (qs, ks, vs,
            cache_k[0].transpose(0, 2, 3, 1),
            cache_v[0].transpose(0, 2, 3, 1), n_new)
    steps = (batch * seq) // FA_ROWS
    per_call = steps // (ATT_HEADS // FA_HEADS)
    assert 2 * per_call == dec_batch, "the two prompt FFN calls must cover the sample batch"

    xp1, o_t, nk, nv = ffn_attn(xp, *w1, ln1_g, ln1_b, *samp, 0)
    win_p = min(ATT_WINDOW_MAX, seq)
    q, k, v, qh, kh, ih, lf, gh, kt_p, vt_p = proj(xp1, w_in_b, cos_p, sin_p, hg_lower_bound, seq // ROW_TILE,
                                                   win_tiles=win_p // ROW_TILE)
    o_att = attn_prompt(q, k, v, batch, seq)
    o_hg, s_p, *w2, w_out_b = hgrn_prompt(qh, kh, ih, lf, gh, hg_norm_g, batch, seq,
                                          cast=(ffn2_w_gate[0], ffn2_w_up[0], ffn2_w_down[0], w_out[0]))
    yp, o_t, nk, nv = ffn_attn(xp1, *w2, ln3_g, ln3_b, *samp, per_call, prev=(o_t, nk, nv),
                               mix=(o_att, o_hg, w_out_b, ln2_g, ln2_b))

    o_att_s = o_t.reshape(dec_batch * n_new, ATT_WIDTH)

    o_hg_s, s_s = hgrn_sample(qhs, khs, ihs, lfs, ghs, hg_norm_g, state_hgrn[0], n_new)
    ys, = ffn_ln(xs1, *w2, ln3_g, ln3_b, mix=(o_att_s, o_hg_s, w_out_b, ln2_g, ln2_b))

    return (yp.reshape(batch, seq, D_MODEL),
            ys.reshape(dec_batch, n_new, D_MODEL),
            kt_p.transpose(0, 3, 1, 2)[None],
            vt_p.transpose(0, 3, 1, 2)[None],
            s_p[None],
            nk.transpose(0, 3, 1, 2)[None],
            nv.transpose(0, 3, 1, 2)[None],
            s_s[None])
```

```python
import functools

import jax
import jax.numpy as jnp
from jax import lax
from jax.experimental import pallas as pl
from jax.experimental.pallas import tpu as pltpu

F32 = jnp.float32
BF16 = jnp.bfloat16

D_MODEL = 1024
ATT_WIDTH = 512
HG_WIDTH = 512
ATT_HEAD_DIM = 64
ATT_HEADS = 8
DILATIONS = ((128, 1), (512, 4), (2048, 16))
ATT_WINDOW_MAX = 2048
ATT_BLOCK = 128
ROPE_THETA = 10000.0
HG_HEADS = 4
HG_DK = 128
HG_DV = 128
HG_CHUNK = 32
D_FF = 2816
IN_COLS = 3 * ATT_WIDTH + 4 * HG_WIDTH
PAST_LEN = 8192
DN_ALPHA = 2.0 ** 0.25
LN_EPS = 1e-5
NEG_INF = -1e30

LANES = 128
ROW_TILE = 512
VMEM_LIMIT = 56 * 1024 * 1024


def _const_spec(shape, block=None):
    index = tuple(block) if block is not None else (0,) * len(shape)
    return pl.BlockSpec(shape, lambda *_: index, pipeline_mode=pl.Buffered(1))


def _layer_norm(r, g, b):
    mu = jnp.mean(r, axis=-1, keepdims=True)
    c = r - mu
    var = jnp.mean(c * c, axis=-1, keepdims=True)
    return c * lax.rsqrt(var + LN_EPS) * g + b


def _sigmoid(x):
    return 1.0 / (1.0 + jnp.exp(-x))


def _split3(x):
    hi = x.astype(BF16)
    r1 = x - hi.astype(F32)
    mid = r1.astype(BF16)
    lo = (r1 - mid.astype(F32)).astype(BF16)
    return hi, mid, lo


def _w_out_specs():
    assert ATT_WIDTH == HG_WIDTH
    return [_const_spec((ATT_WIDTH, D_MODEL), (0, 0)), _const_spec((HG_WIDTH, D_MODEL), (1, 0))]


FFN_CHUNK = 256


def _ffn_ln_body(*refs, cast, mixed):
    n_x = 7 if mixed else 1
    wg_ref, wu_ref, wd_ref, g_ref, b_ref, o_ref = refs[n_x:n_x + 6]
    xin_ref, xb_ref, acc_ref = refs[-3:]
    j = pl.program_id(0)

    @pl.when(j == 0)
    def _():
        if mixed:
            x1_ref, oa_ref, oh_ref, wa_ref, wh_ref, g2_ref, b2_ref = refs[:n_x]
            mix = jnp.dot(oa_ref[...].astype(BF16), wa_ref[...], preferred_element_type=F32)
            mix = mix + jnp.dot(oh_ref[...].astype(BF16), wh_ref[...], preferred_element_type=F32)
            x = _layer_norm(DN_ALPHA * x1_ref[...] + mix, g2_ref[...], b2_ref[...])
        else:
            x = refs[0][...]
        xin_ref[...] = x
        xb_ref[...] = x.astype(BF16)
        acc_ref[...] = jnp.zeros(acc_ref.shape, F32)

    wgb, wub, wdb = wg_ref[...], wu_ref[...], wd_ref[...]
    if cast:
        wgb, wub, wdb = wgb.astype(BF16), wub.astype(BF16), wdb.astype(BF16)
        for dst, w in zip(refs[n_x + 6:n_x + 9], (wgb, wub, wdb)):
            dst[...] = w
    xb = xb_ref[...]
    gate = jnp.dot(xb, wgb, preferred_element_type=F32)
    up = jnp.dot(xb, wub, preferred_element_type=F32)
    h = (gate * _sigmoid(gate) * up).astype(BF16)
    acc_ref[...] += jnp.dot(h, wdb, preferred_element_type=F32)

    @pl.when(j == pl.num_programs(0) - 1)
    def _():
        o_ref[...] = _layer_norm(DN_ALPHA * xin_ref[...] + 0.5 * acc_ref[...], g_ref[...], b_ref[...])


def ffn_ln(x, wg, wu, wd, g, b, mix=None):
    m = x.shape[0]
    cast = wg.dtype == F32
    col = pl.BlockSpec((D_MODEL, FFN_CHUNK), lambda j: (0, j))
    rowc = pl.BlockSpec((FFN_CHUNK, D_MODEL), lambda j: (j, 0))
    xs = pl.BlockSpec((m, D_MODEL), lambda j: (0, 0))
    in_specs, args = [xs], [x]
    if mix is not None:
        half = pl.BlockSpec((m, ATT_WIDTH), lambda j: (0, 0))
        in_specs += [half, half] + _w_out_specs() + [_const_spec((1, D_MODEL)), _const_spec((1, D_MODEL))]
        o_att, o_hg, w_out, ln_g, ln_b = mix
        args += [o_att, o_hg, w_out, w_out, ln_g, ln_b]
    in_specs += [col, col, rowc, _const_spec((1, D_MODEL)), _const_spec((1, D_MODEL))]
    args += [wg, wu, wd, g, b]
    out_specs = [xs] + ([col, col, rowc] if cast else [])
    out_shape = [jax.ShapeDtypeStruct((m, D_MODEL), F32)]
    if cast:
        out_shape += [jax.ShapeDtypeStruct(w.shape, BF16) for w in (wg, wu, wd)]
    return pl.pallas_call(
        functools.partial(_ffn_ln_body, cast=cast, mixed=mix is not None),
        grid=(D_FF // FFN_CHUNK,),
        in_specs=in_specs,
        out_specs=out_specs,
        out_shape=out_shape,
        scratch_shapes=[pltpu.VMEM((m, D_MODEL), F32), pltpu.VMEM((m, D_MODEL), BF16),
                        pltpu.VMEM((m, D_MODEL), F32)],
        compiler_params=pltpu.CompilerParams(dimension_semantics=("arbitrary",),
                                             vmem_limit_bytes=VMEM_LIMIT),
        name="ffn_ln",
    )(*args)


def _rope(z, cos_t, sin_t):
    lane = lax.broadcasted_iota(jnp.int32, (1, LANES), 1)
    first_half = (lane % ATT_HEAD_DIM) < (ATT_HEAD_DIM // 2)
    outs = []
    for c in range(z.shape[1] // LANES):
        zc = z[:, c * LANES:(c + 1) * LANES]
        partner = jnp.where(first_half, pltpu.roll(zc, LANES - 32, 1), pltpu.roll(zc, 32, 1))
        outs.append(zc * cos_t + partner * sin_t)
    return outs


def _proj_body(x_ref, w_ref, cos_ref, sin_ref, lbraw_ref,
               q_ref, k_ref, v_ref, qh_ref, kh_ref, ih_ref, lf_ref, gh_ref, *extra, emit, window):
    wb_ref = extra[0] if emit else None
    kt_ref, vt_ref = extra[-2:] if window else (None, None)
    xb = x_ref[...].astype(BF16)
    aw = ATT_WIDTH
    base = 3 * aw

    def transposed(ref, c, val):
        ref[0, 2 * c:2 * c + 2] = val.T.reshape(2, ATT_HEAD_DIM, val.shape[0])

    def cols(start, width):
        w = w_ref[:, start:start + width]
        if wb_ref is not None:
            w = w.astype(BF16)
            wb_ref[:, start:start + width] = w
        return jnp.dot(xb, w, preferred_element_type=F32)

    cos_t = cos_ref[...]
    sin_t = sin_ref[...]
    for c, val in enumerate(_rope(cols(0, aw), cos_t, sin_t)):
        q_ref[:, c * LANES:(c + 1) * LANES] = val * (ATT_HEAD_DIM ** -0.5)
    for c, val in enumerate(_rope(cols(aw, aw), cos_t, sin_t)):
        k_ref[:, c * LANES:(c + 1) * LANES] = val
        if window:
            transposed(kt_ref, c, val)
    lbr = lbraw_ref[...]
    mx = jnp.max(lbr, axis=0, keepdims=True)
    ex = jnp.exp(lbr - mx)
    lb = ex[0:1, :] / jnp.sum(ex, axis=0, keepdims=True)
    zf = cols(base + HG_WIDTH, HG_WIDTH)
    lf_ref[...] = jnp.log(lb + (1.0 - lb) * _sigmoid(zf))
    kh_ref[...] = (1.0 - lb) * _sigmoid(-zf)
    zq = cols(base, HG_WIDTH)
    qh_ref[...] = zq * _sigmoid(zq)
    v = cols(2 * aw, aw)
    v_ref[...] = v
    if window:
        for c in range(aw // LANES):
            transposed(vt_ref, c, v[:, c * LANES:(c + 1) * LANES])
    ih_ref[...] = cols(base + 2 * HG_WIDTH, HG_WIDTH)
    gh_ref[...] = cols(base + 3 * HG_WIDTH, HG_WIDTH)


def proj(x, w_in, cos_t, sin_t, lb_raw, table_tiles, win_tiles=0):
    m = x.shape[0]
    emit = w_in.dtype == F32
    assert not emit or m == ROW_TILE, "the weight copy is written once, by a single grid step"
    row = pl.BlockSpec((ROW_TILE, D_MODEL), lambda i: (i, 0))
    tab = pl.BlockSpec((ROW_TILE, LANES), lambda i: (i % table_tiles, 0))
    out = pl.BlockSpec((ROW_TILE, ATT_WIDTH), lambda i: (i, 0))
    sds = jax.ShapeDtypeStruct((m, ATT_WIDTH), F32)
    out_specs, out_shape = [out] * 8, [sds] * 8
    if emit:
        out_specs.append(pl.BlockSpec((D_MODEL, IN_COLS), lambda i: (0, 0)))
        out_shape.append(jax.ShapeDtypeStruct(w_in.shape, BF16))
    if win_tiles:
        skipped = table_tiles - win_tiles
        wblk = pl.BlockSpec((1, ATT_HEADS, ATT_HEAD_DIM, ROW_TILE),
                            lambda i: (i // table_tiles, 0, 0, jnp.maximum(i % table_tiles - skipped, 0)))
        wsds = jax.ShapeDtypeStruct((m // (table_tiles * ROW_TILE), ATT_HEADS, ATT_HEAD_DIM,
                                     win_tiles * ROW_TILE), F32)
        out_specs += [wblk, wblk]
        out_shape += [wsds, wsds]
    return pl.pallas_call(
        functools.partial(_proj_body, emit=emit, window=bool(win_tiles)),
        grid=(m // ROW_TILE,),
        in_specs=[row, _const_spec((D_MODEL, IN_COLS)), tab, tab, _const_spec((2, HG_WIDTH))],
        out_specs=out_specs,
        out_shape=out_shape,
        compiler_params=pltpu.CompilerParams(dimension_semantics=("arbitrary",),
                                             vmem_limit_bytes=VMEM_LIMIT),
        name="proj",
    )(x, w_in, cos_t, sin_t, lb_raw)


ATT_UNROLL = (8, 8, 4)
ATT_COMBINE_ROWS = 512


def _attn_prompt_body(q_ref, k_ref, v_ref, o_ref, acc_ref, m_ref, l_ref, *, seq):
    blk = ATT_BLOCK
    lane = lax.broadcasted_iota(jnp.int32, (1, LANES), 1)
    head0 = lane < ATT_HEAD_DIM
    qi = lax.broadcasted_iota(jnp.int32, (blk, 2 * blk), 0)
    kc = lax.broadcasted_iota(jnp.int32, (blk, 2 * blk), 1)
    dist = qi + blk - kc
    first_mask = (lax.broadcasted_iota(jnp.int32, (blk, blk), 1)
                  <= lax.broadcasted_iota(jnp.int32, (blk, blk), 0))
    nt = (((1,), (1,)), ((), ()))
    log2e = 1.4426950408889634

    for br, (window, dil) in enumerate(DILATIONS):
        w_sub = window // dil
        rel = (dist >= 0) & (dist <= w_sub)
        span = blk * dil
        n_blk = seq // span
        n_units = ATT_UNROLL[br]
        run = min(n_units, n_blk)
        n_res = n_units // run
        runs_per_res = n_blk // run

        def units(it, carry, br=br, dil=dil, rel=rel, span=span, run=run, n_res=n_res,
                  runs_per_res=runs_per_res, n_units=n_units):
            starts, masks, vbs, scores = [], [], [], []
            for rr in range(n_res):
                if runs_per_res == 1:
                    r, n0 = it * n_res + rr, 0
                else:
                    r, n0 = it // runs_per_res, (it % runs_per_res) * run
                base = r + n0 * span

                def kv_block(j):
                    rows = pl.ds(jnp.maximum(base + j * span, r), blk, stride=dil)
                    return k_ref[rows, :].astype(BF16), v_ref[rows, :].astype(BF16)

                prev = None if runs_per_res == 1 else kv_block(-1)
                for j in range(run):
                    cur = kv_block(j)
                    start = base + j * span
                    q = q_ref[pl.ds(start, blk, stride=dil), :] * log2e
                    if prev is None:
                        kb, vb, mask = cur[0], cur[1], first_mask
                    else:
                        kb = jnp.concatenate([prev[0], cur[0]], axis=0)
                        vb = jnp.concatenate([prev[1], cur[1]], axis=0)
                        mask = rel & (kc >= jnp.where(n0 > 0, 0, blk)) if (j == 0 and runs_per_res > 1) else rel
                    starts.append(start)
                    masks.append(mask)
                    vbs.append(vb)
                    for hsel in (head0, ~head0):
                        qh = jnp.where(hsel, q, 0.0).astype(BF16)
                        scores.append(lax.dot_general(qh, kb, nt, preferred_element_type=F32))
                    prev = cur
            probs, ls, ms = [], [], []
            for i, s in enumerate(scores):
                s = jnp.where(masks[i // 2], s, NEG_INF)
                m = jnp.max(s, axis=-1, keepdims=True)
                p = jnp.exp2(s - m)
                ls.append(jnp.sum(p, axis=-1, keepdims=True))
                ms.append(m)
                probs.append(p.astype(BF16))
            outs = [jnp.dot(p, vbs[i // 2], preferred_element_type=F32) for i, p in enumerate(probs)]
            for uu in range(n_units):
                i0, i1 = 2 * uu, 2 * uu + 1
                rows = pl.ds(starts[uu], blk, stride=dil)
                acc_ref[br, rows, :] = jnp.where(head0, outs[i0], outs[i1])
                m_ref[br, rows, :] = jnp.broadcast_to(jnp.where(head0, ms[i0], ms[i1]), (blk, LANES))
                l_ref[br, rows, :] = jnp.broadcast_to(jnp.where(head0, ls[i0], ls[i1]), (blk, LANES))
            return carry

        lax.fori_loop(0, (dil * n_blk) // n_units, units, 0)

    def combine(c, carry):
        rows = pl.ds(pl.multiple_of(c * ATT_COMBINE_ROWS, ATT_COMBINE_ROWS), ATT_COMBINE_ROWS)
        m0, m1, m2 = m_ref[0, rows, :], m_ref[1, rows, :], m_ref[2, rows, :]
        mx = jnp.maximum(jnp.maximum(m0, m1), m2)
        w0, w1, w2 = jnp.exp2(m0 - mx), jnp.exp2(m1 - mx), jnp.exp2(m2 - mx)
        num = w0 * acc_ref[0, rows, :] + w1 * acc_ref[1, rows, :] + w2 * acc_ref[2, rows, :]
        den = w0 * l_ref[0, rows, :] + w1 * l_ref[1, rows, :] + w2 * l_ref[2, rows, :]
        o_ref[rows, :] = num / den
        return carry

    lax.fori_loop(0, seq // ATT_COMBINE_ROWS, combine, 0)


def attn_prompt(q, k, v, batch, seq):
    n_hp = ATT_WIDTH // LANES
    blk = pl.BlockSpec((seq, LANES), lambda b, h: (b, h))
    return pl.pallas_call(
        functools.partial(_attn_prompt_body, seq=seq),
        grid=(batch, n_hp),
        in_specs=[blk, blk, blk],
        out_specs=blk,
        out_shape=jax.ShapeDtypeStruct((batch * seq, ATT_WIDTH), F32),
        scratch_shapes=[pltpu.VMEM((3, seq, LANES), F32)] * 3,
        compiler_params=pltpu.CompilerParams(dimension_semantics=("arbitrary", "arbitrary"),
                                             vmem_limit_bytes=VMEM_LIMIT),
        name="attn_prompt",
    )(q, k, v)


def _hgrn_prompt_body(*refs, rows_per_step, n_cast):
    q_ref, k_ref, v_ref, lf_ref, gh_ref, ng_ref = refs[:6]
    o_ref, s_ref = refs[6 + n_cast:8 + n_cast]
    st_ref = refs[-1]
    for w_ref, wb_ref in zip(refs[6:6 + n_cast], refs[8 + n_cast:8 + 2 * n_cast]):
        wb_ref[...] = w_ref[...].astype(BF16)
    grp = LANES
    n_ch = grp // HG_CHUNK
    ri = lax.broadcasted_iota(jnp.int32, (grp, grp), 0)
    ci = lax.broadcasted_iota(jnp.int32, (grp, grp), 1)
    same_chunk = (ri // HG_CHUNK) == (ci // HG_CHUNK)
    causal = same_chunk & (ci <= ri)
    summat = jnp.where(causal, 1.0, 0.0).astype(BF16)
    nt = (((1,), (1,)), ((), ()))
    tb = pl.program_id(1)

    @pl.when(tb == 0)
    def _():
        st_ref[...] = jnp.zeros(st_ref.shape, F32)

    heads = [slice(h * LANES, (h + 1) * LANES) for h in range(HG_HEADS)]
    in_chunk = [(ri // HG_CHUNK) == c for c in range(n_ch)]

    def by_chunk(x):
        return jnp.concatenate([jnp.where(m, x, 0.0).astype(BF16) for m in in_chunk], axis=1)

    def groups(it, carry):
        gs = range(HG_GROUPS)
        rows = [pl.ds(pl.multiple_of((it * HG_GROUPS + g) * grp, grp), grp) for g in gs]
        cum = []
        for g in gs:
            parts = _split3(lf_ref[rows[g], :])
            cum.append(jnp.dot(summat, parts[0], preferred_element_type=F32)
                       + jnp.dot(summat, parts[1], preferred_element_type=F32)
                       + jnp.dot(summat, parts[2], preferred_element_type=F32))
        q_dec, decay, a, vs, k_end = [], [], [], [], []
        for g in gs:
            cum_end = jnp.concatenate(
                [jnp.broadcast_to(cum[g][(c + 1) * HG_CHUNK - 1:(c + 1) * HG_CHUNK, :], (HG_CHUNK, HG_WIDTH))
                 for c in range(n_ch)], axis=0)
            q, k, v = q_ref[rows[g], :], k_ref[rows[g], :], v_ref[rows[g], :]
            q_dec.append(q * jnp.exp(cum[g]))
            k_dec = (k * jnp.exp(-cum[g])).astype(BF16)
            k_end.append(k * jnp.exp(cum_end - cum[g]))
            decay.append(jnp.exp(cum_end))
            q_decb = q_dec[g].astype(BF16)
            vs.append(v)
            a.append([lax.dot_general(q_decb[:, hs], k_dec[:, hs], nt, preferred_element_type=F32)
                      for hs in heads])
        o = [[jnp.dot(jnp.where(causal, a[g][h], 0.0).astype(BF16), vs[g][:, hs].astype(BF16),
                      preferred_element_type=F32) for h, hs in enumerate(heads)] for g in gs]
        ut = [[jnp.dot(vs[g][:, hs].T.astype(BF16), by_chunk(k_end[g][:, hs]), preferred_element_type=F32)
               for hs in heads] for g in gs]
        starts = [[[] for _ in heads] for _ in gs]
        for h, hs in enumerate(heads):
            st = st_ref[h]
            for g in gs:
                for c in range(n_ch):
                    starts[g][h].append(st.astype(BF16))
                    st = (st * decay[g][c * HG_CHUNK:c * HG_CHUNK + 1, hs]
                          + ut[g][h][:, c * LANES:(c + 1) * LANES])
            st_ref[h] = st
        inter = [[lax.dot_general(by_chunk(q_dec[g][:, hs]), jnp.concatenate(starts[g][h], axis=1), nt,
                                  preferred_element_type=F32) for h, hs in enumerate(heads)] for g in gs]
        for g in gs:
            for h, hs in enumerate(heads):
                oh = o[g][h] + inter[g][h]
                ms = jnp.mean(oh * oh, axis=-1, keepdims=True)
                gate = gh_ref[rows[g], hs]
                o_ref[rows[g], hs] = oh * lax.rsqrt(ms + LN_EPS) * ng_ref[:, hs] * (gate * _sigmoid(gate))
        return carry

    lax.fori_loop(0, rows_per_step // (grp * HG_GROUPS), groups, 0)

    @pl.when(tb == pl.num_programs(1) - 1)
    def _():
        for h in range(HG_HEADS):
            s_ref[0, h] = st_ref[h].T


HG_ROWS = 1024
HG_GROUPS = 4


def hgrn_prompt(qh, kh, ih, lf, gh, norm_g, batch, seq, cast=()):
    n_tb = seq // HG_ROWS
    steps = batch * n_tb

    def rows(b, t):
        return (b * n_tb + t, 0)

    blk = pl.BlockSpec((HG_ROWS, HG_WIDTH), rows)
    cast_specs = [pl.BlockSpec((w.shape[0] // steps, w.shape[1]), rows) for w in cast]
    return pl.pallas_call(
        functools.partial(_hgrn_prompt_body, rows_per_step=HG_ROWS, n_cast=len(cast)),
        grid=(batch, n_tb),
        in_specs=[blk, blk, blk, blk, blk, _const_spec((1, HG_WIDTH))] + cast_specs,
        out_specs=[blk, pl.BlockSpec((1, HG_HEADS, HG_DK, HG_DV), lambda b, t: (b, 0, 0, 0))] + cast_specs,
        out_shape=[jax.ShapeDtypeStruct((batch * seq, HG_WIDTH), F32),
                   jax.ShapeDtypeStruct((batch, HG_HEADS, HG_DK, HG_DV), F32)]
                  + [jax.ShapeDtypeStruct(w.shape, BF16) for w in cast],
        scratch_shapes=[pltpu.VMEM((HG_HEADS, HG_DV, HG_DK), F32)],
        compiler_params=pltpu.CompilerParams(dimension_semantics=("arbitrary", "arbitrary"),
                                             vmem_limit_bytes=VMEM_LIMIT),
        name="hgrn_prompt",
    )(qh, kh, ih, lf, gh, norm_g, *cast)


FA_ROWS = 128
FA_HEADS = 4
FA_SLOTS = 4


def _divmod_pow2(s, n):
    assert n & (n - 1) == 0
    if isinstance(s, int):
        return s // n, s % n
    return lax.shift_right_logical(s, n.bit_length() - 1), s & (n - 1)


def _ffn_attn_body(*refs, n_new, past, b_off, mixed, n_steps):
    n_x = 10 if mixed else 1
    wg_ref, wu_ref, wd_ref, g_ref, b_ref, q_ref, kn_ref, vn_ref, kc_hbm, vc_hbm = refs[n_x:n_x + 10]
    y_ref, o_ref, nk_ref, nv_ref, kbuf, vbuf, sems = refs[-8:-1] if mixed else refs[-7:]
    halves = ATT_HEADS // FA_HEADS
    step = pl.program_id(0)

    split = _divmod_pow2
    odd = (b_off + split(step, halves)[0]) & 1 == 1

    def window_copies(s):
        slot = split(s, FA_SLOTS)[1]
        b_rel, half = split(s, halves)
        src = (b_off + b_rel, pl.ds(half * FA_HEADS, FA_HEADS))
        return (pltpu.make_async_copy(kc_hbm.at[src], kbuf.at[slot], sems.at[slot, 0]),
                pltpu.make_async_copy(vc_hbm.at[src], vbuf.at[slot], sems.at[slot, 1]))

    @pl.when(step == 0)
    def _():
        for s in range(min(FA_SLOTS - 1, n_steps)):
            for c in window_copies(s):
                c.start()

    @pl.when(step + (FA_SLOTS - 1) < n_steps)
    def _():
        for c in window_copies(step + (FA_SLOTS - 1)):
            c.start()

    for c in window_copies(step):
        c.wait()
    slot = split(step, FA_SLOTS)[1]
    kc_ref, vc_ref = kbuf.at[slot], vbuf.at[slot]

    def new_rows(ref):
        both = ref[...]
        return jnp.where(odd, both[n_new:2 * n_new], both[0:n_new])

    q_new, k_new, v_new = new_rows(q_ref), new_rows(kn_ref), new_rows(vn_ref)
    head_cols = [slice(h * ATT_HEAD_DIM, (h + 1) * ATT_HEAD_DIM) for h in range(FA_HEADS)]
    srow = lax.broadcasted_iota(jnp.int32, (n_new, past), 0)
    tcol = lax.broadcasted_iota(jnp.int32, (n_new, past), 1)
    dist = past + srow - tcol
    def on_stride(d, dil):
        return (d & (dil - 1)) == 0 if dil & (dil - 1) == 0 else d % dil == 0

    cnt = jnp.zeros((n_new, past), F32)
    for window, dil in DILATIONS:
        cnt = cnt + jnp.where(on_stride(dist, dil) & (dist <= window), 1.0, 0.0)
    si = lax.broadcasted_iota(jnp.int32, (n_new, n_new), 0)
    sj = lax.broadcasted_iota(jnp.int32, (n_new, n_new), 1)
    dn = si - sj
    cnt_new = jnp.zeros((n_new, n_new), F32)
    for window, dil in DILATIONS:
        cnt_new = cnt_new + jnp.where((dn >= 0) & on_stride(dn, dil) & (dn <= window), 1.0, 0.0)
    lane = lax.broadcasted_iota(jnp.int32, (1, LANES), 1)
    nt = (((1,), (1,)), ((), ()))
    shift = past - n_new
    hs = range(FA_HEADS)

    def scores():
        qs = [q_new[:, c].astype(BF16) for c in head_cols]
        s_c = [jnp.dot(qs[h], kc_ref[h].astype(BF16), preferred_element_type=F32) for h in hs]
        s_n = [lax.dot_general(qs[h], k_new[:, head_cols[h]].astype(BF16), nt, preferred_element_type=F32)
               for h in hs]
        return s_c, s_n

    if mixed:
        first, nxt = refs[0:3], refs[3:6]
        wa_ref, wh_ref, g2_ref, b2_ref = refs[6:n_x]
        xin_ref = refs[-1]

        def mixed_rows(x1_ref, oa_ref, oh_ref):
            mix = jnp.dot(oa_ref[...].astype(BF16), wa_ref[...], preferred_element_type=F32)
            mix = mix + jnp.dot(oh_ref[...].astype(BF16), wh_ref[...], preferred_element_type=F32)
            return _layer_norm(DN_ALPHA * x1_ref[...] + mix, g2_ref[...], b2_ref[...])

        @pl.when(step == 0)
        def _():
            xin_ref[...] = mixed_rows(*first)

        x = xin_ref[...]
        x_next = mixed_rows(*nxt)
    else:
        x = refs[0][...]
    xb = x.astype(BF16)
    gate = jnp.dot(xb, wg_ref[...], preferred_element_type=F32)
    up = jnp.dot(xb, wu_ref[...], preferred_element_type=F32)
    s_c, s_n = scores()
    knt = [k_new[:, c].T for c in head_cols]
    vnt = [v_new[:, c].T for c in head_cols]

    hmid = (gate * _sigmoid(gate) * up).astype(BF16)
    y = jnp.dot(hmid, wd_ref[...], preferred_element_type=F32)
    y_ref[...] = _layer_norm(DN_ALPHA * x + 0.5 * y, g_ref[...], b_ref[...])
    if mixed:
        xin_ref[...] = x_next

    p_c, p_n, den = [], [], []
    for h in hs:
        sc = jnp.where(cnt > 0, s_c[h], NEG_INF)
        sn = jnp.where(cnt_new > 0, s_n[h], NEG_INF)
        m = jnp.maximum(jnp.max(sc, axis=-1, keepdims=True), jnp.max(sn, axis=-1, keepdims=True))
        p_c.append((cnt * jnp.exp(sc - m)).astype(BF16))
        p_n.append((cnt_new * jnp.exp(sn - m)).astype(BF16))
        den.append(jnp.sum(p_c[h].astype(F32), axis=-1, keepdims=True)
                   + jnp.sum(p_n[h].astype(F32), axis=-1, keepdims=True))
    outs = []
    for h in hs:
        num = (lax.dot_general(p_c[h], vc_ref[h].astype(BF16), nt, preferred_element_type=F32)
               + jnp.dot(p_n[h], v_new[:, head_cols[h]].astype(BF16), preferred_element_type=F32))
        outs.append(num / den[h])
    o_ref[0] = jnp.concatenate(outs, axis=1)

    for h in hs:
        for src_ref, new, dst in ((kc_ref, knt[h], nk_ref), (vc_ref, vnt[h], nv_ref)):
            rolled = pltpu.roll(src_ref[h], shift, 1)
            dst[0, h, :, 0:past - LANES] = rolled[:, 0:past - LANES]
            last = rolled[:, past - LANES:past]
            for s in range(n_new):
                col = jnp.broadcast_to(new[:, s:s + 1], last.shape)
                last = jnp.where(lane == LANES - n_new + s, col, last)
            dst[0, h, :, past - LANES:past] = last


def ffn_attn(x, wg, wu, wd, g, b, q, kn, vn, kc, vc, n_new, b_off, prev=None, mix=None):
    m = x.shape[0]
    bsz, n_heads, _, past = kc.shape
    halves = n_heads // FA_HEADS
    steps = m // FA_ROWS
    assert b_off % 2 == 0 and bsz % 2 == 0, "new-token rows are fetched per pair of batch rows"

    def where(i):
        b_rel, half = _divmod_pow2(i, halves)
        return b_off + b_rel, half

    def samp(i):
        return (*where(i), 0, 0)

    row = pl.BlockSpec((FA_ROWS, D_MODEL), lambda i: (i, 0))
    pair = pl.BlockSpec((2 * n_new, FA_HEADS * ATT_HEAD_DIM),
                        lambda i: (_divmod_pow2(where(i)[0], 2)[0], where(i)[1]))
    o_blk = pl.BlockSpec((1, n_new, FA_HEADS * ATT_HEAD_DIM), lambda i: (where(i)[0], 0, where(i)[1]))
    big = pl.BlockSpec((1, FA_HEADS, ATT_HEAD_DIM, past), samp)
    if mix is None:
        in_specs, args = [row], [x]
    else:
        o_att, o_hg, w_out, ln_g, ln_b = mix
        in_specs, args = [], []
        for index in (lambda i: (0, 0), lambda i: (jnp.minimum(i + 1, steps - 1), 0)):
            in_specs += [pl.BlockSpec((FA_ROWS, D_MODEL), index), pl.BlockSpec((FA_ROWS, ATT_WIDTH), index),
                         pl.BlockSpec((FA_ROWS, HG_WIDTH), index)]
            args += [x, o_att, o_hg]
        in_specs += _w_out_specs() + [_const_spec((1, D_MODEL)), _const_spec((1, D_MODEL))]
        args += [w_out, w_out, ln_g, ln_b]
    hbm = pl.BlockSpec(memory_space=pl.ANY)
    in_specs += [_const_spec((D_MODEL, D_FF)), _const_spec((D_MODEL, D_FF)), _const_spec((D_FF, D_MODEL)),
                 _const_spec((1, D_MODEL)), _const_spec((1, D_MODEL)), pair, pair, pair, hbm, hbm]
    args += [wg, wu, wd, g, b, q, kn, vn, kc, vc]
    aliases = {}
    if prev is not None:
        for j, a in enumerate(prev):
            aliases[len(args)] = 1 + j
            in_specs.append(pl.BlockSpec(memory_space=pl.ANY))
            args.append(a)
    big_sds = jax.ShapeDtypeStruct(kc.shape, F32)
    return pl.pallas_call(
        functools.partial(_ffn_attn_body, n_new=n_new, past=past, b_off=b_off, mixed=mix is not None,
                          n_steps=steps),
        grid=(steps,),
        in_specs=in_specs,
        out_specs=[row, o_blk, big, big],
        out_shape=[jax.ShapeDtypeStruct((m, D_MODEL), F32),
                   jax.ShapeDtypeStruct((bsz, n_new, n_heads * ATT_HEAD_DIM), F32), big_sds, big_sds],
        scratch_shapes=[pltpu.VMEM((FA_SLOTS, FA_HEADS, ATT_HEAD_DIM, past), F32),
                        pltpu.VMEM((FA_SLOTS, FA_HEADS, ATT_HEAD_DIM, past), F32),
                        pltpu.SemaphoreType.DMA((FA_SLOTS, 2))]
                       + ([pltpu.VMEM((FA_ROWS, D_MODEL), F32)] if mix is not None else []),
        input_output_aliases=aliases,
        compiler_params=pltpu.CompilerParams(dimension_semantics=("arbitrary",),
                                             vmem_limit_bytes=VMEM_LIMIT),
        name="ffn_attn",
    )(*args)


HS_BATCH = 16


def _hgrn_sample_body(q_ref, k_ref, v_ref, lf_ref, gh_ref, ng_ref, s0_ref, o_ref, s_ref, *, n_new):
    row = lax.broadcasted_iota(jnp.int32, (n_new, 1), 0)
    tn = (((0,), (0,)), ((), ()))
    ones12 = jnp.ones((3 * n_new, HG_DV), BF16)

    sub = 8
    per_tile = sub // n_new

    def tile(t, carry):
        rows = pl.ds(pl.multiple_of(t * sub, sub), sub)
        units = []
        for h in range(HG_HEADS):
            cols = slice(h * LANES, (h + 1) * LANES)
            q8, k8, v8, g8 = q_ref[rows, cols], k_ref[rows, cols], v_ref[rows, cols], lf_ref[rows, cols]
            for i in range(per_tile):
                part = slice(i * n_new, (i + 1) * n_new)
                q, k, v, g = q8[part], k8[part], v8[part], g8[part]
                cum = jnp.zeros((n_new, LANES), F32)
                for j in range(n_new):
                    cum = cum + jnp.where(row >= j, g[j:j + 1, :], 0.0)
                cum_end = cum[n_new - 1:n_new, :]
                units.append(dict(h=h, b=t * per_tile + i, v=v, g=g, q_dec=q * jnp.exp(cum),
                                  k_dec=k * jnp.exp(-cum), k_end=k * jnp.exp(cum_end - cum)))
        for u in units:
            u["s0"] = s0_ref[u["b"], u["h"]]
            u["o"] = jnp.dot(u["q_dec"].astype(BF16), u["s0"].astype(BF16), preferred_element_type=F32)
        for u in units:
            dec = lax.dot_general(jnp.concatenate(_split3(u["g"]), axis=0), ones12, tn,
                                  preferred_element_type=F32)
            upd = lax.dot_general(u["k_end"].astype(BF16), u["v"].astype(BF16), tn,
                                  preferred_element_type=F32)
            s_ref[u["b"], u["h"]] = jnp.exp(dec) * u["s0"] + upd
        for h in range(HG_HEADS):
            cols = slice(h * LANES, (h + 1) * LANES)
            outs = []
            for u in units[h * per_tile:(h + 1) * per_tile]:
                o = u["o"]
                for j in range(n_new):
                    a_j = jnp.sum(u["q_dec"] * u["k_dec"][j:j + 1, :], axis=-1, keepdims=True)
                    o = o + jnp.where(row >= j, a_j, 0.0) * u["v"][j:j + 1, :]
                outs.append(o * lax.rsqrt(jnp.mean(o * o, axis=-1, keepdims=True) + LN_EPS))
            gate = gh_ref[rows, cols]
            o_ref[rows, cols] = jnp.concatenate(outs, axis=0) * ng_ref[:, cols] * (gate * _sigmoid(gate))
        return carry

    lax.fori_loop(0, HS_BATCH // per_tile, tile, 0)


def hgrn_sample(qh, kh, ih, lf, gh, norm_g, s0, n_new):
    bsz = s0.shape[0]
    assert bsz % HS_BATCH == 0 and 8 % n_new == 0, "whole grid steps of whole 8-row tiles"
    blk = pl.BlockSpec((HS_BATCH * n_new, HG_WIDTH), lambda i: (i, 0))
    sblk = pl.BlockSpec((HS_BATCH, HG_HEADS, HG_DK, HG_DV), lambda i: (i, 0, 0, 0))
    return pl.pallas_call(
        functools.partial(_hgrn_sample_body, n_new=n_new),
        grid=(bsz // HS_BATCH,),
        in_specs=[blk, blk, blk, blk, blk, _const_spec((1, HG_WIDTH)), sblk],
        out_specs=[blk, sblk],
        out_shape=[jax.ShapeDtypeStruct((bsz * n_new, HG_WIDTH), F32),
                   jax.ShapeDtypeStruct(s0.shape, F32)],
        compiler_params=pltpu.CompilerParams(dimension_semantics=("arbitrary",),
                                             vmem_limit_bytes=VMEM_LIMIT),
        name="hgrn_sample",
    )(qh, kh, ih, lf, gh, norm_g, s0)


def _rope_tables(pos):
    half = ATT_HEAD_DIM // 2
    inv = ROPE_THETA ** (-jnp.arange(half, dtype=F32) / half)
    ang = pos.astype(F32)[:, None] * inv[None, :]
    cos, sin = jnp.cos(ang), jnp.sin(ang)
    return jnp.tile(cos, (1, 4)), jnp.tile(jnp.concatenate([-sin, sin], axis=1), (1, 2))


def kernel(x_prompt, x_sample, cache_k, cache_v, state_hgrn, ffn1_w_gate, ffn1_w_up, ffn1_w_down, ln1_g, ln1_b, w_in, hg_lower_bound, hg_norm_g, w_out, ln2_g, ln2_b, ffn2_w_gate, ffn2_w_up, ffn2_w_down, ln3_g, ln3_b):
    batch, seq, _ = x_prompt.shape
    dec_batch, n_new, _ = x_sample.shape
    assert ffn1_w_gate.shape[0] == 1, "single layer"
    assert seq % (ATT_BLOCK * max(d for _, d in DILATIONS)) == 0 and seq % HG_ROWS == 0 and seq % ROW_TILE == 0
    assert dec_batch * n_new == ROW_TILE, "the sample rows form one row tile"


    cos_p, sin_p = _rope_tables(jnp.arange(seq, dtype=jnp.int32))
    pos_s = PAST_LEN + (jnp.arange(dec_batch * n_new, dtype=jnp.int32) % n_new)
    cos_s, sin_s = _rope_tables(pos_s)

    xp = x_prompt.reshape(batch * seq, D_MODEL)
    xs = x_sample.reshape(dec_batch * n_new, D_MODEL)

    xs1, *w1 = ffn_ln(xs, ffn1_w_gate[0], ffn1_w_up[0], ffn1_w_down[0], ln1_g, ln1_b)
    qs, ks, vs, qhs, khs, ihs, lfs, ghs, w_in_b = proj(xs1, w_in[0], cos_s, sin_s, hg_lower_bound, 1)

    samp = (qs, ks, vs,
            cache_k[0].transpose(0, 2, 3, 1),
            cache_v[0].transpose(0, 2, 3, 1), n_new)
    steps = (batch * seq) // FA_ROWS
    per_call = steps // (ATT_HEADS // FA_HEADS)
    assert 2 * per_call == dec_batch, "the two prompt FFN calls must cover the sample batch"

    xp1, o_t, nk, nv = ffn_attn(xp, *w1, ln1_g, ln1_b, *samp, 0)
    win_p = min(ATT_WINDOW_MAX, seq)
    q, k, v, qh, kh, ih, lf, gh, kt_p, vt_p = proj(xp1, w_in_b, cos_p, sin_p, hg_lower_bound, seq // ROW_TILE,
                                                   win_tiles=win_p // ROW_TILE)
    o_att = attn_prompt(q, k, v, batch, seq)
    o_hg, s_p, *w2, w_out_b = hgrn_prompt(qh, kh, ih, lf, gh, hg_norm_g, batch, seq,
                                          cast=(ffn2_w_gate[0], ffn2_w_up[0], ffn2_w_down[0], w_out[0]))
    yp, o_t, nk, nv = ffn_attn(xp1, *w2, ln3_g, ln3_b, *samp, per_call, prev=(o_t, nk, nv),
                               mix=(o_att, o_hg, w_out_b, ln2_g, ln2_b))

    o_att_s = o_t.reshape(dec_batch * n_new, ATT_WIDTH)

    o_hg_s, s_s = hgrn_sample(qhs, khs, ihs, lfs, ghs, hg_norm_g, state_hgrn[0], n_new)
    ys, = ffn_ln(xs1, *w2, ln3_g, ln3_b, mix=(o_att_s, o_hg_s, w_out_b, ln2_g, ln2_b))

    return (yp.reshape(batch, seq, D_MODEL),
            ys.reshape(dec_batch, n_new, D_MODEL),
            kt_p.transpose(0, 3, 1, 2)[None],
            vt_p.transpose(0, 3, 1, 2)[None],
            s_p[None],
            nk.transpose(0, 3, 1, 2)[None],
            nv.transpose(0, 3, 1, 2)[None],
            s_s[None])
```

```python
import functools

import jax
import jax.numpy as jnp
from jax import lax
from jax.experimental import pallas as pl
from jax.experimental.pallas import tpu as pltpu

F32 = jnp.float32
BF16 = jnp.bfloat16

D_MODEL = 1024
ATT_WIDTH = 512
HG_WIDTH = 512
ATT_HEAD_DIM = 64
ATT_HEADS = 8
DILATIONS = ((128, 1), (512, 4), (2048, 16))
ATT_WINDOW_MAX = 2048
ATT_BLOCK = 128
ROPE_THETA = 10000.0
HG_HEADS = 4
HG_DK = 128
HG_DV = 128
HG_CHUNK = 32
D_FF = 2816
IN_COLS = 3 * ATT_WIDTH + 4 * HG_WIDTH
PAST_LEN = 8192
DN_ALPHA = 2.0 ** 0.25
LN_EPS = 1e-5
NEG_INF = -1e30

LANES = 128
ROW_TILE = 512
VMEM_LIMIT = 56 * 1024 * 1024


def _const_spec(shape, block=None):
    index = tuple(block) if block is not None else (0,) * len(shape)
    return pl.BlockSpec(shape, lambda *_: index, pipeline_mode=pl.Buffered(1))


def _layer_norm(r, g, b):
    mu = jnp.mean(r, axis=-1, keepdims=True)
    c = r - mu
    var = jnp.mean(c * c, axis=-1, keepdims=True)
    return c * lax.rsqrt(var + LN_EPS) * g + b


def _sigmoid(x):
    return 1.0 / (1.0 + jnp.exp(-x))


def _split3(x):
    hi = x.astype(BF16)
    r1 = x - hi.astype(F32)
    mid = r1.astype(BF16)
    lo = (r1 - mid.astype(F32)).astype(BF16)
    return hi, mid, lo


def _w_out_specs():
    assert ATT_WIDTH == HG_WIDTH
    return [_const_spec((ATT_WIDTH, D_MODEL), (0, 0)), _const_spec((HG_WIDTH, D_MODEL), (1, 0))]


FFN_CHUNK = 256


def _ffn_ln_body(*refs, cast, mixed):
    n_x = 7 if mixed else 1
    wg_ref, wu_ref, wd_ref, g_ref, b_ref, o_ref = refs[n_x:n_x + 6]
    xin_ref, xb_ref, acc_ref = refs[-3:]
    j = pl.program_id(0)

    @pl.when(j == 0)
    def _():
        if mixed:
            x1_ref, oa_ref, oh_ref, wa_ref, wh_ref, g2_ref, b2_ref = refs[:n_x]
            mix = jnp.dot(oa_ref[...].astype(BF16), wa_ref[...], preferred_element_type=F32)
            mix = mix + jnp.dot(oh_ref[...].astype(BF16), wh_ref[...], preferred_element_type=F32)
            x = _layer_norm(DN_ALPHA * x1_ref[...] + mix, g2_ref[...], b2_ref[...])
        else:
            x = refs[0][...]
        xin_ref[...] = x
        xb_ref[...] = x.astype(BF16)
        acc_ref[...] = jnp.zeros(acc_ref.shape, F32)

    wgb, wub, wdb = wg_ref[...], wu_ref[...], wd_ref[...]
    if cast:
        wgb, wub, wdb = wgb.astype(BF16), wub.astype(BF16), wdb.astype(BF16)
        for dst, w in zip(refs[n_x + 6:n_x + 9], (wgb, wub, wdb)):
            dst[...] = w
    xb = xb_ref[...]
    gate = jnp.dot(xb, wgb, preferred_element_type=F32)
    up = jnp.dot(xb, wub, preferred_element_type=F32)
    h = (gate * _sigmoid(gate) * up).astype(BF16)
    acc_ref[...] += jnp.dot(h, wdb, preferred_element_type=F32)

    @pl.when(j == pl.num_programs(0) - 1)
    def _():
        o_ref[...] = _layer_norm(DN_ALPHA * xin_ref[...] + 0.5 * acc_ref[...], g_ref[...], b_ref[...])


def ffn_ln(x, wg, wu, wd, g, b, mix=None):
    m = x.shape[0]
    cast = wg.dtype == F32
    col = pl.BlockSpec((D_MODEL, FFN_CHUNK), lambda j: (0, j))
    rowc = pl.BlockSpec((FFN_CHUNK, D_MODEL), lambda j: (j, 0))
    xs = pl.BlockSpec((m, D_MODEL), lambda j: (0, 0))
    in_specs, args = [xs], [x]
    if mix is not None:
        half = pl.BlockSpec((m, ATT_WIDTH), lambda j: (0, 0))
        in_specs += [half, half] + _w_out_specs() + [_const_spec((1, D_MODEL)), _const_spec((1, D_MODEL))]
        o_att, o_hg, w_out, ln_g, ln_b = mix
        args += [o_att, o_hg, w_out, w_out, ln_g, ln_b]
    in_specs += [col, col, rowc, _const_spec((1, D_MODEL)), _const_spec((1, D_MODEL))]
    args += [wg, wu, wd, g, b]
    out_specs = [xs] + ([col, col, rowc] if cast else [])
    out_shape = [jax.ShapeDtypeStruct((m, D_MODEL), F32)]
    if cast:
        out_shape += [jax.ShapeDtypeStruct(w.shape, BF16) for w in (wg, wu, wd)]
    return pl.pallas_call(
        functools.partial(_ffn_ln_body, cast=cast, mixed=mix is not None),
        grid=(D_FF // FFN_CHUNK,),
        in_specs=in_specs,
        out_specs=out_specs,
        out_shape=out_shape,
        scratch_shapes=[pltpu.VMEM((m, D_MODEL), F32), pltpu.VMEM((m, D_MODEL), BF16),
                        pltpu.VMEM((m, D_MODEL), F32)],
        compiler_params=pltpu.CompilerParams(dimension_semantics=("arbitrary",),
                                             vmem_limit_bytes=VMEM_LIMIT),
        name="ffn_ln",
    )(*args)


def _rope(z, cos_t, sin_t):
    lane = lax.broadcasted_iota(jnp.int32, (1, LANES), 1)
    first_half = (lane % ATT_HEAD_DIM) < (ATT_HEAD_DIM // 2)
    outs = []
    for c in range(z.shape[1] // LANES):
        zc = z[:, c * LANES:(c + 1) * LANES]
        partner = jnp.where(first_half, pltpu.roll(zc, LANES - 32, 1), pltpu.roll(zc, 32, 1))
        outs.append(zc * cos_t + partner * sin_t)
    return outs


def _proj_body(x_ref, w_ref, cos_ref, sin_ref, lbraw_ref,
               q_ref, k_ref, v_ref, qh_ref, kh_ref, ih_ref, lf_ref, gh_ref, *extra, emit, window):
    wb_ref = extra[0] if emit else None
    kt_ref, vt_ref = extra[-2:] if window else (None, None)
    xb = x_ref[...].astype(BF16)
    aw = ATT_WIDTH
    base = 3 * aw

    def transposed(ref, c, val):
        ref[0, 2 * c:2 * c + 2] = val.T.reshape(2, ATT_HEAD_DIM, val.shape[0])

    def cols(start, width):
        w = w_ref[:, start:start + width]
        if wb_ref is not None:
            w = w.astype(BF16)
            wb_ref[:, start:start + width] = w
        return jnp.dot(xb, w, preferred_element_type=F32)

    cos_t = cos_ref[...]
    sin_t = sin_ref[...]
    for c, val in enumerate(_rope(cols(0, aw), cos_t, sin_t)):
        q_ref[:, c * LANES:(c + 1) * LANES] = val * (ATT_HEAD_DIM ** -0.5)
    for c, val in enumerate(_rope(cols(aw, aw), cos_t, sin_t)):
        k_ref[:, c * LANES:(c + 1) * LANES] = val
        if window:
            transposed(kt_ref, c, val)
    lbr = lbraw_ref[...]
    mx = jnp.max(lbr, axis=0, keepdims=True)
    ex = jnp.exp(lbr - mx)
    lb = ex[0:1, :] / jnp.sum(ex, axis=0, keepdims=True)
    zf = cols(base + HG_WIDTH, HG_WIDTH)
    lf_ref[...] = jnp.log(lb + (1.0 - lb) * _sigmoid(zf))
    kh_ref[...] = (1.0 - lb) * _sigmoid(-zf)
    zq = cols(base, HG_WIDTH)
    qh_ref[...] = zq * _sigmoid(zq)
    v = cols(2 * aw, aw)
    v_ref[...] = v
    if window:
        for c in range(aw // LANES):
            transposed(vt_ref, c, v[:, c * LANES:(c + 1) * LANES])
    ih_ref[...] = cols(base + 2 * HG_WIDTH, HG_WIDTH)
    gh_ref[...] = cols(base + 3 * HG_WIDTH, HG_WIDTH)


def proj(x, w_in, cos_t, sin_t, lb_raw, table_tiles, win_tiles=0):
    m = x.shape[0]
    emit = w_in.dtype == F32
    assert not emit or m == ROW_TILE, "the weight copy is written once, by a single grid step"
    row = pl.BlockSpec((ROW_TILE, D_MODEL), lambda i: (i, 0))
    tab = pl.BlockSpec((ROW_TILE, LANES), lambda i: (i % table_tiles, 0))
    out = pl.BlockSpec((ROW_TILE, ATT_WIDTH), lambda i: (i, 0))
    sds = jax.ShapeDtypeStruct((m, ATT_WIDTH), F32)
    out_specs, out_shape = [out] * 8, [sds] * 8
    if emit:
        out_specs.append(pl.BlockSpec((D_MODEL, IN_COLS), lambda i: (0, 0)))
        out_shape.append(jax.ShapeDtypeStruct(w_in.shape, BF16))
    if win_tiles:
        skipped = table_tiles - win_tiles
        wblk = pl.BlockSpec((1, ATT_HEADS, ATT_HEAD_DIM, ROW_TILE),
                            lambda i: (i // table_tiles, 0, 0, jnp.maximum(i % table_tiles - skipped, 0)))
        wsds = jax.ShapeDtypeStruct((m // (table_tiles * ROW_TILE), ATT_HEADS, ATT_HEAD_DIM,
                                     win_tiles * ROW_TILE), F32)
        out_specs += [wblk, wblk]
        out_shape += [wsds, wsds]
    return pl.pallas_call(
        functools.partial(_proj_body, emit=emit, window=bool(win_tiles)),
        grid=(m // ROW_TILE,),
        in_specs=[row, _const_spec((D_MODEL, IN_COLS)), tab, tab, _const_spec((2, HG_WIDTH))],
        out_specs=out_specs,
        out_shape=out_shape,
        compiler_params=pltpu.CompilerParams(dimension_semantics=("arbitrary",),
                                             vmem_limit_bytes=VMEM_LIMIT),
        name="proj",
    )(x, w_in, cos_t, sin_t, lb_raw)


ATT_UNROLL = (8, 8, 4)
ATT_COMBINE_ROWS = 512


def _attn_prompt_body(q_ref, k_ref, v_ref, o_ref, acc_ref, m_ref, l_ref, *, seq):
    blk = ATT_BLOCK
    lane = lax.broadcasted_iota(jnp.int32, (1, LANES), 1)
    head0 = lane < ATT_HEAD_DIM
    qi = lax.broadcasted_iota(jnp.int32, (blk, 2 * blk), 0)
    kc = lax.broadcasted_iota(jnp.int32, (blk, 2 * blk), 1)
    dist = qi + blk - kc
    first_mask = (lax.broadcasted_iota(jnp.int32, (blk, blk), 1)
                  <= lax.broadcasted_iota(jnp.int32, (blk, blk), 0))
    nt = (((1,), (1,)), ((), ()))
    log2e = 1.4426950408889634

    for br, (window, dil) in enumerate(DILATIONS):
        w_sub = window // dil
        rel = (dist >= 0) & (dist <= w_sub)
        span = blk * dil
        n_blk = seq // span
        n_units = ATT_UNROLL[br]
        run = min(n_units, n_blk)
        n_res = n_units // run
        runs_per_res = n_blk // run

        def units(it, carry, br=br, dil=dil, rel=rel, span=span, run=run, n_res=n_res,
                  runs_per_res=runs_per_res, n_units=n_units):
            starts, masks, vbs, scores = [], [], [], []
            for rr in range(n_res):
                if runs_per_res == 1:
                    r, n0 = it * n_res + rr, 0
                else:
                    r, n0 = it // runs_per_res, (it % runs_per_res) * run
                base = r + n0 * span

                def kv_block(j):
                    rows = pl.ds(jnp.maximum(base + j * span, r), blk, stride=dil)
                    return k_ref[rows, :].astype(BF16), v_ref[rows, :].astype(BF16)

                prev = None if runs_per_res == 1 else kv_block(-1)
                for j in range(run):
                    cur = kv_block(j)
                    start = base + j * span
                    q = q_ref[pl.ds(start, blk, stride=dil), :] * log2e
                    if prev is None:
                        kb, vb, mask = cur[0], cur[1], first_mask
                    else:
                        kb = jnp.concatenate([prev[0], cur[0]], axis=0)
                        vb = jnp.concatenate([prev[1], cur[1]], axis=0)
                        mask = rel & (kc >= jnp.where(n0 > 0, 0, blk)) if (j == 0 and runs_per_res > 1) else rel
                    starts.append(start)
                    masks.append(mask)
                    vbs.append(vb)
                    for hsel in (head0, ~head0):
                        qh = jnp.where(hsel, q, 0.0).astype(BF16)
                        scores.append(lax.dot_general(qh, kb, nt, preferred_element_type=F32))
                    prev = cur
            probs, ls, ms = [], [], []
            for i, s in enumerate(scores):
                s = jnp.where(masks[i // 2], s, NEG_INF)
                m = jnp.max(s, axis=-1, keepdims=True)
                p = jnp.exp2(s - m)
                ls.append(jnp.sum(p, axis=-1, keepdims=True))
                ms.append(m)
                probs.append(p.astype(BF16))
            outs = [jnp.dot(p, vbs[i // 2], preferred_element_type=F32) for i, p in enumerate(probs)]
            for uu in range(n_units):
                i0, i1 = 2 * uu, 2 * uu + 1
                rows = pl.ds(starts[uu], blk, stride=dil)
                acc_ref[br, rows, :] = jnp.where(head0, outs[i0], outs[i1])
                m_ref[br, rows, :] = jnp.broadcast_to(jnp.where(head0, ms[i0], ms[i1]), (blk, LANES))
                l_ref[br, rows, :] = jnp.broadcast_to(jnp.where(head0, ls[i0], ls[i1]), (blk, LANES))
            return carry

        lax.fori_loop(0, (dil * n_blk) // n_units, units, 0)

    def combine(c, carry):
        rows = pl.ds(pl.multiple_of(c * ATT_COMBINE_ROWS, ATT_COMBINE_ROWS), ATT_COMBINE_ROWS)
        m0, m1, m2 = m_ref[0, rows, :], m_ref[1, rows, :], m_ref[2, rows, :]
        mx = jnp.maximum(jnp.maximum(m0, m1), m2)
        w0, w1, w2 = jnp.exp2(m0 - mx), jnp.exp2(m1 - mx), jnp.exp2(m2 - mx)
        num = w0 * acc_ref[0, rows, :] + w1 * acc_ref[1, rows, :] + w2 * acc_ref[2, rows, :]
        den = w0 * l_ref[0, rows, :] + w1 * l_ref[1, rows, :] + w2 * l_ref[2, rows, :]
        o_ref[rows, :] = num / den
        return carry

    lax.fori_loop(0, seq // ATT_COMBINE_ROWS, combine, 0)


def attn_prompt(q, k, v, batch, seq):
    n_hp = ATT_WIDTH // LANES
    blk = pl.BlockSpec((seq, LANES), lambda b, h: (b, h))
    return pl.pallas_call(
        functools.partial(_attn_prompt_body, seq=seq),
        grid=(batch, n_hp),
        in_specs=[blk, blk, blk],
        out_specs=blk,
        out_shape=jax.ShapeDtypeStruct((batch * seq, ATT_WIDTH), F32),
        scratch_shapes=[pltpu.VMEM((3, seq, LANES), F32)] * 3,
        compiler_params=pltpu.CompilerParams(dimension_semantics=("arbitrary", "arbitrary"),
                                             vmem_limit_bytes=VMEM_LIMIT),
        name="attn_prompt",
    )(q, k, v)


def _hgrn_prompt_body(*refs, rows_per_step, n_cast):
    q_ref, k_ref, v_ref, lf_ref, gh_ref, ng_ref = refs[:6]
    o_ref, s_ref = refs[6 + n_cast:8 + n_cast]
    st_ref = refs[-1]
    for w_ref, wb_ref in zip(refs[6:6 + n_cast], refs[8 + n_cast:8 + 2 * n_cast]):
        wb_ref[...] = w_ref[...].astype(BF16)
    grp = LANES
    n_ch = grp // HG_CHUNK
    ri = lax.broadcasted_iota(jnp.int32, (grp, grp), 0)
    ci = lax.broadcasted_iota(jnp.int32, (grp, grp), 1)
    same_chunk = (ri // HG_CHUNK) == (ci // HG_CHUNK)
    causal = same_chunk & (ci <= ri)
    summat = jnp.where(causal, 1.0, 0.0).astype(BF16)
    nt = (((1,), (1,)), ((), ()))
    tb = pl.program_id(1)

    @pl.when(tb == 0)
    def _():
        st_ref[...] = jnp.zeros(st_ref.shape, F32)

    heads = [slice(h * LANES, (h + 1) * LANES) for h in range(HG_HEADS)]
    in_chunk = [(ri // HG_CHUNK) == c for c in range(n_ch)]

    def by_chunk(x):
        return jnp.concatenate([jnp.where(m, x, 0.0).astype(BF16) for m in in_chunk], axis=1)

    def groups(it, carry):
        gs = range(HG_GROUPS)
        rows = [pl.ds(pl.multiple_of((it * HG_GROUPS + g) * grp, grp), grp) for g in gs]
        cum = []
        for g in gs:
            parts = _split3(lf_ref[rows[g], :])
            cum.append(jnp.dot(summat, parts[0], preferred_element_type=F32)
                       + jnp.dot(summat, parts[1], preferred_element_type=F32)
                       + jnp.dot(summat, parts[2], preferred_element_type=F32))
        q_dec, decay, a, vs, k_end = [], [], [], [], []
        for g in gs:
            cum_end = jnp.concatenate(
                [jnp.broadcast_to(cum[g][(c + 1) * HG_CHUNK - 1:(c + 1) * HG_CHUNK, :], (HG_CHUNK, HG_WIDTH))
                 for c in range(n_ch)], axis=0)
            q, k, v = q_ref[rows[g], :], k_ref[rows[g], :], v_ref[rows[g], :]
            q_dec.append(q * jnp.exp(cum[g]))
            k_dec = (k * jnp.exp(-cum[g])).astype(BF16)
            k_end.append(k * jnp.exp(cum_end - cum[g]))
            decay.append(jnp.exp(cum_end))
            q_decb = q_dec[g].astype(BF16)
            vs.append(v)
            a.append([lax.dot_general(q_decb[:, hs], k_dec[:, hs], nt, preferred_element_type=F32)
                      for hs in heads])
        o = [[jnp.dot(jnp.where(causal, a[g][h], 0.0).astype(BF16), vs[g][:, hs].astype(BF16),
                      preferred_element_type=F32) for h, hs in enumerate(heads)] for g in gs]
        ut = [[jnp.dot(vs[g][:, hs].T.astype(BF16), by_chunk(k_end[g][:, hs]), preferred_element_type=F32)
               for hs in heads] for g in gs]
        starts = [[[] for _ in heads] for _ in gs]
        for h, hs in enumerate(heads):
            st = st_ref[h]
            for g in gs:
                for c in range(n_ch):
                    starts[g][h].append(st.astype(BF16))
                    st = (st * decay[g][c * HG_CHUNK:c * HG_CHUNK + 1, hs]
                          + ut[g][h][:, c * LANES:(c + 1) * LANES])
            st_ref[h] = st
        inter = [[lax.dot_general(by_chunk(q_dec[g][:, hs]), jnp.concatenate(starts[g][h], axis=1), nt,
                                  preferred_element_type=F32) for h, hs in enumerate(heads)] for g in gs]
        for g in gs:
            for h, hs in enumerate(heads):
                oh = o[g][h] + inter[g][h]
                ms = jnp.mean(oh * oh, axis=-1, keepdims=True)
                gate = gh_ref[rows[g], hs]
                o_ref[rows[g], hs] = oh * lax.rsqrt(ms + LN_EPS) * ng_ref[:, hs] * (gate * _sigmoid(gate))
        return carry

    lax.fori_loop(0, rows_per_step // (grp * HG_GROUPS), groups, 0)

    @pl.when(tb == pl.num_programs(1) - 1)
    def _():
        for h in range(HG_HEADS):
            s_ref[0, h] = st_ref[h].T


HG_ROWS = 1024
HG_GROUPS = 4


def hgrn_prompt(qh, kh, ih, lf, gh, norm_g, batch, seq, cast=()):
    n_tb = seq // HG_ROWS
    steps = batch * n_tb

    def rows(b, t):
        return (b * n_tb + t, 0)

    blk = pl.BlockSpec((HG_ROWS, HG_WIDTH), rows)
    cast_specs = [pl.BlockSpec((w.shape[0] // steps, w.shape[1]), rows) for w in cast]
    return pl.pallas_call(
        functools.partial(_hgrn_prompt_body, rows_per_step=HG_ROWS, n_cast=len(cast)),
        grid=(batch, n_tb),
        in_specs=[blk, blk, blk, blk, blk, _const_spec((1, HG_WIDTH))] + cast_specs,
        out_specs=[blk, pl.BlockSpec((1, HG_HEADS, HG_DK, HG_DV), lambda b, t: (b, 0, 0, 0))] + cast_specs,
        out_shape=[jax.ShapeDtypeStruct((batch * seq, HG_WIDTH), F32),
                   jax.ShapeDtypeStruct((batch, HG_HEADS, HG_DK, HG_DV), F32)]
                  + [jax.ShapeDtypeStruct(w.shape, BF16) for w in cast],
        scratch_shapes=[pltpu.VMEM((HG_HEADS, HG_DV, HG_DK), F32)],
        compiler_params=pltpu.CompilerParams(dimension_semantics=("arbitrary", "arbitrary"),
                                             vmem_limit_bytes=VMEM_LIMIT),
        name="hgrn_prompt",
    )(qh, kh, ih, lf, gh, norm_g, *cast)


FA_ROWS = 128
FA_HEADS = 4
FA_SLOTS = 4


def _divmod_pow2(s, n):
    assert n & (n - 1) == 0
    if isinstance(s, int):
        return s // n, s % n
    return lax.shift_right_logical(s, n.bit_length() - 1), s & (n - 1)


def _ffn_attn_body(*refs, n_new, past, b_off, mixed, n_steps):
    n_x = 10 if mixed else 1
    wg_ref, wu_ref, wd_ref, g_ref, b_ref, q_ref, kn_ref, vn_ref, kc_hbm, vc_hbm = refs[n_x:n_x + 10]
    y_ref, o_ref, nk_ref, nv_ref, kbuf, vbuf, sems = refs[-8:-1] if mixed else refs[-7:]
    halves = ATT_HEADS // FA_HEADS
    step = pl.program_id(0)

    split = _divmod_pow2
    odd = (b_off + split(step, halves)[0]) & 1 == 1

    def window_copies(s):
        slot = split(s, FA_SLOTS)[1]
        b_rel, half = split(s, halves)
        src = (b_off + b_rel, pl.ds(half * FA_HEADS, FA_HEADS))
        return (pltpu.make_async_copy(kc_hbm.at[src], kbuf.at[slot], sems.at[slot, 0]),
                pltpu.make_async_copy(vc_hbm.at[src], vbuf.at[slot], sems.at[slot, 1]))

    @pl.when(step == 0)
    def _():
        for s in range(min(FA_SLOTS - 1, n_steps)):
            for c in window_copies(s):
                c.start()

    @pl.when(step + (FA_SLOTS - 1) < n_steps)
    def _():
        for c in window_copies(step + (FA_SLOTS - 1)):
            c.start()

    for c in window_copies(step):
        c.wait()
    slot = split(step, FA_SLOTS)[1]
    kc_ref, vc_ref = kbuf.at[slot], vbuf.at[slot]

    def new_rows(ref):
        both = ref[...]
        return jnp.where(odd, both[n_new:2 * n_new], both[0:n_new])

    q_new, k_new, v_new = new_rows(q_ref), new_rows(kn_ref), new_rows(vn_ref)
    head_cols = [slice(h * ATT_HEAD_DIM, (h + 1) * ATT_HEAD_DIM) for h in range(FA_HEADS)]
    srow = lax.broadcasted_iota(jnp.int32, (n_new, past), 0)
    tcol = lax.broadcasted_iota(jnp.int32, (n_new, past), 1)
    dist = past + srow - tcol
    def on_stride(d, dil):
        return (d & (dil - 1)) == 0 if dil & (dil - 1) == 0 else d % dil == 0

    cnt = jnp.zeros((n_new, past), F32)
    for window, dil in DILATIONS:
        cnt = cnt + jnp.where(on_stride(dist, dil) & (dist <= window), 1.0, 0.0)
    si = lax.broadcasted_iota(jnp.int32, (n_new, n_new), 0)
    sj = lax.broadcasted_iota(jnp.int32, (n_new, n_new), 1)
    dn = si - sj
    cnt_new = jnp.zeros((n_new, n_new), F32)
    for window, dil in DILATIONS:
        cnt_new = cnt_new + jnp.where((dn >= 0) & on_stride(dn, dil) & (dn <= window), 1.0, 0.0)
    lane = lax.broadcasted_iota(jnp.int32, (1, LANES), 1)
    nt = (((1,), (1,)), ((), ()))
    shift = past - n_new
    hs = range(FA_HEADS)

    def scores():
        qs = [q_new[:, c].astype(BF16) for c in head_cols]
        s_c = [jnp.dot(qs[h], kc_ref[h].astype(BF16), preferred_element_type=F32) for h in hs]
        s_n = [lax.dot_general(qs[h], k_new[:, head_cols[h]].astype(BF16), nt, preferred_element_type=F32)
               for h in hs]
        return s_c, s_n

    if mixed:
        first, nxt = refs[0:3], refs[3:6]
        wa_ref, wh_ref, g2_ref, b2_ref = refs[6:n_x]
        xin_ref = refs[-1]

        def mixed_rows(x1_ref, oa_ref, oh_ref):
            mix = jnp.dot(oa_ref[...].astype(BF16), wa_ref[...], preferred_element_type=F32)
            mix = mix + jnp.dot(oh_ref[...].astype(BF16), wh_ref[...], preferred_element_type=F32)
            return _layer_norm(DN_ALPHA * x1_ref[...] + mix, g2_ref[...], b2_ref[...])

        @pl.when(step == 0)
        def _():
            xin_ref[...] = mixed_rows(*first)

        x = xin_ref[...]
        x_next = mixed_rows(*nxt)
    else:
        x = refs[0][...]
    xb = x.astype(BF16)
    gate = jnp.dot(xb, wg_ref[...], preferred_element_type=F32)
    up = jnp.dot(xb, wu_ref[...], preferred_element_type=F32)
    s_c, s_n = scores()
    knt = [k_new[:, c].T for c in head_cols]
    vnt = [v_new[:, c].T for c in head_cols]

    hmid = (gate * _sigmoid(gate) * up).astype(BF16)
    y = jnp.dot(hmid, wd_ref[...], preferred_element_type=F32)
    y_ref[...] = _layer_norm(DN_ALPHA * x + 0.5 * y, g_ref[...], b_ref[...])
    if mixed:
        xin_ref[...] = x_next

    p_c, p_n, den = [], [], []
    for h in hs:
        sc = jnp.where(cnt > 0, s_c[h], NEG_INF)
        sn = jnp.where(cnt_new > 0, s_n[h], NEG_INF)
        m = jnp.maximum(jnp.max(sc, axis=-1, keepdims=True), jnp.max(sn, axis=-1, keepdims=True))
        p_c.append((cnt * jnp.exp(sc - m)).astype(BF16))
        p_n.append((cnt_new * jnp.exp(sn - m)).astype(BF16))
        den.append(jnp.sum(p_c[h].astype(F32), axis=-1, keepdims=True)
                   + jnp.sum(p_n[h].astype(F32), axis=-1, keepdims=True))
    outs = []
    for h in hs:
        num = (lax.dot_general(p_c[h], vc_ref[h].astype(BF16), nt, preferred_element_type=F32)
               + jnp.dot(p_n[h], v_new[:, head_cols[h]].astype(BF16), preferred_element_type=F32))
        outs.append(num / den[h])
    o_ref[0] = jnp.concatenate(outs, axis=1)

    for h in hs:
        for src_ref, new, dst in ((kc_ref, knt[h], nk_ref), (vc_ref, vnt[h], nv_ref)):
            rolled = pltpu.roll(src_ref[h], shift, 1)
            dst[0, h, :, 0:past - LANES] = rolled[:, 0:past - LANES]
            last = rolled[:, past - LANES:past]
            for s in range(n_new):
                col = jnp.broadcast_to(new[:, s:s + 1], last.shape)
                last = jnp.where(lane == LANES - n_new + s, col, last)
            dst[0, h, :, past - LANES:past] = last


def ffn_attn(x, wg, wu, wd, g, b, q, kn, vn, kc, vc, n_new, b_off, prev=None, mix=None):
    m = x.shape[0]
    bsz, n_heads, _, past = kc.shape
    halves = n_heads // FA_HEADS
    steps = m // FA_ROWS
    assert b_off % 2 == 0 and bsz % 2 == 0, "new-token rows are fetched per pair of batch rows"

    def where(i):
        b_rel, half = _divmod_pow2(i, halves)
        return b_off + b_rel, half

    def samp(i):
        return (*where(i), 0, 0)

    row = pl.BlockSpec((FA_ROWS, D_MODEL), lambda i: (i, 0))
    pair = pl.BlockSpec((2 * n_new, FA_HEADS * ATT_HEAD_DIM),
                        lambda i: (_divmod_pow2(where(i)[0], 2)[0], where(i)[1]))
    o_blk = pl.BlockSpec((1, n_new, FA_HEADS * ATT_HEAD_DIM), lambda i: (where(i)[0], 0, where(i)[1]))
    big = pl.BlockSpec((1, FA_HEADS, ATT_HEAD_DIM, past), samp)
    if mix is None:
        in_specs, args = [row], [x]
    else:
        o_att, o_hg, w_out, ln_g, ln_b = mix
        in_specs, args = [], []
        for index in (lambda i: (0, 0), lambda i: (jnp.minimum(i + 1, steps - 1), 0)):
            in_specs += [pl.BlockSpec((FA_ROWS, D_MODEL), index), pl.BlockSpec((FA_ROWS, ATT_WIDTH), index),
                         pl.BlockSpec((FA_ROWS, HG_WIDTH), index)]
            args += [x, o_att, o_hg]
        in_specs += _w_out_specs() + [_const_spec((1, D_MODEL)), _const_spec((1, D_MODEL))]
        args += [w_out, w_out, ln_g, ln_b]
    hbm = pl.BlockSpec(memory_space=pl.ANY)
    in_specs += [_const_spec((D_MODEL, D_FF)), _const_spec((D_MODEL, D_FF)), _const_spec((D_FF, D_MODEL)),
                 _const_spec((1, D_MODEL)), _const_spec((1, D_MODEL)), pair, pair, pair, hbm, hbm]
    args += [wg, wu, wd, g, b, q, kn, vn, kc, vc]
    aliases = {}
    if prev is not None:
        for j, a in enumerate(prev):
            aliases[len(args)] = 1 + j
            in_specs.append(pl.BlockSpec(memory_space=pl.ANY))
            args.append(a)
    big_sds = jax.ShapeDtypeStruct(kc.shape, F32)
    return pl.pallas_call(
        functools.partial(_ffn_attn_body, n_new=n_new, past=past, b_off=b_off, mixed=mix is not None,
                          n_steps=steps),
        grid=(steps,),
        in_specs=in_specs,
        out_specs=[row, o_blk, big, big],
        out_shape=[jax.ShapeDtypeStruct((m, D_MODEL), F32),
                   jax.ShapeDtypeStruct((bsz, n_new, n_heads * ATT_HEAD_DIM), F32), big_sds, big_sds],
        scratch_shapes=[pltpu.VMEM((FA_SLOTS, FA_HEADS, ATT_HEAD_DIM, past), F32),
                        pltpu.VMEM((FA_SLOTS, FA_HEADS, ATT_HEAD_DIM, past), F32),
                        pltpu.SemaphoreType.DMA((FA_SLOTS, 2))]
                       + ([pltpu.VMEM((FA_ROWS, D_MODEL), F32)] if mix is not None else []),
        input_output_aliases=aliases,
        compiler_params=pltpu.CompilerParams(dimension_semantics=("arbitrary",),
                                             vmem_limit_bytes=VMEM_LIMIT),
        name="ffn_attn",
    )(*args)


HS_BATCH = 8


def _hgrn_sample_body(q_ref, k_ref, v_ref, lf_ref, gh_ref, ng_ref, s0_ref, o_ref, s_ref, *, n_new):
    row = lax.broadcasted_iota(jnp.int32, (n_new, 1), 0)
    tn = (((0,), (0,)), ((), ()))
    ones12 = jnp.ones((3 * n_new, HG_DV), BF16)

    sub = 8
    per_tile = sub // n_new

    def tile(t, carry):
        rows = pl.ds(pl.multiple_of(t * sub, sub), sub)
        units = []
        for h in range(HG_HEADS):
            cols = slice(h * LANES, (h + 1) * LANES)
            q8, k8, v8, g8 = q_ref[rows, cols], k_ref[rows, cols], v_ref[rows, cols], lf_ref[rows, cols]
            for i in range(per_tile):
                part = slice(i * n_new, (i + 1) * n_new)
                q, k, v, g = q8[part], k8[part], v8[part], g8[part]
                cum = jnp.zeros((n_new, LANES), F32)
                for j in range(n_new):
                    cum = cum + jnp.where(row >= j, g[j:j + 1, :], 0.0)
                cum_end = cum[n_new - 1:n_new, :]
                units.append(dict(h=h, b=t * per_tile + i, v=v, g=g, q_dec=q * jnp.exp(cum),
                                  k_dec=k * jnp.exp(-cum), k_end=k * jnp.exp(cum_end - cum)))
        for u in units:
            u["s0"] = s0_ref[u["b"], u["h"]]
            u["o"] = jnp.dot(u["q_dec"].astype(BF16), u["s0"].astype(BF16), preferred_element_type=F32)
        for u in units:
            dec = lax.dot_general(jnp.concatenate(_split3(u["g"]), axis=0), ones12, tn,
                                  preferred_element_type=F32)
            upd = lax.dot_general(u["k_end"].astype(BF16), u["v"].astype(BF16), tn,
                                  preferred_element_type=F32)
            s_ref[u["b"], u["h"]] = jnp.exp(dec) * u["s0"] + upd
        for h in range(HG_HEADS):
            cols = slice(h * LANES, (h + 1) * LANES)
            outs = []
            for u in units[h * per_tile:(h + 1) * per_tile]:
                o = u["o"]
                for j in range(n_new):
                    a_j = jnp.sum(u["q_dec"] * u["k_dec"][j:j + 1, :], axis=-1, keepdims=True)
                    o = o + jnp.where(row >= j, a_j, 0.0) * u["v"][j:j + 1, :]
                outs.append(o * lax.rsqrt(jnp.mean(o * o, axis=-1, keepdims=True) + LN_EPS))
            gate = gh_ref[rows, cols]
            o_ref[rows, cols] = jnp.concatenate(outs, axis=0) * ng_ref[:, cols] * (gate * _sigmoid(gate))
        return carry

    lax.fori_loop(0, HS_BATCH // per_tile, tile, 0)


def hgrn_sample(qh, kh, ih, lf, gh, norm_g, s0, n_new):
    bsz = s0.shape[0]
    assert bsz % HS_BATCH == 0 and 8 % n_new == 0, "whole grid steps of whole 8-row tiles"
    blk = pl.BlockSpec((HS_BATCH * n_new, HG_WIDTH), lambda i: (i, 0))
    sblk = pl.BlockSpec((HS_BATCH, HG_HEADS, HG_DK, HG_DV), lambda i: (i, 0, 0, 0))
    return pl.pallas_call(
        functools.partial(_hgrn_sample_body, n_new=n_new),
        grid=(bsz // HS_BATCH,),
        in_specs=[blk, blk, blk, blk, blk, _const_spec((1, HG_WIDTH)), sblk],
        out_specs=[blk, sblk],
        out_shape=[jax.ShapeDtypeStruct((bsz * n_new, HG_WIDTH), F32),
                   jax.ShapeDtypeStruct(s0.shape, F32)],
        compiler_params=pltpu.CompilerParams(dimension_semantics=("arbitrary",),
                                             vmem_limit_bytes=VMEM_LIMIT),
        name="hgrn_sample",
    )(qh, kh, ih, lf, gh, norm_g, s0)


def _rope_tables(pos):
    half = ATT_HEAD_DIM // 2
    inv = ROPE_THETA ** (-jnp.arange(half, dtype=F32) / half)
    ang = pos.astype(F32)[:, None] * inv[None, :]
    cos, sin = jnp.cos(ang), jnp.sin(ang)
    return jnp.tile(cos, (1, 4)), jnp.tile(jnp.concatenate([-sin, sin], axis=1), (1, 2))


def kernel(x_prompt, x_sample, cache_k, cache_v, state_hgrn, ffn1_w_gate, ffn1_w_up, ffn1_w_down, ln1_g, ln1_b, w_in, hg_lower_bound, hg_norm_g, w_out, ln2_g, ln2_b, ffn2_w_gate, ffn2_w_up, ffn2_w_down, ln3_g, ln3_b):
    batch, seq, _ = x_prompt.shape
    dec_batch, n_new, _ = x_sample.shape
    assert ffn1_w_gate.shape[0] == 1, "single layer"
    assert seq % (ATT_BLOCK * max(d for _, d in DILATIONS)) == 0 and seq % HG_ROWS == 0 and seq % ROW_TILE == 0
    assert dec_batch * n_new == ROW_TILE, "the sample rows form one row tile"


    cos_p, sin_p = _rope_tables(jnp.arange(seq, dtype=jnp.int32))
    pos_s = PAST_LEN + (jnp.arange(dec_batch * n_new, dtype=jnp.int32) % n_new)
    cos_s, sin_s = _rope_tables(pos_s)

    xp = x_prompt.reshape(batch * seq, D_MODEL)
    xs = x_sample.reshape(dec_batch * n_new, D_MODEL)

    xs1, *w1 = ffn_ln(xs, ffn1_w_gate[0], ffn1_w_up[0], ffn1_w_down[0], ln1_g, ln1_b)
    qs, ks, vs, qhs, khs, ihs, lfs, ghs, w_in_b = proj(xs1, w_in[0], cos_s, sin_s, hg_lower_bound, 1)

    samp = (qs, ks, vs,
            cache_k[0].transpose(0, 2, 3, 1),
            cache_v[0].transpose(0, 2, 3, 1), n_new)
    steps = (batch * seq) // FA_ROWS
    per_call = steps // (ATT_HEADS // FA_HEADS)
    assert 2 * per_call == dec_batch, "the two prompt FFN calls must cover the sample batch"

    xp1, o_t, nk, nv = ffn_attn(xp, *w1, ln1_g, ln1_b, *samp, 0)
    win_p = min(ATT_WINDOW_MAX, seq)
    q, k, v, qh, kh, ih, lf, gh, kt_p, vt_p = proj(xp1, w_in_b, cos_p, sin_p, hg_lower_bound, seq // ROW_TILE,
                                                   win_tiles=win_p // ROW_TILE)
    o_att = attn_prompt(q, k, v, batch, seq)
    o_hg, s_p, *w2, w_out_b = hgrn_prompt(qh, kh, ih, lf, gh, hg_norm_g, batch, seq,
                                          cast=(ffn2_w_gate[0], ffn2_w_up[0], ffn2_w_down[0], w_out[0]))
    yp, o_t, nk, nv = ffn_attn(xp1, *w2, ln3_g, ln3_b, *samp, per_call, prev=(o_t, nk, nv),
                               mix=(o_att, o_hg, w_out_b, ln2_g, ln2_b))

    o_att_s = o_t.reshape(dec_batch * n_new, ATT_WIDTH)

    o_hg_s, s_s = hgrn_sample(qhs, khs, ihs, lfs, ghs, hg_norm_g, state_hgrn[0], n_new)
    ys, = ffn_ln(xs1, *w2, ln3_g, ln3_b, mix=(o_att_s, o_hg_s, w_out_b, ln2_g, ln2_b))

    return (yp.reshape(batch, seq, D_MODEL),
            ys.reshape(dec_batch, n_new, D_MODEL),
            kt_p.transpose(0, 3, 1, 2)[None],
            vt_p.transpose(0, 3, 1, 2)[None],
            s_p[None],
            nk.transpose(0, 3, 1, 2)[None],
            nv.transpose(0, 3, 1, 2)[None],
            s_s[None])
```

```python
import functools

import jax
import jax.numpy as jnp
from jax import lax
from jax.experimental import pallas as pl
from jax.experimental.pallas import tpu as pltpu

F32 = jnp.float32
BF16 = jnp.bfloat16

D_MODEL = 1024
ATT_WIDTH = 512
HG_WIDTH = 512
ATT_HEAD_DIM = 64
ATT_HEADS = 8
DILATIONS = ((128, 1), (512, 4), (2048, 16))
ATT_WINDOW_MAX = 2048
ATT_BLOCK = 128
ROPE_THETA = 10000.0
HG_HEADS = 4
HG_DK = 128
HG_DV = 128
HG_CHUNK = 32
D_FF = 2816
IN_COLS = 3 * ATT_WIDTH + 4 * HG_WIDTH
PAST_LEN = 8192
DN_ALPHA = 2.0 ** 0.25
LN_EPS = 1e-5
NEG_INF = -1e30

LANES = 128
ROW_TILE = 512
VMEM_LIMIT = 56 * 1024 * 1024


def _const_spec(shape, block=None):
    index = tuple(block) if block is not None else (0,) * len(shape)
    return pl.BlockSpec(shape, lambda *_: index, pipeline_mode=pl.Buffered(1))


def _layer_norm(r, g, b):
    mu = jnp.mean(r, axis=-1, keepdims=True)
    c = r - mu
    var = jnp.mean(c * c, axis=-1, keepdims=True)
    return c * lax.rsqrt(var + LN_EPS) * g + b


def _sigmoid(x):
    return 1.0 / (1.0 + jnp.exp(-x))


def _split3(x):
    hi = x.astype(BF16)
    r1 = x - hi.astype(F32)
    mid = r1.astype(BF16)
    lo = (r1 - mid.astype(F32)).astype(BF16)
    return hi, mid, lo


def _w_out_specs():
    assert ATT_WIDTH == HG_WIDTH
    return [_const_spec((ATT_WIDTH, D_MODEL), (0, 0)), _const_spec((HG_WIDTH, D_MODEL), (1, 0))]


FFN_CHUNK = 256


def _ffn_ln_body(*refs, cast, mixed):
    n_x = 7 if mixed else 1
    wg_ref, wu_ref, wd_ref, g_ref, b_ref, o_ref = refs[n_x:n_x + 6]
    xin_ref, xb_ref, acc_ref = refs[-3:]
    j = pl.program_id(0)

    @pl.when(j == 0)
    def _():
        if mixed:
            x1_ref, oa_ref, oh_ref, wa_ref, wh_ref, g2_ref, b2_ref = refs[:n_x]
            mix = jnp.dot(oa_ref[...].astype(BF16), wa_ref[...], preferred_element_type=F32)
            mix = mix + jnp.dot(oh_ref[...].astype(BF16), wh_ref[...], preferred_element_type=F32)
            x = _layer_norm(DN_ALPHA * x1_ref[...] + mix, g2_ref[...], b2_ref[...])
        else:
            x = refs[0][...]
        xin_ref[...] = x
        xb_ref[...] = x.astype(BF16)
        acc_ref[...] = jnp.zeros(acc_ref.shape, F32)

    wgb, wub, wdb = wg_ref[...], wu_ref[...], wd_ref[...]
    if cast:
        wgb, wub, wdb = wgb.astype(BF16), wub.astype(BF16), wdb.astype(BF16)
        for dst, w in zip(refs[n_x + 6:n_x + 9], (wgb, wub, wdb)):
            dst[...] = w
    xb = xb_ref[...]
    gate = jnp.dot(xb, wgb, preferred_element_type=F32)
    up = jnp.dot(xb, wub, preferred_element_type=F32)
    h = (gate * _sigmoid(gate) * up).astype(BF16)
    acc_ref[...] += jnp.dot(h, wdb, preferred_element_type=F32)

    @pl.when(j == pl.num_programs(0) - 1)
    def _():
        o_ref[...] = _layer_norm(DN_ALPHA * xin_ref[...] + 0.5 * acc_ref[...], g_ref[...], b_ref[...])


def ffn_ln(x, wg, wu, wd, g, b, mix=None):
    m = x.shape[0]
    cast = wg.dtype == F32
    col = pl.BlockSpec((D_MODEL, FFN_CHUNK), lambda j: (0, j))
    rowc = pl.BlockSpec((FFN_CHUNK, D_MODEL), lambda j: (j, 0))
    xs = pl.BlockSpec((m, D_MODEL), lambda j: (0, 0))
    in_specs, args = [xs], [x]
    if mix is not None:
        half = pl.BlockSpec((m, ATT_WIDTH), lambda j: (0, 0))
        in_specs += [half, half] + _w_out_specs() + [_const_spec((1, D_MODEL)), _const_spec((1, D_MODEL))]
        o_att, o_hg, w_out, ln_g, ln_b = mix
        args += [o_att, o_hg, w_out, w_out, ln_g, ln_b]
    in_specs += [col, col, rowc, _const_spec((1, D_MODEL)), _const_spec((1, D_MODEL))]
    args += [wg, wu, wd, g, b]
    out_specs = [xs] + ([col, col, rowc] if cast else [])
    out_shape = [jax.ShapeDtypeStruct((m, D_MODEL), F32)]
    if cast:
        out_shape += [jax.ShapeDtypeStruct(w.shape, BF16) for w in (wg, wu, wd)]
    return pl.pallas_call(
        functools.partial(_ffn_ln_body, cast=cast, mixed=mix is not None),
        grid=(D_FF // FFN_CHUNK,),
        in_specs=in_specs,
        out_specs=out_specs,
        out_shape=out_shape,
        scratch_shapes=[pltpu.VMEM((m, D_MODEL), F32), pltpu.VMEM((m, D_MODEL), BF16),
                        pltpu.VMEM((m, D_MODEL), F32)],
        compiler_params=pltpu.CompilerParams(dimension_semantics=("arbitrary",),
                                             vmem_limit_bytes=VMEM_LIMIT),
        name="ffn_ln",
    )(*args)


def _rope(z, cos_t, sin_t):
    lane = lax.broadcasted_iota(jnp.int32, (1, LANES), 1)
    first_half = (lane % ATT_HEAD_DIM) < (ATT_HEAD_DIM // 2)
    outs = []
    for c in range(z.shape[1] // LANES):
        zc = z[:, c * LANES:(c + 1) * LANES]
        partner = jnp.where(first_half, pltpu.roll(zc, LANES - 32, 1), pltpu.roll(zc, 32, 1))
        outs.append(zc * cos_t + partner * sin_t)
    return outs


def _proj_body(x_ref, w_ref, cos_ref, sin_ref, lbraw_ref,
               q_ref, k_ref, v_ref, qh_ref, kh_ref, ih_ref, lf_ref, gh_ref, *extra, emit, window):
    wb_ref = extra[0] if emit else None
    kt_ref, vt_ref = extra[-2:] if window else (None, None)
    xb = x_ref[...].astype(BF16)
    aw = ATT_WIDTH
    base = 3 * aw

    def transposed(ref, c, val):
        ref[0, 2 * c:2 * c + 2] = val.T.reshape(2, ATT_HEAD_DIM, val.shape[0])

    def cols(start, width):
        w = w_ref[:, start:start + width]
        if wb_ref is not None:
            w = w.astype(BF16)
            wb_ref[:, start:start + width] = w
        return jnp.dot(xb, w, preferred_element_type=F32)

    cos_t = cos_ref[...]
    sin_t = sin_ref[...]
    for c, val in enumerate(_rope(cols(0, aw), cos_t, sin_t)):
        q_ref[:, c * LANES:(c + 1) * LANES] = val * (ATT_HEAD_DIM ** -0.5)
    for c, val in enumerate(_rope(cols(aw, aw), cos_t, sin_t)):
        k_ref[:, c * LANES:(c + 1) * LANES] = val
        if window:
            transposed(kt_ref, c, val)
    lbr = lbraw_ref[...]
    mx = jnp.max(lbr, axis=0, keepdims=True)
    ex = jnp.exp(lbr - mx)
    lb = ex[0:1, :] / jnp.sum(ex, axis=0, keepdims=True)
    zf = cols(base + HG_WIDTH, HG_WIDTH)
    lf_ref[...] = jnp.log(lb + (1.0 - lb) * _sigmoid(zf))
    kh_ref[...] = (1.0 - lb) * _sigmoid(-zf)
    zq = cols(base, HG_WIDTH)
    qh_ref[...] = zq * _sigmoid(zq)
    v = cols(2 * aw, aw)
    v_ref[...] = v
    if window:
        for c in range(aw // LANES):
            transposed(vt_ref, c, v[:, c * LANES:(c + 1) * LANES])
    ih_ref[...] = cols(base + 2 * HG_WIDTH, HG_WIDTH)
    gh_ref[...] = cols(base + 3 * HG_WIDTH, HG_WIDTH)


def proj(x, w_in, cos_t, sin_t, lb_raw, table_tiles, win_tiles=0):
    m = x.shape[0]
    emit = w_in.dtype == F32
    assert not emit or m == ROW_TILE, "the weight copy is written once, by a single grid step"
    row = pl.BlockSpec((ROW_TILE, D_MODEL), lambda i: (i, 0))
    tab = pl.BlockSpec((ROW_TILE, LANES), lambda i: (i % table_tiles, 0))
    out = pl.BlockSpec((ROW_TILE, ATT_WIDTH), lambda i: (i, 0))
    sds = jax.ShapeDtypeStruct((m, ATT_WIDTH), F32)
    out_specs, out_shape = [out] * 8, [sds] * 8
    if emit:
        out_specs.append(pl.BlockSpec((D_MODEL, IN_COLS), lambda i: (0, 0)))
        out_shape.append(jax.ShapeDtypeStruct(w_in.shape, BF16))
    if win_tiles:
        skipped = table_tiles - win_tiles
        wblk = pl.BlockSpec((1, ATT_HEADS, ATT_HEAD_DIM, ROW_TILE),
                            lambda i: (i // table_tiles, 0, 0, jnp.maximum(i % table_tiles - skipped, 0)))
        wsds = jax.ShapeDtypeStruct((m // (table_tiles * ROW_TILE), ATT_HEADS, ATT_HEAD_DIM,
                                     win_tiles * ROW_TILE), F32)
        out_specs += [wblk, wblk]
        out_shape += [wsds, wsds]
    return pl.pallas_call(
        functools.partial(_proj_body, emit=emit, window=bool(win_tiles)),
        grid=(m // ROW_TILE,),
        in_specs=[row, _const_spec((D_MODEL, IN_COLS)), tab, tab, _const_spec((2, HG_WIDTH))],
        out_specs=out_specs,
        out_shape=out_shape,
        compiler_params=pltpu.CompilerParams(dimension_semantics=("arbitrary",),
                                             vmem_limit_bytes=VMEM_LIMIT),
        name="proj",
    )(x, w_in, cos_t, sin_t, lb_raw)


ATT_UNROLL = (8, 8, 8)
ATT_COMBINE_ROWS = 512


def _attn_prompt_body(q_ref, k_ref, v_ref, o_ref, acc_ref, m_ref, l_ref, *, seq):
    blk = ATT_BLOCK
    lane = lax.broadcasted_iota(jnp.int32, (1, LANES), 1)
    head0 = lane < ATT_HEAD_DIM
    qi = lax.broadcasted_iota(jnp.int32, (blk, 2 * blk), 0)
    kc = lax.broadcasted_iota(jnp.int32, (blk, 2 * blk), 1)
    dist = qi + blk - kc
    first_mask = (lax.broadcasted_iota(jnp.int32, (blk, blk), 1)
                  <= lax.broadcasted_iota(jnp.int32, (blk, blk), 0))
    nt = (((1,), (1,)), ((), ()))
    log2e = 1.4426950408889634

    for br, (window, dil) in enumerate(DILATIONS):
        w_sub = window // dil
        rel = (dist >= 0) & (dist <= w_sub)
        span = blk * dil
        n_blk = seq // span
        n_units = ATT_UNROLL[br]
        run = min(n_units, n_blk)
        n_res = n_units // run
        runs_per_res = n_blk // run

        def units(it, carry, br=br, dil=dil, rel=rel, span=span, run=run, n_res=n_res,
                  runs_per_res=runs_per_res, n_units=n_units):
            starts, masks, vbs, scores = [], [], [], []
            for rr in range(n_res):
                if runs_per_res == 1:
                    r, n0 = it * n_res + rr, 0
                else:
                    r, n0 = it // runs_per_res, (it % runs_per_res) * run
                base = r + n0 * span

                def kv_block(j):
                    rows = pl.ds(jnp.maximum(base + j * span, r), blk, stride=dil)
                    return k_ref[rows, :].astype(BF16), v_ref[rows, :].astype(BF16)

                prev = None if runs_per_res == 1 else kv_block(-1)
                for j in range(run):
                    cur = kv_block(j)
                    start = base + j * span
                    q = q_ref[pl.ds(start, blk, stride=dil), :] * log2e
                    if prev is None:
                        kb, vb, mask = cur[0], cur[1], first_mask
                    else:
                        kb = jnp.concatenate([prev[0], cur[0]], axis=0)
                        vb = jnp.concatenate([prev[1], cur[1]], axis=0)
                        mask = rel & (kc >= jnp.where(n0 > 0, 0, blk)) if (j == 0 and runs_per_res > 1) else rel
                    starts.append(start)
                    masks.append(mask)
                    vbs.append(vb)
                    for hsel in (head0, ~head0):
                        qh = jnp.where(hsel, q, 0.0).astype(BF16)
                        scores.append(lax.dot_general(qh, kb, nt, preferred_element_type=F32))
                    prev = cur
            probs, ls, ms = [], [], []
            for i, s in enumerate(scores):
                s = jnp.where(masks[i // 2], s, NEG_INF)
                m = jnp.max(s, axis=-1, keepdims=True)
                p = jnp.exp2(s - m)
                ls.append(jnp.sum(p, axis=-1, keepdims=True))
                ms.append(m)
                probs.append(p.astype(BF16))
            outs = [jnp.dot(p, vbs[i // 2], preferred_element_type=F32) for i, p in enumerate(probs)]
            for uu in range(n_units):
                i0, i1 = 2 * uu, 2 * uu + 1
                rows = pl.ds(starts[uu], blk, stride=dil)
                acc_ref[br, rows, :] = jnp.where(head0, outs[i0], outs[i1])
                m_ref[br, rows, :] = jnp.broadcast_to(jnp.where(head0, ms[i0], ms[i1]), (blk, LANES))
                l_ref[br, rows, :] = jnp.broadcast_to(jnp.where(head0, ls[i0], ls[i1]), (blk, LANES))
            return carry

        lax.fori_loop(0, (dil * n_blk) // n_units, units, 0)

    def combine(c, carry):
        rows = pl.ds(pl.multiple_of(c * ATT_COMBINE_ROWS, ATT_COMBINE_ROWS), ATT_COMBINE_ROWS)
        m0, m1, m2 = m_ref[0, rows, :], m_ref[1, rows, :], m_ref[2, rows, :]
        mx = jnp.maximum(jnp.maximum(m0, m1), m2)
        w0, w1, w2 = jnp.exp2(m0 - mx), jnp.exp2(m1 - mx), jnp.exp2(m2 - mx)
        num = w0 * acc_ref[0, rows, :] + w1 * acc_ref[1, rows, :] + w2 * acc_ref[2, rows, :]
        den = w0 * l_ref[0, rows, :] + w1 * l_ref[1, rows, :] + w2 * l_ref[2, rows, :]
        o_ref[rows, :] = num / den
        return carry

    lax.fori_loop(0, seq // ATT_COMBINE_ROWS, combine, 0)


def attn_prompt(q, k, v, batch, seq):
    n_hp = ATT_WIDTH // LANES
    blk = pl.BlockSpec((seq, LANES), lambda b, h: (b, h))
    return pl.pallas_call(
        functools.partial(_attn_prompt_body, seq=seq),
        grid=(batch, n_hp),
        in_specs=[blk, blk, blk],
        out_specs=blk,
        out_shape=jax.ShapeDtypeStruct((batch * seq, ATT_WIDTH), F32),
        scratch_shapes=[pltpu.VMEM((3, seq, LANES), F32)] * 3,
        compiler_params=pltpu.CompilerParams(dimension_semantics=("arbitrary", "arbitrary"),
                                             vmem_limit_bytes=VMEM_LIMIT),
        name="attn_prompt",
    )(q, k, v)


def _hgrn_prompt_body(*refs, rows_per_step, n_cast):
    q_ref, k_ref, v_ref, lf_ref, gh_ref, ng_ref = refs[:6]
    o_ref, s_ref = refs[6 + n_cast:8 + n_cast]
    st_ref = refs[-1]
    for w_ref, wb_ref in zip(refs[6:6 + n_cast], refs[8 + n_cast:8 + 2 * n_cast]):
        wb_ref[...] = w_ref[...].astype(BF16)
    grp = LANES
    n_ch = grp // HG_CHUNK
    ri = lax.broadcasted_iota(jnp.int32, (grp, grp), 0)
    ci = lax.broadcasted_iota(jnp.int32, (grp, grp), 1)
    same_chunk = (ri // HG_CHUNK) == (ci // HG_CHUNK)
    causal = same_chunk & (ci <= ri)
    summat = jnp.where(causal, 1.0, 0.0).astype(BF16)
    nt = (((1,), (1,)), ((), ()))
    tb = pl.program_id(1)

    @pl.when(tb == 0)
    def _():
        st_ref[...] = jnp.zeros(st_ref.shape, F32)

    heads = [slice(h * LANES, (h + 1) * LANES) for h in range(HG_HEADS)]
    in_chunk = [(ri // HG_CHUNK) == c for c in range(n_ch)]

    def by_chunk(x):
        return jnp.concatenate([jnp.where(m, x, 0.0).astype(BF16) for m in in_chunk], axis=1)

    def groups(it, carry):
        gs = range(HG_GROUPS)
        rows = [pl.ds(pl.multiple_of((it * HG_GROUPS + g) * grp, grp), grp) for g in gs]
        cum = []
        for g in gs:
            parts = _split3(lf_ref[rows[g], :])
            cum.append(jnp.dot(summat, parts[0], preferred_element_type=F32)
                       + jnp.dot(summat, parts[1], preferred_element_type=F32)
                       + jnp.dot(summat, parts[2], preferred_element_type=F32))
        q_dec, decay, a, vs, k_end = [], [], [], [], []
        for g in gs:
            cum_end = jnp.concatenate(
                [jnp.broadcast_to(cum[g][(c + 1) * HG_CHUNK - 1:(c + 1) * HG_CHUNK, :], (HG_CHUNK, HG_WIDTH))
                 for c in range(n_ch)], axis=0)
            q, k, v = q_ref[rows[g], :], k_ref[rows[g], :], v_ref[rows[g], :]
            q_dec.append(q * jnp.exp(cum[g]))
            k_dec = (k * jnp.exp(-cum[g])).astype(BF16)
            k_end.append(k * jnp.exp(cum_end - cum[g]))
            decay.append(jnp.exp(cum_end))
            q_decb = q_dec[g].astype(BF16)
            vs.append(v)
            a.append([lax.dot_general(q_decb[:, hs], k_dec[:, hs], nt, preferred_element_type=F32)
                      for hs in heads])
        o = [[jnp.dot(jnp.where(causal, a[g][h], 0.0).astype(BF16), vs[g][:, hs].astype(BF16),
                      preferred_element_type=F32) for h, hs in enumerate(heads)] for g in gs]
        ut = [[jnp.dot(vs[g][:, hs].T.astype(BF16), by_chunk(k_end[g][:, hs]), preferred_element_type=F32)
               for hs in heads] for g in gs]
        starts = [[[] for _ in heads] for _ in gs]
        for h, hs in enumerate(heads):
            st = st_ref[h]
            for g in gs:
                for c in range(n_ch):
                    starts[g][h].append(st.astype(BF16))
                    st = (st * decay[g][c * HG_CHUNK:c * HG_CHUNK + 1, hs]
                          + ut[g][h][:, c * LANES:(c + 1) * LANES])
            st_ref[h] = st
        inter = [[lax.dot_general(by_chunk(q_dec[g][:, hs]), jnp.concatenate(starts[g][h], axis=1), nt,
                                  preferred_element_type=F32) for h, hs in enumerate(heads)] for g in gs]
        for g in gs:
            for h, hs in enumerate(heads):
                oh = o[g][h] + inter[g][h]
                ms = jnp.mean(oh * oh, axis=-1, keepdims=True)
                gate = gh_ref[rows[g], hs]
                o_ref[rows[g], hs] = oh * lax.rsqrt(ms + LN_EPS) * ng_ref[:, hs] * (gate * _sigmoid(gate))
        return carry

    lax.fori_loop(0, rows_per_step // (grp * HG_GROUPS), groups, 0)

    @pl.when(tb == pl.num_programs(1) - 1)
    def _():
        for h in range(HG_HEADS):
            s_ref[0, h] = st_ref[h].T


HG_ROWS = 1024
HG_GROUPS = 4


def hgrn_prompt(qh, kh, ih, lf, gh, norm_g, batch, seq, cast=()):
    n_tb = seq // HG_ROWS
    steps = batch * n_tb

    def rows(b, t):
        return (b * n_tb + t, 0)

    blk = pl.BlockSpec((HG_ROWS, HG_WIDTH), rows)
    cast_specs = [pl.BlockSpec((w.shape[0] // steps, w.shape[1]), rows) for w in cast]
    return pl.pallas_call(
        functools.partial(_hgrn_prompt_body, rows_per_step=HG_ROWS, n_cast=len(cast)),
        grid=(batch, n_tb),
        in_specs=[blk, blk, blk, blk, blk, _const_spec((1, HG_WIDTH))] + cast_specs,
        out_specs=[blk, pl.BlockSpec((1, HG_HEADS, HG_DK, HG_DV), lambda b, t: (b, 0, 0, 0))] + cast_specs,
        out_shape=[jax.ShapeDtypeStruct((batch * seq, HG_WIDTH), F32),
                   jax.ShapeDtypeStruct((batch, HG_HEADS, HG_DK, HG_DV), F32)]
                  + [jax.ShapeDtypeStruct(w.shape, BF16) for w in cast],
        scratch_shapes=[pltpu.VMEM((HG_HEADS, HG_DV, HG_DK), F32)],
        compiler_params=pltpu.CompilerParams(dimension_semantics=("arbitrary", "arbitrary"),
                                             vmem_limit_bytes=VMEM_LIMIT),
        name="hgrn_prompt",
    )(qh, kh, ih, lf, gh, norm_g, *cast)


FA_ROWS = 128
FA_HEADS = 4
FA_SLOTS = 4


def _divmod_pow2(s, n):
    assert n & (n - 1) == 0
    if isinstance(s, int):
        return s // n, s % n
    return lax.shift_right_logical(s, n.bit_length() - 1), s & (n - 1)


def _ffn_attn_body(*refs, n_new, past, b_off, mixed, n_steps):
    n_x = 10 if mixed else 1
    wg_ref, wu_ref, wd_ref, g_ref, b_ref, q_ref, kn_ref, vn_ref, kc_hbm, vc_hbm = refs[n_x:n_x + 10]
    y_ref, o_ref, nk_ref, nv_ref, kbuf, vbuf, sems = refs[-8:-1] if mixed else refs[-7:]
    halves = ATT_HEADS // FA_HEADS
    step = pl.program_id(0)

    split = _divmod_pow2
    odd = (b_off + split(step, halves)[0]) & 1 == 1

    def window_copies(s):
        slot = split(s, FA_SLOTS)[1]
        b_rel, half = split(s, halves)
        src = (b_off + b_rel, pl.ds(half * FA_HEADS, FA_HEADS))
        return (pltpu.make_async_copy(kc_hbm.at[src], kbuf.at[slot], sems.at[slot, 0]),
                pltpu.make_async_copy(vc_hbm.at[src], vbuf.at[slot], sems.at[slot, 1]))

    @pl.when(step == 0)
    def _():
        for s in range(min(FA_SLOTS - 1, n_steps)):
            for c in window_copies(s):
                c.start()

    @pl.when(step + (FA_SLOTS - 1) < n_steps)
    def _():
        for c in window_copies(step + (FA_SLOTS - 1)):
            c.start()

    for c in window_copies(step):
        c.wait()
    slot = split(step, FA_SLOTS)[1]
    kc_ref, vc_ref = kbuf.at[slot], vbuf.at[slot]

    def new_rows(ref):
        both = ref[...]
        return jnp.where(odd, both[n_new:2 * n_new], both[0:n_new])

    q_new, k_new, v_new = new_rows(q_ref), new_rows(kn_ref), new_rows(vn_ref)
    head_cols = [slice(h * ATT_HEAD_DIM, (h + 1) * ATT_HEAD_DIM) for h in range(FA_HEADS)]
    srow = lax.broadcasted_iota(jnp.int32, (n_new, past), 0)
    tcol = lax.broadcasted_iota(jnp.int32, (n_new, past), 1)
    dist = past + srow - tcol
    def on_stride(d, dil):
        return (d & (dil - 1)) == 0 if dil & (dil - 1) == 0 else d % dil == 0

    cnt = jnp.zeros((n_new, past), F32)
    for window, dil in DILATIONS:
        cnt = cnt + jnp.where(on_stride(dist, dil) & (dist <= window), 1.0, 0.0)
    si = lax.broadcasted_iota(jnp.int32, (n_new, n_new), 0)
    sj = lax.broadcasted_iota(jnp.int32, (n_new, n_new), 1)
    dn = si - sj
    cnt_new = jnp.zeros((n_new, n_new), F32)
    for window, dil in DILATIONS:
        cnt_new = cnt_new + jnp.where((dn >= 0) & on_stride(dn, dil) & (dn <= window), 1.0, 0.0)
    lane = lax.broadcasted_iota(jnp.int32, (1, LANES), 1)
    nt = (((1,), (1,)), ((), ()))
    shift = past - n_new
    hs = range(FA_HEADS)

    def scores():
        qs = [q_new[:, c].astype(BF16) for c in head_cols]
        s_c = [jnp.dot(qs[h], kc_ref[h].astype(BF16), preferred_element_type=F32) for h in hs]
        s_n = [lax.dot_general(qs[h], k_new[:, head_cols[h]].astype(BF16), nt, preferred_element_type=F32)
               for h in hs]
        return s_c, s_n

    if mixed:
        first, nxt = refs[0:3], refs[3:6]
        wa_ref, wh_ref, g2_ref, b2_ref = refs[6:n_x]
        xin_ref = refs[-1]

        def mixed_rows(x1_ref, oa_ref, oh_ref):
            mix = jnp.dot(oa_ref[...].astype(BF16), wa_ref[...], preferred_element_type=F32)
            mix = mix + jnp.dot(oh_ref[...].astype(BF16), wh_ref[...], preferred_element_type=F32)
            return _layer_norm(DN_ALPHA * x1_ref[...] + mix, g2_ref[...], b2_ref[...])

        @pl.when(step == 0)
        def _():
            xin_ref[...] = mixed_rows(*first)

        x = xin_ref[...]
        x_next = mixed_rows(*nxt)
    else:
        x = refs[0][...]
    xb = x.astype(BF16)
    gate = jnp.dot(xb, wg_ref[...], preferred_element_type=F32)
    up = jnp.dot(xb, wu_ref[...], preferred_element_type=F32)
    s_c, s_n = scores()
    knt = [k_new[:, c].T for c in head_cols]
    vnt = [v_new[:, c].T for c in head_cols]

    hmid = (gate * _sigmoid(gate) * up).astype(BF16)
    y = jnp.dot(hmid, wd_ref[...], preferred_element_type=F32)
    y_ref[...] = _layer_norm(DN_ALPHA * x + 0.5 * y, g_ref[...], b_ref[...])
    if mixed:
        xin_ref[...] = x_next

    p_c, p_n, den = [], [], []
    for h in hs:
        sc = jnp.where(cnt > 0, s_c[h], NEG_INF)
        sn = jnp.where(cnt_new > 0, s_n[h], NEG_INF)
        m = jnp.maximum(jnp.max(sc, axis=-1, keepdims=True), jnp.max(sn, axis=-1, keepdims=True))
        p_c.append((cnt * jnp.exp(sc - m)).astype(BF16))
        p_n.append((cnt_new * jnp.exp(sn - m)).astype(BF16))
        den.append(jnp.sum(p_c[h].astype(F32), axis=-1, keepdims=True)
                   + jnp.sum(p_n[h].astype(F32), axis=-1, keepdims=True))
    outs = []
    for h in hs:
        num = (lax.dot_general(p_c[h], vc_ref[h].astype(BF16), nt, preferred_element_type=F32)
               + jnp.dot(p_n[h], v_new[:, head_cols[h]].astype(BF16), preferred_element_type=F32))
        outs.append(num / den[h])
    o_ref[0] = jnp.concatenate(outs, axis=1)

    for h in hs:
        for src_ref, new, dst in ((kc_ref, knt[h], nk_ref), (vc_ref, vnt[h], nv_ref)):
            rolled = pltpu.roll(src_ref[h], shift, 1)
            dst[0, h, :, 0:past - LANES] = rolled[:, 0:past - LANES]
            last = rolled[:, past - LANES:past]
            for s in range(n_new):
                col = jnp.broadcast_to(new[:, s:s + 1], last.shape)
                last = jnp.where(lane == LANES - n_new + s, col, last)
            dst[0, h, :, past - LANES:past] = last


def ffn_attn(x, wg, wu, wd, g, b, q, kn, vn, kc, vc, n_new, b_off, prev=None, mix=None):
    m = x.shape[0]
    bsz, n_heads, _, past = kc.shape
    halves = n_heads // FA_HEADS
    steps = m // FA_ROWS
    assert b_off % 2 == 0 and bsz % 2 == 0, "new-token rows are fetched per pair of batch rows"

    def where(i):
        b_rel, half = _divmod_pow2(i, halves)
        return b_off + b_rel, half

    def samp(i):
        return (*where(i), 0, 0)

    row = pl.BlockSpec((FA_ROWS, D_MODEL), lambda i: (i, 0))
    pair = pl.BlockSpec((2 * n_new, FA_HEADS * ATT_HEAD_DIM),
                        lambda i: (_divmod_pow2(where(i)[0], 2)[0], where(i)[1]))
    o_blk = pl.BlockSpec((1, n_new, FA_HEADS * ATT_HEAD_DIM), lambda i: (where(i)[0], 0, where(i)[1]))
    big = pl.BlockSpec((1, FA_HEADS, ATT_HEAD_DIM, past), samp)
    if mix is None:
        in_specs, args = [row], [x]
    else:
        o_att, o_hg, w_out, ln_g, ln_b = mix
        in_specs, args = [], []
        for index in (lambda i: (0, 0), lambda i: (jnp.minimum(i + 1, steps - 1), 0)):
            in_specs += [pl.BlockSpec((FA_ROWS, D_MODEL), index), pl.BlockSpec((FA_ROWS, ATT_WIDTH), index),
                         pl.BlockSpec((FA_ROWS, HG_WIDTH), index)]
            args += [x, o_att, o_hg]
        in_specs += _w_out_specs() + [_const_spec((1, D_MODEL)), _const_spec((1, D_MODEL))]
        args += [w_out, w_out, ln_g, ln_b]
    hbm = pl.BlockSpec(memory_space=pl.ANY)
    in_specs += [_const_spec((D_MODEL, D_FF)), _const_spec((D_MODEL, D_FF)), _const_spec((D_FF, D_MODEL)),
                 _const_spec((1, D_MODEL)), _const_spec((1, D_MODEL)), pair, pair, pair, hbm, hbm]
    args += [wg, wu, wd, g, b, q, kn, vn, kc, vc]
    aliases = {}
    if prev is not None:
        for j, a in enumerate(prev):
            aliases[len(args)] = 1 + j
            in_specs.append(pl.BlockSpec(memory_space=pl.ANY))
            args.append(a)
    big_sds = jax.ShapeDtypeStruct(kc.shape, F32)
    return pl.pallas_call(
        functools.partial(_ffn_attn_body, n_new=n_new, past=past, b_off=b_off, mixed=mix is not None,
                          n_steps=steps),
        grid=(steps,),
        in_specs=in_specs,
        out_specs=[row, o_blk, big, big],
        out_shape=[jax.ShapeDtypeStruct((m, D_MODEL), F32),
                   jax.ShapeDtypeStruct((bsz, n_new, n_heads * ATT_HEAD_DIM), F32), big_sds, big_sds],
        scratch_shapes=[pltpu.VMEM((FA_SLOTS, FA_HEADS, ATT_HEAD_DIM, past), F32),
                        pltpu.VMEM((FA_SLOTS, FA_HEADS, ATT_HEAD_DIM, past), F32),
                        pltpu.SemaphoreType.DMA((FA_SLOTS, 2))]
                       + ([pltpu.VMEM((FA_ROWS, D_MODEL), F32)] if mix is not None else []),
        input_output_aliases=aliases,
        compiler_params=pltpu.CompilerParams(dimension_semantics=("arbitrary",),
                                             vmem_limit_bytes=VMEM_LIMIT),
        name="ffn_attn",
    )(*args)


HS_BATCH = 8


def _hgrn_sample_body(q_ref, k_ref, v_ref, lf_ref, gh_ref, ng_ref, s0_ref, o_ref, s_ref, *, n_new):
    row = lax.broadcasted_iota(jnp.int32, (n_new, 1), 0)
    tn = (((0,), (0,)), ((), ()))
    ones12 = jnp.ones((3 * n_new, HG_DV), BF16)

    sub = 8
    per_tile = sub // n_new

    def tile(t, carry):
        rows = pl.ds(pl.multiple_of(t * sub, sub), sub)
        units = []
        for h in range(HG_HEADS):
            cols = slice(h * LANES, (h + 1) * LANES)
            q8, k8, v8, g8 = q_ref[rows, cols], k_ref[rows, cols], v_ref[rows, cols], lf_ref[rows, cols]
            for i in range(per_tile):
                part = slice(i * n_new, (i + 1) * n_new)
                q, k, v, g = q8[part], k8[part], v8[part], g8[part]
                cum = jnp.zeros((n_new, LANES), F32)
                for j in range(n_new):
                    cum = cum + jnp.where(row >= j, g[j:j + 1, :], 0.0)
                cum_end = cum[n_new - 1:n_new, :]
                units.append(dict(h=h, b=t * per_tile + i, v=v, g=g, q_dec=q * jnp.exp(cum),
                                  k_dec=k * jnp.exp(-cum), k_end=k * jnp.exp(cum_end - cum)))
        for u in units:
            u["s0"] = s0_ref[u["b"], u["h"]]
            u["o"] = jnp.dot(u["q_dec"].astype(BF16), u["s0"].astype(BF16), preferred_element_type=F32)
        for u in units:
            dec = lax.dot_general(jnp.concatenate(_split3(u["g"]), axis=0), ones12, tn,
                                  preferred_element_type=F32)
            upd = lax.dot_general(u["k_end"].astype(BF16), u["v"].astype(BF16), tn,
                                  preferred_element_type=F32)
            s_ref[u["b"], u["h"]] = jnp.exp(dec) * u["s0"] + upd
        for h in range(HG_HEADS):
            cols = slice(h * LANES, (h + 1) * LANES)
            outs = []
            for u in units[h * per_tile:(h + 1) * per_tile]:
                o = u["o"]
                for j in range(n_new):
                    a_j = jnp.sum(u["q_dec"] * u["k_dec"][j:j + 1, :], axis=-1, keepdims=True)
                    o = o + jnp.where(row >= j, a_j, 0.0) * u["v"][j:j + 1, :]
                outs.append(o * lax.rsqrt(jnp.mean(o * o, axis=-1, keepdims=True) + LN_EPS))
            gate = gh_ref[rows, cols]
            o_ref[rows, cols] = jnp.concatenate(outs, axis=0) * ng_ref[:, cols] * (gate * _sigmoid(gate))
        return carry

    lax.fori_loop(0, HS_BATCH // per_tile, tile, 0)


def hgrn_sample(qh, kh, ih, lf, gh, norm_g, s0, n_new):
    bsz = s0.shape[0]
    assert bsz % HS_BATCH == 0 and 8 % n_new == 0, "whole grid steps of whole 8-row tiles"
    blk = pl.BlockSpec((HS_BATCH * n_new, HG_WIDTH), lambda i: (i, 0))
    sblk = pl.BlockSpec((HS_BATCH, HG_HEADS, HG_DK, HG_DV), lambda i: (i, 0, 0, 0))
    return pl.pallas_call(
        functools.partial(_hgrn_sample_body, n_new=n_new),
        grid=(bsz // HS_BATCH,),
        in_specs=[blk, blk, blk, blk, blk, _const_spec((1, HG_WIDTH)), sblk],
        out_specs=[blk, sblk],
        out_shape=[jax.ShapeDtypeStruct((bsz * n_new, HG_WIDTH), F32),
                   jax.ShapeDtypeStruct(s0.shape, F32)],
        compiler_params=pltpu.CompilerParams(dimension_semantics=("arbitrary",),
                                             vmem_limit_bytes=VMEM_LIMIT),
        name="hgrn_sample",
    )(qh, kh, ih, lf, gh, norm_g, s0)


def _rope_tables(pos):
    half = ATT_HEAD_DIM // 2
    inv = ROPE_THETA ** (-jnp.arange(half, dtype=F32) / half)
    ang = pos.astype(F32)[:, None] * inv[None, :]
    cos, sin = jnp.cos(ang), jnp.sin(ang)
    return jnp.tile(cos, (1, 4)), jnp.tile(jnp.concatenate([-sin, sin], axis=1), (1, 2))


def kernel(x_prompt, x_sample, cache_k, cache_v, state_hgrn, ffn1_w_gate, ffn1_w_up, ffn1_w_down, ln1_g, ln1_b, w_in, hg_lower_bound, hg_norm_g, w_out, ln2_g, ln2_b, ffn2_w_gate, ffn2_w_up, ffn2_w_down, ln3_g, ln3_b):
    batch, seq, _ = x_prompt.shape
    dec_batch, n_new, _ = x_sample.shape
    assert ffn1_w_gate.shape[0] == 1, "single layer"
    assert seq % (ATT_BLOCK * max(d for _, d in DILATIONS)) == 0 and seq % HG_ROWS == 0 and seq % ROW_TILE == 0
    assert dec_batch * n_new == ROW_TILE, "the sample rows form one row tile"


    cos_p, sin_p = _rope_tables(jnp.arange(seq, dtype=jnp.int32))
    pos_s = PAST_LEN + (jnp.arange(dec_batch * n_new, dtype=jnp.int32) % n_new)
    cos_s, sin_s = _rope_tables(pos_s)

    xp = x_prompt.reshape(batch * seq, D_MODEL)
    xs = x_sample.reshape(dec_batch * n_new, D_MODEL)

    xs1, *w1 = ffn_ln(xs, ffn1_w_gate[0], ffn1_w_up[0], ffn1_w_down[0], ln1_g, ln1_b)
    qs, ks, vs, qhs, khs, ihs, lfs, ghs, w_in_b = proj(xs1, w_in[0], cos_s, sin_s, hg_lower_bound, 1)

    samp = (qs, ks, vs,
            cache_k[0].transpose(0, 2, 3, 1),
            cache_v[0].transpose(0, 2, 3, 1), n_new)
    steps = (batch * seq) // FA_ROWS
    per_call = steps // (ATT_HEADS // FA_HEADS)
    assert 2 * per_call == dec_batch, "the two prompt FFN calls must cover the sample batch"

    xp1, o_t, nk, nv = ffn_attn(xp, *w1, ln1_g, ln1_b, *samp, 0)
    win_p = min(ATT_WINDOW_MAX, seq)
    q, k, v, qh, kh, ih, lf, gh, kt_p, vt_p = proj(xp1, w_in_b, cos_p, sin_p, hg_lower_bound, seq // ROW_TILE,
                                                   win_tiles=win_p // ROW_TILE)
    o_att = attn_prompt(q, k, v, batch, seq)
    o_hg, s_p, *w2, w_out_b = hgrn_prompt(qh, kh, ih, lf, gh, hg_norm_g, batch, seq,
                                          cast=(ffn2_w_gate[0], ffn2_w_up[0], ffn2_w_down[0], w_out[0]))
    yp, o_t, nk, nv = ffn_attn(xp1, *w2, ln3_g, ln3_b, *samp, per_call, prev=(o_t, nk, nv),
                               mix=(o_att, o_hg, w_out_b, ln2_g, ln2_b))

    o_att_s = o_t.reshape(dec_batch * n_new, ATT_WIDTH)

    o_hg_s, s_s = hgrn_sample(qhs, khs, ihs, lfs, ghs, hg_norm_g, state_hgrn[0], n_new)
    ys, = ffn_ln(xs1, *w2, ln3_g, ln3_b, mix=(o_att_s, o_hg_s, w_out_b, ln2_g, ln2_b))

    return (yp.reshape(batch, seq, D_MODEL),
            ys.reshape(dec_batch, n_new, D_MODEL),
            kt_p.transpose(0, 3, 1, 2)[None],
            vt_p.transpose(0, 3, 1, 2)[None],
            s_p[None],
            nk.transpose(0, 3, 1, 2)[None],
            nv.transpose(0, 3, 1, 2)[None],
            s_s[None])
```

```python
import functools

import jax
import jax.numpy as jnp
from jax import lax
from jax.experimental import pallas as pl
from jax.experimental.pallas import tpu as pltpu

F32 = jnp.float32
BF16 = jnp.bfloat16

D_MODEL = 1024
ATT_WIDTH = 512
HG_WIDTH = 512
ATT_HEAD_DIM = 64
ATT_HEADS = 8
DILATIONS = ((128, 1), (512, 4), (2048, 16))
ATT_WINDOW_MAX = 2048
ATT_BLOCK = 128
ROPE_THETA = 10000.0
HG_HEADS = 4
HG_DK = 128
HG_DV = 128
HG_CHUNK = 32
D_FF = 2816
IN_COLS = 3 * ATT_WIDTH + 4 * HG_WIDTH
PAST_LEN = 8192
DN_ALPHA = 2.0 ** 0.25
LN_EPS = 1e-5
NEG_INF = -1e30

LANES = 128
ROW_TILE = 512
VMEM_LIMIT = 56 * 1024 * 1024


def _const_spec(shape, block=None):
    index = tuple(block) if block is not None else (0,) * len(shape)
    return pl.BlockSpec(shape, lambda *_: index, pipeline_mode=pl.Buffered(1))


def _layer_norm(r, g, b):
    mu = jnp.mean(r, axis=-1, keepdims=True)
    c = r - mu
    var = jnp.mean(c * c, axis=-1, keepdims=True)
    return c * lax.rsqrt(var + LN_EPS) * g + b


def _sigmoid(x):
    return 1.0 / (1.0 + jnp.exp(-x))


def _split3(x):
    hi = x.astype(BF16)
    r1 = x - hi.astype(F32)
    mid = r1.astype(BF16)
    lo = (r1 - mid.astype(F32)).astype(BF16)
    return hi, mid, lo


def _w_out_specs():
    assert ATT_WIDTH == HG_WIDTH
    return [_const_spec((ATT_WIDTH, D_MODEL), (0, 0)), _const_spec((HG_WIDTH, D_MODEL), (1, 0))]


FFN_CHUNK = 256


def _ffn_ln_body(*refs, cast, mixed):
    n_x = 7 if mixed else 1
    wg_ref, wu_ref, wd_ref, g_ref, b_ref, o_ref = refs[n_x:n_x + 6]
    xin_ref, xb_ref, acc_ref = refs[-3:]
    j = pl.program_id(0)

    @pl.when(j == 0)
    def _():
        if mixed:
            x1_ref, oa_ref, oh_ref, wa_ref, wh_ref, g2_ref, b2_ref = refs[:n_x]
            mix = jnp.dot(oa_ref[...].astype(BF16), wa_ref[...], preferred_element_type=F32)
            mix = mix + jnp.dot(oh_ref[...].astype(BF16), wh_ref[...], preferred_element_type=F32)
            x = _layer_norm(DN_ALPHA * x1_ref[...] + mix, g2_ref[...], b2_ref[...])
        else:
            x = refs[0][...]
        xin_ref[...] = x
        xb_ref[...] = x.astype(BF16)
        acc_ref[...] = jnp.zeros(acc_ref.shape, F32)

    wgb, wub, wdb = wg_ref[...], wu_ref[...], wd_ref[...]
    if cast:
        wgb, wub, wdb = wgb.astype(BF16), wub.astype(BF16), wdb.astype(BF16)
        for dst, w in zip(refs[n_x + 6:n_x + 9], (wgb, wub, wdb)):
            dst[...] = w
    xb = xb_ref[...]
    gate = jnp.dot(xb, wgb, preferred_element_type=F32)
    up = jnp.dot(xb, wub, preferred_element_type=F32)
    h = (gate * _sigmoid(gate) * up).astype(BF16)
    acc_ref[...] += jnp.dot(h, wdb, preferred_element_type=F32)

    @pl.when(j == pl.num_programs(0) - 1)
    def _():
        o_ref[...] = _layer_norm(DN_ALPHA * xin_ref[...] + 0.5 * acc_ref[...], g_ref[...], b_ref[...])


def ffn_ln(x, wg, wu, wd, g, b, mix=None):
    m = x.shape[0]
    cast = wg.dtype == F32
    col = pl.BlockSpec((D_MODEL, FFN_CHUNK), lambda j: (0, j))
    rowc = pl.BlockSpec((FFN_CHUNK, D_MODEL), lambda j: (j, 0))
    xs = pl.BlockSpec((m, D_MODEL), lambda j: (0, 0))
    in_specs, args = [xs], [x]
    if mix is not None:
        half = pl.BlockSpec((m, ATT_WIDTH), lambda j: (0, 0))
        in_specs += [half, half] + _w_out_specs() + [_const_spec((1, D_MODEL)), _const_spec((1, D_MODEL))]
        o_att, o_hg, w_out, ln_g, ln_b = mix
        args += [o_att, o_hg, w_out, w_out, ln_g, ln_b]
    in_specs += [col, col, rowc, _const_spec((1, D_MODEL)), _const_spec((1, D_MODEL))]
    args += [wg, wu, wd, g, b]
    out_specs = [xs] + ([col, col, rowc] if cast else [])
    out_shape = [jax.ShapeDtypeStruct((m, D_MODEL), F32)]
    if cast:
        out_shape += [jax.ShapeDtypeStruct(w.shape, BF16) for w in (wg, wu, wd)]
    return pl.pallas_call(
        functools.partial(_ffn_ln_body, cast=cast, mixed=mix is not None),
        grid=(D_FF // FFN_CHUNK,),
        in_specs=in_specs,
        out_specs=out_specs,
        out_shape=out_shape,
        scratch_shapes=[pltpu.VMEM((m, D_MODEL), F32), pltpu.VMEM((m, D_MODEL), BF16),
                        pltpu.VMEM((m, D_MODEL), F32)],
        compiler_params=pltpu.CompilerParams(dimension_semantics=("arbitrary",),
                                             vmem_limit_bytes=VMEM_LIMIT),
        name="ffn_ln",
    )(*args)


def _rope(z, cos_t, sin_t):
    lane = lax.broadcasted_iota(jnp.int32, (1, LANES), 1)
    first_half = (lane % ATT_HEAD_DIM) < (ATT_HEAD_DIM // 2)
    outs = []
    for c in range(z.shape[1] // LANES):
        zc = z[:, c * LANES:(c + 1) * LANES]
        partner = jnp.where(first_half, pltpu.roll(zc, LANES - 32, 1), pltpu.roll(zc, 32, 1))
        outs.append(zc * cos_t + partner * sin_t)
    return outs


def _proj_body(x_ref, w_ref, cos_ref, sin_ref, lbraw_ref,
               q_ref, k_ref, v_ref, qh_ref, kh_ref, ih_ref, lf_ref, gh_ref, *extra, emit, window):
    wb_ref = extra[0] if emit else None
    kt_ref, vt_ref = extra[-2:] if window else (None, None)
    xb = x_ref[...].astype(BF16)
    aw = ATT_WIDTH
    base = 3 * aw

    def transposed(ref, c, val):
        ref[0, 2 * c:2 * c + 2] = val.T.reshape(2, ATT_HEAD_DIM, val.shape[0])

    def cols(start, width):
        w = w_ref[:, start:start + width]
        if wb_ref is not None:
            w = w.astype(BF16)
            wb_ref[:, start:start + width] = w
        return jnp.dot(xb, w, preferred_element_type=F32)

    cos_t = cos_ref[...]
    sin_t = sin_ref[...]
    for c, val in enumerate(_rope(cols(0, aw), cos_t, sin_t)):
        q_ref[:, c * LANES:(c + 1) * LANES] = val * (ATT_HEAD_DIM ** -0.5)
    for c, val in enumerate(_rope(cols(aw, aw), cos_t, sin_t)):
        k_ref[:, c * LANES:(c + 1) * LANES] = val
        if window:
            transposed(kt_ref, c, val)
    lbr = lbraw_ref[...]
    mx = jnp.max(lbr, axis=0, keepdims=True)
    ex = jnp.exp(lbr - mx)
    lb = ex[0:1, :] / jnp.sum(ex, axis=0, keepdims=True)
    zf = cols(base + HG_WIDTH, HG_WIDTH)
    lf_ref[...] = jnp.log(lb + (1.0 - lb) * _sigmoid(zf))
    kh_ref[...] = (1.0 - lb) * _sigmoid(-zf)
    zq = cols(base, HG_WIDTH)
    qh_ref[...] = zq * _sigmoid(zq)
    v = cols(2 * aw, aw)
    v_ref[...] = v
    if window:
        for c in range(aw // LANES):
            transposed(vt_ref, c, v[:, c * LANES:(c + 1) * LANES])
    ih_ref[...] = cols(base + 2 * HG_WIDTH, HG_WIDTH)
    gh_ref[...] = cols(base + 3 * HG_WIDTH, HG_WIDTH)


def proj(x, w_in, cos_t, sin_t, lb_raw, table_tiles, win_tiles=0):
    m = x.shape[0]
    emit = w_in.dtype == F32
    assert not emit or m == ROW_TILE, "the weight copy is written once, by a single grid step"
    row = pl.BlockSpec((ROW_TILE, D_MODEL), lambda i: (i, 0))
    tab = pl.BlockSpec((ROW_TILE, LANES), lambda i: (i % table_tiles, 0))
    out = pl.BlockSpec((ROW_TILE, ATT_WIDTH), lambda i: (i, 0))
    sds = jax.ShapeDtypeStruct((m, ATT_WIDTH), F32)
    out_specs, out_shape = [out] * 8, [sds] * 8
    if emit:
        out_specs.append(pl.BlockSpec((D_MODEL, IN_COLS), lambda i: (0, 0)))
        out_shape.append(jax.ShapeDtypeStruct(w_in.shape, BF16))
    if win_tiles:
        skipped = table_tiles - win_tiles
        wblk = pl.BlockSpec((1, ATT_HEADS, ATT_HEAD_DIM, ROW_TILE),
                            lambda i: (i // table_tiles, 0, 0, jnp.maximum(i % table_tiles - skipped, 0)))
        wsds = jax.ShapeDtypeStruct((m // (table_tiles * ROW_TILE), ATT_HEADS, ATT_HEAD_DIM,
                                     win_tiles * ROW_TILE), F32)
        out_specs += [wblk, wblk]
        out_shape += [wsds, wsds]
    return pl.pallas_call(
        functools.partial(_proj_body, emit=emit, window=bool(win_tiles)),
        grid=(m // ROW_TILE,),
        in_specs=[row, _const_spec((D_MODEL, IN_COLS)), tab, tab, _const_spec((2, HG_WIDTH))],
        out_specs=out_specs,
        out_shape=out_shape,
        compiler_params=pltpu.CompilerParams(dimension_semantics=("arbitrary",),
                                             vmem_limit_bytes=VMEM_LIMIT),
        name="proj",
    )(x, w_in, cos_t, sin_t, lb_raw)


ATT_UNROLL = (8, 8, 8)
ATT_COMBINE_ROWS = 512


def _attn_prompt_body(q_ref, k_ref, v_ref, o_ref, acc_ref, m_ref, l_ref, *, seq):
    blk = ATT_BLOCK
    lane = lax.broadcasted_iota(jnp.int32, (1, LANES), 1)
    head0 = lane < ATT_HEAD_DIM
    qi = lax.broadcasted_iota(jnp.int32, (blk, 2 * blk), 0)
    kc = lax.broadcasted_iota(jnp.int32, (blk, 2 * blk), 1)
    dist = qi + blk - kc
    first_mask = (lax.broadcasted_iota(jnp.int32, (blk, blk), 1)
                  <= lax.broadcasted_iota(jnp.int32, (blk, blk), 0))
    nt = (((1,), (1,)), ((), ()))
    log2e = 1.4426950408889634

    for br, (window, dil) in enumerate(DILATIONS):
        w_sub = window // dil
        rel = (dist >= 0) & (dist <= w_sub)
        span = blk * dil
        n_blk = seq // span
        n_units = ATT_UNROLL[br]
        run = min(n_units, n_blk)
        n_res = n_units // run
        runs_per_res = n_blk // run

        def units(it, carry, br=br, dil=dil, rel=rel, span=span, run=run, n_res=n_res,
                  runs_per_res=runs_per_res, n_units=n_units):
            starts, masks, vbs, scores = [], [], [], []
            for rr in range(n_res):
                if runs_per_res == 1:
                    r, n0 = it * n_res + rr, 0
                else:
                    r, n0 = it // runs_per_res, (it % runs_per_res) * run
                base = r + n0 * span

                def kv_block(j):
                    rows = pl.ds(jnp.maximum(base + j * span, r), blk, stride=dil)
                    return k_ref[rows, :].astype(BF16), v_ref[rows, :].astype(BF16)

                prev = None if runs_per_res == 1 else kv_block(-1)
                for j in range(run):
                    cur = kv_block(j)
                    start = base + j * span
                    q = q_ref[pl.ds(start, blk, stride=dil), :] * log2e
                    if prev is None:
                        kb, vb, mask = cur[0], cur[1], first_mask
                    else:
                        kb = jnp.concatenate([prev[0], cur[0]], axis=0)
                        vb = jnp.concatenate([prev[1], cur[1]], axis=0)
                        mask = rel & (kc >= jnp.where(n0 > 0, 0, blk)) if (j == 0 and runs_per_res > 1) else rel
                    starts.append(start)
                    masks.append(mask)
                    vbs.append(vb)
                    for hsel in (head0, ~head0):
                        qh = jnp.where(hsel, q, 0.0).astype(BF16)
                        scores.append(lax.dot_general(qh, kb, nt, preferred_element_type=F32))
                    prev = cur
            probs, ls, ms = [], [], []
            for i, s in enumerate(scores):
                s = jnp.where(masks[i // 2], s, NEG_INF)
                m = jnp.max(s, axis=-1, keepdims=True)
                p = jnp.exp2(s - m)
                ls.append(jnp.sum(p, axis=-1, keepdims=True))
                ms.append(m)
                probs.append(p.astype(BF16))
            outs = [jnp.dot(p, vbs[i // 2], preferred_element_type=F32) for i, p in enumerate(probs)]
            for uu in range(n_units):
                i0, i1 = 2 * uu, 2 * uu + 1
                rows = pl.ds(starts[uu], blk, stride=dil)
                acc_ref[br, rows, :] = jnp.where(head0, outs[i0], outs[i1])
                m_ref[br, rows, :] = jnp.broadcast_to(jnp.where(head0, ms[i0], ms[i1]), (blk, LANES))
                l_ref[br, rows, :] = jnp.broadcast_to(jnp.where(head0, ls[i0], ls[i1]), (blk, LANES))
            return carry

        lax.fori_loop(0, (dil * n_blk) // n_units, units, 0)

    def combine(c, carry):
        rows = pl.ds(pl.multiple_of(c * ATT_COMBINE_ROWS, ATT_COMBINE_ROWS), ATT_COMBINE_ROWS)
        m0, m1, m2 = m_ref[0, rows, :], m_ref[1, rows, :], m_ref[2, rows, :]
        mx = jnp.maximum(jnp.maximum(m0, m1), m2)
        w0, w1, w2 = jnp.exp2(m0 - mx), jnp.exp2(m1 - mx), jnp.exp2(m2 - mx)
        num = w0 * acc_ref[0, rows, :] + w1 * acc_ref[1, rows, :] + w2 * acc_ref[2, rows, :]
        den = w0 * l_ref[0, rows, :] + w1 * l_ref[1, rows, :] + w2 * l_ref[2, rows, :]
        o_ref[rows, :] = num / den
        return carry

    lax.fori_loop(0, seq // ATT_COMBINE_ROWS, combine, 0)


def attn_prompt(q, k, v, batch, seq):
    n_hp = ATT_WIDTH // LANES
    blk = pl.BlockSpec((seq, LANES), lambda b, h: (b, h))
    return pl.pallas_call(
        functools.partial(_attn_prompt_body, seq=seq),
        grid=(batch, n_hp),
        in_specs=[blk, blk, blk],
        out_specs=blk,
        out_shape=jax.ShapeDtypeStruct((batch * seq, ATT_WIDTH), F32),
        scratch_shapes=[pltpu.VMEM((3, seq, LANES), F32)] * 3,
        compiler_params=pltpu.CompilerParams(dimension_semantics=("arbitrary", "arbitrary"),
                                             vmem_limit_bytes=VMEM_LIMIT),
        name="attn_prompt",
    )(q, k, v)


def _hgrn_prompt_body(*refs, rows_per_step, n_cast):
    q_ref, k_ref, v_ref, lf_ref, gh_ref, ng_ref = refs[:6]
    o_ref, s_ref = refs[6 + n_cast:8 + n_cast]
    st_ref = refs[-1]
    for w_ref, wb_ref in zip(refs[6:6 + n_cast], refs[8 + n_cast:8 + 2 * n_cast]):
        wb_ref[...] = w_ref[...].astype(BF16)
    grp = LANES
    n_ch = grp // HG_CHUNK
    ri = lax.broadcasted_iota(jnp.int32, (grp, grp), 0)
    ci = lax.broadcasted_iota(jnp.int32, (grp, grp), 1)
    same_chunk = (ri // HG_CHUNK) == (ci // HG_CHUNK)
    causal = same_chunk & (ci <= ri)
    summat = jnp.where(causal, 1.0, 0.0).astype(BF16)
    nt = (((1,), (1,)), ((), ()))
    tb = pl.program_id(1)

    @pl.when(tb == 0)
    def _():
        st_ref[...] = jnp.zeros(st_ref.shape, F32)

    heads = [slice(h * LANES, (h + 1) * LANES) for h in range(HG_HEADS)]
    in_chunk = [(ri // HG_CHUNK) == c for c in range(n_ch)]

    def by_chunk(x):
        return jnp.concatenate([jnp.where(m, x, 0.0).astype(BF16) for m in in_chunk], axis=1)

    def groups(it, carry):
        gs = range(HG_GROUPS)
        rows = [pl.ds(pl.multiple_of((it * HG_GROUPS + g) * grp, grp), grp) for g in gs]
        cum = []
        for g in gs:
            parts = _split3(lf_ref[rows[g], :])
            cum.append(jnp.dot(summat, parts[0], preferred_element_type=F32)
                       + jnp.dot(summat, parts[1], preferred_element_type=F32)
                       + jnp.dot(summat, parts[2], preferred_element_type=F32))
        q_dec, decay, a, vs, k_end = [], [], [], [], []
        for g in gs:
            cum_end = jnp.concatenate(
                [jnp.broadcast_to(cum[g][(c + 1) * HG_CHUNK - 1:(c + 1) * HG_CHUNK, :], (HG_CHUNK, HG_WIDTH))
                 for c in range(n_ch)], axis=0)
            q, k, v = q_ref[rows[g], :], k_ref[rows[g], :], v_ref[rows[g], :]
            q_dec.append(q * jnp.exp(cum[g]))
            k_dec = (k * jnp.exp(-cum[g])).astype(BF16)
            k_end.append(k * jnp.exp(cum_end - cum[g]))
            decay.append(jnp.exp(cum_end))
            q_decb = q_dec[g].astype(BF16)
            vs.append(v)
            a.append([lax.dot_general(q_decb[:, hs], k_dec[:, hs], nt, preferred_element_type=F32)
                      for hs in heads])
        o = [[jnp.dot(jnp.where(causal, a[g][h], 0.0).astype(BF16), vs[g][:, hs].astype(BF16),
                      preferred_element_type=F32) for h, hs in enumerate(heads)] for g in gs]
        ut = [[jnp.dot(vs[g][:, hs].T.astype(BF16), by_chunk(k_end[g][:, hs]), preferred_element_type=F32)
               for hs in heads] for g in gs]
        starts = [[[] for _ in heads] for _ in gs]
        for h, hs in enumerate(heads):
            st = st_ref[h]
            for g in gs:
                for c in range(n_ch):
                    starts[g][h].append(st.astype(BF16))
                    st = (st * decay[g][c * HG_CHUNK:c * HG_CHUNK + 1, hs]
                          + ut[g][h][:, c * LANES:(c + 1) * LANES])
            st_ref[h] = st
        inter = [[lax.dot_general(by_chunk(q_dec[g][:, hs]), jnp.concatenate(starts[g][h], axis=1), nt,
                                  preferred_element_type=F32) for h, hs in enumerate(heads)] for g in gs]
        for g in gs:
            for h, hs in enumerate(heads):
                oh = o[g][h] + inter[g][h]
                ms = jnp.mean(oh * oh, axis=-1, keepdims=True)
                gate = gh_ref[rows[g], hs]
                o_ref[rows[g], hs] = oh * lax.rsqrt(ms + LN_EPS) * ng_ref[:, hs] * (gate * _sigmoid(gate))
        return carry

    lax.fori_loop(0, rows_per_step // (grp * HG_GROUPS), groups, 0)

    @pl.when(tb == pl.num_programs(1) - 1)
    def _():
        for h in range(HG_HEADS):
            s_ref[0, h] = st_ref[h].T


HG_ROWS = 1024
HG_GROUPS = 4


def hgrn_prompt(qh, kh, ih, lf, gh, norm_g, batch, seq, cast=()):
    n_tb = seq // HG_ROWS
    steps = batch * n_tb

    def rows(b, t):
        return (b * n_tb + t, 0)

    blk = pl.BlockSpec((HG_ROWS, HG_WIDTH), rows)
    cast_specs = [pl.BlockSpec((w.shape[0] // steps, w.shape[1]), rows) for w in cast]
    return pl.pallas_call(
        functools.partial(_hgrn_prompt_body, rows_per_step=HG_ROWS, n_cast=len(cast)),
        grid=(batch, n_tb),
        in_specs=[blk, blk, blk, blk, blk, _const_spec((1, HG_WIDTH))] + cast_specs,
        out_specs=[blk, pl.BlockSpec((1, HG_HEADS, HG_DK, HG_DV), lambda b, t: (b, 0, 0, 0))] + cast_specs,
        out_shape=[jax.ShapeDtypeStruct((batch * seq, HG_WIDTH), F32),
                   jax.ShapeDtypeStruct((batch, HG_HEADS, HG_DK, HG_DV), F32)]
                  + [jax.ShapeDtypeStruct(w.shape, BF16) for w in cast],
        scratch_shapes=[pltpu.VMEM((HG_HEADS, HG_DV, HG_DK), F32)],
        compiler_params=pltpu.CompilerParams(dimension_semantics=("arbitrary", "arbitrary"),
                                             vmem_limit_bytes=VMEM_LIMIT),
        name="hgrn_prompt",
    )(qh, kh, ih, lf, gh, norm_g, *cast)


FA_ROWS = 128
FA_HEADS = 4
FA_SLOTS = 4


def _divmod_pow2(s, n):
    assert n & (n - 1) == 0
    if isinstance(s, int):
        return s // n, s % n
    return lax.shift_right_logical(s, n.bit_length() - 1), s & (n - 1)


def _ffn_attn_body(*refs, n_new, past, b_off, mixed, n_steps):
    n_x = 10 if mixed else 1
    wg_ref, wu_ref, wd_ref, g_ref, b_ref, q_ref, kn_ref, vn_ref, kc_hbm, vc_hbm = refs[n_x:n_x + 10]
    y_ref, o_ref, nk_ref, nv_ref, kbuf, vbuf, sems = refs[-8:-1] if mixed else refs[-7:]
    halves = ATT_HEADS // FA_HEADS
    step = pl.program_id(0)

    split = _divmod_pow2
    odd = (b_off + split(step, halves)[0]) & 1 == 1

    def window_copies(s):
        slot = split(s, FA_SLOTS)[1]
        b_rel, half = split(s, halves)
        src = (b_off + b_rel, pl.ds(half * FA_HEADS, FA_HEADS))
        return (pltpu.make_async_copy(kc_hbm.at[src], kbuf.at[slot], sems.at[slot, 0]),
                pltpu.make_async_copy(vc_hbm.at[src], vbuf.at[slot], sems.at[slot, 1]))

    def start_window(s):
        for thread, c in enumerate(window_copies(s)):
            c.start(priority=thread)

    @pl.when(step == 0)
    def _():
        for s in range(min(FA_SLOTS - 1, n_steps)):
            start_window(s)

    @pl.when(step + (FA_SLOTS - 1) < n_steps)
    def _():
        start_window(step + (FA_SLOTS - 1))

    for c in window_copies(step):
        c.wait()
    slot = split(step, FA_SLOTS)[1]
    kc_ref, vc_ref = kbuf.at[slot], vbuf.at[slot]

    def new_rows(ref):
        both = ref[...]
        return jnp.where(odd, both[n_new:2 * n_new], both[0:n_new])

    q_new, k_new, v_new = new_rows(q_ref), new_rows(kn_ref), new_rows(vn_ref)
    head_cols = [slice(h * ATT_HEAD_DIM, (h + 1) * ATT_HEAD_DIM) for h in range(FA_HEADS)]
    srow = lax.broadcasted_iota(jnp.int32, (n_new, past), 0)
    tcol = lax.broadcasted_iota(jnp.int32, (n_new, past), 1)
    dist = past + srow - tcol
    def on_stride(d, dil):
        return (d & (dil - 1)) == 0 if dil & (dil - 1) == 0 else d % dil == 0

    cnt = jnp.zeros((n_new, past), F32)
    for window, dil in DILATIONS:
        cnt = cnt + jnp.where(on_stride(dist, dil) & (dist <= window), 1.0, 0.0)
    si = lax.broadcasted_iota(jnp.int32, (n_new, n_new), 0)
    sj = lax.broadcasted_iota(jnp.int32, (n_new, n_new), 1)
    dn = si - sj
    cnt_new = jnp.zeros((n_new, n_new), F32)
    for window, dil in DILATIONS:
        cnt_new = cnt_new + jnp.where((dn >= 0) & on_stride(dn, dil) & (dn <= window), 1.0, 0.0)
    lane = lax.broadcasted_iota(jnp.int32, (1, LANES), 1)
    nt = (((1,), (1,)), ((), ()))
    shift = past - n_new
    hs = range(FA_HEADS)

    def scores():
        qs = [q_new[:, c].astype(BF16) for c in head_cols]
        s_c = [jnp.dot(qs[h], kc_ref[h].astype(BF16), preferred_element_type=F32) for h in hs]
        s_n = [lax.dot_general(qs[h], k_new[:, head_cols[h]].astype(BF16), nt, preferred_element_type=F32)
               for h in hs]
        return s_c, s_n

    if mixed:
        first, nxt = refs[0:3], refs[3:6]
        wa_ref, wh_ref, g2_ref, b2_ref = refs[6:n_x]
        xin_ref = refs[-1]

        def mixed_rows(x1_ref, oa_ref, oh_ref):
            mix = jnp.dot(oa_ref[...].astype(BF16), wa_ref[...], preferred_element_type=F32)
            mix = mix + jnp.dot(oh_ref[...].astype(BF16), wh_ref[...], preferred_element_type=F32)
            return _layer_norm(DN_ALPHA * x1_ref[...] + mix, g2_ref[...], b2_ref[...])

        @pl.when(step == 0)
        def _():
            xin_ref[...] = mixed_rows(*first)

        x = xin_ref[...]
        x_next = mixed_rows(*nxt)
    else:
        x = refs[0][...]
    xb = x.astype(BF16)
    gate = jnp.dot(xb, wg_ref[...], preferred_element_type=F32)
    up = jnp.dot(xb, wu_ref[...], preferred_element_type=F32)
    s_c, s_n = scores()
    knt = [k_new[:, c].T for c in head_cols]
    vnt = [v_new[:, c].T for c in head_cols]

    hmid = (gate * _sigmoid(gate) * up).astype(BF16)
    y = jnp.dot(hmid, wd_ref[...], preferred_element_type=F32)
    y_ref[...] = _layer_norm(DN_ALPHA * x + 0.5 * y, g_ref[...], b_ref[...])
    if mixed:
        xin_ref[...] = x_next

    p_c, p_n, den = [], [], []
    for h in hs:
        sc = jnp.where(cnt > 0, s_c[h], NEG_INF)
        sn = jnp.where(cnt_new > 0, s_n[h], NEG_INF)
        m = jnp.maximum(jnp.max(sc, axis=-1, keepdims=True), jnp.max(sn, axis=-1, keepdims=True))
        p_c.append((cnt * jnp.exp(sc - m)).astype(BF16))
        p_n.append((cnt_new * jnp.exp(sn - m)).astype(BF16))
        den.append(jnp.sum(p_c[h].astype(F32), axis=-1, keepdims=True)
                   + jnp.sum(p_n[h].astype(F32), axis=-1, keepdims=True))
    outs = []
    for h in hs:
        num = (lax.dot_general(p_c[h], vc_ref[h].astype(BF16), nt, preferred_element_type=F32)
               + jnp.dot(p_n[h], v_new[:, head_cols[h]].astype(BF16), preferred_element_type=F32))
        outs.append(num / den[h])
    o_ref[0] = jnp.concatenate(outs, axis=1)

    for h in hs:
        for src_ref, new, dst in ((kc_ref, knt[h], nk_ref), (vc_ref, vnt[h], nv_ref)):
            rolled = pltpu.roll(src_ref[h], shift, 1)
            dst[0, h, :, 0:past - LANES] = rolled[:, 0:past - LANES]
            last = rolled[:, past - LANES:past]
            for s in range(n_new):
                col = jnp.broadcast_to(new[:, s:s + 1], last.shape)
                last = jnp.where(lane == LANES - n_new + s, col, last)
            dst[0, h, :, past - LANES:past] = last


def ffn_attn(x, wg, wu, wd, g, b, q, kn, vn, kc, vc, n_new, b_off, prev=None, mix=None):
    m = x.shape[0]
    bsz, n_heads, _, past = kc.shape
    halves = n_heads // FA_HEADS
    steps = m // FA_ROWS
    assert b_off % 2 == 0 and bsz % 2 == 0, "new-token rows are fetched per pair of batch rows"

    def where(i):
        b_rel, half = _divmod_pow2(i, halves)
        return b_off + b_rel, half

    def samp(i):
        return (*where(i), 0, 0)

    row = pl.BlockSpec((FA_ROWS, D_MODEL), lambda i: (i, 0))
    pair = pl.BlockSpec((2 * n_new, FA_HEADS * ATT_HEAD_DIM),
                        lambda i: (_divmod_pow2(where(i)[0], 2)[0], where(i)[1]))
    o_blk = pl.BlockSpec((1, n_new, FA_HEADS * ATT_HEAD_DIM), lambda i: (where(i)[0], 0, where(i)[1]))
    big = pl.BlockSpec((1, FA_HEADS, ATT_HEAD_DIM, past), samp)
    if mix is None:
        in_specs, args = [row], [x]
    else:
        o_att, o_hg, w_out, ln_g, ln_b = mix
        in_specs, args = [], []
        for index in (lambda i: (0, 0), lambda i: (jnp.minimum(i + 1, steps - 1), 0)):
            in_specs += [pl.BlockSpec((FA_ROWS, D_MODEL), index), pl.BlockSpec((FA_ROWS, ATT_WIDTH), index),
                         pl.BlockSpec((FA_ROWS, HG_WIDTH), index)]
            args += [x, o_att, o_hg]
        in_specs += _w_out_specs() + [_const_spec((1, D_MODEL)), _const_spec((1, D_MODEL))]
        args += [w_out, w_out, ln_g, ln_b]
    hbm = pl.BlockSpec(memory_space=pl.ANY)
    in_specs += [_const_spec((D_MODEL, D_FF)), _const_spec((D_MODEL, D_FF)), _const_spec((D_FF, D_MODEL)),
                 _const_spec((1, D_MODEL)), _const_spec((1, D_MODEL)), pair, pair, pair, hbm, hbm]
    args += [wg, wu, wd, g, b, q, kn, vn, kc, vc]
    aliases = {}
    if prev is not None:
        for j, a in enumerate(prev):
            aliases[len(args)] = 1 + j
            in_specs.append(pl.BlockSpec(memory_space=pl.ANY))
            args.append(a)
    big_sds = jax.ShapeDtypeStruct(kc.shape, F32)
    return pl.pallas_call(
        functools.partial(_ffn_attn_body, n_new=n_new, past=past, b_off=b_off, mixed=mix is not None,
                          n_steps=steps),
        grid=(steps,),
        in_specs=in_specs,
        out_specs=[row, o_blk, big, big],
        out_shape=[jax.ShapeDtypeStruct((m, D_MODEL), F32),
                   jax.ShapeDtypeStruct((bsz, n_new, n_heads * ATT_HEAD_DIM), F32), big_sds, big_sds],
        scratch_shapes=[pltpu.VMEM((FA_SLOTS, FA_HEADS, ATT_HEAD_DIM, past), F32),
                        pltpu.VMEM((FA_SLOTS, FA_HEADS, ATT_HEAD_DIM, past), F32),
                        pltpu.SemaphoreType.DMA((FA_SLOTS, 2))]
                       + ([pltpu.VMEM((FA_ROWS, D_MODEL), F32)] if mix is not None else []),
        input_output_aliases=aliases,
        compiler_params=pltpu.CompilerParams(dimension_semantics=("arbitrary",),
                                             vmem_limit_bytes=VMEM_LIMIT),
        name="ffn_attn",
    )(*args)


HS_BATCH = 8


def _hgrn_sample_body(q_ref, k_ref, v_ref, lf_ref, gh_ref, ng_ref, s0_ref, o_ref, s_ref, *, n_new):
    row = lax.broadcasted_iota(jnp.int32, (n_new, 1), 0)
    tn = (((0,), (0,)), ((), ()))
    ones12 = jnp.ones((3 * n_new, HG_DV), BF16)

    sub = 8
    per_tile = sub // n_new

    def tile(t, carry):
        rows = pl.ds(pl.multiple_of(t * sub, sub), sub)
        units = []
        for h in range(HG_HEADS):
            cols = slice(h * LANES, (h + 1) * LANES)
            q8, k8, v8, g8 = q_ref[rows, cols], k_ref[rows, cols], v_ref[rows, cols], lf_ref[rows, cols]
            for i in range(per_tile):
                part = slice(i * n_new, (i + 1) * n_new)
                q, k, v, g = q8[part], k8[part], v8[part], g8[part]
                cum = jnp.zeros((n_new, LANES), F32)
                for j in range(n_new):
                    cum = cum + jnp.where(row >= j, g[j:j + 1, :], 0.0)
                cum_end = cum[n_new - 1:n_new, :]
                units.append(dict(h=h, b=t * per_tile + i, v=v, g=g, q_dec=q * jnp.exp(cum),
                                  k_dec=k * jnp.exp(-cum), k_end=k * jnp.exp(cum_end - cum)))
        for u in units:
            u["s0"] = s0_ref[u["b"], u["h"]]
            u["o"] = jnp.dot(u["q_dec"].astype(BF16), u["s0"].astype(BF16), preferred_element_type=F32)
        for u in units:
            dec = lax.dot_general(jnp.concatenate(_split3(u["g"]), axis=0), ones12, tn,
                                  preferred_element_type=F32)
            upd = lax.dot_general(u["k_end"].astype(BF16), u["v"].astype(BF16), tn,
                                  preferred_element_type=F32)
            s_ref[u["b"], u["h"]] = jnp.exp(dec) * u["s0"] + upd
        for h in range(HG_HEADS):
            cols = slice(h * LANES, (h + 1) * LANES)
            outs = []
            for u in units[h * per_tile:(h + 1) * per_tile]:
                o = u["o"]
                for j in range(n_new):
                    a_j = jnp.sum(u["q_dec"] * u["k_dec"][j:j + 1, :], axis=-1, keepdims=True)
                    o = o + jnp.where(row >= j, a_j, 0.0) * u["v"][j:j + 1, :]
                outs.append(o * lax.rsqrt(jnp.mean(o * o, axis=-1, keepdims=True) + LN_EPS))
            gate = gh_ref[rows, cols]
            o_ref[rows, cols] = jnp.concatenate(outs, axis=0) * ng_ref[:, cols] * (gate * _sigmoid(gate))
        return carry

    lax.fori_loop(0, HS_BATCH // per_tile, tile, 0)


def hgrn_sample(qh, kh, ih, lf, gh, norm_g, s0, n_new):
    bsz = s0.shape[0]
    assert bsz % HS_BATCH == 0 and 8 % n_new == 0, "whole grid steps of whole 8-row tiles"
    blk = pl.BlockSpec((HS_BATCH * n_new, HG_WIDTH), lambda i: (i, 0))
    sblk = pl.BlockSpec((HS_BATCH, HG_HEADS, HG_DK, HG_DV), lambda i: (i, 0, 0, 0))
    return pl.pallas_call(
        functools.partial(_hgrn_sample_body, n_new=n_new),
        grid=(bsz // HS_BATCH,),
        in_specs=[blk, blk, blk, blk, blk, _const_spec((1, HG_WIDTH)), sblk],
        out_specs=[blk, sblk],
        out_shape=[jax.ShapeDtypeStruct((bsz * n_new, HG_WIDTH), F32),
                   jax.ShapeDtypeStruct(s0.shape, F32)],
        compiler_params=pltpu.CompilerParams(dimension_semantics=("arbitrary",),
                                             vmem_limit_bytes=VMEM_LIMIT),
        name="hgrn_sample",
    )(qh, kh, ih, lf, gh, norm_g, s0)


def _rope_tables(pos):
    half = ATT_HEAD_DIM // 2
    inv = ROPE_THETA ** (-jnp.arange(half, dtype=F32) / half)
    ang = pos.astype(F32)[:, None] * inv[None, :]
    cos, sin = jnp.cos(ang), jnp.sin(ang)
    return jnp.tile(cos, (1, 4)), jnp.tile(jnp.concatenate([-sin, sin], axis=1), (1, 2))


def kernel(x_prompt, x_sample, cache_k, cache_v, state_hgrn, ffn1_w_gate, ffn1_w_up, ffn1_w_down, ln1_g, ln1_b, w_in, hg_lower_bound, hg_norm_g, w_out, ln2_g, ln2_b, ffn2_w_gate, ffn2_w_up, ffn2_w_down, ln3_g, ln3_b):
    batch, seq, _ = x_prompt.shape
    dec_batch, n_new, _ = x_sample.shape
    assert ffn1_w_gate.shape[0] == 1, "single layer"
    assert seq % (ATT_BLOCK * max(d for _, d in DILATIONS)) == 0 and seq % HG_ROWS == 0 and seq % ROW_TILE == 0
    assert dec_batch * n_new == ROW_TILE, "the sample rows form one row tile"


    cos_p, sin_p = _rope_tables(jnp.arange(seq, dtype=jnp.int32))
    pos_s = PAST_LEN + (jnp.arange(dec_batch * n_new, dtype=jnp.int32) % n_new)
    cos_s, sin_s = _rope_tables(pos_s)

    xp = x_prompt.reshape(batch * seq, D_MODEL)
    xs = x_sample.reshape(dec_batch * n_new, D_MODEL)

    xs1, *w1 = ffn_ln(xs, ffn1_w_gate[0], ffn1_w_up[0], ffn1_w_down[0], ln1_g, ln1_b)
    qs, ks, vs, qhs, khs, ihs, lfs, ghs, w_in_b = proj(xs1, w_in[0], cos_s, sin_s, hg_lower_bound, 1)

    samp = (qs, ks, vs,
            cache_k[0].transpose(0, 2, 3, 1),
            cache_v[0].transpose(0, 2, 3, 1), n_new)
    steps = (batch * seq) // FA_ROWS
    per_call = steps // (ATT_HEADS // FA_HEADS)
    assert 2 * per_call == dec_batch, "the two prompt FFN calls must cover the sample batch"

    xp1, o_t, nk, nv = ffn_attn(xp, *w1, ln1_g, ln1_b, *samp, 0)
    win_p = min(ATT_WINDOW_MAX, seq)
    q, k, v, qh, kh, ih, lf, gh, kt_p, vt_p = proj(xp1, w_in_b, cos_p, sin_p, hg_lower_bound, seq // ROW_TILE,
                                                   win_tiles=win_p // ROW_TILE)
    o_att = attn_prompt(q, k, v, batch, seq)
    o_hg, s_p, *w2, w_out_b = hgrn_prompt(qh, kh, ih, lf, gh, hg_norm_g, batch, seq,
                                          cast=(ffn2_w_gate[0], ffn2_w_up[0], ffn2_w_down[0], w_out[0]))
    yp, o_t, nk, nv = ffn_attn(xp1, *w2, ln3_g, ln3_b, *samp, per_call, prev=(o_t, nk, nv),
                               mix=(o_att, o_hg, w_out_b, ln2_g, ln2_b))

    o_att_s = o_t.reshape(dec_batch * n_new, ATT_WIDTH)

    o_hg_s, s_s = hgrn_sample(qhs, khs, ihs, lfs, ghs, hg_norm_g, state_hgrn[0], n_new)
    ys, = ffn_ln(xs1, *w2, ln3_g, ln3_b, mix=(o_att_s, o_hg_s, w_out_b, ln2_g, ln2_b))

    return (yp.reshape(batch, seq, D_MODEL),
            ys.reshape(dec_batch, n_new, D_MODEL),
            kt_p.transpose(0, 3, 1, 2)[None],
            vt_p.transpose(0, 3, 1, 2)[None],
            s_p[None],
            nk.transpose(0, 3, 1, 2)[None],
            nv.transpose(0, 3, 1, 2)[None],
            s_s[None])
```
